```python
import jax, jax.numpy as jnp
from jax import lax
import numpy as np

D_MODEL = 2048
BATCH = 8
SEQ = 4096
DEPTH = 2

CHUNK = 64
GLA_HEADS = 4
GLA_V = D_MODEL // 2
GLA_DV = GLA_V // GLA_HEADS
GLA_DK = GLA_DV // 2
GLA_QK = GLA_HEADS * GLA_DK
GLA_LOWRANK = 16
GLA_TAU = 16.0
CONV_WIDTH = D_MODEL // 2
CONV_K = 3
D_FF = 11 * D_MODEL // 4
EPS = 1e-6
SPLITS = (GLA_QK, GLA_QK, GLA_V, GLA_V, GLA_LOWRANK,
          CONV_WIDTH, CONV_WIDTH, CONV_WIDTH, D_MODEL, D_MODEL)
N_IN = sum(SPLITS)

kernel_name = "hybrid_gla_shortconv_convffn_adaln"


def rmsnorm(x, g):
    xf = x.astype(jnp.float32)
    y = xf * lax.rsqrt(jnp.mean(xf * xf, axis=-1, keepdims=True) + EPS)
    return (y * g.astype(jnp.float32)).astype(x.dtype)


def causal_dwconv(x, w):
    k = w.shape[0]
    s = x.shape[1]
    xp = jnp.pad(x, ((0, 0), (k - 1, 0), (0, 0)))
    y = w[0] * xp[:, 0:s]
    for i in range(1, k):
        y = y + w[i] * xp[:, i:i + s]
    return y


def gla_chunked(q, k, v, log_a):
    b, s, h, dk = q.shape
    dv = v.shape[-1]
    nc = s // CHUNK

    def to_chunks(t):
        return t.reshape(b, nc, CHUNK, h, t.shape[-1]).transpose(1, 0, 3, 2, 4)

    def step(state, inp):
        q_c, k_c, v_c, a_c = inp
        cum = jnp.cumsum(a_c, axis=2)
        cum_end = cum[:, :, -1:, :]
        k_dec = k_c * jnp.exp(cum_end - cum)
        state = (jnp.exp(cum_end[:, :, 0, :])[..., None] * state
                 + jnp.einsum('bhld,bhle->bhde', k_dec, v_c))
        o_c = jnp.einsum('bhld,bhde->bhle', q_c, state)
        return state, o_c

    s0 = jnp.zeros((b, h, dk, dv), jnp.float32)
    _, o = lax.scan(step, s0, (to_chunks(q), to_chunks(k), to_chunks(v), to_chunks(log_a)))
    return o.transpose(1, 0, 3, 2, 4).reshape(b, s, h, dv)


def token_mixer(h, w_in, w_a2, b_a2, gla_norm_g, w_out_gla, conv_mix_w, w_out_conv, w_o):
    bsz, s, _ = h.shape
    split_idx = [int(i) for i in np.cumsum(SPLITS)[:-1]]
    q, k, v, r, lr, cb, cc, cx, ga, gb = jnp.split(h @ w_in, split_idx, axis=-1)

    log_a = jax.nn.log_sigmoid((lr @ w_a2 + b_a2).astype(jnp.float32)) / GLA_TAU
    heads = lambda t, d: t.astype(jnp.float32).reshape(bsz, s, GLA_HEADS, d)
    o = gla_chunked(heads(q, GLA_DK) * (GLA_DK ** -0.5), heads(k, GLA_DK),
                    heads(v, GLA_DV), heads(log_a, GLA_DK))
    o = rmsnorm(o, gla_norm_g).astype(h.dtype).reshape(bsz, s, GLA_V)
    y_a = (o * jax.nn.silu(r)) @ w_out_gla

    y_b = (cb * causal_dwconv(cc * cx, conv_mix_w)) @ w_out_conv

    m = jax.nn.sigmoid(ga) * y_a + jax.nn.sigmoid(gb) * y_b
    return m @ w_o


def channel_mixer(h, w_up, ffn_conv_w, w_down):
    gate, up = jnp.split(h @ w_up, 2, axis=-1)
    return (jax.nn.gelu(causal_dwconv(gate, ffn_conv_w)) * up) @ w_down


def _fwd_setup_inputs(seed: int = 0) -> dict:
    key = jax.random.key(seed)
    ks = jax.random.split(key, 16)
    nrm = lambda k, shape, scale: jax.random.normal(k, shape, jnp.float32) * scale
    return {
        "x": nrm(ks[0], (BATCH, SEQ, D_MODEL), 1.0),
        "c": nrm(ks[1], (BATCH, D_MODEL), 1.0),
        "w_ada": nrm(ks[2], (DEPTH, D_MODEL, 6 * D_MODEL), 0.5 * D_MODEL ** -0.5),
        "b_ada": nrm(ks[3], (DEPTH, 6 * D_MODEL), 0.01),
        "norm_g": 1.0 + nrm(ks[4], (DEPTH, 4, D_MODEL), 0.05),
        "w_in": nrm(ks[5], (DEPTH, D_MODEL, N_IN), D_MODEL ** -0.5),
        "w_a2": nrm(ks[6], (DEPTH, GLA_LOWRANK, GLA_QK), GLA_LOWRANK ** -0.5),
        "b_a2": nrm(ks[7], (DEPTH, GLA_QK), 0.1),
        "gla_norm_g": 1.0 + nrm(ks[8], (DEPTH, GLA_DV), 0.05),
        "w_out_gla": nrm(ks[9], (DEPTH, GLA_V, D_MODEL), GLA_V ** -0.5),
        "conv_mix_w": nrm(ks[10], (DEPTH, CONV_K, CONV_WIDTH), 0.5),
        "w_out_conv": nrm(ks[11], (DEPTH, CONV_WIDTH, D_MODEL), CONV_WIDTH ** -0.5),
        "w_o": nrm(ks[12], (DEPTH, D_MODEL, D_MODEL), D_MODEL ** -0.5),
        "w_up": nrm(ks[13], (DEPTH, D_MODEL, 2 * D_FF), D_MODEL ** -0.5),
        "ffn_conv_w": nrm(ks[14], (DEPTH, CONV_K, D_FF), 0.5),
        "w_down": nrm(ks[15], (DEPTH, D_FF, D_MODEL), D_FF ** -0.5),
    }


def _fwd_reference(x, c, w_ada, b_ada, norm_g, w_in, w_a2, b_a2, gla_norm_g, w_out_gla,
              conv_mix_w, w_out_conv, w_o, w_up, ffn_conv_w, w_down):
    for l in range(DEPTH):
        mod = jax.nn.silu(c) @ w_ada[l] + b_ada[l]
        sh1, sc1, g1, sh2, sc2, g2 = [t[:, None, :] for t in jnp.split(mod, 6, axis=-1)]
        h = rmsnorm(x, norm_g[l, 0]) * (1.0 + sc1) + sh1
        y = token_mixer(h, w_in[l], w_a2[l], b_a2[l], gla_norm_g[l], w_out_gla[l],
                        conv_mix_w[l], w_out_conv[l], w_o[l])
        x = x + g1 * rmsnorm(y, norm_g[l, 1])
        h = rmsnorm(x, norm_g[l, 2]) * (1.0 + sc2) + sh2
        y = channel_mixer(h, w_up[l], ffn_conv_w[l], w_down[l])
        x = x + g2 * rmsnorm(y, norm_g[l, 3])
    return x


import jax as _jax
import jax.numpy as _jnp

TWIN_FORMAT = 'train_step'
FWD_PARAMS = ['x', 'c', 'w_ada', 'b_ada', 'norm_g', 'w_in', 'w_a2', 'b_a2', 'gla_norm_g', 'w_out_gla', 'conv_mix_w', 'w_out_conv', 'w_o', 'w_up', 'ffn_conv_w', 'w_down']
TWIN_WEIGHTS = ['w_ada', 'b_ada', 'norm_g', 'w_in', 'w_a2', 'b_a2', 'gla_norm_g', 'w_out_gla', 'conv_mix_w', 'w_out_conv', 'w_o', 'w_up', 'ffn_conv_w', 'w_down']
TWIN_DIFF_INPUT = 'x'
TWIN_INPUTS = ['x', 'c', 'w_ada', 'b_ada', 'norm_g', 'w_in', 'w_a2', 'b_a2', 'gla_norm_g', 'w_out_gla', 'conv_mix_w', 'w_out_conv', 'w_o', 'w_up', 'ffn_conv_w', 'w_down', 'loss_target', 'm_w_ada', 'm_b_ada', 'm_norm_g', 'm_w_in', 'm_w_a2', 'm_b_a2', 'm_gla_norm_g', 'm_w_out_gla', 'm_conv_mix_w', 'm_w_out_conv', 'm_w_o', 'm_w_up', 'm_ffn_conv_w', 'm_w_down', 'v_w_ada', 'v_b_ada', 'v_norm_g', 'v_w_in', 'v_w_a2', 'v_b_a2', 'v_gla_norm_g', 'v_w_out_gla', 'v_conv_mix_w', 'v_w_out_conv', 'v_w_o', 'v_w_up', 'v_ffn_conv_w', 'v_w_down']
TWIN_OUTPUTS = ['loss', 'grad_x', 'grad_w_ada', 'grad_b_ada', 'grad_norm_g', 'grad_w_in', 'grad_w_a2', 'grad_b_a2', 'grad_gla_norm_g', 'grad_w_out_gla', 'grad_conv_mix_w', 'grad_w_out_conv', 'grad_w_o', 'grad_w_up', 'grad_ffn_conv_w', 'grad_w_down', 'delta_w_ada', 'delta_b_ada', 'delta_norm_g', 'delta_w_in', 'delta_w_a2', 'delta_b_a2', 'delta_gla_norm_g', 'delta_w_out_gla', 'delta_conv_mix_w', 'delta_w_out_conv', 'delta_w_o', 'delta_w_up', 'delta_ffn_conv_w', 'delta_w_down', 'new_m_w_ada', 'new_m_b_ada', 'new_m_norm_g', 'new_m_w_in', 'new_m_w_a2', 'new_m_b_a2', 'new_m_gla_norm_g', 'new_m_w_out_gla', 'new_m_conv_mix_w', 'new_m_w_out_conv', 'new_m_w_o', 'new_m_w_up', 'new_m_ffn_conv_w', 'new_m_w_down', 'new_v_w_ada', 'new_v_b_ada', 'new_v_norm_g', 'new_v_w_in', 'new_v_w_a2', 'new_v_b_a2', 'new_v_gla_norm_g', 'new_v_w_out_gla', 'new_v_conv_mix_w', 'new_v_w_out_conv', 'new_v_w_o', 'new_v_w_up', 'new_v_ffn_conv_w', 'new_v_w_down']
TWIN_LEAF_KINDS = {'loss': 'loss', 'grad_x': 'grad_x', 'grad_w_ada': 'grad_w', 'grad_b_ada': 'grad_w', 'grad_norm_g': 'grad_w', 'grad_w_in': 'grad_w', 'grad_w_a2': 'grad_w', 'grad_b_a2': 'grad_w', 'grad_gla_norm_g': 'grad_w', 'grad_w_out_gla': 'grad_w', 'grad_conv_mix_w': 'grad_w', 'grad_w_out_conv': 'grad_w', 'grad_w_o': 'grad_w', 'grad_w_up': 'grad_w', 'grad_ffn_conv_w': 'grad_w', 'grad_w_down': 'grad_w', 'delta_w_ada': 'delta_w', 'delta_b_ada': 'delta_w', 'delta_norm_g': 'delta_w', 'delta_w_in': 'delta_w', 'delta_w_a2': 'delta_w', 'delta_b_a2': 'delta_w', 'delta_gla_norm_g': 'delta_w', 'delta_w_out_gla': 'delta_w', 'delta_conv_mix_w': 'delta_w', 'delta_w_out_conv': 'delta_w', 'delta_w_o': 'delta_w', 'delta_w_up': 'delta_w', 'delta_ffn_conv_w': 'delta_w', 'delta_w_down': 'delta_w', 'new_m_w_ada': 'new_m', 'new_m_b_ada': 'new_m', 'new_m_norm_g': 'new_m', 'new_m_w_in': 'new_m', 'new_m_w_a2': 'new_m', 'new_m_b_a2': 'new_m', 'new_m_gla_norm_g': 'new_m', 'new_m_w_out_gla': 'new_m', 'new_m_conv_mix_w': 'new_m', 'new_m_w_out_conv': 'new_m', 'new_m_w_o': 'new_m', 'new_m_w_up': 'new_m', 'new_m_ffn_conv_w': 'new_m', 'new_m_w_down': 'new_m', 'new_v_w_ada': 'new_v', 'new_v_b_ada': 'new_v', 'new_v_norm_g': 'new_v', 'new_v_w_in': 'new_v', 'new_v_w_a2': 'new_v', 'new_v_b_a2': 'new_v', 'new_v_gla_norm_g': 'new_v', 'new_v_w_out_gla': 'new_v', 'new_v_conv_mix_w': 'new_v', 'new_v_w_out_conv': 'new_v', 'new_v_w_o': 'new_v', 'new_v_w_up': 'new_v', 'new_v_ffn_conv_w': 'new_v', 'new_v_w_down': 'new_v'}


def _forward(args):
    return _fwd_reference(*[args[k] for k in FWD_PARAMS])


def _output_shape():
    def fwd():
        inp = _fwd_setup_inputs(0)
        return _fwd_reference(*[inp[k] for k in FWD_PARAMS])
    out = _jax.eval_shape(fwd)
    return out.shape, out.dtype

N_MICROBATCH = 1
ADAM_LR = 0.001
ADAM_B1 = 0.9
ADAM_B2 = 0.999
ADAM_EPS = 1e-08
ADAM_WD = 0.01
ADAM_STEP = 10
PER_EXAMPLE_BATCH_AXIS = {'x': 0, 'c': 0, 'loss_target': 0}
SHARED_INPUTS = []
_WEIGHT_DTYPES = {'w_ada': _jnp.float32, 'b_ada': _jnp.float32, 'norm_g': _jnp.float32, 'w_in': _jnp.float32, 'w_a2': _jnp.float32, 'b_a2': _jnp.float32, 'gla_norm_g': _jnp.float32, 'w_out_gla': _jnp.float32, 'conv_mix_w': _jnp.float32, 'w_out_conv': _jnp.float32, 'w_o': _jnp.float32, 'w_up': _jnp.float32, 'ffn_conv_w': _jnp.float32, 'w_down': _jnp.float32}
MOMENT_SCALE = {'w_ada': 5.965881e-01, 'b_ada': 1.298262e+00, 'norm_g': 1.140209e+00, 'w_in': 4.171536e-02, 'w_a2': 1.027065e-02, 'b_a2': 2.664295e-02, 'gla_norm_g': 7.617186e-02, 'w_out_gla': 2.608535e-02, 'conv_mix_w': 7.032832e-02, 'w_out_conv': 4.421178e-02, 'w_o': 5.270729e-02, 'w_up': 2.990202e-02, 'ffn_conv_w': 3.686741e-02, 'w_down': 5.158401e-02}


def _to_microbatches(a, axis):
    t = _jnp.moveaxis(a, axis, 0)
    t = t.reshape((N_MICROBATCH, t.shape[0] // N_MICROBATCH) + t.shape[1:])
    return _jnp.moveaxis(t, 1, axis + 1)


def setup_inputs(seed: int = 0) -> dict:
    inp = _fwd_setup_inputs(seed)
    key = _jax.random.fold_in(_jax.random.key(seed), 7919)
    shape, _ = _output_shape()
    out = dict(inp)
    out["loss_target"] = _jax.random.normal(_jax.random.fold_in(key, 0), shape, _jnp.float32)
    for i, name in enumerate(TWIN_WEIGHTS):
        w = inp[name].astype(_jnp.float32)
        if MOMENT_SCALE is None:
            s = _jnp.sqrt(_jnp.mean(_jnp.square(w)) + 1e-30)
        else:
            s = MOMENT_SCALE[name]
        km, kv = _jax.random.split(_jax.random.fold_in(key, i + 1))
        out[name] = w
        out["m_" + name] = s * _jax.random.normal(km, w.shape, _jnp.float32)
        out["v_" + name] = (s * s) * _jax.random.uniform(kv, w.shape, _jnp.float32, 0.5, 1.5)
    if N_MICROBATCH > 1:
        for name, axis in PER_EXAMPLE_BATCH_AXIS.items():
            out[name] = _to_microbatches(out[name], axis)
    return {'x': out['x'], 'c': out['c'], 'w_ada': out['w_ada'], 'b_ada': out['b_ada'], 'norm_g': out['norm_g'], 'w_in': out['w_in'], 'w_a2': out['w_a2'], 'b_a2': out['b_a2'], 'gla_norm_g': out['gla_norm_g'], 'w_out_gla': out['w_out_gla'], 'conv_mix_w': out['conv_mix_w'], 'w_out_conv': out['w_out_conv'], 'w_o': out['w_o'], 'w_up': out['w_up'], 'ffn_conv_w': out['ffn_conv_w'], 'w_down': out['w_down'], 'loss_target': out['loss_target'], 'm_w_ada': out['m_w_ada'], 'm_b_ada': out['m_b_ada'], 'm_norm_g': out['m_norm_g'], 'm_w_in': out['m_w_in'], 'm_w_a2': out['m_w_a2'], 'm_b_a2': out['m_b_a2'], 'm_gla_norm_g': out['m_gla_norm_g'], 'm_w_out_gla': out['m_w_out_gla'], 'm_conv_mix_w': out['m_conv_mix_w'], 'm_w_out_conv': out['m_w_out_conv'], 'm_w_o': out['m_w_o'], 'm_w_up': out['m_w_up'], 'm_ffn_conv_w': out['m_ffn_conv_w'], 'm_w_down': out['m_w_down'], 'v_w_ada': out['v_w_ada'], 'v_b_ada': out['v_b_ada'], 'v_norm_g': out['v_norm_g'], 'v_w_in': out['v_w_in'], 'v_w_a2': out['v_w_a2'], 'v_b_a2': out['v_b_a2'], 'v_gla_norm_g': out['v_gla_norm_g'], 'v_w_out_gla': out['v_w_out_gla'], 'v_conv_mix_w': out['v_conv_mix_w'], 'v_w_out_conv': out['v_w_out_conv'], 'v_w_o': out['v_w_o'], 'v_w_up': out['v_w_up'], 'v_ffn_conv_w': out['v_ffn_conv_w'], 'v_w_down': out['v_w_down']}


def _loss(weights, diff, rest, loss_target):
    with _jax.named_scope("forward"):
        args = {**rest, TWIN_DIFF_INPUT: diff, **{k: w.astype(_WEIGHT_DTYPES[k]) for k, w in weights.items()}}
        y = _forward(args)
    with _jax.named_scope("loss_head"):
        err = _jnp.square(y.astype(_jnp.float32) - loss_target)
        return 0.5 * _jnp.sum(_jnp.mean(err, axis=-1)) if err.ndim else 0.5 * err


def _adamw(w, g, m, v):
    m = ADAM_B1 * m + (1.0 - ADAM_B1) * g
    v = ADAM_B2 * v + (1.0 - ADAM_B2) * _jnp.square(g)
    m_hat = m / (1.0 - ADAM_B1 ** ADAM_STEP)
    v_hat = v / (1.0 - ADAM_B2 ** ADAM_STEP)
    delta = -ADAM_LR * (m_hat / (_jnp.sqrt(v_hat) + ADAM_EPS) + ADAM_WD * w)
    return delta, m, v


def reference(x, c, w_ada, b_ada, norm_g, w_in, w_a2, b_a2, gla_norm_g, w_out_gla, conv_mix_w, w_out_conv, w_o, w_up, ffn_conv_w, w_down, loss_target, m_w_ada, m_b_ada, m_norm_g, m_w_in, m_w_a2, m_b_a2, m_gla_norm_g, m_w_out_gla, m_conv_mix_w, m_w_out_conv, m_w_o, m_w_up, m_ffn_conv_w, m_w_down, v_w_ada, v_b_ada, v_norm_g, v_w_in, v_w_a2, v_b_a2, v_gla_norm_g, v_w_out_gla, v_conv_mix_w, v_w_out_conv, v_w_o, v_w_up, v_ffn_conv_w, v_w_down):
    given = dict(x=x, c=c, w_ada=w_ada, b_ada=b_ada, norm_g=norm_g, w_in=w_in, w_a2=w_a2, b_a2=b_a2, gla_norm_g=gla_norm_g, w_out_gla=w_out_gla, conv_mix_w=conv_mix_w, w_out_conv=w_out_conv, w_o=w_o, w_up=w_up, ffn_conv_w=ffn_conv_w, w_down=w_down, loss_target=loss_target, m_w_ada=m_w_ada, m_b_ada=m_b_ada, m_norm_g=m_norm_g, m_w_in=m_w_in, m_w_a2=m_w_a2, m_b_a2=m_b_a2, m_gla_norm_g=m_gla_norm_g, m_w_out_gla=m_w_out_gla, m_conv_mix_w=m_conv_mix_w, m_w_out_conv=m_w_out_conv, m_w_o=m_w_o, m_w_up=m_w_up, m_ffn_conv_w=m_ffn_conv_w, m_w_down=m_w_down, v_w_ada=v_w_ada, v_b_ada=v_b_ada, v_norm_g=v_norm_g, v_w_in=v_w_in, v_w_a2=v_w_a2, v_b_a2=v_b_a2, v_gla_norm_g=v_gla_norm_g, v_w_out_gla=v_w_out_gla, v_conv_mix_w=v_conv_mix_w, v_w_out_conv=v_w_out_conv, v_w_o=v_w_o, v_w_up=v_w_up, v_ffn_conv_w=v_ffn_conv_w, v_w_down=v_w_down)
    weights = {n: given[n] for n in TWIN_WEIGHTS}
    shared = {n: given[n] for n in SHARED_INPUTS}
    per_example = {n: given[n] for n in ['x', 'c']}
    grad_fn = _jax.value_and_grad(_loss, argnums=(0, 1))

    def one_microbatch(ex, loss_target):
        ex = dict(ex)
        diff = ex.pop(TWIN_DIFF_INPUT)
        return grad_fn(weights, diff, {**shared, **ex}, loss_target)

    if N_MICROBATCH == 1:
        loss, (grad_w, grad_x) = one_microbatch(per_example, given["loss_target"])
    else:
        def body(carry, xs):
            loss_sum, grad_sum = carry
            l_k, (gw_k, gx_k) = one_microbatch(xs[0], xs[1])
            with _jax.named_scope("update"):
                return (loss_sum + l_k, _jax.tree.map(_jnp.add, grad_sum, gw_k)), gx_k

        init = (_jnp.zeros((), _jnp.float32), _jax.tree.map(_jnp.zeros_like, weights))
        (loss, grad_w), grad_x = _jax.lax.scan(body, init, (per_example, given["loss_target"]))
    with _jax.named_scope("update"):
        delta_w, new_m, new_v = {}, {}, {}
        for n in TWIN_WEIGHTS:
            delta_w[n], new_m[n], new_v[n] = _adamw(weights[n], grad_w[n], given["m_" + n], given["v_" + n])
    return (loss, grad_x, *[grad_w[n] for n in TWIN_WEIGHTS], *[delta_w[n] for n in TWIN_WEIGHTS],
            *[new_m[n] for n in TWIN_WEIGHTS], *[new_v[n] for n in TWIN_WEIGHTS])
```

```python
import functools
import math
from typing import Any, NamedTuple

import jax
import jax.numpy as jnp
from jax import lax
from jax.experimental import pallas as pl
from jax.experimental.pallas import tpu as pltpu

F32 = jnp.float32
BF16 = jnp.bfloat16
MESH_AXES = ("x", "y", "c")
NDEV = 8
EPS = 1e-6
CHUNK = 64
HEADS = 4
GLA_TAU = 16.0
CONV_K = 3
ADAM_LR = 0.001
ADAM_B1 = 0.9
ADAM_B2 = 0.999
ADAM_EPS = 1e-08
ADAM_WD = 0.01
ADAM_STEP = 10

LANES = 128
SUBLANES = 8
HALO = SUBLANES
VMEM_LIMIT = 56 * 1024 * 1024


def _params(sem=None):
    return pltpu.CompilerParams(dimension_semantics=sem, vmem_limit_bytes=VMEM_LIMIT)


def _pick(n, cap, unit=LANES):
    best = None
    for d in range(unit, min(n, cap) + 1, unit):
        if n % d == 0:
            best = d
    return best if best is not None else n


def _mm(a, b, *, ta=False, tb=False, out_dtype=F32, name):
    if ta:
        K, M = a.shape
    else:
        M, K = a.shape
    if tb:
        N, Kb = b.shape
    else:
        Kb, N = b.shape
    assert K == Kb, (a.shape, b.shape, ta, tb)
    tm = _pick(M, 1024)
    tn = _pick(N, 1152)
    tk = _pick(K, 576)
    nk = K // tk
    dims = (((0 if ta else 1,), (1 if tb else 0,)), ((), ()))

    def body(a_ref, b_ref, o_ref, acc_ref):
        k = pl.program_id(2)

        @pl.when(k == 0)
        def _():
            acc_ref[...] = jnp.zeros_like(acc_ref)

        acc_ref[...] += lax.dot_general(a_ref[...].astype(BF16), b_ref[...].astype(BF16), dims,
                                        preferred_element_type=F32)

        @pl.when(k == nk - 1)
        def _():
            o_ref[...] = acc_ref[...].astype(o_ref.dtype)

    a_spec = (pl.BlockSpec((tk, tm), lambda i, j, k: (k, i)) if ta
              else pl.BlockSpec((tm, tk), lambda i, j, k: (i, k)))
    b_spec = (pl.BlockSpec((tn, tk), lambda i, j, k: (j, k)) if tb
              else pl.BlockSpec((tk, tn), lambda i, j, k: (k, j)))
    return pl.pallas_call(
        body, grid=(M // tm, N // tn, nk), in_specs=[a_spec, b_spec],
        out_specs=pl.BlockSpec((tm, tn), lambda i, j, k: (i, j)),
        out_shape=jax.ShapeDtypeStruct((M, N), out_dtype),
        scratch_shapes=[pltpu.VMEM((tm, tn), F32)],
        compiler_params=_params(("parallel", "parallel", "arbitrary")), name=name)(a, b)


class V(NamedTuple):
    arr: Any
    width: int
    base: Any


def _v(arr, width=None, col0=0):
    width = arr.shape[-1] if width is None else width
    assert col0 % width == 0
    return V(arr, width, col0 // width)


def _row_specs(v, tr, nrow8, halo):
    main = pl.BlockSpec((tr, v.width), lambda j, i, b=v.base: (i, b + j))
    if not halo:
        return [main]
    per = tr // HALO
    prev = pl.BlockSpec((HALO, v.width), lambda j, i, b=v.base: (jnp.maximum(i * per - 1, 0), b + j))
    nxt = pl.BlockSpec((HALO, v.width), lambda j, i, b=v.base: (jnp.minimum((i + 1) * per, nrow8 - 1), b + j))
    return [prev, main, nxt]


def _par_spec(p):
    if p.base is None:
        return pl.BlockSpec(p.arr.shape, lambda j, i: (0,) * p.arr.ndim)
    return pl.BlockSpec((p.arr.shape[0], p.width), lambda j, i, b=p.base: (0, b + j))


def _load_rows(refs, i, nrow, halo):
    if not halo:
        return refs[0][...].astype(F32)
    prev, main, nxt = refs
    pv = jnp.where(i > 0, prev[...].astype(F32), 0.0)
    nv = jnp.where(i < nrow - 1, nxt[...].astype(F32), 0.0)
    return jnp.concatenate([pv, main[...].astype(F32), nv], axis=0)


def _center_mask(tr):
    row = lax.broadcasted_iota(jnp.int32, (tr + 2 * HALO, 1), 0)
    return ((row >= HALO) & (row < HALO + tr)).astype(F32)


def _rw_fwd(f, rows, params, outs, *, tr, ncol=1, halo=False, name):
    S = rows[0].arr.shape[0]
    tr = min(tr, S)
    nrow = S // tr
    per = 3 if halo else 1
    nr, npar = len(rows), len(params)

    def body(*refs):
        i = pl.program_id(1)
        vals = [_load_rows(refs[per * r: per * (r + 1)], i, nrow, halo) for r in range(nr)]
        pvals = [refs[per * nr + p][...].astype(F32) for p in range(npar)]
        out_refs = refs[per * nr + npar:]
        res = f(*vals, *pvals, _center_mask(tr)) if halo else f(*vals, *pvals)
        for r, o in zip(res, out_refs):
            o[...] = (r[HALO:HALO + tr] if halo else r).astype(o.dtype)

    in_specs = []
    for v in rows:
        in_specs += _row_specs(v, tr, S // HALO, halo)
    in_specs += [_par_spec(p) for p in params]
    args = []
    for v in rows:
        args += [v.arr] * per
    args += [p.arr for p in params]
    res = pl.pallas_call(
        body, grid=(ncol, nrow), in_specs=in_specs,
        out_specs=[pl.BlockSpec((tr, tot // ncol), lambda j, i: (i, j)) for tot, _ in outs],
        out_shape=[jax.ShapeDtypeStruct((S, tot), dt) for tot, dt in outs],
        compiler_params=_params(("parallel", "arbitrary")), name=name)(*args)
    return res


def _rw_bwd(f, rows, params, douts, *, row_grads, par_grads, adds=None, tr, ncol=1, halo=False, name):
    S = rows[0].arr.shape[0]
    tr = min(tr, S)
    nrow = S // tr
    per = 3 if halo else 1
    adds = adds or {}
    nr, npar, nd = len(rows), len(params), len(douts)
    add_keys = sorted(adds)
    rg_idx = [k for k in range(nr) if row_grads[k]]
    pg_idx = [k for k in range(npar) if par_grads[k]]

    def body(*refs):
        i = pl.program_id(1)
        pos = 0
        vals = [_load_rows(refs[per * r: per * (r + 1)], i, nrow, halo) for r in range(nr)]
        pos = per * nr
        pvals = [refs[pos + p][...].astype(F32) for p in range(npar)]
        pos += npar
        dvals = [_load_rows(refs[pos + per * d: pos + per * (d + 1)], i, nrow, halo) for d in range(nd)]
        pos += per * nd
        add_vals = {k: refs[pos + n][...].astype(F32) for n, k in enumerate(add_keys)}
        pos += len(add_keys)
        rg_refs = refs[pos: pos + len(rg_idx)]
        pg_refs = refs[pos + len(rg_idx):]
        if halo:
            center = _center_mask(tr)
            fun = lambda *a: f(*a, center)
        else:
            fun = f
        _, vjp = jax.vjp(fun, *vals, *pvals)
        grads = vjp(tuple(dvals))
        for ref, k in zip(rg_refs, rg_idx):
            g = grads[k]
            g = g[HALO:HALO + tr] if halo else g
            if k in add_vals:
                g = g + add_vals[k]
            ref[...] = g.astype(ref.dtype)
        for ref, k in zip(pg_refs, pg_idx):
            @pl.when(i == 0)
            def _(ref=ref):
                ref[...] = jnp.zeros_like(ref)
            ref[...] += grads[nr + k]

    in_specs, args = [], []
    for v in list(rows):
        in_specs += _row_specs(v, tr, S // HALO, halo)
        args += [v.arr] * per
    in_specs += [_par_spec(p) for p in params]
    args += [p.arr for p in params]
    for v in douts:
        in_specs += _row_specs(v, tr, S // HALO, halo)
        args += [v.arr] * per
    for k in add_keys:
        in_specs += _row_specs(_v(adds[k], rows[k].width), tr, S // HALO, False)
        args += [adds[k]]
    out_specs, out_shape = [], []
    for k in rg_idx:
        w = rows[k].width
        out_specs.append(pl.BlockSpec((tr, w), lambda j, i: (i, j)))
        out_shape.append(jax.ShapeDtypeStruct((S, ncol * w), F32))
    for k in pg_idx:
        p = params[k]
        blk = p.arr.shape if p.base is None else (p.arr.shape[0], p.width)
        out_specs.append(pl.BlockSpec((None,) + tuple(blk), lambda j, i, n=len(blk): (j,) + (0,) * n))
        out_shape.append(jax.ShapeDtypeStruct((ncol,) + tuple(blk), F32))
    res = pl.pallas_call(
        body, grid=(ncol, nrow), in_specs=in_specs, out_specs=out_specs, out_shape=out_shape,
        compiler_params=_params(("parallel", "arbitrary")), name=name)(*args)
    rg = [None] * nr
    for n, k in enumerate(rg_idx):
        rg[k] = res[n]
    pg = [None] * npar
    for n, k in enumerate(pg_idx):
        pg[k] = res[len(rg_idx) + n]
    return rg, pg


def _sigmoid(x):
    return 1.0 / (1.0 + jnp.exp(-x))


def _rms(x, g):
    return x * lax.rsqrt(jnp.mean(x * x, axis=-1, keepdims=True) + EPS) * g


def _f_normmod(x, g, sc, sh):
    return (_rms(x, g) * (1.0 + sc) + sh,)


def _f_resid(x, y, gate, g):
    return (x + gate * _rms(y, g),)


def _f_merge(ga, gb, ya, yb):
    return (_sigmoid(ga) * ya + _sigmoid(gb) * yb,)


def _f_glapost(o, r, g):
    return (_rms(o, g) * (r * _sigmoid(r)),)


def _roll_rows(x, k):
    return pltpu.roll(x, k % x.shape[0], 0)


@functools.partial(jax.custom_vjp, nondiff_argnums=(1,))
def _shift(x, k):
    return _roll_rows(x, k)


def _shift_fwd(x, k):
    return _roll_rows(x, k), None


def _shift_bwd(k, _, g):
    return (_roll_rows(g, -k),)


_shift.defvjp(_shift_fwd, _shift_bwd)


def _taps(w, center):
    return center * w + (1.0 - center) * lax.stop_gradient(w)


def _conv3(u, w0, w1, w2, center):
    return (_taps(w0, center) * _shift(u, 2) + _taps(w1, center) * _shift(u, 1)) + _taps(w2, center) * u


def _f_convmix(cb, cc, cx, w0, w1, w2, center):
    return (cb * _conv3(cc * cx, w0, w1, w2, center),)


def _gelu_tanh(x):
    return 0.5 * x * (1.0 + jnp.tanh(math.sqrt(2.0 / math.pi) * (x + 0.044715 * (x * x * x))))


def _f_ffn(gate, up, w0, w1, w2, center):
    return (_gelu_tanh(_conv3(gate, w0, w1, w2, center)) * up,)


@jax.custom_vjp
def _bdot(a, b):
    return jnp.dot(a.astype(BF16), b.astype(BF16), preferred_element_type=F32)


def _bdot_fwd(a, b):
    return _bdot(a, b), (a, b)


def _bdot_bwd(res, g):
    a, b = res
    gb = g.astype(BF16)
    da = lax.dot_general(gb, b.astype(BF16), (((1,), (1,)), ((), ())), preferred_element_type=F32)
    db = lax.dot_general(a.astype(BF16), gb, (((0,), (0,)), ((), ())), preferred_element_type=F32)
    return da, db


_bdot.defvjp(_bdot_fwd, _bdot_bwd)


@jax.custom_vjp
def _log_sigmoid(z):
    return jnp.minimum(z, 0.0) - jnp.log(1.0 + jnp.exp(-jnp.abs(z)))


def _log_sigmoid_fwd(z):
    return _log_sigmoid(z), z


def _log_sigmoid_bwd(z, g):
    return (g * _sigmoid(-z),)


_log_sigmoid.defvjp(_log_sigmoid_fwd, _log_sigmoid_bwd)


def _f_loga(lr, w, b):
    return (_log_sigmoid(_bdot(lr, w) + b) / GLA_TAU,)


def _split3(x):
    hi = x.astype(BF16)
    r1 = x - hi.astype(F32)
    mid = r1.astype(BF16)
    lo = (r1 - mid.astype(F32)).astype(BF16)
    return hi, mid, lo


def _dot_exact(a01, x, dims):
    hi, mid, lo = _split3(x)
    d = lambda p: lax.dot_general(a01, p, dims, preferred_element_type=F32)
    return (d(lo) + d(mid)) + d(hi)


def _dot_exact_lhs(x, a01, dims):
    hi, mid, lo = _split3(x)
    d = lambda p: lax.dot_general(p, a01, dims, preferred_element_type=F32)
    return (d(lo) + d(mid)) + d(hi)


NN = (((1,), (0,)), ((), ()))
NT = (((1,), (1,)), ((), ()))
TN = (((0,), (0,)), ((), ()))


def _bf_dot(a, b, dims):
    return lax.dot_general(a.astype(BF16), b.astype(BF16), dims, preferred_element_type=F32)


def _chunk_decay(a, dv):
    L = a.shape[0]
    r = lax.broadcasted_iota(jnp.int32, (L, L), 0)
    c = lax.broadcasted_iota(jnp.int32, (L, L), 1)
    tri = (r >= c).astype(BF16)
    cum = _dot_exact(tri, a, NN)
    ce = cum[L - 1:L, :]
    dec = jnp.exp(_dot_exact_lhs(a, jnp.ones((L, dv), BF16), TN))
    return cum, ce, jnp.exp(ce - cum), dec


def _gla_fwd(P, a, *, qk, vw, name):
    S = P.shape[0]
    nc = S // CHUNK
    dk, dv = qk // HEADS, vw // HEADS
    scale = dk ** -0.5

    def body(q_ref, k_ref, v_ref, a_ref, o_ref, st_ref, state):
        @pl.when(pl.program_id(0) == 0)
        def _():
            state[...] = jnp.zeros_like(state)

        for h in range(HEADS):
            ks, vs = slice(h * dk, (h + 1) * dk), slice(h * dv, (h + 1) * dv)
            _, _, E, dec = _chunk_decay(a_ref[:, ks], dv)
            kd = k_ref[:, ks] * E
            s_new = dec * state[h] + _bf_dot(kd, v_ref[:, vs], TN)
            state[h] = s_new
            st_ref[0, h] = s_new
            o_ref[:, vs] = _bf_dot(q_ref[:, ks] * scale, s_new, NN)

    return pl.pallas_call(
        body, grid=(nc,),
        in_specs=[pl.BlockSpec((CHUNK, qk), lambda i: (i, 0)),
                  pl.BlockSpec((CHUNK, qk), lambda i: (i, 1)),
                  pl.BlockSpec((CHUNK, vw), lambda i: (i, 2 * qk // vw)),
                  pl.BlockSpec((CHUNK, qk), lambda i: (i, 0))],
        out_specs=[pl.BlockSpec((CHUNK, vw), lambda i: (i, 0)),
                   pl.BlockSpec((1, HEADS, dk, dv), lambda i: (i, 0, 0, 0))],
        out_shape=[jax.ShapeDtypeStruct((S, vw), F32), jax.ShapeDtypeStruct((nc, HEADS, dk, dv), F32)],
        scratch_shapes=[pltpu.VMEM((HEADS, dk, dv), F32)],
        compiler_params=_params(("arbitrary",)), name=name)(P, P, P, a)


def _gla_bwd(P, a, states, do, *, qk, vw, name):
    S = P.shape[0]
    nc = S // CHUNK
    dk, dv = qk // HEADS, vw // HEADS
    scale = dk ** -0.5

    def body(q_ref, k_ref, v_ref, a_ref, do_ref, sc_ref, sp_ref, dq_ref, dk_ref, dv_ref, da_ref, ds_carry):
        i = pl.program_id(0)

        @pl.when(i == 0)
        def _():
            ds_carry[...] = jnp.zeros_like(ds_carry)

        has_prev = (i < nc - 1).astype(F32)
        L = CHUNK
        r = lax.broadcasted_iota(jnp.int32, (L, L), 0)
        c = lax.broadcasted_iota(jnp.int32, (L, L), 1)
        tri_t = (c >= r).astype(BF16)
        last = (lax.broadcasted_iota(jnp.int32, (L, 1), 0) == L - 1).astype(F32)
        for h in range(HEADS):
            ks, vs = slice(h * dk, (h + 1) * dk), slice(h * dv, (h + 1) * dv)
            _, _, E, dec = _chunk_decay(a_ref[:, ks], dv)
            kk, vv, dov = k_ref[:, ks], v_ref[:, vs], do_ref[:, vs]
            kd = kk * E
            ds = ds_carry[h] + _bf_dot(q_ref[:, ks] * scale, dov, TN)
            dq_ref[:, ks] = _bf_dot(dov, sc_ref[0, h], NT) * scale
            dkd = _bf_dot(vv, ds, NT)
            dv_ref[:, vs] = _bf_dot(kd, ds, NN)
            dk_ref[:, ks] = dkd * E
            darg = dkd * kd
            w = ds * (sp_ref[0, h] * has_prev) * dec
            dce = _dot_exact(jnp.ones((SUBLANES, dv), BF16), w, NT)[0:1, :]
            dce = dce + jnp.sum(darg, axis=0, keepdims=True)
            dcum = last * dce - darg
            da_ref[:, ks] = _dot_exact(tri_t, dcum, NN)
            ds_carry[h] = dec * ds

    rev = lambda i: nc - 1 - i
    return pl.pallas_call(
        body, grid=(nc,),
        in_specs=[pl.BlockSpec((CHUNK, qk), lambda i: (rev(i), 0)),
                  pl.BlockSpec((CHUNK, qk), lambda i: (rev(i), 1)),
                  pl.BlockSpec((CHUNK, vw), lambda i: (rev(i), 2 * qk // vw)),
                  pl.BlockSpec((CHUNK, qk), lambda i: (rev(i), 0)),
                  pl.BlockSpec((CHUNK, vw), lambda i: (rev(i), 0)),
                  pl.BlockSpec((1, HEADS, dk, dv), lambda i: (rev(i), 0, 0, 0)),
                  pl.BlockSpec((1, HEADS, dk, dv), lambda i: (jnp.maximum(rev(i) - 1, 0), 0, 0, 0))],
        out_specs=[pl.BlockSpec((CHUNK, qk), lambda i: (rev(i), 0)),
                   pl.BlockSpec((CHUNK, qk), lambda i: (rev(i), 0)),
                   pl.BlockSpec((CHUNK, vw), lambda i: (rev(i), 0)),
                   pl.BlockSpec((CHUNK, qk), lambda i: (rev(i), 0))],
        out_shape=[jax.ShapeDtypeStruct((S, qk), F32), jax.ShapeDtypeStruct((S, qk), F32),
                   jax.ShapeDtypeStruct((S, vw), F32), jax.ShapeDtypeStruct((S, qk), F32)],
        scratch_shapes=[pltpu.VMEM((HEADS, dk, dv), F32)],
        compiler_params=_params(("arbitrary",)), name=name)(P, P, P, a, do, states, states)


def _loss_head(y, target, *, tr, name):
    S, D = y.shape
    tr = min(tr, S)

    def body(y_ref, t_ref, l_ref, dy_ref):
        @pl.when(pl.program_id(0) == 0)
        def _():
            l_ref[...] = jnp.zeros_like(l_ref)

        e = y_ref[...] - t_ref[...]
        dy_ref[...] = e / D
        l_ref[...] += 0.5 * jnp.sum(jnp.mean(e * e, axis=-1, keepdims=True), axis=0, keepdims=True)

    loss, dy = pl.pallas_call(
        body, grid=(S // tr,),
        in_specs=[pl.BlockSpec((tr, D), lambda i: (i, 0))] * 2,
        out_specs=[pl.BlockSpec((SUBLANES, LANES), lambda i: (0, 0)), pl.BlockSpec((tr, D), lambda i: (i, 0))],
        out_shape=[jax.ShapeDtypeStruct((SUBLANES, LANES), F32), jax.ShapeDtypeStruct((S, D), F32)],
        compiler_params=_params(("arbitrary",)), name=name)(y, target)
    return loss[0, 0], dy


def _adamw(parts, w, m, v, *, name):
    n, R, C = parts.shape
    tr = _pick(R, max(SUBLANES, (128 * 1024) // C), SUBLANES)
    bc1 = 1.0 - ADAM_B1 ** ADAM_STEP
    bc2 = 1.0 - ADAM_B2 ** ADAM_STEP

    def body(p_ref, w_ref, m_ref, v_ref, g_ref, d_ref, mo_ref, vo_ref):
        g = p_ref[0]
        for s in range(1, n):
            g = g + p_ref[s]
        m2 = ADAM_B1 * m_ref[...] + (1.0 - ADAM_B1) * g
        v2 = ADAM_B2 * v_ref[...] + (1.0 - ADAM_B2) * (g * g)
        g_ref[...] = g
        mo_ref[...] = m2
        vo_ref[...] = v2
        d_ref[...] = -ADAM_LR * ((m2 / bc1) / (jnp.sqrt(v2 / bc2) + ADAM_EPS) + ADAM_WD * w_ref[...])

    blk = pl.BlockSpec((tr, C), lambda i: (i, 0))
    return pl.pallas_call(
        body, grid=(R // tr,),
        in_specs=[pl.BlockSpec((n, tr, C), lambda i: (0, i, 0)), blk, blk, blk],
        out_specs=[blk] * 4, out_shape=[jax.ShapeDtypeStruct((R, C), F32)] * 4,
        compiler_params=_params(("parallel",)), name=name)(parts, w, m, v)


def _adamw_nd(parts, w, m, v, *, name):
    shape = w.shape
    C = shape[-1]
    R = math.prod(shape[:-1])
    outs = _adamw(parts.reshape(parts.shape[0], R, C), w.reshape(R, C), m.reshape(R, C), v.reshape(R, C), name=name)
    return [o.reshape(shape) for o in outs]


def _position():
    return lax.axis_index("x"), lax.axis_index("y"), lax.axis_index("c")


def _index(p):
    return 4 * p[0] + 2 * p[1] + p[2]


def _all_gather(arrs, *, name):
    n = len(arrs)
    HBM = pl.BlockSpec(memory_space=pl.ANY)

    def body(*refs):
        ins, outs = refs[:n], refs[n:2 * n]
        send_sems, recv_sems, local_sems = refs[2 * n:]
        x, y, c = _position()
        me, sibling = (x, y, c), (x, y, 1 - c)
        chips = [(1 - x, y), (x, 1 - y), (1 - x, 1 - y)]

        def copy(t, k, block, to, src=None):
            rows = outs[t].at[_index(block)]
            return pltpu.make_async_remote_copy(
                src_ref=rows if src is None else src, dst_ref=rows,
                send_sem=send_sems.at[t, k], recv_sem=recv_sems.at[t, k],
                device_id=to, device_id_type=pl.DeviceIdType.MESH)

        started = []
        mine = []
        for t in range(n):
            cp = pltpu.make_async_copy(ins[t], outs[t].at[_index(me)], local_sems.at[t])
            cp.start()
            mine.append(cp)
            first = [copy(t, 0, me, sibling, src=ins[t])]
            first += [copy(t, 1 + j, me, (*chip, c), src=ins[t]) for j, chip in enumerate(chips)]
            for cp in first:
                cp.start()
            started += first
        for t in range(n):
            for j, chip in enumerate(chips):
                copy(t, 1 + j, (*chip, c), me).wait_recv()
                cp = copy(t, 4 + j, (*chip, c), sibling)
                cp.start()
                started.append(cp)
        for t in range(n):
            copy(t, 0, sibling, me).wait_recv()
            for j, chip in enumerate(chips):
                copy(t, 4 + j, (*chip, 1 - c), me).wait_recv()
        for cp in started:
            cp.wait_send()
        for cp in mine:
            cp.wait()

    return pl.pallas_call(
        body, in_specs=[HBM] * n, out_specs=[HBM] * n,
        out_shape=[jax.ShapeDtypeStruct((NDEV,) + a.shape, a.dtype) for a in arrs],
        scratch_shapes=[pltpu.SemaphoreType.DMA((n, 7)), pltpu.SemaphoreType.DMA((n, 7)),
                        pltpu.SemaphoreType.DMA((n,))],
        name=name)(*arrs)


def _all_to_all(arrs, *, name):
    n = len(arrs)
    HBM = pl.BlockSpec(memory_space=pl.ANY)

    def body(*refs):
        ins, outs = refs[:n], refs[n:2 * n]
        send_sems, recv_sems, local_sems = refs[2 * n:]
        x, y, c = _position()
        me = (x, y, c)

        def peer(k):
            return ((1 - x) if k & 4 else x, (1 - y) if k & 2 else y, (1 - c) if k & 1 else c)

        def copy(t, k):
            return pltpu.make_async_remote_copy(
                src_ref=ins[t].at[_index(peer(k))], dst_ref=outs[t].at[_index(me)],
                send_sem=send_sems.at[t, k - 1], recv_sem=recv_sems.at[t, k - 1],
                device_id=peer(k), device_id_type=pl.DeviceIdType.MESH)

        def arrival(t, k):
            return pltpu.make_async_remote_copy(
                src_ref=ins[t].at[_index(me)], dst_ref=outs[t].at[_index(peer(k))],
                send_sem=send_sems.at[t, k - 1], recv_sem=recv_sems.at[t, k - 1],
                device_id=peer(k), device_id_type=pl.DeviceIdType.MESH)

        mine, sent = [], []
        for t in range(n):
            cp = pltpu.make_async_copy(ins[t].at[_index(me)], outs[t].at[_index(me)], local_sems.at[t])
            cp.start()
            mine.append(cp)
            for k in range(1, NDEV):
                cp = copy(t, k)
                cp.start()
                sent.append(cp)
        for t in range(n):
            for k in range(1, NDEV):
                arrival(t, k).wait_recv()
        for cp in sent:
            cp.wait_send()
        for cp in mine:
            cp.wait()

    return pl.pallas_call(
        body, in_specs=[HBM] * n, out_specs=[HBM] * n,
        out_shape=[jax.ShapeDtypeStruct(a.shape, a.dtype) for a in arrs],
        scratch_shapes=[pltpu.SemaphoreType.DMA((n, 7)), pltpu.SemaphoreType.DMA((n, 7)),
                        pltpu.SemaphoreType.DMA((n,))],
        name=name)(*arrs)


class _Pack:
    def __init__(self, arrs):
        self.shapes = [a.shape for a in arrs]
        self.sizes = [math.prod(s) for s in self.shapes]
        total = sum(self.sizes)
        unit = SUBLANES * LANES
        self.padded = -(-total // unit) * unit
        flat = jnp.concatenate([a.reshape(-1).astype(F32) for a in arrs] + [jnp.zeros((self.padded - total,), F32)])
        self.packed = flat.reshape(self.padded // LANES, LANES)

    def unpack(self, gathered):
        flat = gathered.reshape(NDEV, self.padded)
        out, off = [], 0
        for shape, size in zip(self.shapes, self.sizes):
            out.append(flat[:, off:off + size].reshape((NDEV,) + tuple(shape)))
            off += size
        return out


def _cols_from_shards(g):
    g = jnp.moveaxis(g, 0, -2)
    return g.reshape(g.shape[:-2] + (g.shape[-2] * g.shape[-1],))


def _cols_to_shards(w):
    K, N = w.shape
    return jnp.moveaxis(w.reshape(K, NDEV, N // NDEV), 1, 0)


def kernel(x, c, w_ada, b_ada, norm_g, w_in, w_a2, b_a2, gla_norm_g, w_out_gla, conv_mix_w, w_out_conv, w_o, w_up, ffn_conv_w, w_down, loss_target, m_w_ada, m_b_ada, m_norm_g, m_w_in, m_w_a2, m_b_a2, m_gla_norm_g, m_w_out_gla, m_conv_mix_w, m_w_out_conv, m_w_o, m_w_up, m_ffn_conv_w, m_w_down, v_w_ada, v_b_ada, v_norm_g, v_w_in, v_w_a2, v_b_a2, v_gla_norm_g, v_w_out_gla, v_conv_mix_w, v_w_out_conv, v_w_o, v_w_up, v_ffn_conv_w, v_w_down):
    depth = w_ada.shape[0]
    S, D = x.shape[1], x.shape[2]
    QK = w_a2.shape[2] * NDEV
    LR = w_a2.shape[1]
    VW = w_out_gla.shape[1]
    CW = w_out_conv.shape[1]
    FF = w_down.shape[1] * NDEV
    NIN = w_in.shape[2] * NDEV
    NADA = w_ada.shape[2]
    LRP = -(-LR // LANES) * LANES
    off_lr = 2 * QK + 2 * VW
    NP = NIN - LR + LRP
    assert NIN == 2 * QK + 2 * VW + LR + 3 * CW + 2 * D
    xi, yi, ci = _position()
    me = _index((xi, yi, ci))
    xs = x.reshape(S, D)
    target = loss_target.reshape(S, D)
    TR = 256
    TRC = 512
    TC = _pick(CW, 512)
    TCF = _pick(FF, 512)

    small = _Pack([c, norm_g, w_a2, conv_mix_w, ffn_conv_w])
    big = [w_in, w_out_gla, w_out_conv, w_o, w_up, w_down]
    gathered = _all_gather([small.packed] + [w.astype(BF16) for w in big], name="gather_weights")
    c_all, norm_g_s, w_a2_s, conv_w_s, ffn_w_s = small.unpack(gathered[0])
    c_all = c_all.reshape(NDEV, D)
    norm_g_f = _cols_from_shards(norm_g_s)
    w_a2_f = _cols_from_shards(w_a2_s)
    conv_w_f = _cols_from_shards(conv_w_s)
    ffn_w_f = _cols_from_shards(ffn_w_s)
    g_in, g_og, g_oc, g_o, g_up, g_down = gathered[1:]
    W_in = _cols_from_shards(g_in)
    W_all = jnp.concatenate([W_in[..., :off_lr], W_in[..., off_lr + LR:], W_in[..., off_lr:off_lr + LR],
                             jnp.zeros((depth, D, LRP - LR), BF16)], axis=-1)
    W_og = _cols_from_shards(g_og)
    W_oc = _cols_from_shards(g_oc)
    W_o = jnp.moveaxis(g_o, 0, 1).reshape(depth, D, D)
    W_up = _cols_from_shards(g_up)
    W_down = jnp.moveaxis(g_down, 0, 1).reshape(depth, FF, D)
    w_a2_p = jnp.concatenate([w_a2_f, jnp.zeros((depth, LRP - LR, QK), F32)], axis=1)

    MP = LANES
    c_pad = jnp.concatenate([c_all, jnp.zeros((MP - NDEV, D), F32)], axis=0)
    (silu_c,) = _rw_fwd(lambda t: (t * _sigmoid(t),), [_v(c_pad)], [], [(D, BF16)], tr=MP, name="silu_c")
    mod_cols = [_mm(silu_c, w_ada[l], name=f"ada_fwd{l}")[:NDEV] for l in range(depth)]
    (mod_g,) = _all_gather([jnp.stack(mod_cols)], name="gather_mod")
    mod_all = jnp.moveaxis(mod_g, 0, 2).reshape(depth, NDEV, NDEV * NADA)
    mod_me = lax.dynamic_slice_in_dim(mod_all, me, 1, axis=1).reshape(depth, NDEV * NADA)
    (mod_me,) = _rw_fwd(lambda a, b: (a + b,), [_v(mod_me), _v(b_ada)], [], [(NDEV * NADA, F32)],
                        tr=depth, name="mod_bias")

    col = {}
    off = 0
    for nm, wd in (("q", QK), ("k", QK), ("v", VW), ("r", VW), ("cb", CW), ("cc", CW), ("cx", CW),
                   ("ga", D), ("gb", D), ("lr", LRP)):
        col[nm] = off
        off += wd

    saved = []
    xin = xs
    for l in range(depth):
        sh1, sc1, g1, sh2, sc2, g2 = [mod_me[l:l + 1, i * D:(i + 1) * D] for i in range(6)]
        ng = [norm_g_f[l, i:i + 1] for i in range(4)]
        cw = [conv_w_f[l, i:i + 1] for i in range(CONV_K)]
        fw = [ffn_w_f[l, i:i + 1] for i in range(CONV_K)]
        gn = gla_norm_g[l:l + 1]
        ba2 = b_a2[l:l + 1]
        (h1,) = _rw_fwd(_f_normmod, [_v(xin)], [_v(ng[0]), _v(sc1), _v(sh1)], [(D, BF16)], tr=TR, name=f"normmod1_{l}")
        P = _mm(h1, W_all[l], name=f"in_proj{l}")
        (a,) = _rw_fwd(_f_loga, [_v(P, LRP, col["lr"])], [V(w_a2_p[l], QK, None), _v(ba2)], [(QK, F32)],
                       tr=TR, name=f"loga{l}")
        o, states = _gla_fwd(P, a, qk=QK, vw=VW, name=f"gla_fwd{l}")
        dvh = VW // HEADS
        (ya_in,) = _rw_fwd(_f_glapost, [_v(o, dvh), _v(P, dvh, col["r"])], [V(gn, dvh, None)], [(VW, BF16)],
                           tr=TR, ncol=HEADS, name=f"glapost{l}")
        ya = _mm(ya_in, W_og[l], name=f"out_gla{l}")
        (yb_in,) = _rw_fwd(_f_convmix, [_v(P, TC, col["cb"]), _v(P, TC, col["cc"]), _v(P, TC, col["cx"])],
                           [_v(w, TC) for w in cw], [(CW, BF16)], tr=TRC, ncol=CW // TC, halo=True, name=f"convmix{l}")
        yb = _mm(yb_in, W_oc[l], name=f"out_conv{l}")
        (mg,) = _rw_fwd(_f_merge, [_v(P, D, col["ga"]), _v(P, D, col["gb"]), _v(ya), _v(yb)], [], [(D, BF16)],
                        tr=TR, name=f"merge{l}")
        y = _mm(mg, W_o[l], name=f"o_proj{l}")
        (x1,) = _rw_fwd(_f_resid, [_v(xin), _v(y)], [_v(g1), _v(ng[1])], [(D, F32)], tr=TR, name=f"resid1_{l}")
        (h2,) = _rw_fwd(_f_normmod, [_v(x1)], [_v(ng[2]), _v(sc2), _v(sh2)], [(D, BF16)], tr=TR, name=f"normmod2_{l}")
        U = _mm(h2, W_up[l], name=f"up_proj{l}")
        (ff,) = _rw_fwd(_f_ffn, [_v(U, TCF, 0), _v(U, TCF, FF)], [_v(w, TCF) for w in fw], [(FF, BF16)],
                        tr=TRC, ncol=FF // TCF, halo=True, name=f"ffn{l}")
        y2 = _mm(ff, W_down[l], name=f"down_proj{l}")
        (x2,) = _rw_fwd(_f_resid, [_v(x1), _v(y2)], [_v(g2), _v(ng[3])], [(D, F32)], tr=TR, name=f"resid2_{l}")
        saved.append(dict(xin=xin, h1=h1, P=P, a=a, o=o, states=states, ya_in=ya_in, ya=ya, yb_in=yb_in, yb=yb,
                          mg=mg, y=y, x1=x1, h2=h2, U=U, ff=ff, y2=y2, mods=(sh1, sc1, g1, sh2, sc2, g2),
                          ng=ng, cw=cw, fw=fw, gn=gn, ba2=ba2))
        xin = x2

    loss_local, dx = _loss_head(xin, target, tr=TR, name="loss_head")
    loss = lax.psum(loss_local, MESH_AXES)

    gW = [None] * depth
    gsmall = [None] * depth
    dmod = [None] * depth
    for l in reversed(range(depth)):
        s = saved[l]
        sh1, sc1, g1, sh2, sc2, g2 = s["mods"]
        ng, cw, fw = s["ng"], s["cw"], s["fw"]
        (dx1_a, dy2), (dg2, dng3) = _rw_bwd(
            _f_resid, [_v(s["x1"]), _v(s["y2"])], [_v(g2), _v(ng[3])], [_v(dx)],
            row_grads=[True, True], par_grads=[True, True], tr=TR, name=f"resid2_bwd{l}")
        dff = _mm(dy2, W_down[l], tb=True, name=f"down_dgrad{l}")
        gW_down = _mm(s["ff"], dy2, ta=True, name=f"down_wgrad{l}")
        U = s["U"]
        (dgate, dup), dfw = _rw_bwd(
            _f_ffn, [_v(U, TCF, 0), _v(U, TCF, FF)], [_v(w, TCF) for w in fw], [_v(dff, TCF)],
            row_grads=[True, True], par_grads=[True] * 3, tr=TRC, ncol=FF // TCF, halo=True, name=f"ffn_bwd{l}")
        dU = jnp.concatenate([dgate, dup], axis=1)
        dh2 = _mm(dU, W_up[l], tb=True, name=f"up_dgrad{l}")
        gW_up = _mm(s["h2"], dU, ta=True, name=f"up_wgrad{l}")
        (dx1,), (dng2, dsc2, dsh2) = _rw_bwd(
            _f_normmod, [_v(s["x1"])], [_v(ng[2]), _v(sc2), _v(sh2)], [_v(dh2)],
            row_grads=[True], par_grads=[True] * 3, adds={0: dx1_a}, tr=TR, name=f"normmod2_bwd{l}")
        (dxin_a, dy), (dg1, dng1) = _rw_bwd(
            _f_resid, [_v(s["xin"]), _v(s["y"])], [_v(g1), _v(ng[1])], [_v(dx1)],
            row_grads=[True, True], par_grads=[True, True], tr=TR, name=f"resid1_bwd{l}")
        dmg = _mm(dy, W_o[l], tb=True, name=f"o_dgrad{l}")
        gW_o = _mm(s["mg"], dy, ta=True, name=f"o_wgrad{l}")
        P = s["P"]
        (dga, dgb, dya, dyb), _ = _rw_bwd(
            _f_merge, [_v(P, D, col["ga"]), _v(P, D, col["gb"]), _v(s["ya"]), _v(s["yb"])], [], [_v(dmg)],
            row_grads=[True] * 4, par_grads=[], tr=TR, name=f"merge_bwd{l}")
        dya_in = _mm(dya, W_og[l], tb=True, name=f"out_gla_dgrad{l}")
        gW_og = _mm(s["ya_in"], dya, ta=True, name=f"out_gla_wgrad{l}")
        dyb_in = _mm(dyb, W_oc[l], tb=True, name=f"out_conv_dgrad{l}")
        gW_oc = _mm(s["yb_in"], dyb, ta=True, name=f"out_conv_wgrad{l}")
        (dcb, dcc, dcx), dcw = _rw_bwd(
            _f_convmix, [_v(P, TC, col["cb"]), _v(P, TC, col["cc"]), _v(P, TC, col["cx"])],
            [_v(w, TC) for w in cw], [_v(dyb_in, TC)],
            row_grads=[True] * 3, par_grads=[True] * 3, tr=TRC, ncol=CW // TC, halo=True, name=f"convmix_bwd{l}")
        dvh = VW // HEADS
        (do, dr), (dgn,) = _rw_bwd(
            _f_glapost, [_v(s["o"], dvh), _v(P, dvh, col["r"])], [V(s["gn"], dvh, None)], [_v(dya_in, dvh)],
            row_grads=[True, True], par_grads=[True], tr=TR, ncol=HEADS, name=f"glapost_bwd{l}")
        dq, dk, dv, da = _gla_bwd(P, s["a"], s["states"], do, qk=QK, vw=VW, name=f"gla_bwd{l}")
        (dlr,), (dwa2, dba2) = _rw_bwd(
            _f_loga, [_v(P, LRP, col["lr"])], [V(w_a2_p[l], QK, None), _v(s["ba2"])], [_v(da)],
            row_grads=[True], par_grads=[True, True], tr=TR, name=f"loga_bwd{l}")
        dP = jnp.concatenate([dq, dk, dv, dr, dcb, dcc, dcx, dga, dgb, dlr], axis=1)
        dh1 = _mm(dP, W_all[l], tb=True, name=f"in_dgrad{l}")
        gW_all = _mm(s["h1"], dP, ta=True, name=f"in_wgrad{l}")
        (dx,), (dng0, dsc1, dsh1) = _rw_bwd(
            _f_normmod, [_v(s["xin"])], [_v(ng[0]), _v(sc1), _v(sh1)], [_v(dh1)],
            row_grads=[True], par_grads=[True] * 3, adds={0: dxin_a}, tr=TR, name=f"normmod1_bwd{l}")
        gW_in = jnp.concatenate([gW_all[:, :off_lr], gW_all[:, NIN - LR:NIN], gW_all[:, off_lr:NIN - LR]], axis=1)
        gW[l] = dict(w_in=_cols_to_shards(gW_in), w_out_gla=_cols_to_shards(gW_og),
                     w_out_conv=_cols_to_shards(gW_oc), w_o=gW_o.reshape(NDEV, D // NDEV, D),
                     w_up=_cols_to_shards(gW_up), w_down=gW_down.reshape(NDEV, FF // NDEV, D))
        vec = lambda t: t.reshape(1, -1)
        gsmall[l] = dict(
            norm_g=jnp.concatenate([vec(dng0), vec(dng1), vec(dng2), vec(dng3)], axis=0),
            w_a2=dwa2[0, :LR], b_a2=dba2.reshape(QK), gla_norm_g=jnp.sum(dgn, axis=0).reshape(dvh),
            conv_mix_w=jnp.concatenate([vec(t) for t in dcw], axis=0),
            ffn_conv_w=jnp.concatenate([vec(t) for t in dfw], axis=0))
        dmod[l] = jnp.concatenate([vec(t) for t in (dsh1, dsc1, dg1, dsh2, dsc2, dg2)], axis=1).reshape(-1)
    grad_x = dx.reshape(x.shape)

    stack = lambda key, src: jnp.stack([src[l][key] for l in range(depth)])
    small_names = ["norm_g", "w_a2", "b_a2", "gla_norm_g", "conv_mix_w", "ffn_conv_w"]
    gpack = _Pack([jnp.stack(dmod)] + [stack(k, gsmall) for k in small_names])
    big_names = ["w_in", "w_out_gla", "w_out_conv", "w_o", "w_up", "w_down"]
    send = [jnp.stack([gW[l][k] for l in range(depth)], axis=1) for k in big_names]
    (gsm,) = _all_gather([gpack.packed], name="gather_small_grads")
    parts_big = _all_to_all(send, name="exchange_grads")
    gs = gpack.unpack(gsm)
    dmod_all = gs[0]
    parts_small = dict(zip(small_names, gs[1:]))

    def my_cols(t, n):
        return lax.dynamic_slice_in_dim(t, me * n, n, axis=t.ndim - 1)

    gW_ada = []
    for l in range(depth):
        dcols = my_cols(dmod_all[:, l], NADA)
        dcols = jnp.concatenate([dcols, jnp.zeros((MP - NDEV, NADA), F32)], axis=0)
        gW_ada.append(_mm(silu_c, dcols, ta=True, name=f"ada_wgrad{l}"))
    parts = dict(zip(big_names, parts_big))
    parts["w_ada"] = jnp.stack(gW_ada)[None]
    parts["b_ada"] = dmod_all
    parts["norm_g"] = my_cols(parts_small["norm_g"], D // NDEV)
    parts["w_a2"] = my_cols(parts_small["w_a2"], QK // NDEV)
    parts["b_a2"] = parts_small["b_a2"]
    parts["gla_norm_g"] = parts_small["gla_norm_g"]
    parts["conv_mix_w"] = my_cols(parts_small["conv_mix_w"], CW // NDEV)
    parts["ffn_conv_w"] = my_cols(parts_small["ffn_conv_w"], FF // NDEV)

    weights = dict(w_ada=(w_ada, m_w_ada, v_w_ada), b_ada=(b_ada, m_b_ada, v_b_ada),
                   norm_g=(norm_g, m_norm_g, v_norm_g), w_in=(w_in, m_w_in, v_w_in), w_a2=(w_a2, m_w_a2, v_w_a2),
                   b_a2=(b_a2, m_b_a2, v_b_a2), gla_norm_g=(gla_norm_g, m_gla_norm_g, v_gla_norm_g),
                   w_out_gla=(w_out_gla, m_w_out_gla, v_w_out_gla), conv_mix_w=(conv_mix_w, m_conv_mix_w, v_conv_mix_w),
                   w_out_conv=(w_out_conv, m_w_out_conv, v_w_out_conv), w_o=(w_o, m_w_o, v_w_o),
                   w_up=(w_up, m_w_up, v_w_up), ffn_conv_w=(ffn_conv_w, m_ffn_conv_w, v_ffn_conv_w),
                   w_down=(w_down, m_w_down, v_w_down))
    grads, deltas, new_m, new_v = [], [], [], []
    for nm, (w, m, v) in weights.items():
        g, d, m2, v2 = _adamw_nd(parts[nm], w, m, v, name=f"adamw_{nm}")
        grads.append(g)
        deltas.append(d)
        new_m.append(m2)
        new_v.append(v2)
    return (loss, grad_x, *grads, *deltas, *new_m, *new_v)
```

```python
import functools
import math
from typing import Any, NamedTuple

import jax
import jax.numpy as jnp
from jax import lax
from jax.experimental import pallas as pl
from jax.experimental.pallas import tpu as pltpu

F32 = jnp.float32
BF16 = jnp.bfloat16
MESH_AXES = ("x", "y", "c")
NDEV = 8
EPS = 1e-6
CHUNK = 64
HEADS = 4
GLA_TAU = 16.0
CONV_K = 3
ADAM_LR = 0.001
ADAM_B1 = 0.9
ADAM_B2 = 0.999
ADAM_EPS = 1e-08
ADAM_WD = 0.01
ADAM_STEP = 10

LANES = 128
SUBLANES = 8
HALO = SUBLANES
VMEM_LIMIT = 56 * 1024 * 1024


def _params(sem=None):
    return pltpu.CompilerParams(dimension_semantics=sem, vmem_limit_bytes=VMEM_LIMIT)


def _pick(n, cap, unit=LANES):
    best = None
    for d in range(unit, min(n, cap) + 1, unit):
        if n % d == 0:
            best = d
    return best if best is not None else n


def _position():
    return lax.axis_index("x"), lax.axis_index("y"), lax.axis_index("c")


def _index(p):
    return 4 * p[0] + 2 * p[1] + p[2]


def _peer(k):
    x, y, c = _position()
    return ((1 - x) if k & 4 else x, (1 - y) if k & 2 else y, (1 - c) if k & 1 else c)


class _Exchange:
    def __init__(self, arrs, gather):
        self.arrs = list(arrs)
        self.gather = gather
        n = len(self.arrs)
        self.out_shape = [jax.ShapeDtypeStruct(((NDEV,) + a.shape) if gather else a.shape, a.dtype) for a in self.arrs]
        self.scratch = [pltpu.SemaphoreType.DMA((n, NDEV - 1)), pltpu.SemaphoreType.DMA((n, NDEV - 1)),
                        pltpu.SemaphoreType.DMA((n,))]

    def _copies(self, ins, outs, sems):
        send_sems, recv_sems, local_sems = sems
        me = _index(_position())
        local, sends, arrivals = [], [], []
        for t in range(len(self.arrs)):
            src = (lambda d, t=t: ins[t]) if self.gather else (lambda d, t=t: ins[t].at[d])
            local.append(pltpu.make_async_copy(src(me), outs[t].at[me], local_sems.at[t]))
            for k in range(1, NDEV):
                other = _peer(k)
                common = dict(send_sem=send_sems.at[t, k - 1], recv_sem=recv_sems.at[t, k - 1],
                              device_id=other, device_id_type=pl.DeviceIdType.MESH)
                sends.append(pltpu.make_async_remote_copy(src_ref=src(_index(other)), dst_ref=outs[t].at[me], **common))
                arrivals.append(pltpu.make_async_remote_copy(src_ref=src(me), dst_ref=outs[t].at[_index(other)], **common))
        return local, sends, arrivals

    def start(self, ins, outs, sems):
        local, sends, _ = self._copies(ins, outs, sems)
        for cp in local + sends:
            cp.start()

    def wait(self, ins, outs, sems):
        local, sends, arrivals = self._copies(ins, outs, sems)
        for cp in arrivals:
            cp.wait_recv()
        for cp in sends:
            cp.wait_send()
        for cp in local:
            cp.wait()


def _call(body, *, grid, in_specs, out_specs, out_shape, scratch_shapes=(), args, name, comm=None):
    sem = ("arbitrary",) * len(grid)
    if comm is None:
        res = pl.pallas_call(body, grid=grid, in_specs=in_specs, out_specs=out_specs, out_shape=out_shape,
                             scratch_shapes=list(scratch_shapes), compiler_params=_params(sem), name=name)(*args)
        return res, None
    nin, nout, nscr, nc = len(in_specs), len(out_specs), len(scratch_shapes), len(comm.arrs)
    hbm = pl.BlockSpec(memory_space=pl.ANY)

    def both(*refs):
        ins, c_ins = refs[:nin], refs[nin:nin + nc]
        outs, c_outs = refs[nin + nc:nin + nc + nout], refs[nin + nc + nout:nin + 2 * nc + nout]
        scr, sems = refs[nin + 2 * nc + nout:nin + 2 * nc + nout + nscr], refs[nin + 2 * nc + nout + nscr:]
        first = functools.reduce(lambda p, q: p & q, [pl.program_id(d) == 0 for d in range(len(grid))])
        last = functools.reduce(lambda p, q: p & q, [pl.program_id(d) == grid[d] - 1 for d in range(len(grid))])

        @pl.when(first)
        def _():
            comm.start(c_ins, c_outs, sems)

        body(*ins, *outs, *scr)

        @pl.when(last)
        def _():
            comm.wait(c_ins, c_outs, sems)

    res = pl.pallas_call(both, grid=grid, in_specs=list(in_specs) + [hbm] * nc, out_specs=list(out_specs) + [hbm] * nc,
                         out_shape=list(out_shape) + comm.out_shape, scratch_shapes=list(scratch_shapes) + comm.scratch,
                         compiler_params=_params(sem), name=name)(*args, *comm.arrs)
    return res[:nout], res[nout:]


def _run_exchange(comm, *, name):
    hbm = pl.BlockSpec(memory_space=pl.ANY)
    nc = len(comm.arrs)

    def body(*refs):
        comm.start(refs[:nc], refs[nc:2 * nc], refs[2 * nc:])
        comm.wait(refs[:nc], refs[nc:2 * nc], refs[2 * nc:])

    return pl.pallas_call(body, in_specs=[hbm] * nc, out_specs=[hbm] * nc, out_shape=comm.out_shape,
                          scratch_shapes=comm.scratch, name=name)(*comm.arrs)


def _mm(a, b, *, ta=False, tb=False, out_dtype=F32, name, comm=None):
    if ta:
        K, M = a.shape
    else:
        M, K = a.shape
    if tb:
        N, Kb = b.shape
    else:
        Kb, N = b.shape
    assert K == Kb, (a.shape, b.shape, ta, tb)
    tm = _pick(M, 1024)
    tn = _pick(N, 1152)
    tk = _pick(K, 576)
    nk = K // tk
    dims = (((0 if ta else 1,), (1 if tb else 0,)), ((), ()))

    def body(a_ref, b_ref, o_ref, acc_ref):
        k = pl.program_id(2)

        @pl.when(k == 0)
        def _():
            acc_ref[...] = jnp.zeros_like(acc_ref)

        acc_ref[...] += lax.dot_general(a_ref[...].astype(BF16), b_ref[...].astype(BF16), dims,
                                        preferred_element_type=F32)

        @pl.when(k == nk - 1)
        def _():
            o_ref[...] = acc_ref[...].astype(o_ref.dtype)

    a_spec = (pl.BlockSpec((tk, tm), lambda i, j, k: (k, i)) if ta
              else pl.BlockSpec((tm, tk), lambda i, j, k: (i, k)))
    b_spec = (pl.BlockSpec((tn, tk), lambda i, j, k: (j, k)) if tb
              else pl.BlockSpec((tk, tn), lambda i, j, k: (k, j)))
    (out,), received = _call(
        body, grid=(M // tm, N // tn, nk), in_specs=[a_spec, b_spec],
        out_specs=[pl.BlockSpec((tm, tn), lambda i, j, k: (i, j))],
        out_shape=[jax.ShapeDtypeStruct((M, N), out_dtype)],
        scratch_shapes=[pltpu.VMEM((tm, tn), F32)], args=(a, b), name=name, comm=comm)
    return out if comm is None else (out, received)


class V(NamedTuple):
    arr: Any
    width: int
    base: Any


def _v(arr, width=None, col0=0):
    width = arr.shape[-1] if width is None else width
    assert col0 % width == 0
    return V(arr, width, col0 // width)


def _row_specs(v, tr, nrow8, halo):
    main = pl.BlockSpec((tr, v.width), lambda j, i, b=v.base: (i, b + j))
    if not halo:
        return [main]
    per = tr // HALO
    prev = pl.BlockSpec((HALO, v.width), lambda j, i, b=v.base: (jnp.maximum(i * per - 1, 0), b + j))
    nxt = pl.BlockSpec((HALO, v.width), lambda j, i, b=v.base: (jnp.minimum((i + 1) * per, nrow8 - 1), b + j))
    return [prev, main, nxt]


def _par_spec(p):
    if p.base is None:
        return pl.BlockSpec(p.arr.shape, lambda j, i: (0,) * p.arr.ndim)
    return pl.BlockSpec((p.arr.shape[0], p.width), lambda j, i, b=p.base: (0, b + j))


def _load_rows(refs, i, nrow, halo):
    if not halo:
        return refs[0][...].astype(F32)
    prev, main, nxt = refs
    pv = jnp.where(i > 0, prev[...].astype(F32), 0.0)
    nv = jnp.where(i < nrow - 1, nxt[...].astype(F32), 0.0)
    return jnp.concatenate([pv, main[...].astype(F32), nv], axis=0)


def _center_mask(tr):
    row = lax.broadcasted_iota(jnp.int32, (tr + 2 * HALO, 1), 0)
    return ((row >= HALO) & (row < HALO + tr)).astype(F32)


def _rw_fwd(f, rows, params, outs, *, tr, ncol=1, halo=False, name):
    S = rows[0].arr.shape[0]
    tr = min(tr, S)
    nrow = S // tr
    per = 3 if halo else 1
    nr, npar = len(rows), len(params)

    def body(*refs):
        i = pl.program_id(1)
        vals = [_load_rows(refs[per * r: per * (r + 1)], i, nrow, halo) for r in range(nr)]
        pvals = [refs[per * nr + p][...].astype(F32) for p in range(npar)]
        out_refs = refs[per * nr + npar:]
        res = f(*vals, *pvals, _center_mask(tr)) if halo else f(*vals, *pvals)
        for r, o in zip(res, out_refs):
            o[...] = (r[HALO:HALO + tr] if halo else r).astype(o.dtype)

    in_specs = []
    for v in rows:
        in_specs += _row_specs(v, tr, S // HALO, halo)
    in_specs += [_par_spec(p) for p in params]
    args = []
    for v in rows:
        args += [v.arr] * per
    args += [p.arr for p in params]
    res = pl.pallas_call(
        body, grid=(ncol, nrow), in_specs=in_specs,
        out_specs=[pl.BlockSpec((tr, tot // ncol), lambda j, i: (i, j)) for tot, _ in outs],
        out_shape=[jax.ShapeDtypeStruct((S, tot), dt) for tot, dt in outs],
        compiler_params=_params(("parallel", "arbitrary")), name=name)(*args)
    return res


def _rw_bwd(f, rows, params, douts, *, row_grads, par_grads, adds=None, tr, ncol=1, halo=False, name):
    S = rows[0].arr.shape[0]
    tr = min(tr, S)
    nrow = S // tr
    per = 3 if halo else 1
    adds = adds or {}
    nr, npar, nd = len(rows), len(params), len(douts)
    add_keys = sorted(adds)
    rg_idx = [k for k in range(nr) if row_grads[k]]
    pg_idx = [k for k in range(npar) if par_grads[k]]

    def body(*refs):
        i = pl.program_id(1)
        pos = 0
        vals = [_load_rows(refs[per * r: per * (r + 1)], i, nrow, halo) for r in range(nr)]
        pos = per * nr
        pvals = [refs[pos + p][...].astype(F32) for p in range(npar)]
        pos += npar
        dvals = [_load_rows(refs[pos + per * d: pos + per * (d + 1)], i, nrow, halo) for d in range(nd)]
        pos += per * nd
        add_vals = {k: refs[pos + n][...].astype(F32) for n, k in enumerate(add_keys)}
        pos += len(add_keys)
        rg_refs = refs[pos: pos + len(rg_idx)]
        pg_refs = refs[pos + len(rg_idx):]
        if halo:
            center = _center_mask(tr)
            fun = lambda *a: f(*a, center)
        else:
            fun = f
        _, vjp = jax.vjp(fun, *vals, *pvals)
        grads = vjp(tuple(dvals))
        for ref, k in zip(rg_refs, rg_idx):
            g = grads[k]
            g = g[HALO:HALO + tr] if halo else g
            if k in add_vals:
                g = g + add_vals[k]
            ref[...] = g.astype(ref.dtype)
        for ref, k in zip(pg_refs, pg_idx):
            @pl.when(i == 0)
            def _(ref=ref):
                ref[...] = jnp.zeros_like(ref)
            ref[...] += grads[nr + k]

    in_specs, args = [], []
    for v in list(rows):
        in_specs += _row_specs(v, tr, S // HALO, halo)
        args += [v.arr] * per
    in_specs += [_par_spec(p) for p in params]
    args += [p.arr for p in params]
    for v in douts:
        in_specs += _row_specs(v, tr, S // HALO, halo)
        args += [v.arr] * per
    for k in add_keys:
        in_specs += _row_specs(_v(adds[k], rows[k].width), tr, S // HALO, False)
        args += [adds[k]]
    out_specs, out_shape = [], []
    for k in rg_idx:
        w = rows[k].width
        out_specs.append(pl.BlockSpec((tr, w), lambda j, i: (i, j)))
        out_shape.append(jax.ShapeDtypeStruct((S, ncol * w), F32))
    for k in pg_idx:
        p = params[k]
        blk = p.arr.shape if p.base is None else (p.arr.shape[0], p.width)
        out_specs.append(pl.BlockSpec((None,) + tuple(blk), lambda j, i, n=len(blk): (j,) + (0,) * n))
        out_shape.append(jax.ShapeDtypeStruct((ncol,) + tuple(blk), F32))
    res = pl.pallas_call(
        body, grid=(ncol, nrow), in_specs=in_specs, out_specs=out_specs, out_shape=out_shape,
        compiler_params=_params(("parallel", "arbitrary")), name=name)(*args)
    rg = [None] * nr
    for n, k in enumerate(rg_idx):
        rg[k] = res[n]
    pg = [None] * npar
    for n, k in enumerate(pg_idx):
        pg[k] = res[len(rg_idx) + n]
    return rg, pg


def _sigmoid(x):
    return 1.0 / (1.0 + jnp.exp(-x))


def _rms(x, g):
    return x * lax.rsqrt(jnp.mean(x * x, axis=-1, keepdims=True) + EPS) * g


def _f_normmod(x, g, sc, sh):
    return (_rms(x, g) * (1.0 + sc) + sh,)


def _f_resid(x, y, gate, g):
    return (x + gate * _rms(y, g),)


def _f_merge(ga, gb, ya, yb):
    return (_sigmoid(ga) * ya + _sigmoid(gb) * yb,)


def _f_glapost(o, r, g):
    return (_rms(o, g) * (r * _sigmoid(r)),)


def _roll_rows(x, k):
    return pltpu.roll(x, k % x.shape[0], 0)


@functools.partial(jax.custom_vjp, nondiff_argnums=(1,))
def _shift(x, k):
    return _roll_rows(x, k)


def _shift_fwd(x, k):
    return _roll_rows(x, k), None


def _shift_bwd(k, _, g):
    return (_roll_rows(g, -k),)


_shift.defvjp(_shift_fwd, _shift_bwd)


def _taps(w, center):
    return center * w + (1.0 - center) * lax.stop_gradient(w)


def _conv3(u, w0, w1, w2, center):
    return (_taps(w0, center) * _shift(u, 2) + _taps(w1, center) * _shift(u, 1)) + _taps(w2, center) * u


def _f_convmix(cb, cc, cx, w0, w1, w2, center):
    return (cb * _conv3(cc * cx, w0, w1, w2, center),)


def _gelu_tanh(x):
    return 0.5 * x * (1.0 + jnp.tanh(math.sqrt(2.0 / math.pi) * (x + 0.044715 * (x * x * x))))


def _f_ffn(gate, up, w0, w1, w2, center):
    return (_gelu_tanh(_conv3(gate, w0, w1, w2, center)) * up,)


@jax.custom_vjp
def _bdot(a, b):
    return jnp.dot(a.astype(BF16), b.astype(BF16), preferred_element_type=F32)


def _bdot_fwd(a, b):
    return _bdot(a, b), (a, b)


def _bdot_bwd(res, g):
    a, b = res
    gb = g.astype(BF16)
    da = lax.dot_general(gb, b.astype(BF16), (((1,), (1,)), ((), ())), preferred_element_type=F32)
    db = lax.dot_general(a.astype(BF16), gb, (((0,), (0,)), ((), ())), preferred_element_type=F32)
    return da, db


_bdot.defvjp(_bdot_fwd, _bdot_bwd)


@jax.custom_vjp
def _log_sigmoid(z):
    return jnp.minimum(z, 0.0) - jnp.log(1.0 + jnp.exp(-jnp.abs(z)))


def _log_sigmoid_fwd(z):
    return _log_sigmoid(z), z


def _log_sigmoid_bwd(z, g):
    return (g * _sigmoid(-z),)


_log_sigmoid.defvjp(_log_sigmoid_fwd, _log_sigmoid_bwd)


def _f_loga(lr, w, b):
    return (_log_sigmoid(_bdot(lr, w) + b) / GLA_TAU,)


def _split3(x):
    hi = x.astype(BF16)
    r1 = x - hi.astype(F32)
    mid = r1.astype(BF16)
    lo = (r1 - mid.astype(F32)).astype(BF16)
    return hi, mid, lo


def _dot_exact(a01, x, dims):
    hi, mid, lo = _split3(x)
    d = lambda p: lax.dot_general(a01, p, dims, preferred_element_type=F32)
    return (d(lo) + d(mid)) + d(hi)


def _dot_exact_lhs(x, a01, dims):
    hi, mid, lo = _split3(x)
    d = lambda p: lax.dot_general(p, a01, dims, preferred_element_type=F32)
    return (d(lo) + d(mid)) + d(hi)


NN = (((1,), (0,)), ((), ()))
NT = (((1,), (1,)), ((), ()))
TN = (((0,), (0,)), ((), ()))


def _bf_dot(a, b, dims):
    return lax.dot_general(a.astype(BF16), b.astype(BF16), dims, preferred_element_type=F32)


def _chunk_decay(a, dv):
    L = a.shape[0]
    r = lax.broadcasted_iota(jnp.int32, (L, L), 0)
    c = lax.broadcasted_iota(jnp.int32, (L, L), 1)
    tri = (r >= c).astype(BF16)
    cum = _dot_exact(tri, a, NN)
    ce = cum[L - 1:L, :]
    dec = jnp.exp(_dot_exact_lhs(a, jnp.ones((L, dv), BF16), TN))
    return cum, ce, jnp.exp(ce - cum), dec


def _gla_fwd(P, a, *, qk, vw, name):
    S = P.shape[0]
    nc = S // CHUNK
    dk, dv = qk // HEADS, vw // HEADS
    scale = dk ** -0.5

    def body(q_ref, k_ref, v_ref, a_ref, o_ref, st_ref, state):
        @pl.when(pl.program_id(0) == 0)
        def _():
            state[...] = jnp.zeros_like(state)

        for h in range(HEADS):
            ks, vs = slice(h * dk, (h + 1) * dk), slice(h * dv, (h + 1) * dv)
            _, _, E, dec = _chunk_decay(a_ref[:, ks], dv)
            kd = k_ref[:, ks] * E
            s_new = dec * state[h] + _bf_dot(kd, v_ref[:, vs], TN)
            state[h] = s_new
            st_ref[0, h] = s_new
            o_ref[:, vs] = _bf_dot(q_ref[:, ks] * scale, s_new, NN)

    return pl.pallas_call(
        body, grid=(nc,),
        in_specs=[pl.BlockSpec((CHUNK, qk), lambda i: (i, 0)),
                  pl.BlockSpec((CHUNK, qk), lambda i: (i, 1)),
                  pl.BlockSpec((CHUNK, vw), lambda i: (i, 2 * qk // vw)),
                  pl.BlockSpec((CHUNK, qk), lambda i: (i, 0))],
        out_specs=[pl.BlockSpec((CHUNK, vw), lambda i: (i, 0)),
                   pl.BlockSpec((1, HEADS, dk, dv), lambda i: (i, 0, 0, 0))],
        out_shape=[jax.ShapeDtypeStruct((S, vw), F32), jax.ShapeDtypeStruct((nc, HEADS, dk, dv), F32)],
        scratch_shapes=[pltpu.VMEM((HEADS, dk, dv), F32)],
        compiler_params=_params(("arbitrary",)), name=name)(P, P, P, a)


def _gla_bwd(P, a, states, do, *, qk, vw, name):
    S = P.shape[0]
    nc = S // CHUNK
    dk, dv = qk // HEADS, vw // HEADS
    scale = dk ** -0.5

    def body(q_ref, k_ref, v_ref, a_ref, do_ref, sc_ref, sp_ref, dq_ref, dk_ref, dv_ref, da_ref, ds_carry):
        i = pl.program_id(0)

        @pl.when(i == 0)
        def _():
            ds_carry[...] = jnp.zeros_like(ds_carry)

        has_prev = (i < nc - 1).astype(F32)
        L = CHUNK
        r = lax.broadcasted_iota(jnp.int32, (L, L), 0)
        c = lax.broadcasted_iota(jnp.int32, (L, L), 1)
        tri_t = (c >= r).astype(BF16)
        last = (lax.broadcasted_iota(jnp.int32, (L, 1), 0) == L - 1).astype(F32)
        for h in range(HEADS):
            ks, vs = slice(h * dk, (h + 1) * dk), slice(h * dv, (h + 1) * dv)
            _, _, E, dec = _chunk_decay(a_ref[:, ks], dv)
            kk, vv, dov = k_ref[:, ks], v_ref[:, vs], do_ref[:, vs]
            kd = kk * E
            ds = ds_carry[h] + _bf_dot(q_ref[:, ks] * scale, dov, TN)
            dq_ref[:, ks] = _bf_dot(dov, sc_ref[0, h], NT) * scale
            dkd = _bf_dot(vv, ds, NT)
            dv_ref[:, vs] = _bf_dot(kd, ds, NN)
            dk_ref[:, ks] = dkd * E
            darg = dkd * kd
            w = ds * (sp_ref[0, h] * has_prev) * dec
            dce = _dot_exact(jnp.ones((SUBLANES, dv), BF16), w, NT)[0:1, :]
            dce = dce + jnp.sum(darg, axis=0, keepdims=True)
            dcum = last * dce - darg
            da_ref[:, ks] = _dot_exact(tri_t, dcum, NN)
            ds_carry[h] = dec * ds

    rev = lambda i: nc - 1 - i
    return pl.pallas_call(
        body, grid=(nc,),
        in_specs=[pl.BlockSpec((CHUNK, qk), lambda i: (rev(i), 0)),
                  pl.BlockSpec((CHUNK, qk), lambda i: (rev(i), 1)),
                  pl.BlockSpec((CHUNK, vw), lambda i: (rev(i), 2 * qk // vw)),
                  pl.BlockSpec((CHUNK, qk), lambda i: (rev(i), 0)),
                  pl.BlockSpec((CHUNK, vw), lambda i: (rev(i), 0)),
                  pl.BlockSpec((1, HEADS, dk, dv), lambda i: (rev(i), 0, 0, 0)),
                  pl.BlockSpec((1, HEADS, dk, dv), lambda i: (jnp.maximum(rev(i) - 1, 0), 0, 0, 0))],
        out_specs=[pl.BlockSpec((CHUNK, qk), lambda i: (rev(i), 0)),
                   pl.BlockSpec((CHUNK, qk), lambda i: (rev(i), 0)),
                   pl.BlockSpec((CHUNK, vw), lambda i: (rev(i), 0)),
                   pl.BlockSpec((CHUNK, qk), lambda i: (rev(i), 0))],
        out_shape=[jax.ShapeDtypeStruct((S, qk), F32), jax.ShapeDtypeStruct((S, qk), F32),
                   jax.ShapeDtypeStruct((S, vw), F32), jax.ShapeDtypeStruct((S, qk), F32)],
        scratch_shapes=[pltpu.VMEM((HEADS, dk, dv), F32)],
        compiler_params=_params(("arbitrary",)), name=name)(P, P, P, a, do, states, states)


def _loss_head(y, target, *, tr, name):
    S, D = y.shape
    tr = min(tr, S)

    def body(y_ref, t_ref, l_ref, dy_ref):
        @pl.when(pl.program_id(0) == 0)
        def _():
            l_ref[...] = jnp.zeros_like(l_ref)

        e = y_ref[...] - t_ref[...]
        dy_ref[...] = e / D
        l_ref[...] += 0.5 * jnp.sum(jnp.mean(e * e, axis=-1, keepdims=True), axis=0, keepdims=True)

    loss, dy = pl.pallas_call(
        body, grid=(S // tr,),
        in_specs=[pl.BlockSpec((tr, D), lambda i: (i, 0))] * 2,
        out_specs=[pl.BlockSpec((SUBLANES, LANES), lambda i: (0, 0)), pl.BlockSpec((tr, D), lambda i: (i, 0))],
        out_shape=[jax.ShapeDtypeStruct((SUBLANES, LANES), F32), jax.ShapeDtypeStruct((S, D), F32)],
        compiler_params=_params(("arbitrary",)), name=name)(y, target)
    return loss[0, 0], dy


def _adamw(parts, w, m, v, *, name, comm=None):
    G, R, C = w.shape
    n = parts[0].shape[0]
    unit = 2 * SUBLANES if parts[0].dtype == BF16 else SUBLANES
    tr = _pick(R, max(unit, (128 * 1024) // C), unit)
    nblk = R // tr
    bc1 = 1.0 - ADAM_B1 ** ADAM_STEP
    bc2 = 1.0 - ADAM_B2 ** ADAM_STEP

    def body(*refs):
        p_refs = refs[:G]
        w_ref, m_ref, v_ref, g_ref, d_ref, mo_ref, vo_ref = refs[G:]
        for q in range(G):
            @pl.when(pl.program_id(0) == q)
            def _(q=q):
                g = p_refs[q][0].astype(F32)
                for s in range(1, n):
                    g = g + p_refs[q][s].astype(F32)
                m2 = ADAM_B1 * m_ref[...] + (1.0 - ADAM_B1) * g
                v2 = ADAM_B2 * v_ref[...] + (1.0 - ADAM_B2) * (g * g)
                g_ref[...] = g
                mo_ref[...] = m2
                vo_ref[...] = v2
                d_ref[...] = -ADAM_LR * ((m2 / bc1) / (jnp.sqrt(v2 / bc2) + ADAM_EPS) + ADAM_WD * w_ref[...])

    def part_spec(q):
        return pl.BlockSpec((n, tr, C), lambda g, i: (0, jnp.where(g == q, i, jnp.where(g < q, 0, nblk - 1)), 0))

    blk = pl.BlockSpec((None, tr, C), lambda g, i: (g, i, 0))
    outs, received = _call(
        body, grid=(G, nblk), in_specs=[part_spec(q) for q in range(G)] + [blk, blk, blk],
        out_specs=[blk] * 4, out_shape=[jax.ShapeDtypeStruct((G, R, C), F32)] * 4,
        args=(*parts, w, m, v), name=name, comm=comm)
    return outs if comm is None else (outs, received)


def _adamw_nd(parts, w, m, v, *, name):
    shape = w.shape
    C = shape[-1]
    R = math.prod(shape[:-1])
    outs = _adamw([parts.reshape(parts.shape[0], R, C)], w.reshape(1, R, C), m.reshape(1, R, C), v.reshape(1, R, C),
                  name=name)
    return [o.reshape(shape) for o in outs]


def _all_gather(arrs, *, name):
    n = len(arrs)
    HBM = pl.BlockSpec(memory_space=pl.ANY)

    def body(*refs):
        ins, outs = refs[:n], refs[n:2 * n]
        send_sems, recv_sems, local_sems = refs[2 * n:]
        x, y, c = _position()
        me, sibling = (x, y, c), (x, y, 1 - c)
        chips = [(1 - x, y), (x, 1 - y), (1 - x, 1 - y)]

        def copy(t, k, block, to, src=None):
            rows = outs[t].at[_index(block)]
            return pltpu.make_async_remote_copy(
                src_ref=rows if src is None else src, dst_ref=rows,
                send_sem=send_sems.at[t, k], recv_sem=recv_sems.at[t, k],
                device_id=to, device_id_type=pl.DeviceIdType.MESH)

        started = []
        mine = []
        for t in range(n):
            cp = pltpu.make_async_copy(ins[t], outs[t].at[_index(me)], local_sems.at[t])
            cp.start()
            mine.append(cp)
            first = [copy(t, 0, me, sibling, src=ins[t])]
            first += [copy(t, 1 + j, me, (*chip, c), src=ins[t]) for j, chip in enumerate(chips)]
            for cp in first:
                cp.start()
            started += first
        for t in range(n):
            for j, chip in enumerate(chips):
                copy(t, 1 + j, (*chip, c), me).wait_recv()
                cp = copy(t, 4 + j, (*chip, c), sibling)
                cp.start()
                started.append(cp)
        for t in range(n):
            copy(t, 0, sibling, me).wait_recv()
            for j, chip in enumerate(chips):
                copy(t, 4 + j, (*chip, 1 - c), me).wait_recv()
        for cp in started:
            cp.wait_send()
        for cp in mine:
            cp.wait()

    return pl.pallas_call(
        body, in_specs=[HBM] * n, out_specs=[HBM] * n,
        out_shape=[jax.ShapeDtypeStruct((NDEV,) + a.shape, a.dtype) for a in arrs],
        scratch_shapes=[pltpu.SemaphoreType.DMA((n, 7)), pltpu.SemaphoreType.DMA((n, 7)),
                        pltpu.SemaphoreType.DMA((n,))],
        name=name)(*arrs)


class _Pack:
    def __init__(self, arrs):
        self.shapes = [a.shape for a in arrs]
        self.sizes = [math.prod(s) for s in self.shapes]
        total = sum(self.sizes)
        unit = SUBLANES * LANES
        self.padded = -(-total // unit) * unit
        flat = jnp.concatenate([a.reshape(-1).astype(F32) for a in arrs] + [jnp.zeros((self.padded - total,), F32)])
        self.packed = flat.reshape(self.padded // LANES, LANES)

    def unpack(self, gathered):
        flat = gathered.reshape(NDEV, self.padded)
        out, off = [], 0
        for shape, size in zip(self.shapes, self.sizes):
            out.append(flat[:, off:off + size].reshape((NDEV,) + tuple(shape)))
            off += size
        return out


def _cols_from_shards(g):
    g = jnp.moveaxis(g, 0, -2)
    return g.reshape(g.shape[:-2] + (g.shape[-2] * g.shape[-1],))


def _cols_to_shards(w):
    K, N = w.shape
    return jnp.moveaxis(w.reshape(K, NDEV, N // NDEV), 1, 0)


def kernel(x, c, w_ada, b_ada, norm_g, w_in, w_a2, b_a2, gla_norm_g, w_out_gla, conv_mix_w, w_out_conv, w_o, w_up, ffn_conv_w, w_down, loss_target, m_w_ada, m_b_ada, m_norm_g, m_w_in, m_w_a2, m_b_a2, m_gla_norm_g, m_w_out_gla, m_conv_mix_w, m_w_out_conv, m_w_o, m_w_up, m_ffn_conv_w, m_w_down, v_w_ada, v_b_ada, v_norm_g, v_w_in, v_w_a2, v_b_a2, v_gla_norm_g, v_w_out_gla, v_conv_mix_w, v_w_out_conv, v_w_o, v_w_up, v_ffn_conv_w, v_w_down):
    depth = w_ada.shape[0]
    S, D = x.shape[1], x.shape[2]
    QK = w_a2.shape[2] * NDEV
    LR = w_a2.shape[1]
    VW = w_out_gla.shape[1]
    CW = w_out_conv.shape[1]
    FF = w_down.shape[1] * NDEV
    NIN = w_in.shape[2] * NDEV
    NADA = w_ada.shape[2]
    LRP = -(-LR // LANES) * LANES
    off_lr = 2 * QK + 2 * VW
    NP = NIN - LR + LRP
    assert NIN == 2 * QK + 2 * VW + LR + 3 * CW + 2 * D
    xi, yi, ci = _position()
    me = _index((xi, yi, ci))
    xs = x.reshape(S, D)
    target = loss_target.reshape(S, D)
    TR = 256
    TRC = 512
    TC = _pick(CW, 512)
    TCF = _pick(FF, 512)

    small = _Pack([c, norm_g, w_a2, conv_mix_w, ffn_conv_w])
    big_names = ["w_in", "w_out_gla", "w_out_conv", "w_o", "w_up", "w_down"]
    big = dict(w_in=w_in, w_out_gla=w_out_gla, w_out_conv=w_out_conv, w_o=w_o, w_up=w_up, w_down=w_down)

    def assemble(g):
        W_in = _cols_from_shards(g["w_in"])
        W_all = jnp.concatenate([W_in[:, :off_lr], W_in[:, off_lr + LR:], W_in[:, off_lr:off_lr + LR],
                                 jnp.zeros((D, LRP - LR), BF16)], axis=-1)
        return dict(w_in=W_all, w_out_gla=_cols_from_shards(g["w_out_gla"]), w_out_conv=_cols_from_shards(g["w_out_conv"]),
                    w_o=g["w_o"].reshape(D, D), w_up=_cols_from_shards(g["w_up"]), w_down=g["w_down"].reshape(FF, D))

    gathered = _all_gather([small.packed] + [big[k][0].astype(BF16) for k in big_names], name="gather_weights0")
    c_all, norm_g_s, w_a2_s, conv_w_s, ffn_w_s = small.unpack(gathered[0])
    c_all = c_all.reshape(NDEV, D)
    norm_g_f = _cols_from_shards(norm_g_s)
    w_a2_f = _cols_from_shards(w_a2_s)
    conv_w_f = _cols_from_shards(conv_w_s)
    ffn_w_f = _cols_from_shards(ffn_w_s)
    W = [None] * depth
    W[0] = assemble(dict(zip(big_names, gathered[1:])))
    w_a2_p = jnp.concatenate([w_a2_f, jnp.zeros((depth, LRP - LR, QK), F32)], axis=1)

    MP = LANES
    c_pad = jnp.concatenate([c_all, jnp.zeros((MP - NDEV, D), F32)], axis=0)
    (silu_c,) = _rw_fwd(lambda t: (t * _sigmoid(t),), [_v(c_pad)], [], [(D, BF16)], tr=MP, name="silu_c")
    mod_cols = [_mm(silu_c, w_ada[l], name=f"ada_fwd{l}")[:NDEV] for l in range(depth)]
    (mod_g,) = _all_gather([jnp.stack(mod_cols)], name="gather_mod")
    mod_all = jnp.moveaxis(mod_g, 0, 2).reshape(depth, NDEV, NDEV * NADA)
    mod_me = lax.dynamic_slice_in_dim(mod_all, me, 1, axis=1).reshape(depth, NDEV * NADA)
    (mod_me,) = _rw_fwd(lambda a, b: (a + b,), [_v(mod_me), _v(b_ada)], [], [(NDEV * NADA, F32)],
                        tr=depth, name="mod_bias")

    col = {}
    off = 0
    for nm, wd in (("q", QK), ("k", QK), ("v", VW), ("r", VW), ("cb", CW), ("cc", CW), ("cx", CW),
                   ("ga", D), ("gb", D), ("lr", LRP)):
        col[nm] = off
        off += wd

    saved = []
    xin = xs
    for l in range(depth):
        sh1, sc1, g1, sh2, sc2, g2 = [mod_me[l:l + 1, i * D:(i + 1) * D] for i in range(6)]
        ng = [norm_g_f[l, i:i + 1] for i in range(4)]
        cw = [conv_w_f[l, i:i + 1] for i in range(CONV_K)]
        fw = [ffn_w_f[l, i:i + 1] for i in range(CONV_K)]
        gn = gla_norm_g[l:l + 1]
        ba2 = b_a2[l:l + 1]
        nxt = {}

        def proj(act, key, name):
            if l + 1 == depth:
                return _mm(act, W[l][key], name=name)
            out, (got,) = _mm(act, W[l][key], name=name, comm=_Exchange([big[key][l + 1].astype(BF16)], gather=True))
            nxt[key] = got
            return out

        (h1,) = _rw_fwd(_f_normmod, [_v(xin)], [_v(ng[0]), _v(sc1), _v(sh1)], [(D, BF16)], tr=TR, name=f"normmod1_{l}")
        P = proj(h1, "w_in", f"in_proj{l}")
        (a,) = _rw_fwd(_f_loga, [_v(P, LRP, col["lr"])], [V(w_a2_p[l], QK, None), _v(ba2)], [(QK, F32)],
                       tr=TR, name=f"loga{l}")
        o, states = _gla_fwd(P, a, qk=QK, vw=VW, name=f"gla_fwd{l}")
        dvh = VW // HEADS
        (ya_in,) = _rw_fwd(_f_glapost, [_v(o, dvh), _v(P, dvh, col["r"])], [V(gn, dvh, None)], [(VW, BF16)],
                           tr=TR, ncol=HEADS, name=f"glapost{l}")
        ya = proj(ya_in, "w_out_gla", f"out_gla{l}")
        (yb_in,) = _rw_fwd(_f_convmix, [_v(P, TC, col["cb"]), _v(P, TC, col["cc"]), _v(P, TC, col["cx"])],
                           [_v(w, TC) for w in cw], [(CW, BF16)], tr=TRC, ncol=CW // TC, halo=True, name=f"convmix{l}")
        yb = proj(yb_in, "w_out_conv", f"out_conv{l}")
        (mg,) = _rw_fwd(_f_merge, [_v(P, D, col["ga"]), _v(P, D, col["gb"]), _v(ya), _v(yb)], [], [(D, BF16)],
                        tr=TR, name=f"merge{l}")
        y = proj(mg, "w_o", f"o_proj{l}")
        (x1,) = _rw_fwd(_f_resid, [_v(xin), _v(y)], [_v(g1), _v(ng[1])], [(D, F32)], tr=TR, name=f"resid1_{l}")
        (h2,) = _rw_fwd(_f_normmod, [_v(x1)], [_v(ng[2]), _v(sc2), _v(sh2)], [(D, BF16)], tr=TR, name=f"normmod2_{l}")
        U = proj(h2, "w_up", f"up_proj{l}")
        (ff,) = _rw_fwd(_f_ffn, [_v(U, TCF, 0), _v(U, TCF, FF)], [_v(w, TCF) for w in fw], [(FF, BF16)],
                        tr=TRC, ncol=FF // TCF, halo=True, name=f"ffn{l}")
        y2 = proj(ff, "w_down", f"down_proj{l}")
        (x2,) = _rw_fwd(_f_resid, [_v(x1), _v(y2)], [_v(g2), _v(ng[3])], [(D, F32)], tr=TR, name=f"resid2_{l}")
        if l + 1 < depth:
            W[l + 1] = assemble(nxt)
        saved.append(dict(xin=xin, h1=h1, P=P, a=a, o=o, states=states, ya_in=ya_in, ya=ya, yb_in=yb_in, yb=yb,
                          mg=mg, y=y, x1=x1, h2=h2, U=U, ff=ff, y2=y2, mods=(sh1, sc1, g1, sh2, sc2, g2),
                          ng=ng, cw=cw, fw=fw, gn=gn, ba2=ba2))
        xin = x2

    loss_local, dx = _loss_head(xin, target, tr=TR, name="loss_head")
    loss = lax.psum(loss_local, MESH_AXES)

    pending = []
    received = {}

    def riding():
        if not pending:
            return None, []
        keys = [k for k, _ in pending]
        comm = _Exchange([a for _, a in pending], gather=False)
        pending.clear()
        return comm, keys

    def bmm(a, b, **kw):
        comm, keys = riding()
        if comm is None:
            return _mm(a, b, **kw)
        out, got = _mm(a, b, comm=comm, **kw)
        received.update(zip(keys, got))
        return out

    gsmall = [None] * depth
    dmod = [None] * depth
    for l in reversed(range(depth)):
        s = saved[l]
        sh1, sc1, g1, sh2, sc2, g2 = s["mods"]
        ng, cw, fw = s["ng"], s["cw"], s["fw"]
        (dx1_a, dy2), (dg2, dng3) = _rw_bwd(
            _f_resid, [_v(s["x1"]), _v(s["y2"])], [_v(g2), _v(ng[3])], [_v(dx)],
            row_grads=[True, True], par_grads=[True, True], tr=TR, name=f"resid2_bwd{l}")
        gW_down = bmm(s["ff"], dy2, ta=True, out_dtype=BF16, name=f"down_wgrad{l}")
        pending.append(((l, "w_down"), gW_down.reshape(NDEV, FF // NDEV, D)))
        dff = bmm(dy2, W[l]["w_down"], tb=True, name=f"down_dgrad{l}")
        U = s["U"]
        (dgate, dup), dfw = _rw_bwd(
            _f_ffn, [_v(U, TCF, 0), _v(U, TCF, FF)], [_v(w, TCF) for w in fw], [_v(dff, TCF)],
            row_grads=[True, True], par_grads=[True] * 3, tr=TRC, ncol=FF // TCF, halo=True, name=f"ffn_bwd{l}")
        dU = jnp.concatenate([dgate, dup], axis=1)
        gW_up = bmm(s["h2"], dU, ta=True, out_dtype=BF16, name=f"up_wgrad{l}")
        pending.append(((l, "w_up"), _cols_to_shards(gW_up)))
        dh2 = bmm(dU, W[l]["w_up"], tb=True, name=f"up_dgrad{l}")
        (dx1,), (dng2, dsc2, dsh2) = _rw_bwd(
            _f_normmod, [_v(s["x1"])], [_v(ng[2]), _v(sc2), _v(sh2)], [_v(dh2)],
            row_grads=[True], par_grads=[True] * 3, adds={0: dx1_a}, tr=TR, name=f"normmod2_bwd{l}")
        (dxin_a, dy), (dg1, dng1) = _rw_bwd(
            _f_resid, [_v(s["xin"]), _v(s["y"])], [_v(g1), _v(ng[1])], [_v(dx1)],
            row_grads=[True, True], par_grads=[True, True], tr=TR, name=f"resid1_bwd{l}")
        gW_o = bmm(s["mg"], dy, ta=True, out_dtype=BF16, name=f"o_wgrad{l}")
        pending.append(((l, "w_o"), gW_o.reshape(NDEV, D // NDEV, D)))
        dmg = bmm(dy, W[l]["w_o"], tb=True, name=f"o_dgrad{l}")
        P = s["P"]
        (dga, dgb, dya, dyb), _ = _rw_bwd(
            _f_merge, [_v(P, D, col["ga"]), _v(P, D, col["gb"]), _v(s["ya"]), _v(s["yb"])], [], [_v(dmg)],
            row_grads=[True] * 4, par_grads=[], tr=TR, name=f"merge_bwd{l}")
        gW_og = bmm(s["ya_in"], dya, ta=True, out_dtype=BF16, name=f"out_gla_wgrad{l}")
        pending.append(((l, "w_out_gla"), _cols_to_shards(gW_og)))
        dya_in = bmm(dya, W[l]["w_out_gla"], tb=True, name=f"out_gla_dgrad{l}")
        gW_oc = bmm(s["yb_in"], dyb, ta=True, out_dtype=BF16, name=f"out_conv_wgrad{l}")
        pending.append(((l, "w_out_conv"), _cols_to_shards(gW_oc)))
        dyb_in = bmm(dyb, W[l]["w_out_conv"], tb=True, name=f"out_conv_dgrad{l}")
        (dcb, dcc, dcx), dcw = _rw_bwd(
            _f_convmix, [_v(P, TC, col["cb"]), _v(P, TC, col["cc"]), _v(P, TC, col["cx"])],
            [_v(w, TC) for w in cw], [_v(dyb_in, TC)],
            row_grads=[True] * 3, par_grads=[True] * 3, tr=TRC, ncol=CW // TC, halo=True, name=f"convmix_bwd{l}")
        dvh = VW // HEADS
        (do, dr), (dgn,) = _rw_bwd(
            _f_glapost, [_v(s["o"], dvh), _v(P, dvh, col["r"])], [V(s["gn"], dvh, None)], [_v(dya_in, dvh)],
            row_grads=[True, True], par_grads=[True], tr=TR, ncol=HEADS, name=f"glapost_bwd{l}")
        dq, dk, dv, da = _gla_bwd(P, s["a"], s["states"], do, qk=QK, vw=VW, name=f"gla_bwd{l}")
        (dlr,), (dwa2, dba2) = _rw_bwd(
            _f_loga, [_v(P, LRP, col["lr"])], [V(w_a2_p[l], QK, None), _v(s["ba2"])], [_v(da)],
            row_grads=[True], par_grads=[True, True], tr=TR, name=f"loga_bwd{l}")
        dP = jnp.concatenate([dq, dk, dv, dr, dcb, dcc, dcx, dga, dgb, dlr], axis=1)
        gW_all = bmm(s["h1"], dP, ta=True, out_dtype=BF16, name=f"in_wgrad{l}")
        gW_in = jnp.concatenate([gW_all[:, :off_lr], gW_all[:, NIN - LR:NIN], gW_all[:, off_lr:NIN - LR]], axis=1)
        pending.append(((l, "w_in"), _cols_to_shards(gW_in)))
        dh1 = bmm(dP, W[l]["w_in"], tb=True, name=f"in_dgrad{l}")
        (dx,), (dng0, dsc1, dsh1) = _rw_bwd(
            _f_normmod, [_v(s["xin"])], [_v(ng[0]), _v(sc1), _v(sh1)], [_v(dh1)],
            row_grads=[True], par_grads=[True] * 3, adds={0: dxin_a}, tr=TR, name=f"normmod1_bwd{l}")
        vec = lambda t: t.reshape(1, -1)
        gsmall[l] = dict(
            norm_g=jnp.concatenate([vec(dng0), vec(dng1), vec(dng2), vec(dng3)], axis=0),
            w_a2=dwa2[0, :LR], b_a2=dba2.reshape(QK), gla_norm_g=jnp.sum(dgn, axis=0).reshape(dvh),
            conv_mix_w=jnp.concatenate([vec(t) for t in dcw], axis=0),
            ffn_conv_w=jnp.concatenate([vec(t) for t in dfw], axis=0))
        dmod[l] = jnp.concatenate([vec(t) for t in (dsh1, dsc1, dg1, dsh2, dsc2, dg2)], axis=1).reshape(-1)
    grad_x = dx.reshape(x.shape)

    stack = lambda key, src: jnp.stack([src[l][key] for l in range(depth)])
    small_names = ["norm_g", "w_a2", "b_a2", "gla_norm_g", "conv_mix_w", "ffn_conv_w"]
    gpack = _Pack([jnp.stack(dmod)] + [stack(k, gsmall) for k in small_names])
    (gsm,) = _all_gather([gpack.packed], name="gather_small_grads")
    gs = gpack.unpack(gsm)
    dmod_all = gs[0]
    parts_small = dict(zip(small_names, gs[1:]))

    def my_cols(t, n):
        return lax.dynamic_slice_in_dim(t, me * n, n, axis=t.ndim - 1)

    gW_ada = []
    for l in range(depth):
        dcols = my_cols(dmod_all[:, l], NADA)
        dcols = jnp.concatenate([dcols, jnp.zeros((MP - NDEV, NADA), F32)], axis=0)
        gW_ada.append(bmm(silu_c, dcols, ta=True, name=f"ada_wgrad{l}")[None])
    parts = {}
    parts["b_ada"] = dmod_all
    parts["norm_g"] = my_cols(parts_small["norm_g"], D // NDEV)
    parts["w_a2"] = my_cols(parts_small["w_a2"], QK // NDEV)
    parts["b_a2"] = parts_small["b_a2"]
    parts["gla_norm_g"] = parts_small["gla_norm_g"]
    parts["conv_mix_w"] = my_cols(parts_small["conv_mix_w"], CW // NDEV)
    parts["ffn_conv_w"] = my_cols(parts_small["ffn_conv_w"], FF // NDEV)

    weights = dict(w_ada=(w_ada, m_w_ada, v_w_ada), b_ada=(b_ada, m_b_ada, v_b_ada),
                   norm_g=(norm_g, m_norm_g, v_norm_g), w_in=(w_in, m_w_in, v_w_in), w_a2=(w_a2, m_w_a2, v_w_a2),
                   b_a2=(b_a2, m_b_a2, v_b_a2), gla_norm_g=(gla_norm_g, m_gla_norm_g, v_gla_norm_g),
                   w_out_gla=(w_out_gla, m_w_out_gla, v_w_out_gla), conv_mix_w=(conv_mix_w, m_conv_mix_w, v_conv_mix_w),
                   w_out_conv=(w_out_conv, m_w_out_conv, v_w_out_conv), w_o=(w_o, m_w_o, v_w_o),
                   w_up=(w_up, m_w_up, v_w_up), ffn_conv_w=(ffn_conv_w, m_ffn_conv_w, v_ffn_conv_w),
                   w_down=(w_down, m_w_down, v_w_down))
    done = {}
    for nm in ["w_ada", "w_down", "w_up", "w_o", "w_out_gla", "w_out_conv", "w_in"]:
        w, m, v = weights[nm]
        comm, keys = riding()
        mine = gW_ada if nm == "w_ada" else [received[(l, nm)] for l in range(depth)]
        if comm is None:
            done[nm] = _adamw(mine, w, m, v, name=f"adamw_{nm}")
        else:
            done[nm], got = _adamw(mine, w, m, v, name=f"adamw_{nm}", comm=comm)
            received.update(zip(keys, got))
    for nm, (w, m, v) in weights.items():
        if nm not in done:
            done[nm] = _adamw_nd(parts[nm], w, m, v, name=f"adamw_{nm}")
    grads, deltas, new_m, new_v = zip(*[done[nm] for nm in weights])
    return (loss, grad_x, *grads, *deltas, *new_m, *new_v)
```

```python
import functools
import math
from typing import Any, NamedTuple

import jax
import jax.numpy as jnp
from jax import lax
from jax.experimental import pallas as pl
from jax.experimental.pallas import tpu as pltpu

F32 = jnp.float32
BF16 = jnp.bfloat16
MESH_AXES = ("x", "y", "c")
NDEV = 8
EPS = 1e-6
CHUNK = 64
HEADS = 4
GLA_TAU = 16.0
CONV_K = 3
ADAM_LR = 0.001
ADAM_B1 = 0.9
ADAM_B2 = 0.999
ADAM_EPS = 1e-08
ADAM_WD = 0.01
ADAM_STEP = 10

LANES = 128
SUBLANES = 8
HALO = SUBLANES
VMEM_LIMIT = 56 * 1024 * 1024


def _params(sem=None):
    return pltpu.CompilerParams(dimension_semantics=sem, vmem_limit_bytes=VMEM_LIMIT)


def _pick(n, cap, unit=LANES):
    best = None
    for d in range(unit, min(n, cap) + 1, unit):
        if n % d == 0:
            best = d
    return best if best is not None else n


def _position():
    return lax.axis_index("x"), lax.axis_index("y"), lax.axis_index("c")


def _index(p):
    return 4 * p[0] + 2 * p[1] + p[2]


def _peer(k):
    x, y, c = _position()
    return ((1 - x) if k & 4 else x, (1 - y) if k & 2 else y, (1 - c) if k & 1 else c)


ROW_QUANTUM = 2 * SUBLANES
MIN_ROWS = 128


class _Stream:
    def __init__(self, mode, src, rows, cols, rate, after=None):
        self.mode, self.src, self.R, self.rate, self.after = mode, src, rows, rate, after
        self.shape = (NDEV, rows, cols)
        self.buf = None
        self.done = 0


class _Exchange:
    def __init__(self, items):
        self.items = items
        self.ins, self.src_pos, self.alias = [], [], {}
        in_call = {id(s): n for n, (s, _, _) in enumerate(items)}
        for n, (s, _, _) in enumerate(items):
            if s.mode == "gather_b" and id(s.after) in in_call:
                self.src_pos.append(("out", in_call[id(s.after)]))
            else:
                self.src_pos.append(("in", len(self.ins)))
                self.ins.append(s.after.buf if s.mode == "gather_b" else s.src)
            if s.buf is not None:
                self.alias[len(self.ins)] = n
                self.ins.append(s.buf)
        self.out_shape = [jax.ShapeDtypeStruct(s.shape, BF16) for s, _, _ in items]
        n = len(items)
        self.scratch = [pltpu.SemaphoreType.DMA((n, NDEV)), pltpu.SemaphoreType.DMA((n, NDEV)),
                        pltpu.SemaphoreType.DMA((n,))]

    def _copies(self, ins, outs, sems):
        send_sems, recv_sems, local_sems = sems
        me = _index(_position())
        slot = lambda k: _index(_peer(k))
        local, sends, arrivals = [], [], []
        for n, (s, r0, rc) in enumerate(self.items):
            out, rows = outs[n], pl.ds(r0, rc)
            where, pos = self.src_pos[n]
            src = ins[pos] if where == "in" else outs[pos]
            if s.mode == "a2a":
                local.append(pltpu.make_async_copy(src.at[me, rows], out.at[me, rows], local_sems.at[n]))
                moves = [(k, src.at[slot(k), rows], out.at[me, rows], _peer(k), out.at[slot(k), rows])
                         for k in range(1, NDEV)]
            elif s.mode == "gather_a":
                local.append(pltpu.make_async_copy(src.at[rows], out.at[me, rows], local_sems.at[n]))
                moves = [(k, src.at[rows], out.at[me, rows], _peer(k), out.at[slot(k), rows]) for k in (1, 2, 4, 6)]
            else:
                moves = [(k, src.at[slot(k), rows], out.at[slot(k), rows], _peer(1), out.at[slot(k ^ 1), rows])
                         for k in (2, 4, 6)]
            for k, src_ref, dst_ref, to, landing in moves:
                common = dict(send_sem=send_sems.at[n, k], recv_sem=recv_sems.at[n, k],
                              device_id=to, device_id_type=pl.DeviceIdType.MESH)
                sends.append(pltpu.make_async_remote_copy(src_ref=src_ref, dst_ref=dst_ref, **common))
                arrivals.append(pltpu.make_async_remote_copy(src_ref=landing, dst_ref=landing, **common))
        return local, sends, arrivals

    def start(self, ins, outs, sems):
        local, sends, _ = self._copies(ins, outs, sems)
        for cp in local + sends:
            cp.start()

    def wait(self, ins, outs, sems):
        local, sends, arrivals = self._copies(ins, outs, sems)
        for cp in arrivals:
            cp.wait_recv()
        for cp in sends:
            cp.wait_send()
        for cp in local:
            cp.wait()


class _Rides:
    def __init__(self):
        self.queue = []
        self.flushes = 0

    def add(self, stream):
        self.queue.append(stream)
        return stream

    def take(self, budget, only=None):
        ready = {id(s): (s.after.done if s.after is not None else s.R) for s in self.queue}
        items, left = [], budget
        for s in list(self.queue):
            if only is not None and not any(s is t for t in only):
                continue
            avail = ready[id(s)] - s.done
            if avail <= 0:
                continue
            rc = avail if left == float("inf") else min(avail, int(left * s.rate) // ROW_QUANTUM * ROW_QUANTUM)
            if rc < min(avail, MIN_ROWS):
                continue
            items.append((s, s.done, rc))
            s.done += rc
            left -= rc / s.rate
            if s.done == s.R:
                self.queue.remove(s)
        return items

    @staticmethod
    def deliver(items, outs):
        for (s, _, _), o in zip(items, outs):
            s.buf = o

    def need(self, *streams):
        while any(s.done < s.R for s in streams):
            items = self.take(float("inf"), only=streams)
            comm = _Exchange(items)
            hbm = pl.BlockSpec(memory_space=pl.ANY)
            nin, nout = len(comm.ins), len(items)

            def body(*refs, comm=comm, nin=nin, nout=nout):
                comm.start(refs[:nin], refs[nin:nin + nout], refs[nin + nout:])
                comm.wait(refs[:nin], refs[nin:nin + nout], refs[nin + nout:])

            outs = pl.pallas_call(body, in_specs=[hbm] * nin, out_specs=[hbm] * nout, out_shape=comm.out_shape,
                                  scratch_shapes=comm.scratch, input_output_aliases=dict(comm.alias),
                                  name=f"exchange_alone{self.flushes}")(*comm.ins)
            self.flushes += 1
            self.deliver(items, outs)


def _call(body, *, grid, in_specs, out_specs, out_shape, scratch_shapes=(), args, name, rides=None, us=0.0):
    sem = ("arbitrary",) * len(grid)
    items = rides.take(us) if rides is not None else []
    if not items:
        return pl.pallas_call(body, grid=grid, in_specs=in_specs, out_specs=out_specs, out_shape=out_shape,
                              scratch_shapes=list(scratch_shapes), compiler_params=_params(sem), name=name)(*args)
    comm = _Exchange(items)
    nin, nout, nscr, cin, cout = len(in_specs), len(out_specs), len(scratch_shapes), len(comm.ins), len(items)
    hbm = pl.BlockSpec(memory_space=pl.ANY)

    def both(*refs):
        ins, c_ins = refs[:nin], refs[nin:nin + cin]
        pos = nin + cin
        outs, c_outs = refs[pos:pos + nout], refs[pos + nout:pos + nout + cout]
        pos += nout + cout
        scr, sems = refs[pos:pos + nscr], refs[pos + nscr:]
        first = functools.reduce(lambda p, q: p & q, [pl.program_id(d) == 0 for d in range(len(grid))])
        last = functools.reduce(lambda p, q: p & q, [pl.program_id(d) == grid[d] - 1 for d in range(len(grid))])

        @pl.when(first)
        def _():
            comm.start(c_ins, c_outs, sems)

        body(*ins, *outs, *scr)

        @pl.when(last)
        def _():
            comm.wait(c_ins, c_outs, sems)

    res = pl.pallas_call(both, grid=grid, in_specs=list(in_specs) + [hbm] * cin, out_specs=list(out_specs) + [hbm] * cout,
                         out_shape=list(out_shape) + comm.out_shape, scratch_shapes=list(scratch_shapes) + comm.scratch,
                         input_output_aliases={nin + i: nout + o for i, o in comm.alias.items()},
                         compiler_params=_params(sem), name=name)(*args, *comm.ins)
    rides.deliver(items, res[nout:])
    return res[:nout]


US_GLA, US_GLAPOST, US_MERGE, US_FFN = 100.0, 40.0, 45.0, 110.0
MM_VMEM_BUDGET = 44 * 1024 * 1024
MM_FLOPS_PER_US = 700e6


def _mm(a, b, *, ta=False, tb=False, out_dtype=F32, name, rides=None):
    if ta:
        K, M = a.shape
    else:
        M, K = a.shape
    if tb:
        N, Kb = b.shape
    else:
        Kb, N = b.shape
    assert K == Kb, (a.shape, b.shape, ta, tb)
    tm = _pick(M, 1024)
    tn = _pick(N, 1152)
    sa, sb, so = a.dtype.itemsize, b.dtype.itemsize, jnp.dtype(out_dtype).itemsize

    def vmem(tk):
        return 2 * (tm * tk * sa + tk * tn * sb) + 2 * tm * tn * so + (tm * tn * 4 if tk < K else 0)

    tk = K
    if vmem(K) > MM_VMEM_BUDGET:
        fits = [d for d in range(LANES, K, LANES) if K % d == 0 and vmem(d) <= MM_VMEM_BUDGET]
        tk = max(fits) if fits else _pick(K, 512)
    nk = K // tk
    dims = (((0 if ta else 1,), (1 if tb else 0,)), ((), ()))

    def dot(a_ref, b_ref):
        return lax.dot_general(a_ref[...].astype(BF16), b_ref[...].astype(BF16), dims, preferred_element_type=F32)

    def body_one(a_ref, b_ref, o_ref):
        o_ref[...] = dot(a_ref, b_ref).astype(o_ref.dtype)

    def body_acc(a_ref, b_ref, o_ref, acc_ref):
        k = pl.program_id(2)

        @pl.when(k == 0)
        def _():
            acc_ref[...] = dot(a_ref, b_ref)

        @pl.when((k > 0) & (k < nk - 1))
        def _():
            acc_ref[...] += dot(a_ref, b_ref)

        @pl.when(k == nk - 1)
        def _():
            o_ref[...] = (acc_ref[...] + dot(a_ref, b_ref)).astype(o_ref.dtype)

    a_spec = (pl.BlockSpec((tk, tm), lambda i, j, k: (k, i)) if ta
              else pl.BlockSpec((tm, tk), lambda i, j, k: (i, k)))
    b_spec = (pl.BlockSpec((tn, tk), lambda i, j, k: (j, k)) if tb
              else pl.BlockSpec((tk, tn), lambda i, j, k: (k, j)))
    (out,) = _call(
        body_one if nk == 1 else body_acc, grid=(M // tm, N // tn, nk), in_specs=[a_spec, b_spec],
        out_specs=[pl.BlockSpec((tm, tn), lambda i, j, k: (i, j))],
        out_shape=[jax.ShapeDtypeStruct((M, N), out_dtype)],
        scratch_shapes=[] if nk == 1 else [pltpu.VMEM((tm, tn), F32)], args=(a, b), name=name,
        rides=rides, us=2.0 * M * N * K / MM_FLOPS_PER_US)
    return out


class V(NamedTuple):
    arr: Any
    width: int
    base: Any


def _v(arr, width=None, col0=0):
    width = arr.shape[-1] if width is None else width
    assert col0 % width == 0
    return V(arr, width, col0 // width)


def _row_specs(v, tr, nrow8, halo):
    main = pl.BlockSpec((tr, v.width), lambda j, i, b=v.base: (i, b + j))
    if not halo:
        return [main]
    per = tr // HALO
    prev = pl.BlockSpec((HALO, v.width), lambda j, i, b=v.base: (jnp.maximum(i * per - 1, 0), b + j))
    nxt = pl.BlockSpec((HALO, v.width), lambda j, i, b=v.base: (jnp.minimum((i + 1) * per, nrow8 - 1), b + j))
    return [prev, main, nxt]


def _par_spec(p):
    if p.base is None:
        return pl.BlockSpec(p.arr.shape, lambda j, i: (0,) * p.arr.ndim)
    return pl.BlockSpec((p.arr.shape[0], p.width), lambda j, i, b=p.base: (0, b + j))


def _load_rows(refs, i, nrow, halo):
    if not halo:
        return refs[0][...].astype(F32)
    prev, main, nxt = refs
    pv = jnp.where(i > 0, prev[...].astype(F32), 0.0)
    nv = jnp.where(i < nrow - 1, nxt[...].astype(F32), 0.0)
    return jnp.concatenate([pv, main[...].astype(F32), nv], axis=0)


def _center_mask(tr):
    row = lax.broadcasted_iota(jnp.int32, (tr + 2 * HALO, 1), 0)
    return ((row >= HALO) & (row < HALO + tr)).astype(F32)


def _rw_fwd(f, rows, params, outs, *, tr, ncol=1, halo=False, name, rides=None, us=0.0):
    S = rows[0].arr.shape[0]
    tr = min(tr, S)
    nrow = S // tr
    per = 3 if halo else 1
    nr, npar = len(rows), len(params)

    def body(*refs):
        i = pl.program_id(1)
        vals = [_load_rows(refs[per * r: per * (r + 1)], i, nrow, halo) for r in range(nr)]
        pvals = [refs[per * nr + p][...].astype(F32) for p in range(npar)]
        out_refs = refs[per * nr + npar:]
        res = f(*vals, *pvals, _center_mask(tr)) if halo else f(*vals, *pvals)
        for r, o in zip(res, out_refs):
            o[...] = (r[HALO:HALO + tr] if halo else r).astype(o.dtype)

    in_specs = []
    for v in rows:
        in_specs += _row_specs(v, tr, S // HALO, halo)
    in_specs += [_par_spec(p) for p in params]
    args = []
    for v in rows:
        args += [v.arr] * per
    args += [p.arr for p in params]
    return _call(
        body, grid=(ncol, nrow), in_specs=in_specs,
        out_specs=[pl.BlockSpec((tr, tot // ncol), lambda j, i: (i, j)) for tot, _ in outs],
        out_shape=[jax.ShapeDtypeStruct((S, tot), dt) for tot, dt in outs],
        args=args, name=name, rides=rides, us=us)


def _rw_bwd(f, rows, params, douts, *, row_grads, par_grads, adds=None, tr, ncol=1, halo=False, name):
    S = rows[0].arr.shape[0]
    tr = min(tr, S)
    nrow = S // tr
    per = 3 if halo else 1
    adds = adds or {}
    nr, npar, nd = len(rows), len(params), len(douts)
    add_keys = sorted(adds)
    rg_idx = [k for k in range(nr) if row_grads[k]]
    pg_idx = [k for k in range(npar) if par_grads[k]]

    def body(*refs):
        i = pl.program_id(1)
        pos = 0
        vals = [_load_rows(refs[per * r: per * (r + 1)], i, nrow, halo) for r in range(nr)]
        pos = per * nr
        pvals = [refs[pos + p][...].astype(F32) for p in range(npar)]
        pos += npar
        dvals = [_load_rows(refs[pos + per * d: pos + per * (d + 1)], i, nrow, halo) for d in range(nd)]
        pos += per * nd
        add_vals = {k: refs[pos + n][...].astype(F32) for n, k in enumerate(add_keys)}
        pos += len(add_keys)
        rg_refs = refs[pos: pos + len(rg_idx)]
        pg_refs = refs[pos + len(rg_idx):]
        if halo:
            center = _center_mask(tr)
            fun = lambda *a: f(*a, center)
        else:
            fun = f
        _, vjp = jax.vjp(fun, *vals, *pvals)
        grads = vjp(tuple(dvals))
        for ref, k in zip(rg_refs, rg_idx):
            g = grads[k]
            g = g[HALO:HALO + tr] if halo else g
            if k in add_vals:
                g = g + add_vals[k]
            ref[...] = g.astype(ref.dtype)
        for ref, k in zip(pg_refs, pg_idx):
            @pl.when(i == 0)
            def _(ref=ref):
                ref[...] = jnp.zeros_like(ref)
            ref[...] += grads[nr + k]

    in_specs, args = [], []
    for v in list(rows):
        in_specs += _row_specs(v, tr, S // HALO, halo)
        args += [v.arr] * per
    in_specs += [_par_spec(p) for p in params]
    args += [p.arr for p in params]
    for v in douts:
        in_specs += _row_specs(v, tr, S // HALO, halo)
        args += [v.arr] * per
    for k in add_keys:
        in_specs += _row_specs(_v(adds[k], rows[k].width), tr, S // HALO, False)
        args += [adds[k]]
    out_specs, out_shape = [], []
    for k in rg_idx:
        w = rows[k].width
        out_specs.append(pl.BlockSpec((tr, w), lambda j, i: (i, j)))
        out_shape.append(jax.ShapeDtypeStruct((S, ncol * w), row_grads[k]))
    for k in pg_idx:
        p = params[k]
        blk = p.arr.shape if p.base is None else (p.arr.shape[0], p.width)
        out_specs.append(pl.BlockSpec((None,) + tuple(blk), lambda j, i, n=len(blk): (j,) + (0,) * n))
        out_shape.append(jax.ShapeDtypeStruct((ncol,) + tuple(blk), F32))
    res = _call(body, grid=(ncol, nrow), in_specs=in_specs, out_specs=out_specs, out_shape=out_shape, args=args, name=name)
    rg = [None] * nr
    for n, k in enumerate(rg_idx):
        rg[k] = res[n]
    pg = [None] * npar
    for n, k in enumerate(pg_idx):
        pg[k] = res[len(rg_idx) + n]
    return rg, pg


def _sigmoid(x):
    return 1.0 / (1.0 + jnp.exp(-x))


def _rms(x, g):
    return x * lax.rsqrt(jnp.mean(x * x, axis=-1, keepdims=True) + EPS) * g


def _f_normmod(x, g, sc, sh):
    return (_rms(x, g) * (1.0 + sc) + sh,)


def _f_resid(x, y, gate, g):
    return (x + gate * _rms(y, g),)


def _f_merge(ga, gb, ya, yb):
    return (_sigmoid(ga) * ya + _sigmoid(gb) * yb,)


def _f_glapost(o, r, g):
    return (_rms(o, g) * (r * _sigmoid(r)),)


def _roll_rows(x, k):
    return pltpu.roll(x, k % x.shape[0], 0)


@functools.partial(jax.custom_vjp, nondiff_argnums=(1,))
def _shift(x, k):
    return _roll_rows(x, k)


def _shift_fwd(x, k):
    return _roll_rows(x, k), None


def _shift_bwd(k, _, g):
    return (_roll_rows(g, -k),)


_shift.defvjp(_shift_fwd, _shift_bwd)


def _taps(w, center):
    return center * w + (1.0 - center) * lax.stop_gradient(w)


def _conv3(u, w0, w1, w2, center):
    return (_taps(w0, center) * _shift(u, 2) + _taps(w1, center) * _shift(u, 1)) + _taps(w2, center) * u


def _f_convmix(cb, cc, cx, w0, w1, w2, center):
    return (cb * _conv3(cc * cx, w0, w1, w2, center),)


def _gelu_tanh(x):
    return 0.5 * x * (1.0 + jnp.tanh(math.sqrt(2.0 / math.pi) * (x + 0.044715 * (x * x * x))))


def _f_ffn(gate, up, w0, w1, w2, center):
    return (_gelu_tanh(_conv3(gate, w0, w1, w2, center)) * up,)


@jax.custom_vjp
def _bdot(a, b):
    return jnp.dot(a.astype(BF16), b.astype(BF16), preferred_element_type=F32)


def _bdot_fwd(a, b):
    return _bdot(a, b), (a, b)


def _bdot_bwd(res, g):
    a, b = res
    gb = g.astype(BF16)
    da = lax.dot_general(gb, b.astype(BF16), (((1,), (1,)), ((), ())), preferred_element_type=F32)
    db = lax.dot_general(a.astype(BF16), gb, (((0,), (0,)), ((), ())), preferred_element_type=F32)
    return da, db


_bdot.defvjp(_bdot_fwd, _bdot_bwd)


@jax.custom_vjp
def _log_sigmoid(z):
    return jnp.minimum(z, 0.0) - jnp.log(1.0 + jnp.exp(-jnp.abs(z)))


def _log_sigmoid_fwd(z):
    return _log_sigmoid(z), z


def _log_sigmoid_bwd(z, g):
    return (g * _sigmoid(-z),)


_log_sigmoid.defvjp(_log_sigmoid_fwd, _log_sigmoid_bwd)


def _f_loga(lr, w, b):
    return (_log_sigmoid(_bdot(lr, w) + b) / GLA_TAU,)


def _split3(x):
    hi = x.astype(BF16)
    r1 = x - hi.astype(F32)
    mid = r1.astype(BF16)
    lo = (r1 - mid.astype(F32)).astype(BF16)
    return hi, mid, lo


def _dot_exact(a01, x, dims):
    hi, mid, lo = _split3(x)
    d = lambda p: lax.dot_general(a01, p, dims, preferred_element_type=F32)
    return (d(lo) + d(mid)) + d(hi)


def _dot_exact_lhs(x, a01, dims):
    hi, mid, lo = _split3(x)
    d = lambda p: lax.dot_general(p, a01, dims, preferred_element_type=F32)
    return (d(lo) + d(mid)) + d(hi)


NN = (((1,), (0,)), ((), ()))
NT = (((1,), (1,)), ((), ()))
TN = (((0,), (0,)), ((), ()))


def _bf_dot(a, b, dims):
    return lax.dot_general(a.astype(BF16), b.astype(BF16), dims, preferred_element_type=F32)


def _chunk_decay(a, dv):
    L = a.shape[0]
    r = lax.broadcasted_iota(jnp.int32, (L, L), 0)
    c = lax.broadcasted_iota(jnp.int32, (L, L), 1)
    tri = (r >= c).astype(BF16)
    cum = _dot_exact(tri, a, NN)
    ce = cum[L - 1:L, :]
    dec = jnp.exp(_dot_exact_lhs(a, jnp.ones((L, dv), BF16), TN))
    return cum, ce, jnp.exp(ce - cum), dec


def _gla_fwd(P, a, *, qk, vw, name, rides=None, us=0.0):
    S = P.shape[0]
    nc = S // CHUNK
    dk, dv = qk // HEADS, vw // HEADS
    scale = dk ** -0.5

    def body(q_ref, k_ref, v_ref, a_ref, o_ref, st_ref, state):
        @pl.when(pl.program_id(0) == 0)
        def _():
            state[...] = jnp.zeros_like(state)

        for h in range(HEADS):
            ks, vs = slice(h * dk, (h + 1) * dk), slice(h * dv, (h + 1) * dv)
            _, _, E, dec = _chunk_decay(a_ref[:, ks], dv)
            kd = k_ref[:, ks] * E
            s_new = dec * state[h] + _bf_dot(kd, v_ref[:, vs], TN)
            state[h] = s_new
            st_ref[0, h] = s_new
            o_ref[:, vs] = _bf_dot(q_ref[:, ks] * scale, s_new, NN)

    return _call(
        body, grid=(nc,),
        in_specs=[pl.BlockSpec((CHUNK, qk), lambda i: (i, 0)),
                  pl.BlockSpec((CHUNK, qk), lambda i: (i, 1)),
                  pl.BlockSpec((CHUNK, vw), lambda i: (i, 2 * qk // vw)),
                  pl.BlockSpec((CHUNK, qk), lambda i: (i, 0))],
        out_specs=[pl.BlockSpec((CHUNK, vw), lambda i: (i, 0)),
                   pl.BlockSpec((1, HEADS, dk, dv), lambda i: (i, 0, 0, 0))],
        out_shape=[jax.ShapeDtypeStruct((S, vw), F32), jax.ShapeDtypeStruct((nc, HEADS, dk, dv), F32)],
        scratch_shapes=[pltpu.VMEM((HEADS, dk, dv), F32)], args=(P, P, P, a), name=name, rides=rides, us=us)


def _gla_bwd(P, a, states, do, *, qk, vw, name):
    S = P.shape[0]
    nc = S // CHUNK
    dk, dv = qk // HEADS, vw // HEADS
    scale = dk ** -0.5

    def body(q_ref, k_ref, v_ref, a_ref, do_ref, sc_ref, sp_ref, dq_ref, dk_ref, dv_ref, da_ref, ds_carry):
        i = pl.program_id(0)

        @pl.when(i == 0)
        def _():
            ds_carry[...] = jnp.zeros_like(ds_carry)

        has_prev = (i < nc - 1).astype(F32)
        L = CHUNK
        r = lax.broadcasted_iota(jnp.int32, (L, L), 0)
        c = lax.broadcasted_iota(jnp.int32, (L, L), 1)
        tri_t = (c >= r).astype(BF16)
        last = (lax.broadcasted_iota(jnp.int32, (L, 1), 0) == L - 1).astype(F32)
        for h in range(HEADS):
            ks, vs = slice(h * dk, (h + 1) * dk), slice(h * dv, (h + 1) * dv)
            _, _, E, dec = _chunk_decay(a_ref[:, ks], dv)
            kk, vv, dov = k_ref[:, ks], v_ref[:, vs], do_ref[:, vs]
            kd = kk * E
            ds = ds_carry[h] + _bf_dot(q_ref[:, ks] * scale, dov, TN)
            dq_ref[:, ks] = (_bf_dot(dov, sc_ref[0, h], NT) * scale).astype(dq_ref.dtype)
            dkd = _bf_dot(vv, ds, NT)
            dv_ref[:, vs] = _bf_dot(kd, ds, NN).astype(dv_ref.dtype)
            dk_ref[:, ks] = (dkd * E).astype(dk_ref.dtype)
            darg = dkd * kd
            w = ds * (sp_ref[0, h] * has_prev) * dec
            dce = _dot_exact(jnp.ones((SUBLANES, dv), BF16), w, NT)[0:1, :]
            dce = dce + jnp.sum(darg, axis=0, keepdims=True)
            dcum = last * dce - darg
            da_ref[:, ks] = _dot_exact(tri_t, dcum, NN)
            ds_carry[h] = dec * ds

    rev = lambda i: nc - 1 - i
    return pl.pallas_call(
        body, grid=(nc,),
        in_specs=[pl.BlockSpec((CHUNK, qk), lambda i: (rev(i), 0)),
                  pl.BlockSpec((CHUNK, qk), lambda i: (rev(i), 1)),
                  pl.BlockSpec((CHUNK, vw), lambda i: (rev(i), 2 * qk // vw)),
                  pl.BlockSpec((CHUNK, qk), lambda i: (rev(i), 0)),
                  pl.BlockSpec((CHUNK, vw), lambda i: (rev(i), 0)),
                  pl.BlockSpec((1, HEADS, dk, dv), lambda i: (rev(i), 0, 0, 0)),
                  pl.BlockSpec((1, HEADS, dk, dv), lambda i: (jnp.maximum(rev(i) - 1, 0), 0, 0, 0))],
        out_specs=[pl.BlockSpec((CHUNK, qk), lambda i: (rev(i), 0)),
                   pl.BlockSpec((CHUNK, qk), lambda i: (rev(i), 0)),
                   pl.BlockSpec((CHUNK, vw), lambda i: (rev(i), 0)),
                   pl.BlockSpec((CHUNK, qk), lambda i: (rev(i), 0))],
        out_shape=[jax.ShapeDtypeStruct((S, qk), BF16), jax.ShapeDtypeStruct((S, qk), BF16),
                   jax.ShapeDtypeStruct((S, vw), BF16), jax.ShapeDtypeStruct((S, qk), F32)],
        scratch_shapes=[pltpu.VMEM((HEADS, dk, dv), F32)],
        compiler_params=_params(("arbitrary",)), name=name)(P, P, P, a, do, states, states)


def _loss_head(y, target, *, tr, name):
    S, D = y.shape
    tr = min(tr, S)

    def body(y_ref, t_ref, l_ref, dy_ref):
        @pl.when(pl.program_id(0) == 0)
        def _():
            l_ref[...] = jnp.zeros_like(l_ref)

        e = y_ref[...] - t_ref[...]
        dy_ref[...] = e / D
        l_ref[...] += 0.5 * jnp.sum(jnp.mean(e * e, axis=-1, keepdims=True), axis=0, keepdims=True)

    loss, dy = pl.pallas_call(
        body, grid=(S // tr,),
        in_specs=[pl.BlockSpec((tr, D), lambda i: (i, 0))] * 2,
        out_specs=[pl.BlockSpec((SUBLANES, LANES), lambda i: (0, 0)), pl.BlockSpec((tr, D), lambda i: (i, 0))],
        out_shape=[jax.ShapeDtypeStruct((SUBLANES, LANES), F32), jax.ShapeDtypeStruct((S, D), F32)],
        compiler_params=_params(("arbitrary",)), name=name)(y, target)
    return loss[0, 0], dy


ADAMW_BYTES_PER_US = 2.5e6


def _adamw(parts, w, m, v, *, name, rides=None):
    G, R, C = w.shape
    n = parts[0].shape[0]
    unit = 2 * SUBLANES if parts[0].dtype == BF16 else SUBLANES
    tr = _pick(R, max(unit, (128 * 1024) // C), unit)
    nblk = R // tr
    bc1 = 1.0 - ADAM_B1 ** ADAM_STEP
    bc2 = 1.0 - ADAM_B2 ** ADAM_STEP

    def body(*refs):
        p_refs = refs[:G]
        w_ref, m_ref, v_ref, g_ref, d_ref, mo_ref, vo_ref = refs[G:]
        for q in range(G):
            @pl.when(pl.program_id(0) == q)
            def _(q=q):
                g = p_refs[q][0].astype(F32)
                for s in range(1, n):
                    g = g + p_refs[q][s].astype(F32)
                m2 = ADAM_B1 * m_ref[...] + (1.0 - ADAM_B1) * g
                v2 = ADAM_B2 * v_ref[...] + (1.0 - ADAM_B2) * (g * g)
                g_ref[...] = g
                mo_ref[...] = m2
                vo_ref[...] = v2
                d_ref[...] = -ADAM_LR * ((m2 / bc1) / (jnp.sqrt(v2 / bc2) + ADAM_EPS) + ADAM_WD * w_ref[...])

    def part_spec(q):
        return pl.BlockSpec((n, tr, C), lambda g, i: (0, jnp.where(g == q, i, jnp.where(g < q, 0, nblk - 1)), 0))

    blk = pl.BlockSpec((None, tr, C), lambda g, i: (g, i, 0))
    traffic = G * R * C * (n * parts[0].dtype.itemsize + 7 * 4)
    return _call(
        body, grid=(G, nblk), in_specs=[part_spec(q) for q in range(G)] + [blk, blk, blk],
        out_specs=[blk] * 4, out_shape=[jax.ShapeDtypeStruct((G, R, C), F32)] * 4,
        args=(*parts, w, m, v), name=name, rides=rides, us=traffic / ADAMW_BYTES_PER_US)


def _adamw_nd(parts, w, m, v, *, name):
    shape = w.shape
    C = shape[-1]
    R = math.prod(shape[:-1])
    outs = _adamw([parts.reshape(parts.shape[0], R, C)], w.reshape(1, R, C), m.reshape(1, R, C), v.reshape(1, R, C),
                  name=name)
    return [o.reshape(shape) for o in outs]


def _all_gather(arrs, *, name):
    n = len(arrs)
    HBM = pl.BlockSpec(memory_space=pl.ANY)

    def body(*refs):
        ins, outs = refs[:n], refs[n:2 * n]
        send_sems, recv_sems, local_sems = refs[2 * n:]
        x, y, c = _position()
        me, sibling = (x, y, c), (x, y, 1 - c)
        chips = [(1 - x, y), (x, 1 - y), (1 - x, 1 - y)]

        def copy(t, k, block, to, src=None):
            rows = outs[t].at[_index(block)]
            return pltpu.make_async_remote_copy(
                src_ref=rows if src is None else src, dst_ref=rows,
                send_sem=send_sems.at[t, k], recv_sem=recv_sems.at[t, k],
                device_id=to, device_id_type=pl.DeviceIdType.MESH)

        started = []
        mine = []
        for t in range(n):
            cp = pltpu.make_async_copy(ins[t], outs[t].at[_index(me)], local_sems.at[t])
            cp.start()
            mine.append(cp)
            first = [copy(t, 0, me, sibling, src=ins[t])]
            first += [copy(t, 1 + j, me, (*chip, c), src=ins[t]) for j, chip in enumerate(chips)]
            for cp in first:
                cp.start()
            started += first
        for t in range(n):
            for j, chip in enumerate(chips):
                copy(t, 1 + j, (*chip, c), me).wait_recv()
                cp = copy(t, 4 + j, (*chip, c), sibling)
                cp.start()
                started.append(cp)
        for t in range(n):
            copy(t, 0, sibling, me).wait_recv()
            for j, chip in enumerate(chips):
                copy(t, 4 + j, (*chip, 1 - c), me).wait_recv()
        for cp in started:
            cp.wait_send()
        for cp in mine:
            cp.wait()

    return pl.pallas_call(
        body, in_specs=[HBM] * n, out_specs=[HBM] * n,
        out_shape=[jax.ShapeDtypeStruct((NDEV,) + a.shape, a.dtype) for a in arrs],
        scratch_shapes=[pltpu.SemaphoreType.DMA((n, 7)), pltpu.SemaphoreType.DMA((n, 7)),
                        pltpu.SemaphoreType.DMA((n,))],
        name=name)(*arrs)


class _Pack:
    def __init__(self, arrs):
        self.shapes = [a.shape for a in arrs]
        self.sizes = [math.prod(s) for s in self.shapes]
        total = sum(self.sizes)
        unit = SUBLANES * LANES
        self.padded = -(-total // unit) * unit
        flat = jnp.concatenate([a.reshape(-1).astype(F32) for a in arrs] + [jnp.zeros((self.padded - total,), F32)])
        self.packed = flat.reshape(self.padded // LANES, LANES)

    def unpack(self, gathered):
        flat = gathered.reshape(NDEV, self.padded)
        out, off = [], 0
        for shape, size in zip(self.shapes, self.sizes):
            out.append(flat[:, off:off + size].reshape((NDEV,) + tuple(shape)))
            off += size
        return out


def _cols_from_shards(g):
    g = jnp.moveaxis(g, 0, -2)
    return g.reshape(g.shape[:-2] + (g.shape[-2] * g.shape[-1],))


def _cols_to_shards(w):
    K, N = w.shape
    return jnp.moveaxis(w.reshape(K, NDEV, N // NDEV), 1, 0)


def kernel(x, c, w_ada, b_ada, norm_g, w_in, w_a2, b_a2, gla_norm_g, w_out_gla, conv_mix_w, w_out_conv, w_o, w_up, ffn_conv_w, w_down, loss_target, m_w_ada, m_b_ada, m_norm_g, m_w_in, m_w_a2, m_b_a2, m_gla_norm_g, m_w_out_gla, m_conv_mix_w, m_w_out_conv, m_w_o, m_w_up, m_ffn_conv_w, m_w_down, v_w_ada, v_b_ada, v_norm_g, v_w_in, v_w_a2, v_b_a2, v_gla_norm_g, v_w_out_gla, v_conv_mix_w, v_w_out_conv, v_w_o, v_w_up, v_ffn_conv_w, v_w_down):
    depth = w_ada.shape[0]
    S, D = x.shape[1], x.shape[2]
    QK = w_a2.shape[2] * NDEV
    LR = w_a2.shape[1]
    VW = w_out_gla.shape[1]
    CW = w_out_conv.shape[1]
    FF = w_down.shape[1] * NDEV
    NIN = w_in.shape[2] * NDEV
    NADA = w_ada.shape[2]
    LRP = -(-LR // LANES) * LANES
    off_lr = 2 * QK + 2 * VW
    NP = NIN - LR + LRP
    assert NIN == 2 * QK + 2 * VW + LR + 3 * CW + 2 * D
    xi, yi, ci = _position()
    me = _index((xi, yi, ci))
    xs = x.reshape(S, D)
    target = loss_target.reshape(S, D)
    TR = 256
    TRC = 512
    TC = _pick(CW, 512)
    TCF = _pick(FF, 512)

    small = _Pack([c, norm_g, w_a2, conv_mix_w, ffn_conv_w])
    big_names = ["w_in", "w_out_gla", "w_out_conv", "w_o", "w_up", "w_down"]
    big = dict(w_in=w_in, w_out_gla=w_out_gla, w_out_conv=w_out_conv, w_o=w_o, w_up=w_up, w_down=w_down)

    def matrix(key, g):
        if key in ("w_o", "w_down"):
            return g.reshape(g.shape[0] * g.shape[1], g.shape[2])
        if key != "w_in":
            return _cols_from_shards(g)
        W_in = _cols_from_shards(g)
        return jnp.concatenate([W_in[:, :off_lr], W_in[:, off_lr + LR:], W_in[:, off_lr:off_lr + LR],
                                jnp.zeros((D, LRP - LR), BF16)], axis=-1)

    (small_g, w_in0_g) = _all_gather([small.packed, w_in[0].astype(BF16)], name="gather_first")
    c_all, norm_g_s, w_a2_s, conv_w_s, ffn_w_s = small.unpack(small_g)
    c_all = c_all.reshape(NDEV, D)
    norm_g_f = _cols_from_shards(norm_g_s)
    w_a2_f = _cols_from_shards(w_a2_s)
    conv_w_f = _cols_from_shards(conv_w_s)
    ffn_w_f = _cols_from_shards(ffn_w_s)
    w_a2_p = jnp.concatenate([w_a2_f, jnp.zeros((depth, LRP - LR, QK), F32)], axis=1)

    rides = _Rides()
    GATHER_PARAMS_PER_US = 85e3
    FORWARD_PARAMS_PER_US = 800e3
    EXCHANGE_PARAMS_PER_US = 45e3
    weight_streams = {}
    for l in range(depth):
        for key in big_names:
            if (l, key) == (0, "w_in"):
                continue
            shard = big[key][l].astype(BF16)
            rows, cols = shard.shape
            first = rides.add(_Stream("gather_a", shard, rows, cols, GATHER_PARAMS_PER_US / (NDEV * cols)))
            second = rides.add(_Stream("gather_b", None, rows, cols, FORWARD_PARAMS_PER_US / (NDEV * cols), after=first))
            weight_streams[l, key] = (first, second)
    dev = jnp.arange(NDEV)
    near = ((dev % 2 == ci) | (dev // 2 == 2 * xi + yi))[:, None, None]
    W = {(0, "w_in"): matrix("w_in", w_in0_g)}

    def weight(l, key):
        if (l, key) not in W:
            first, second = weight_streams[l, key]
            rides.need(first, second)
            W[l, key] = matrix(key, jnp.where(near, first.buf, second.buf))
        return W[l, key]

    MP = LANES
    c_pad = jnp.concatenate([c_all, jnp.zeros((MP - NDEV, D), F32)], axis=0)
    (silu_c,) = _rw_fwd(lambda t: (t * _sigmoid(t),), [_v(c_pad)], [], [(D, BF16)], tr=MP, name="silu_c")
    mod_cols = [_mm(silu_c, w_ada[l], name=f"ada_fwd{l}")[:NDEV] for l in range(depth)]
    (mod_g,) = _all_gather([jnp.stack(mod_cols)], name="gather_mod")
    mod_all = jnp.moveaxis(mod_g, 0, 2).reshape(depth, NDEV, NDEV * NADA)
    mod_me = lax.dynamic_slice_in_dim(mod_all, me, 1, axis=1).reshape(depth, NDEV * NADA)
    (mod_me,) = _rw_fwd(lambda a, b: (a + b,), [_v(mod_me), _v(b_ada)], [], [(NDEV * NADA, F32)],
                        tr=depth, name="mod_bias")

    col = {}
    off = 0
    for nm, wd in (("q", QK), ("k", QK), ("v", VW), ("r", VW), ("cb", CW), ("cc", CW), ("cx", CW),
                   ("ga", D), ("gb", D), ("lr", LRP)):
        col[nm] = off
        off += wd

    saved = []
    xin = xs
    for l in range(depth):
        sh1, sc1, g1, sh2, sc2, g2 = [mod_me[l:l + 1, i * D:(i + 1) * D] for i in range(6)]
        ng = [norm_g_f[l, i:i + 1] for i in range(4)]
        cw = [conv_w_f[l, i:i + 1] for i in range(CONV_K)]
        fw = [ffn_w_f[l, i:i + 1] for i in range(CONV_K)]
        gn = gla_norm_g[l:l + 1]
        ba2 = b_a2[l:l + 1]
        def proj(act, key, name):
            return _mm(act, weight(l, key), name=name, rides=rides)

        (h1,) = _rw_fwd(_f_normmod, [_v(xin)], [_v(ng[0]), _v(sc1), _v(sh1)], [(D, BF16)], tr=TR, name=f"normmod1_{l}")
        P = proj(h1, "w_in", f"in_proj{l}")
        (a,) = _rw_fwd(_f_loga, [_v(P, LRP, col["lr"])], [V(w_a2_p[l], QK, None), _v(ba2)], [(QK, F32)],
                       tr=TR, name=f"loga{l}")
        o, states = _gla_fwd(P, a, qk=QK, vw=VW, name=f"gla_fwd{l}", rides=rides, us=US_GLA)
        dvh = VW // HEADS
        (ya_in,) = _rw_fwd(_f_glapost, [_v(o, dvh), _v(P, dvh, col["r"])], [V(gn, dvh, None)], [(VW, BF16)],
                           tr=TR, ncol=HEADS, name=f"glapost{l}", rides=rides, us=US_GLAPOST)
        ya = proj(ya_in, "w_out_gla", f"out_gla{l}")
        (yb_in,) = _rw_fwd(_f_convmix, [_v(P, TC, col["cb"]), _v(P, TC, col["cc"]), _v(P, TC, col["cx"])],
                           [_v(w, TC) for w in cw], [(CW, BF16)], tr=TRC, ncol=CW // TC, halo=True, name=f"convmix{l}")
        yb = proj(yb_in, "w_out_conv", f"out_conv{l}")
        (mg,) = _rw_fwd(_f_merge, [_v(P, D, col["ga"]), _v(P, D, col["gb"]), _v(ya), _v(yb)], [], [(D, BF16)],
                        tr=TR, name=f"merge{l}", rides=rides, us=US_MERGE)
        y = proj(mg, "w_o", f"o_proj{l}")
        (x1,) = _rw_fwd(_f_resid, [_v(xin), _v(y)], [_v(g1), _v(ng[1])], [(D, F32)], tr=TR, name=f"resid1_{l}")
        (h2,) = _rw_fwd(_f_normmod, [_v(x1)], [_v(ng[2]), _v(sc2), _v(sh2)], [(D, BF16)], tr=TR, name=f"normmod2_{l}")
        U = proj(h2, "w_up", f"up_proj{l}")
        (ff,) = _rw_fwd(_f_ffn, [_v(U, TCF, 0), _v(U, TCF, FF)], [_v(w, TCF) for w in fw], [(FF, BF16)],
                        tr=TRC, ncol=FF // TCF, halo=True, name=f"ffn{l}", rides=rides, us=US_FFN)
        y2 = proj(ff, "w_down", f"down_proj{l}")
        (x2,) = _rw_fwd(_f_resid, [_v(x1), _v(y2)], [_v(g2), _v(ng[3])], [(D, F32)], tr=TR, name=f"resid2_{l}")
        saved.append(dict(xin=xin, h1=h1, P=P, a=a, o=o, states=states, ya_in=ya_in, ya=ya, yb_in=yb_in, yb=yb,
                          mg=mg, y=y, x1=x1, h2=h2, U=U, ff=ff, y2=y2, mods=(sh1, sc1, g1, sh2, sc2, g2),
                          ng=ng, cw=cw, fw=fw, gn=gn, ba2=ba2))
        xin = x2

    loss_local, dx = _loss_head(xin, target, tr=TR, name="loss_head")
    loss = lax.psum(loss_local, MESH_AXES)

    grad_streams = {}

    def send(l, key, blocks):
        _, rows, cols = blocks.shape
        grad_streams[l, key] = rides.add(_Stream("a2a", blocks, rows, cols, EXCHANGE_PARAMS_PER_US / (NDEV * cols)))

    def bmm(a, b, **kw):
        return _mm(a, b, rides=rides, **kw)

    gsmall = [None] * depth
    dmod = [None] * depth
    for l in reversed(range(depth)):
        s = saved[l]
        sh1, sc1, g1, sh2, sc2, g2 = s["mods"]
        ng, cw, fw = s["ng"], s["cw"], s["fw"]
        (dx1_a, dy2), (dg2, dng3) = _rw_bwd(
            _f_resid, [_v(s["x1"]), _v(s["y2"])], [_v(g2), _v(ng[3])], [_v(dx)],
            row_grads=[F32, BF16], par_grads=[True, True], tr=TR, name=f"resid2_bwd{l}")
        gW_down = bmm(s["ff"], dy2, ta=True, out_dtype=BF16, name=f"down_wgrad{l}")
        send(l, "w_down", gW_down.reshape(NDEV, FF // NDEV, D))
        dff = bmm(dy2, weight(l, "w_down"), tb=True, name=f"down_dgrad{l}")
        U = s["U"]
        (dgate, dup), dfw = _rw_bwd(
            _f_ffn, [_v(U, TCF, 0), _v(U, TCF, FF)], [_v(w, TCF) for w in fw], [_v(dff, TCF)],
            row_grads=[BF16, BF16], par_grads=[True] * 3, tr=TRC, ncol=FF // TCF, halo=True, name=f"ffn_bwd{l}")
        dU = jnp.concatenate([dgate, dup], axis=1)
        gW_up = bmm(s["h2"], dU, ta=True, out_dtype=BF16, name=f"up_wgrad{l}")
        send(l, "w_up", _cols_to_shards(gW_up))
        dh2 = bmm(dU, weight(l, "w_up"), tb=True, name=f"up_dgrad{l}")
        (dx1,), (dng2, dsc2, dsh2) = _rw_bwd(
            _f_normmod, [_v(s["x1"])], [_v(ng[2]), _v(sc2), _v(sh2)], [_v(dh2)],
            row_grads=[F32], par_grads=[True] * 3, adds={0: dx1_a}, tr=TR, name=f"normmod2_bwd{l}")
        (dxin_a, dy), (dg1, dng1) = _rw_bwd(
            _f_resid, [_v(s["xin"]), _v(s["y"])], [_v(g1), _v(ng[1])], [_v(dx1)],
            row_grads=[F32, BF16], par_grads=[True, True], tr=TR, name=f"resid1_bwd{l}")
        gW_o = bmm(s["mg"], dy, ta=True, out_dtype=BF16, name=f"o_wgrad{l}")
        send(l, "w_o", gW_o.reshape(NDEV, D // NDEV, D))
        dmg = bmm(dy, weight(l, "w_o"), tb=True, name=f"o_dgrad{l}")
        P = s["P"]
        (dga, dgb, dya, dyb), _ = _rw_bwd(
            _f_merge, [_v(P, D, col["ga"]), _v(P, D, col["gb"]), _v(s["ya"]), _v(s["yb"])], [], [_v(dmg)],
            row_grads=[BF16] * 4, par_grads=[], tr=TR, name=f"merge_bwd{l}")
        gW_og = bmm(s["ya_in"], dya, ta=True, out_dtype=BF16, name=f"out_gla_wgrad{l}")
        send(l, "w_out_gla", _cols_to_shards(gW_og))
        dya_in = bmm(dya, weight(l, "w_out_gla"), tb=True, name=f"out_gla_dgrad{l}")
        gW_oc = bmm(s["yb_in"], dyb, ta=True, out_dtype=BF16, name=f"out_conv_wgrad{l}")
        send(l, "w_out_conv", _cols_to_shards(gW_oc))
        dyb_in = bmm(dyb, weight(l, "w_out_conv"), tb=True, name=f"out_conv_dgrad{l}")
        (dcb, dcc, dcx), dcw = _rw_bwd(
            _f_convmix, [_v(P, TC, col["cb"]), _v(P, TC, col["cc"]), _v(P, TC, col["cx"])],
            [_v(w, TC) for w in cw], [_v(dyb_in, TC)],
            row_grads=[BF16] * 3, par_grads=[True] * 3, tr=TRC, ncol=CW // TC, halo=True, name=f"convmix_bwd{l}")
        dvh = VW // HEADS
        (do, dr), (dgn,) = _rw_bwd(
            _f_glapost, [_v(s["o"], dvh), _v(P, dvh, col["r"])], [V(s["gn"], dvh, None)], [_v(dya_in, dvh)],
            row_grads=[BF16, BF16], par_grads=[True], tr=TR, ncol=HEADS, name=f"glapost_bwd{l}")
        dq, dk, dv, da = _gla_bwd(P, s["a"], s["states"], do, qk=QK, vw=VW, name=f"gla_bwd{l}")
        (dlr,), (dwa2, dba2) = _rw_bwd(
            _f_loga, [_v(P, LRP, col["lr"])], [V(w_a2_p[l], QK, None), _v(s["ba2"])], [_v(da)],
            row_grads=[BF16], par_grads=[True, True], tr=TR, name=f"loga_bwd{l}")
        dP = jnp.concatenate([dq, dk, dv, dr, dcb, dcc, dcx, dga, dgb, dlr], axis=1)
        gW_all = bmm(s["h1"], dP, ta=True, out_dtype=BF16, name=f"in_wgrad{l}")
        gW_in = jnp.concatenate([gW_all[:, :off_lr], gW_all[:, NIN - LR:NIN], gW_all[:, off_lr:NIN - LR]], axis=1)
        send(l, "w_in", _cols_to_shards(gW_in))
        dh1 = bmm(dP, weight(l, "w_in"), tb=True, name=f"in_dgrad{l}")
        (dx,), (dng0, dsc1, dsh1) = _rw_bwd(
            _f_normmod, [_v(s["xin"])], [_v(ng[0]), _v(sc1), _v(sh1)], [_v(dh1)],
            row_grads=[F32], par_grads=[True] * 3, adds={0: dxin_a}, tr=TR, name=f"normmod1_bwd{l}")
        vec = lambda t: t.reshape(1, -1)
        gsmall[l] = dict(
            norm_g=jnp.concatenate([vec(dng0), vec(dng1), vec(dng2), vec(dng3)], axis=0),
            w_a2=dwa2[0, :LR], b_a2=dba2.reshape(QK), gla_norm_g=jnp.sum(dgn, axis=0).reshape(dvh),
            conv_mix_w=jnp.concatenate([vec(t) for t in dcw], axis=0),
            ffn_conv_w=jnp.concatenate([vec(t) for t in dfw], axis=0))
        dmod[l] = jnp.concatenate([vec(t) for t in (dsh1, dsc1, dg1, dsh2, dsc2, dg2)], axis=1).reshape(-1)
    grad_x = dx.reshape(x.shape)

    stack = lambda key, src: jnp.stack([src[l][key] for l in range(depth)])
    small_names = ["norm_g", "w_a2", "b_a2", "gla_norm_g", "conv_mix_w", "ffn_conv_w"]
    gpack = _Pack([jnp.stack(dmod)] + [stack(k, gsmall) for k in small_names])
    (gsm,) = _all_gather([gpack.packed], name="gather_small_grads")
    gs = gpack.unpack(gsm)
    dmod_all = gs[0]
    parts_small = dict(zip(small_names, gs[1:]))

    def my_cols(t, n):
        return lax.dynamic_slice_in_dim(t, me * n, n, axis=t.ndim - 1)

    gW_ada = []
    for l in range(depth):
        dcols = my_cols(dmod_all[:, l], NADA)
        dcols = jnp.concatenate([dcols, jnp.zeros((MP - NDEV, NADA), F32)], axis=0)
        gW_ada.append(bmm(silu_c, dcols, ta=True, name=f"ada_wgrad{l}")[None])
    parts = {}
    parts["b_ada"] = dmod_all
    parts["norm_g"] = my_cols(parts_small["norm_g"], D // NDEV)
    parts["w_a2"] = my_cols(parts_small["w_a2"], QK // NDEV)
    parts["b_a2"] = parts_small["b_a2"]
    parts["gla_norm_g"] = parts_small["gla_norm_g"]
    parts["conv_mix_w"] = my_cols(parts_small["conv_mix_w"], CW // NDEV)
    parts["ffn_conv_w"] = my_cols(parts_small["ffn_conv_w"], FF // NDEV)

    weights = dict(w_ada=(w_ada, m_w_ada, v_w_ada), b_ada=(b_ada, m_b_ada, v_b_ada),
                   norm_g=(norm_g, m_norm_g, v_norm_g), w_in=(w_in, m_w_in, v_w_in), w_a2=(w_a2, m_w_a2, v_w_a2),
                   b_a2=(b_a2, m_b_a2, v_b_a2), gla_norm_g=(gla_norm_g, m_gla_norm_g, v_gla_norm_g),
                   w_out_gla=(w_out_gla, m_w_out_gla, v_w_out_gla), conv_mix_w=(conv_mix_w, m_conv_mix_w, v_conv_mix_w),
                   w_out_conv=(w_out_conv, m_w_out_conv, v_w_out_conv), w_o=(w_o, m_w_o, v_w_o),
                   w_up=(w_up, m_w_up, v_w_up), ffn_conv_w=(ffn_conv_w, m_ffn_conv_w, v_ffn_conv_w),
                   w_down=(w_down, m_w_down, v_w_down))
    done = {}
    for nm in ["w_ada", "w_down", "w_up", "w_o", "w_out_gla", "w_out_conv", "w_in"]:
        w, m, v = weights[nm]
        if nm == "w_ada":
            mine = gW_ada
        else:
            streams = [grad_streams[l, nm] for l in range(depth)]
            rides.need(*streams)
            mine = [s.buf for s in streams]
        done[nm] = _adamw(mine, w, m, v, name=f"adamw_{nm}", rides=rides)
    for nm, (w, m, v) in weights.items():
        if nm not in done:
            done[nm] = _adamw_nd(parts[nm], w, m, v, name=f"adamw_{nm}")
    grads, deltas, new_m, new_v = zip(*[done[nm] for nm in weights])
    return (loss, grad_x, *grads, *deltas, *new_m, *new_v)
```

```python
import functools
import math
from typing import Any, NamedTuple

import jax
import jax.numpy as jnp
from jax import lax
from jax.experimental import pallas as pl
from jax.experimental.pallas import tpu as pltpu

F32 = jnp.float32
BF16 = jnp.bfloat16
MESH_AXES = ("x", "y", "c")
NDEV = 8
EPS = 1e-6
CHUNK = 64
HEADS = 4
GLA_TAU = 16.0
CONV_K = 3
ADAM_LR = 0.001
ADAM_B1 = 0.9
ADAM_B2 = 0.999
ADAM_EPS = 1e-08
ADAM_WD = 0.01
ADAM_STEP = 10

LANES = 128
SUBLANES = 8
HALO = 2 * SUBLANES
VMEM_LIMIT = 56 * 1024 * 1024


def _params(sem=None):
    return pltpu.CompilerParams(dimension_semantics=sem, vmem_limit_bytes=VMEM_LIMIT)


def _pick(n, cap, unit=LANES):
    best = None
    for d in range(unit, min(n, cap) + 1, unit):
        if n % d == 0:
            best = d
    return best if best is not None else n


def _position():
    return lax.axis_index("x"), lax.axis_index("y"), lax.axis_index("c")


def _index(p):
    return 4 * p[0] + 2 * p[1] + p[2]


def _peer(k):
    x, y, c = _position()
    return ((1 - x) if k & 4 else x, (1 - y) if k & 2 else y, (1 - c) if k & 1 else c)


ROW_QUANTUM = 2 * SUBLANES
MIN_ROWS = 128


class _Stream:
    def __init__(self, mode, src, rows, cols, rate, after=None):
        self.mode, self.src, self.R, self.rate, self.after = mode, src, rows, rate, after
        self.shape = (NDEV, rows, cols)
        self.holder = self if after is None else after
        self.buf = None
        self.done = 0


class _Exchange:
    def __init__(self, items):
        self.items = items
        self.ins, self.src_pos, self.alias, self.holders, self.out_pos = [], [], {}, [], []
        for s, _, _ in items:
            if not any(s.holder is h for h in self.holders):
                self.holders.append(s.holder)
        for s, _, _ in items:
            self.out_pos.append([n for n, h in enumerate(self.holders) if h is s.holder][0])
            if s.mode == "gather_b":
                self.src_pos.append(None)
            else:
                self.src_pos.append(len(self.ins))
                self.ins.append(s.src)
        for n, h in enumerate(self.holders):
            if h.buf is not None:
                self.alias[len(self.ins)] = n
                self.ins.append(h.buf)
        self.out_shape = [jax.ShapeDtypeStruct(h.shape, BF16) for h in self.holders]
        n = len(items)
        self.scratch = [pltpu.SemaphoreType.DMA((n, NDEV)), pltpu.SemaphoreType.DMA((n, NDEV)),
                        pltpu.SemaphoreType.DMA((n,))]

    def deliver(self, outs):
        for h, o in zip(self.holders, outs):
            h.buf = o

    def _copies(self, ins, outs, sems):
        send_sems, recv_sems, local_sems = sems
        me = _index(_position())
        slot = lambda k: _index(_peer(k))
        local, sends, arrivals = [], [], []
        for n, (s, r0, rc) in enumerate(self.items):
            out, rows = outs[self.out_pos[n]], pl.ds(r0, rc)
            src = out if self.src_pos[n] is None else ins[self.src_pos[n]]
            if s.mode == "a2a":
                local.append(pltpu.make_async_copy(src.at[me, rows], out.at[me, rows], local_sems.at[n]))
                moves = [(k, src.at[slot(k), rows], out.at[me, rows], _peer(k), out.at[slot(k), rows])
                         for k in range(1, NDEV)]
            elif s.mode == "gather_a":
                local.append(pltpu.make_async_copy(src.at[rows], out.at[me, rows], local_sems.at[n]))
                moves = [(k, src.at[rows], out.at[me, rows], _peer(k), out.at[slot(k), rows]) for k in (1, 2, 4, 6)]
            else:
                moves = [(k, src.at[slot(k), rows], out.at[slot(k), rows], _peer(1), out.at[slot(k ^ 1), rows])
                         for k in (2, 4, 6)]
            for k, src_ref, dst_ref, to, landing in moves:
                common = dict(send_sem=send_sems.at[n, k], recv_sem=recv_sems.at[n, k],
                              device_id=to, device_id_type=pl.DeviceIdType.MESH)
                sends.append(pltpu.make_async_remote_copy(src_ref=src_ref, dst_ref=dst_ref, **common))
                arrivals.append(pltpu.make_async_remote_copy(src_ref=landing, dst_ref=landing, **common))
        return local, sends, arrivals

    def start(self, ins, outs, sems):
        local, sends, _ = self._copies(ins, outs, sems)
        for cp in local + sends:
            cp.start()

    def wait(self, ins, outs, sems):
        local, sends, arrivals = self._copies(ins, outs, sems)
        for cp in arrivals:
            cp.wait_recv()
        for cp in sends:
            cp.wait_send()
        for cp in local:
            cp.wait()


class _Rides:
    def __init__(self):
        self.queue = []
        self.flushes = 0

    def add(self, stream):
        self.queue.append(stream)
        return stream

    def take(self, budget, only=None):
        ready = {id(s): (s.after.done if s.after is not None else s.R) for s in self.queue}
        items, left = [], budget
        for s in list(self.queue):
            if only is not None and not any(s is t for t in only):
                continue
            avail = ready[id(s)] - s.done
            if avail <= 0:
                continue
            rc = avail if left == float("inf") else min(avail, int(left * s.rate) // ROW_QUANTUM * ROW_QUANTUM)
            if rc < min(avail, MIN_ROWS):
                continue
            items.append((s, s.done, rc))
            s.done += rc
            left -= rc / s.rate
            if s.done == s.R:
                self.queue.remove(s)
        return items

    def need(self, *streams):
        while any(s.done < s.R for s in streams):
            items = self.take(float("inf"), only=streams)
            comm = _Exchange(items)
            hbm = pl.BlockSpec(memory_space=pl.ANY)
            nin, nout = len(comm.ins), len(comm.holders)

            def body(*refs, comm=comm, nin=nin, nout=nout):
                comm.start(refs[:nin], refs[nin:nin + nout], refs[nin + nout:])
                comm.wait(refs[:nin], refs[nin:nin + nout], refs[nin + nout:])

            outs = pl.pallas_call(body, in_specs=[hbm] * nin, out_specs=[hbm] * nout, out_shape=comm.out_shape,
                                  scratch_shapes=comm.scratch, input_output_aliases=dict(comm.alias),
                                  name=f"exchange_alone{self.flushes}")(*comm.ins)
            self.flushes += 1
            comm.deliver(outs)


def _call(body, *, grid, in_specs, out_specs, out_shape, scratch_shapes=(), args, name, rides=None, us=0.0):
    sem = ("arbitrary",) * len(grid)
    items = rides.take(us) if rides is not None else []
    if not items:
        return pl.pallas_call(body, grid=grid, in_specs=in_specs, out_specs=out_specs, out_shape=out_shape,
                              scratch_shapes=list(scratch_shapes), compiler_params=_params(sem), name=name)(*args)
    comm = _Exchange(items)
    nin, nout, nscr, cin, cout = len(in_specs), len(out_specs), len(scratch_shapes), len(comm.ins), len(comm.holders)
    hbm = pl.BlockSpec(memory_space=pl.ANY)

    def both(*refs):
        ins, c_ins = refs[:nin], refs[nin:nin + cin]
        pos = nin + cin
        outs, c_outs = refs[pos:pos + nout], refs[pos + nout:pos + nout + cout]
        pos += nout + cout
        scr, sems = refs[pos:pos + nscr], refs[pos + nscr:]
        first = functools.reduce(lambda p, q: p & q, [pl.program_id(d) == 0 for d in range(len(grid))])
        last = functools.reduce(lambda p, q: p & q, [pl.program_id(d) == grid[d] - 1 for d in range(len(grid))])

        @pl.when(first)
        def _():
            comm.start(c_ins, c_outs, sems)

        body(*ins, *outs, *scr)

        @pl.when(last)
        def _():
            comm.wait(c_ins, c_outs, sems)

    res = pl.pallas_call(both, grid=grid, in_specs=list(in_specs) + [hbm] * cin, out_specs=list(out_specs) + [hbm] * cout,
                         out_shape=list(out_shape) + comm.out_shape, scratch_shapes=list(scratch_shapes) + comm.scratch,
                         input_output_aliases={nin + i: nout + o for i, o in comm.alias.items()},
                         compiler_params=_params(sem), name=name)(*args, *comm.ins)
    comm.deliver(res[nout:])
    return res[:nout]


US_GLA, US_GLAPOST, US_MERGE, US_FFN = 100.0, 40.0, 45.0, 110.0
MM_VMEM_BUDGET = 44 * 1024 * 1024
MM_FLOPS_PER_US = 700e6


def _mm(a, b, *, ta=False, tb=False, out_dtype=F32, name, rides=None, out_shards=False):
    if ta:
        K, M = a.shape
    else:
        M, K = a.shape
    sharded = b.ndim == 3
    if sharded:
        n = b.shape[2]
        Kb, N = (NDEV * n, b.shape[1]) if tb else (b.shape[1], NDEV * n)
    elif tb:
        N, Kb = b.shape
    else:
        Kb, N = b.shape
    assert K == Kb, (a.shape, b.shape, ta, tb)
    tm = _pick(M, 1024)
    tn = N // NDEV if (out_shards or (sharded and not tb)) else _pick(N, 1152)
    sa, sb, so = a.dtype.itemsize, b.dtype.itemsize, jnp.dtype(out_dtype).itemsize

    def vmem(tk):
        return 2 * (tm * tk * sa + tk * tn * sb) + 2 * tm * tn * so + (tm * tn * 4 if tk < K else 0)

    if sharded and tb:
        tk = n
    else:
        tk = K
        if vmem(K) > MM_VMEM_BUDGET:
            fits = [d for d in range(LANES, K, LANES) if K % d == 0 and vmem(d) <= MM_VMEM_BUDGET]
            tk = max(fits) if fits else _pick(K, 512)
    nk = K // tk
    dims = (((0 if ta else 1,), (1 if tb else 0,)), ((), ()))

    def dot(a_ref, b_ref):
        return lax.dot_general(a_ref[...].astype(BF16), b_ref[...].astype(BF16), dims, preferred_element_type=F32)

    def body_one(a_ref, b_ref, o_ref):
        o_ref[...] = dot(a_ref, b_ref).astype(o_ref.dtype)

    def body_acc(a_ref, b_ref, o_ref, acc_ref):
        k = pl.program_id(2)

        @pl.when(k == 0)
        def _():
            acc_ref[...] = dot(a_ref, b_ref)

        @pl.when((k > 0) & (k < nk - 1))
        def _():
            acc_ref[...] += dot(a_ref, b_ref)

        @pl.when(k == nk - 1)
        def _():
            o_ref[...] = (acc_ref[...] + dot(a_ref, b_ref)).astype(o_ref.dtype)

    a_spec = (pl.BlockSpec((tk, tm), lambda i, j, k: (k, i)) if ta
              else pl.BlockSpec((tm, tk), lambda i, j, k: (i, k)))
    if sharded:
        b_spec = (pl.BlockSpec((None, tn, tk), lambda i, j, k: (k, j, 0)) if tb
                  else pl.BlockSpec((None, tk, tn), lambda i, j, k: (j, k, 0)))
    else:
        b_spec = (pl.BlockSpec((tn, tk), lambda i, j, k: (j, k)) if tb
                  else pl.BlockSpec((tk, tn), lambda i, j, k: (k, j)))
    if out_shards:
        out_spec = pl.BlockSpec((None, tm, tn), lambda i, j, k: (j, i, 0))
        out_shape = jax.ShapeDtypeStruct((NDEV, M, tn), out_dtype)
    else:
        out_spec = pl.BlockSpec((tm, tn), lambda i, j, k: (i, j))
        out_shape = jax.ShapeDtypeStruct((M, N), out_dtype)
    (out,) = _call(
        body_one if nk == 1 else body_acc, grid=(M // tm, N // tn, nk), in_specs=[a_spec, b_spec],
        out_specs=[out_spec], out_shape=[out_shape],
        scratch_shapes=[] if nk == 1 else [pltpu.VMEM((tm, tn), F32)], args=(a, b), name=name,
        rides=rides, us=2.0 * M * N * K / MM_FLOPS_PER_US)
    return out


class V(NamedTuple):
    arr: Any
    width: int
    base: Any


def _v(arr, width=None, col0=0):
    width = arr.shape[-1] if width is None else width
    assert col0 % width == 0
    return V(arr, width, col0 // width)


def _row_specs(v, tr, nrow8, halo):
    main = pl.BlockSpec((tr, v.width), lambda j, i, b=v.base: (i, b + j))
    if not halo:
        return [main]
    per = tr // HALO
    prev = pl.BlockSpec((HALO, v.width), lambda j, i, b=v.base: (jnp.maximum(i * per - 1, 0), b + j))
    nxt = pl.BlockSpec((HALO, v.width), lambda j, i, b=v.base: (jnp.minimum((i + 1) * per, nrow8 - 1), b + j))
    return [prev, main, nxt]


def _par_spec(p):
    if p.base is None:
        return pl.BlockSpec(p.arr.shape, lambda j, i: (0,) * p.arr.ndim)
    return pl.BlockSpec((p.arr.shape[0], p.width), lambda j, i, b=p.base: (0, b + j))


def _load_rows(refs, i, nrow, halo):
    if not halo:
        return refs[0][...].astype(F32)
    prev, main, nxt = refs
    pv = jnp.where(i > 0, prev[...].astype(F32), 0.0)
    nv = jnp.where(i < nrow - 1, nxt[...].astype(F32), 0.0)
    return jnp.concatenate([pv, main[...].astype(F32), nv], axis=0)


def _center_mask(tr):
    row = lax.broadcasted_iota(jnp.int32, (tr + 2 * HALO, 1), 0)
    return ((row >= HALO) & (row < HALO + tr)).astype(F32)


def _rw_fwd(f, rows, params, outs, *, tr, ncol=1, halo=False, name, rides=None, us=0.0):
    S = rows[0].arr.shape[0]
    tr = min(tr, S)
    nrow = S // tr
    per = 3 if halo else 1
    nr, npar = len(rows), len(params)

    def body(*refs):
        i = pl.program_id(1)
        vals = [_load_rows(refs[per * r: per * (r + 1)], i, nrow, halo) for r in range(nr)]
        pvals = [refs[per * nr + p][...].astype(F32) for p in range(npar)]
        out_refs = refs[per * nr + npar:]
        res = f(*vals, *pvals, _center_mask(tr)) if halo else f(*vals, *pvals)
        for r, o in zip(res, out_refs):
            o[...] = (r[HALO:HALO + tr] if halo else r).astype(o.dtype)

    in_specs = []
    for v in rows:
        in_specs += _row_specs(v, tr, S // HALO, halo)
    in_specs += [_par_spec(p) for p in params]
    args = []
    for v in rows:
        args += [v.arr] * per
    args += [p.arr for p in params]
    return _call(
        body, grid=(ncol, nrow), in_specs=in_specs,
        out_specs=[pl.BlockSpec((tr, tot // ncol), lambda j, i: (i, j)) for tot, _ in outs],
        out_shape=[jax.ShapeDtypeStruct((S, tot), dt) for tot, dt in outs],
        args=args, name=name, rides=rides, us=us)


def _rw_bwd(f, rows, params, douts, *, row_grads, par_grads, adds=None, tr, ncol=1, halo=False, name):
    S = rows[0].arr.shape[0]
    tr = min(tr, S)
    nrow = S // tr
    per = 3 if halo else 1
    adds = adds or {}
    nr, npar, nd = len(rows), len(params), len(douts)
    add_keys = sorted(adds)
    rg_idx = [k for k in range(nr) if row_grads[k]]
    pg_idx = [k for k in range(npar) if par_grads[k]]

    def body(*refs):
        i = pl.program_id(1)
        pos = 0
        vals = [_load_rows(refs[per * r: per * (r + 1)], i, nrow, halo) for r in range(nr)]
        pos = per * nr
        pvals = [refs[pos + p][...].astype(F32) for p in range(npar)]
        pos += npar
        dvals = [_load_rows(refs[pos + per * d: pos + per * (d + 1)], i, nrow, halo) for d in range(nd)]
        pos += per * nd
        add_vals = {k: refs[pos + n][...].astype(F32) for n, k in enumerate(add_keys)}
        pos += len(add_keys)
        rg_refs = refs[pos: pos + len(rg_idx)]
        pg_refs = refs[pos + len(rg_idx):]
        if halo:
            center = _center_mask(tr)
            fun = lambda *a: f(*a, center)
        else:
            fun = f
        _, vjp = jax.vjp(fun, *vals, *pvals)
        grads = vjp(tuple(dvals))
        for ref, k in zip(rg_refs, rg_idx):
            g = grads[k]
            g = g[HALO:HALO + tr] if halo else g
            if k in add_vals:
                g = g + add_vals[k]
            ref[...] = g.astype(ref.dtype)
        for ref, k in zip(pg_refs, pg_idx):
            @pl.when(i == 0)
            def _(ref=ref):
                ref[...] = jnp.zeros_like(ref)
            ref[...] += grads[nr + k]

    in_specs, args = [], []
    for v in list(rows):
        in_specs += _row_specs(v, tr, S // HALO, halo)
        args += [v.arr] * per
    in_specs += [_par_spec(p) for p in params]
    args += [p.arr for p in params]
    for v in douts:
        in_specs += _row_specs(v, tr, S // HALO, halo)
        args += [v.arr] * per
    for k in add_keys:
        in_specs += _row_specs(_v(adds[k], rows[k].width), tr, S // HALO, False)
        args += [adds[k]]
    out_specs, out_shape = [], []
    for k in rg_idx:
        w = rows[k].width
        out_specs.append(pl.BlockSpec((tr, w), lambda j, i: (i, j)))
        out_shape.append(jax.ShapeDtypeStruct((S, ncol * w), row_grads[k]))
    for k in pg_idx:
        p = params[k]
        blk = p.arr.shape if p.base is None else (p.arr.shape[0], p.width)
        out_specs.append(pl.BlockSpec((None,) + tuple(blk), lambda j, i, n=len(blk): (j,) + (0,) * n))
        out_shape.append(jax.ShapeDtypeStruct((ncol,) + tuple(blk), F32))
    res = _call(body, grid=(ncol, nrow), in_specs=in_specs, out_specs=out_specs, out_shape=out_shape, args=args, name=name)
    rg = [None] * nr
    for n, k in enumerate(rg_idx):
        rg[k] = res[n]
    pg = [None] * npar
    for n, k in enumerate(pg_idx):
        pg[k] = res[len(rg_idx) + n]
    return rg, pg


def _sigmoid(x):
    return 1.0 / (1.0 + jnp.exp(-x))


def _rms(x, g):
    return x * lax.rsqrt(jnp.mean(x * x, axis=-1, keepdims=True) + EPS) * g


def _f_normmod(x, g, sc, sh):
    return (_rms(x, g) * (1.0 + sc) + sh,)


def _f_resid(x, y, gate, g):
    return (x + gate * _rms(y, g),)


def _f_merge(ga, gb, ya, yb):
    return (_sigmoid(ga) * ya + _sigmoid(gb) * yb,)


def _f_glapost(o, r, g):
    return (_rms(o, g) * (r * _sigmoid(r)),)


def _roll_rows(x, k):
    return pltpu.roll(x, k % x.shape[0], 0)


@functools.partial(jax.custom_vjp, nondiff_argnums=(1,))
def _shift(x, k):
    return _roll_rows(x, k)


def _shift_fwd(x, k):
    return _roll_rows(x, k), None


def _shift_bwd(k, _, g):
    return (_roll_rows(g, -k),)


_shift.defvjp(_shift_fwd, _shift_bwd)


def _taps(w, center):
    return center * w + (1.0 - center) * lax.stop_gradient(w)


def _conv3(u, w0, w1, w2, center):
    return (_taps(w0, center) * _shift(u, 2) + _taps(w1, center) * _shift(u, 1)) + _taps(w2, center) * u


def _f_convmix(cb, cc, cx, w0, w1, w2, center):
    return (cb * _conv3(cc * cx, w0, w1, w2, center),)


def _gelu_tanh(x):
    return 0.5 * x * (1.0 + jnp.tanh(math.sqrt(2.0 / math.pi) * (x + 0.044715 * (x * x * x))))


def _f_ffn(gate, up, w0, w1, w2, center):
    return (_gelu_tanh(_conv3(gate, w0, w1, w2, center)) * up,)


@jax.custom_vjp
def _bdot(a, b):
    return jnp.dot(a.astype(BF16), b.astype(BF16), preferred_element_type=F32)


def _bdot_fwd(a, b):
    return _bdot(a, b), (a, b)


def _bdot_bwd(res, g):
    a, b = res
    gb = g.astype(BF16)
    da = lax.dot_general(gb, b.astype(BF16), (((1,), (1,)), ((), ())), preferred_element_type=F32)
    db = lax.dot_general(a.astype(BF16), gb, (((0,), (0,)), ((), ())), preferred_element_type=F32)
    return da, db


_bdot.defvjp(_bdot_fwd, _bdot_bwd)


@jax.custom_vjp
def _log_sigmoid(z):
    return jnp.minimum(z, 0.0) - jnp.log(1.0 + jnp.exp(-jnp.abs(z)))


def _log_sigmoid_fwd(z):
    return _log_sigmoid(z), z


def _log_sigmoid_bwd(z, g):
    return (g * _sigmoid(-z),)


_log_sigmoid.defvjp(_log_sigmoid_fwd, _log_sigmoid_bwd)


def _f_loga(lr, w, b):
    return (_log_sigmoid(_bdot(lr, w) + b) / GLA_TAU,)


def _split3(x):
    hi = x.astype(BF16)
    r1 = x - hi.astype(F32)
    mid = r1.astype(BF16)
    lo = (r1 - mid.astype(F32)).astype(BF16)
    return hi, mid, lo


def _dot_exact(a01, x, dims):
    hi, mid, lo = _split3(x)
    d = lambda p: lax.dot_general(a01, p, dims, preferred_element_type=F32)
    return (d(lo) + d(mid)) + d(hi)


def _dot_exact_lhs(x, a01, dims):
    hi, mid, lo = _split3(x)
    d = lambda p: lax.dot_general(p, a01, dims, preferred_element_type=F32)
    return (d(lo) + d(mid)) + d(hi)


NN = (((1,), (0,)), ((), ()))
NT = (((1,), (1,)), ((), ()))
TN = (((0,), (0,)), ((), ()))


def _bf_dot(a, b, dims):
    return lax.dot_general(a.astype(BF16), b.astype(BF16), dims, preferred_element_type=F32)


def _chunk_decay(a, dv):
    L = a.shape[0]
    r = lax.broadcasted_iota(jnp.int32, (L, L), 0)
    c = lax.broadcasted_iota(jnp.int32, (L, L), 1)
    tri = (r >= c).astype(BF16)
    cum = _dot_exact(tri, a, NN)
    ce = cum[L - 1:L, :]
    dec = jnp.exp(_dot_exact_lhs(a, jnp.ones((L, dv), BF16), TN))
    return cum, ce, jnp.exp(ce - cum), dec


def _gla_fwd(P, a, *, qk, vw, name, rides=None, us=0.0):
    S = P.shape[0]
    nc = S // CHUNK
    dk, dv = qk // HEADS, vw // HEADS
    scale = dk ** -0.5

    def body(q_ref, k_ref, v_ref, a_ref, o_ref, st_ref, state):
        @pl.when(pl.program_id(0) == 0)
        def _():
            state[...] = jnp.zeros_like(state)

        for h in range(HEADS):
            ks, vs = slice(h * dk, (h + 1) * dk), slice(h * dv, (h + 1) * dv)
            _, _, E, dec = _chunk_decay(a_ref[:, ks], dv)
            kd = k_ref[:, ks] * E
            s_new = dec * state[h] + _bf_dot(kd, v_ref[:, vs], TN)
            state[h] = s_new
            st_ref[0, h] = s_new
            o_ref[:, vs] = _bf_dot(q_ref[:, ks] * scale, s_new, NN).astype(o_ref.dtype)

    return _call(
        body, grid=(nc,),
        in_specs=[pl.BlockSpec((CHUNK, qk), lambda i: (i, 0)),
                  pl.BlockSpec((CHUNK, qk), lambda i: (i, 1)),
                  pl.BlockSpec((CHUNK, vw), lambda i: (i, 2 * qk // vw)),
                  pl.BlockSpec((CHUNK, qk), lambda i: (i, 0))],
        out_specs=[pl.BlockSpec((CHUNK, vw), lambda i: (i, 0)),
                   pl.BlockSpec((1, HEADS, dk, dv), lambda i: (i, 0, 0, 0))],
        out_shape=[jax.ShapeDtypeStruct((S, vw), BF16), jax.ShapeDtypeStruct((nc, HEADS, dk, dv), F32)],
        scratch_shapes=[pltpu.VMEM((HEADS, dk, dv), F32)], args=(P, P, P, a), name=name, rides=rides, us=us)


def _gla_bwd(P, a, states, do, *, qk, vw, name):
    S = P.shape[0]
    nc = S // CHUNK
    dk, dv = qk // HEADS, vw // HEADS
    scale = dk ** -0.5

    def body(q_ref, k_ref, v_ref, a_ref, do_ref, sc_ref, sp_ref, dq_ref, dk_ref, dv_ref, da_ref, ds_carry):
        i = pl.program_id(0)

        @pl.when(i == 0)
        def _():
            ds_carry[...] = jnp.zeros_like(ds_carry)

        has_prev = (i < nc - 1).astype(F32)
        L = CHUNK
        r = lax.broadcasted_iota(jnp.int32, (L, L), 0)
        c = lax.broadcasted_iota(jnp.int32, (L, L), 1)
        tri_t = (c >= r).astype(BF16)
        last = (lax.broadcasted_iota(jnp.int32, (L, 1), 0) == L - 1).astype(F32)
        for h in range(HEADS):
            ks, vs = slice(h * dk, (h + 1) * dk), slice(h * dv, (h + 1) * dv)
            _, _, E, dec = _chunk_decay(a_ref[:, ks], dv)
            kk, vv, dov = k_ref[:, ks], v_ref[:, vs], do_ref[:, vs]
            kd = kk * E
            ds = ds_carry[h] + _bf_dot(q_ref[:, ks] * scale, dov, TN)
            dq_ref[:, ks] = (_bf_dot(dov, sc_ref[0, h], NT) * scale).astype(dq_ref.dtype)
            dkd = _bf_dot(vv, ds, NT)
            dv_ref[:, vs] = _bf_dot(kd, ds, NN).astype(dv_ref.dtype)
            dk_ref[:, ks] = (dkd * E).astype(dk_ref.dtype)
            darg = dkd * kd
            w = ds * (sp_ref[0, h] * has_prev) * dec
            dce = _dot_exact(jnp.ones((SUBLANES, dv), BF16), w, NT)[0:1, :]
            dce = dce + jnp.sum(darg, axis=0, keepdims=True)
            dcum = last * dce - darg
            da_ref[:, ks] = _dot_exact(tri_t, dcum, NN)
            ds_carry[h] = dec * ds

    rev = lambda i: nc - 1 - i
    return pl.pallas_call(
        body, grid=(nc,),
        in_specs=[pl.BlockSpec((CHUNK, qk), lambda i: (rev(i), 0)),
                  pl.BlockSpec((CHUNK, qk), lambda i: (rev(i), 1)),
                  pl.BlockSpec((CHUNK, vw), lambda i: (rev(i), 2 * qk // vw)),
                  pl.BlockSpec((CHUNK, qk), lambda i: (rev(i), 0)),
                  pl.BlockSpec((CHUNK, vw), lambda i: (rev(i), 0)),
                  pl.BlockSpec((1, HEADS, dk, dv), lambda i: (rev(i), 0, 0, 0)),
                  pl.BlockSpec((1, HEADS, dk, dv), lambda i: (jnp.maximum(rev(i) - 1, 0), 0, 0, 0))],
        out_specs=[pl.BlockSpec((CHUNK, qk), lambda i: (rev(i), 0)),
                   pl.BlockSpec((CHUNK, qk), lambda i: (rev(i), 0)),
                   pl.BlockSpec((CHUNK, vw), lambda i: (rev(i), 0)),
                   pl.BlockSpec((CHUNK, qk), lambda i: (rev(i), 0))],
        out_shape=[jax.ShapeDtypeStruct((S, qk), BF16), jax.ShapeDtypeStruct((S, qk), BF16),
                   jax.ShapeDtypeStruct((S, vw), BF16), jax.ShapeDtypeStruct((S, qk), F32)],
        scratch_shapes=[pltpu.VMEM((HEADS, dk, dv), F32)],
        compiler_params=_params(("arbitrary",)), name=name)(P, P, P, a, do, states, states)


def _loss_head(y, target, *, tr, name):
    S, D = y.shape
    tr = min(tr, S)

    def body(y_ref, t_ref, l_ref, dy_ref):
        @pl.when(pl.program_id(0) == 0)
        def _():
            l_ref[...] = jnp.zeros_like(l_ref)

        e = y_ref[...] - t_ref[...]
        dy_ref[...] = e / D
        l_ref[...] += 0.5 * jnp.sum(jnp.mean(e * e, axis=-1, keepdims=True), axis=0, keepdims=True)

    loss, dy = pl.pallas_call(
        body, grid=(S // tr,),
        in_specs=[pl.BlockSpec((tr, D), lambda i: (i, 0))] * 2,
        out_specs=[pl.BlockSpec((SUBLANES, LANES), lambda i: (0, 0)), pl.BlockSpec((tr, D), lambda i: (i, 0))],
        out_shape=[jax.ShapeDtypeStruct((SUBLANES, LANES), F32), jax.ShapeDtypeStruct((S, D), F32)],
        compiler_params=_params(("arbitrary",)), name=name)(y, target)
    return loss[0, 0], dy


ADAMW_BYTES_PER_US = 2.5e6


def _adamw(parts, w, m, v, *, name, rides=None):
    G, R, C = w.shape
    n = parts[0].shape[0]
    unit = 2 * SUBLANES if parts[0].dtype == BF16 else SUBLANES
    tr = _pick(R, max(unit, (128 * 1024) // C), unit)
    nblk = R // tr
    bc1 = 1.0 - ADAM_B1 ** ADAM_STEP
    bc2 = 1.0 - ADAM_B2 ** ADAM_STEP

    def body(*refs):
        p_refs = refs[:G]
        w_ref, m_ref, v_ref, g_ref, d_ref, mo_ref, vo_ref = refs[G:]
        for q in range(G):
            @pl.when(pl.program_id(0) == q)
            def _(q=q):
                g = p_refs[q][0].astype(F32)
                for s in range(1, n):
                    g = g + p_refs[q][s].astype(F32)
                m2 = ADAM_B1 * m_ref[...] + (1.0 - ADAM_B1) * g
                v2 = ADAM_B2 * v_ref[...] + (1.0 - ADAM_B2) * (g * g)
                g_ref[...] = g
                mo_ref[...] = m2
                vo_ref[...] = v2
                d_ref[...] = -ADAM_LR * ((m2 / bc1) / (jnp.sqrt(v2 / bc2) + ADAM_EPS) + ADAM_WD * w_ref[...])

    def part_spec(q):
        return pl.BlockSpec((n, tr, C), lambda g, i: (0, jnp.where(g == q, i, jnp.where(g < q, 0, nblk - 1)), 0))

    blk = pl.BlockSpec((None, tr, C), lambda g, i: (g, i, 0))
    traffic = G * R * C * (n * parts[0].dtype.itemsize + 7 * 4)
    return _call(
        body, grid=(G, nblk), in_specs=[part_spec(q) for q in range(G)] + [blk, blk, blk],
        out_specs=[blk] * 4, out_shape=[jax.ShapeDtypeStruct((G, R, C), F32)] * 4,
        args=(*parts, w, m, v), name=name, rides=rides, us=traffic / ADAMW_BYTES_PER_US)


def _adamw_nd(parts, w, m, v, *, name):
    shape = w.shape
    C = shape[-1]
    R = math.prod(shape[:-1])
    outs = _adamw([parts.reshape(parts.shape[0], R, C)], w.reshape(1, R, C), m.reshape(1, R, C), v.reshape(1, R, C),
                  name=name)
    return [o.reshape(shape) for o in outs]


def _all_gather(arrs, *, name):
    n = len(arrs)
    HBM = pl.BlockSpec(memory_space=pl.ANY)

    def body(*refs):
        ins, outs = refs[:n], refs[n:2 * n]
        send_sems, recv_sems, local_sems = refs[2 * n:]
        x, y, c = _position()
        me, sibling = (x, y, c), (x, y, 1 - c)
        chips = [(1 - x, y), (x, 1 - y), (1 - x, 1 - y)]

        def copy(t, k, block, to, src=None):
            rows = outs[t].at[_index(block)]
            return pltpu.make_async_remote_copy(
                src_ref=rows if src is None else src, dst_ref=rows,
                send_sem=send_sems.at[t, k], recv_sem=recv_sems.at[t, k],
                device_id=to, device_id_type=pl.DeviceIdType.MESH)

        started = []
        mine = []
        for t in range(n):
            cp = pltpu.make_async_copy(ins[t], outs[t].at[_index(me)], local_sems.at[t])
            cp.start()
            mine.append(cp)
            first = [copy(t, 0, me, sibling, src=ins[t])]
            first += [copy(t, 1 + j, me, (*chip, c), src=ins[t]) for j, chip in enumerate(chips)]
            for cp in first:
                cp.start()
            started += first
        for t in range(n):
            for j, chip in enumerate(chips):
                copy(t, 1 + j, (*chip, c), me).wait_recv()
                cp = copy(t, 4 + j, (*chip, c), sibling)
                cp.start()
                started.append(cp)
        for t in range(n):
            copy(t, 0, sibling, me).wait_recv()
            for j, chip in enumerate(chips):
                copy(t, 4 + j, (*chip, 1 - c), me).wait_recv()
        for cp in started:
            cp.wait_send()
        for cp in mine:
            cp.wait()

    return pl.pallas_call(
        body, in_specs=[HBM] * n, out_specs=[HBM] * n,
        out_shape=[jax.ShapeDtypeStruct((NDEV,) + a.shape, a.dtype) for a in arrs],
        scratch_shapes=[pltpu.SemaphoreType.DMA((n, 7)), pltpu.SemaphoreType.DMA((n, 7)),
                        pltpu.SemaphoreType.DMA((n,))],
        name=name)(*arrs)


class _Pack:
    def __init__(self, arrs):
        self.shapes = [a.shape for a in arrs]
        self.sizes = [math.prod(s) for s in self.shapes]
        total = sum(self.sizes)
        unit = SUBLANES * LANES
        self.padded = -(-total // unit) * unit
        flat = jnp.concatenate([a.reshape(-1).astype(F32) for a in arrs] + [jnp.zeros((self.padded - total,), F32)])
        self.packed = flat.reshape(self.padded // LANES, LANES)

    def unpack(self, gathered):
        flat = gathered.reshape(NDEV, self.padded)
        out, off = [], 0
        for shape, size in zip(self.shapes, self.sizes):
            out.append(flat[:, off:off + size].reshape((NDEV,) + tuple(shape)))
            off += size
        return out


def _cols_from_shards(g):
    g = jnp.moveaxis(g, 0, -2)
    return g.reshape(g.shape[:-2] + (g.shape[-2] * g.shape[-1],))


def _cols_to_shards(w):
    K, N = w.shape
    return jnp.moveaxis(w.reshape(K, NDEV, N // NDEV), 1, 0)


def kernel(x, c, w_ada, b_ada, norm_g, w_in, w_a2, b_a2, gla_norm_g, w_out_gla, conv_mix_w, w_out_conv, w_o, w_up, ffn_conv_w, w_down, loss_target, m_w_ada, m_b_ada, m_norm_g, m_w_in, m_w_a2, m_b_a2, m_gla_norm_g, m_w_out_gla, m_conv_mix_w, m_w_out_conv, m_w_o, m_w_up, m_ffn_conv_w, m_w_down, v_w_ada, v_b_ada, v_norm_g, v_w_in, v_w_a2, v_b_a2, v_gla_norm_g, v_w_out_gla, v_conv_mix_w, v_w_out_conv, v_w_o, v_w_up, v_ffn_conv_w, v_w_down):
    depth = w_ada.shape[0]
    S, D = x.shape[1], x.shape[2]
    QK = w_a2.shape[2] * NDEV
    LR = w_a2.shape[1]
    VW = w_out_gla.shape[1]
    CW = w_out_conv.shape[1]
    FF = w_down.shape[1] * NDEV
    NIN = w_in.shape[2] * NDEV
    NADA = w_ada.shape[2]
    LRP = -(-LR // LANES) * LANES
    off_lr = 2 * QK + 2 * VW
    NP = NIN - LR + LRP
    assert NIN == 2 * QK + 2 * VW + LR + 3 * CW + 2 * D
    xi, yi, ci = _position()
    me = _index((xi, yi, ci))
    xs = x.reshape(S, D)
    target = loss_target.reshape(S, D)
    TR = 256
    TRC = 512
    TC = _pick(CW, 512)
    TCF = _pick(FF, 512)

    small = _Pack([c, norm_g, w_a2, conv_mix_w, ffn_conv_w])
    big_names = ["w_in", "w_out_gla", "w_out_conv", "w_o", "w_up", "w_down"]
    big = dict(w_in=w_in, w_out_gla=w_out_gla, w_out_conv=w_out_conv, w_o=w_o, w_up=w_up, w_down=w_down)

    def matrix(key, g):
        if key == "w_up":
            return g
        if key in ("w_o", "w_down"):
            return g.reshape(g.shape[0] * g.shape[1], g.shape[2])
        if key != "w_in":
            return _cols_from_shards(g)
        W_in = _cols_from_shards(g)
        return jnp.concatenate([W_in[:, :off_lr], W_in[:, off_lr + LR:], W_in[:, off_lr:off_lr + LR],
                                jnp.zeros((D, LRP - LR), BF16)], axis=-1)

    (small_g, w_in0_g) = _all_gather([small.packed, w_in[0].astype(BF16)], name="gather_first")
    c_all, norm_g_s, w_a2_s, conv_w_s, ffn_w_s = small.unpack(small_g)
    c_all = c_all.reshape(NDEV, D)
    norm_g_f = _cols_from_shards(norm_g_s)
    w_a2_f = _cols_from_shards(w_a2_s)
    conv_w_f = _cols_from_shards(conv_w_s)
    ffn_w_f = _cols_from_shards(ffn_w_s)
    w_a2_p = jnp.concatenate([w_a2_f, jnp.zeros((depth, LRP - LR, QK), F32)], axis=1)

    rides = _Rides()
    GATHER_PARAMS_PER_US = 85e3
    FORWARD_PARAMS_PER_US = 800e3
    EXCHANGE_PARAMS_PER_US = 45e3
    weight_streams = {}
    for l in range(depth):
        for key in big_names:
            if (l, key) == (0, "w_in"):
                continue
            shard = big[key][l].astype(BF16)
            rows, cols = shard.shape
            first = rides.add(_Stream("gather_a", shard, rows, cols, GATHER_PARAMS_PER_US / (NDEV * cols)))
            second = rides.add(_Stream("gather_b", None, rows, cols, FORWARD_PARAMS_PER_US / (NDEV * cols), after=first))
            weight_streams[l, key] = (first, second)
    W = {(0, "w_in"): matrix("w_in", w_in0_g)}

    def weight(l, key):
        if (l, key) not in W:
            first, second = weight_streams[l, key]
            rides.need(first, second)
            W[l, key] = matrix(key, first.buf)
        return W[l, key]

    MP = LANES
    c_pad = jnp.concatenate([c_all, jnp.zeros((MP - NDEV, D), F32)], axis=0)
    (silu_c,) = _rw_fwd(lambda t: (t * _sigmoid(t),), [_v(c_pad)], [], [(D, BF16)], tr=MP, name="silu_c")
    mod_cols = [_mm(silu_c, w_ada[l], name=f"ada_fwd{l}")[:NDEV] for l in range(depth)]
    (mod_g,) = _all_gather([jnp.stack(mod_cols)], name="gather_mod")
    mod_all = jnp.moveaxis(mod_g, 0, 2).reshape(depth, NDEV, NDEV * NADA)
    mod_me = lax.dynamic_slice_in_dim(mod_all, me, 1, axis=1).reshape(depth, NDEV * NADA)
    (mod_me,) = _rw_fwd(lambda a, b: (a + b,), [_v(mod_me), _v(b_ada)], [], [(NDEV * NADA, F32)],
                        tr=depth, name="mod_bias")

    col = {}
    off = 0
    for nm, wd in (("q", QK), ("k", QK), ("v", VW), ("r", VW), ("cb", CW), ("cc", CW), ("cx", CW),
                   ("ga", D), ("gb", D), ("lr", LRP)):
        col[nm] = off
        off += wd

    saved = []
    xin = xs
    for l in range(depth):
        sh1, sc1, g1, sh2, sc2, g2 = [mod_me[l:l + 1, i * D:(i + 1) * D] for i in range(6)]
        ng = [norm_g_f[l, i:i + 1] for i in range(4)]
        cw = [conv_w_f[l, i:i + 1] for i in range(CONV_K)]
        fw = [ffn_w_f[l, i:i + 1] for i in range(CONV_K)]
        gn = gla_norm_g[l:l + 1]
        ba2 = b_a2[l:l + 1]
        def proj(act, key, name, out_dtype=BF16):
            return _mm(act, weight(l, key), name=name, rides=rides, out_dtype=out_dtype)

        (h1,) = _rw_fwd(_f_normmod, [_v(xin)], [_v(ng[0]), _v(sc1), _v(sh1)], [(D, BF16)], tr=TR, name=f"normmod1_{l}")
        P = proj(h1, "w_in", f"in_proj{l}")
        (a,) = _rw_fwd(_f_loga, [_v(P, LRP, col["lr"])], [V(w_a2_p[l], QK, None), _v(ba2)], [(QK, F32)],
                       tr=TR, name=f"loga{l}")
        o, states = _gla_fwd(P, a, qk=QK, vw=VW, name=f"gla_fwd{l}", rides=rides, us=US_GLA)
        dvh = VW // HEADS
        (ya_in,) = _rw_fwd(_f_glapost, [_v(o, dvh), _v(P, dvh, col["r"])], [V(gn, dvh, None)], [(VW, BF16)],
                           tr=TR, ncol=HEADS, name=f"glapost{l}", rides=rides, us=US_GLAPOST)
        ya = proj(ya_in, "w_out_gla", f"out_gla{l}")
        (yb_in,) = _rw_fwd(_f_convmix, [_v(P, TC, col["cb"]), _v(P, TC, col["cc"]), _v(P, TC, col["cx"])],
                           [_v(w, TC) for w in cw], [(CW, BF16)], tr=TRC, ncol=CW // TC, halo=True, name=f"convmix{l}")
        yb = proj(yb_in, "w_out_conv", f"out_conv{l}")
        (mg,) = _rw_fwd(_f_merge, [_v(P, D, col["ga"]), _v(P, D, col["gb"]), _v(ya), _v(yb)], [], [(D, BF16)],
                        tr=TR, name=f"merge{l}", rides=rides, us=US_MERGE)
        y = proj(mg, "w_o", f"o_proj{l}", F32)
        (x1,) = _rw_fwd(_f_resid, [_v(xin), _v(y)], [_v(g1), _v(ng[1])], [(D, F32)], tr=TR, name=f"resid1_{l}")
        (h2,) = _rw_fwd(_f_normmod, [_v(x1)], [_v(ng[2]), _v(sc2), _v(sh2)], [(D, BF16)], tr=TR, name=f"normmod2_{l}")
        U = proj(h2, "w_up", f"up_proj{l}")
        (ff,) = _rw_fwd(_f_ffn, [_v(U, TCF, 0), _v(U, TCF, FF)], [_v(w, TCF) for w in fw], [(FF, BF16)],
                        tr=TRC, ncol=FF // TCF, halo=True, name=f"ffn{l}", rides=rides, us=US_FFN)
        y2 = proj(ff, "w_down", f"down_proj{l}", F32)
        (x2,) = _rw_fwd(_f_resid, [_v(x1), _v(y2)], [_v(g2), _v(ng[3])], [(D, F32)], tr=TR, name=f"resid2_{l}")
        saved.append(dict(xin=xin, h1=h1, P=P, a=a, o=o, states=states, ya_in=ya_in, ya=ya, yb_in=yb_in, yb=yb,
                          mg=mg, y=y, x1=x1, h2=h2, U=U, ff=ff, y2=y2, mods=(sh1, sc1, g1, sh2, sc2, g2),
                          ng=ng, cw=cw, fw=fw, gn=gn, ba2=ba2))
        xin = x2

    loss_local, dx = _loss_head(xin, target, tr=TR, name="loss_head")
    loss = lax.psum(loss_local, MESH_AXES)

    grad_streams = {}

    def send(l, key, blocks):
        _, rows, cols = blocks.shape
        grad_streams[l, key] = rides.add(_Stream("a2a", blocks, rows, cols, EXCHANGE_PARAMS_PER_US / (NDEV * cols)))

    def bmm(a, b, **kw):
        kw.setdefault("out_dtype", BF16)
        return _mm(a, b, rides=rides, **kw)

    gsmall = [None] * depth
    dmod = [None] * depth
    for l in reversed(range(depth)):
        s = saved[l]
        sh1, sc1, g1, sh2, sc2, g2 = s["mods"]
        ng, cw, fw = s["ng"], s["cw"], s["fw"]
        (dx1_a, dy2), (dg2, dng3) = _rw_bwd(
            _f_resid, [_v(s["x1"]), _v(s["y2"])], [_v(g2), _v(ng[3])], [_v(dx)],
            row_grads=[F32, BF16], par_grads=[True, True], tr=TR, name=f"resid2_bwd{l}")
        gW_down = bmm(s["ff"], dy2, ta=True, out_dtype=BF16, name=f"down_wgrad{l}")
        send(l, "w_down", gW_down.reshape(NDEV, FF // NDEV, D))
        dff = bmm(dy2, weight(l, "w_down"), tb=True, name=f"down_dgrad{l}")
        U = s["U"]
        (dgate, dup), dfw = _rw_bwd(
            _f_ffn, [_v(U, TCF, 0), _v(U, TCF, FF)], [_v(w, TCF) for w in fw], [_v(dff, TCF)],
            row_grads=[BF16, BF16], par_grads=[True] * 3, tr=TRC, ncol=FF // TCF, halo=True, name=f"ffn_bwd{l}")
        dU = jnp.concatenate([dgate, dup], axis=1)
        gW_up = bmm(s["h2"], dU, ta=True, out_dtype=BF16, out_shards=True, name=f"up_wgrad{l}")
        send(l, "w_up", gW_up)
        dh2 = bmm(dU, weight(l, "w_up"), tb=True, name=f"up_dgrad{l}")
        (dx1,), (dng2, dsc2, dsh2) = _rw_bwd(
            _f_normmod, [_v(s["x1"])], [_v(ng[2]), _v(sc2), _v(sh2)], [_v(dh2)],
            row_grads=[F32], par_grads=[True] * 3, adds={0: dx1_a}, tr=TR, name=f"normmod2_bwd{l}")
        (dxin_a, dy), (dg1, dng1) = _rw_bwd(
            _f_resid, [_v(s["xin"]), _v(s["y"])], [_v(g1), _v(ng[1])], [_v(dx1)],
            row_grads=[F32, BF16], par_grads=[True, True], tr=TR, name=f"resid1_bwd{l}")
        gW_o = bmm(s["mg"], dy, ta=True, out_dtype=BF16, name=f"o_wgrad{l}")
        send(l, "w_o", gW_o.reshape(NDEV, D // NDEV, D))
        dmg = bmm(dy, weight(l, "w_o"), tb=True, name=f"o_dgrad{l}")
        P = s["P"]
        (dga, dgb, dya, dyb), _ = _rw_bwd(
            _f_merge, [_v(P, D, col["ga"]), _v(P, D, col["gb"]), _v(s["ya"]), _v(s["yb"])], [], [_v(dmg)],
            row_grads=[BF16] * 4, par_grads=[], tr=TR, name=f"merge_bwd{l}")
        gW_og = bmm(s["ya_in"], dya, ta=True, out_dtype=BF16, name=f"out_gla_wgrad{l}")
        send(l, "w_out_gla", _cols_to_shards(gW_og))
        dya_in = bmm(dya, weight(l, "w_out_gla"), tb=True, name=f"out_gla_dgrad{l}")
        gW_oc = bmm(s["yb_in"], dyb, ta=True, out_dtype=BF16, name=f"out_conv_wgrad{l}")
        send(l, "w_out_conv", _cols_to_shards(gW_oc))
        dyb_in = bmm(dyb, weight(l, "w_out_conv"), tb=True, name=f"out_conv_dgrad{l}")
        (dcb, dcc, dcx), dcw = _rw_bwd(
            _f_convmix, [_v(P, TC, col["cb"]), _v(P, TC, col["cc"]), _v(P, TC, col["cx"])],
            [_v(w, TC) for w in cw], [_v(dyb_in, TC)],
            row_grads=[BF16] * 3, par_grads=[True] * 3, tr=TRC, ncol=CW // TC, halo=True, name=f"convmix_bwd{l}")
        dvh = VW // HEADS
        (do, dr), (dgn,) = _rw_bwd(
            _f_glapost, [_v(s["o"], dvh), _v(P, dvh, col["r"])], [V(s["gn"], dvh, None)], [_v(dya_in, dvh)],
            row_grads=[BF16, BF16], par_grads=[True], tr=TR, ncol=HEADS, name=f"glapost_bwd{l}")
        dq, dk, dv, da = _gla_bwd(P, s["a"], s["states"], do, qk=QK, vw=VW, name=f"gla_bwd{l}")
        (dlr,), (dwa2, dba2) = _rw_bwd(
            _f_loga, [_v(P, LRP, col["lr"])], [V(w_a2_p[l], QK, None), _v(s["ba2"])], [_v(da)],
            row_grads=[BF16], par_grads=[True, True], tr=TR, name=f"loga_bwd{l}")
        dP = jnp.concatenate([dq, dk, dv, dr, dcb, dcc, dcx, dga, dgb, dlr], axis=1)
        gW_all = bmm(s["h1"], dP, ta=True, out_dtype=BF16, name=f"in_wgrad{l}")
        gW_in = jnp.concatenate([gW_all[:, :off_lr], gW_all[:, NIN - LR:NIN], gW_all[:, off_lr:NIN - LR]], axis=1)
        send(l, "w_in", _cols_to_shards(gW_in))
        dh1 = bmm(dP, weight(l, "w_in"), tb=True, name=f"in_dgrad{l}")
        (dx,), (dng0, dsc1, dsh1) = _rw_bwd(
            _f_normmod, [_v(s["xin"])], [_v(ng[0]), _v(sc1), _v(sh1)], [_v(dh1)],
            row_grads=[F32], par_grads=[True] * 3, adds={0: dxin_a}, tr=TR, name=f"normmod1_bwd{l}")
        vec = lambda t: t.reshape(1, -1)
        gsmall[l] = dict(
            norm_g=jnp.concatenate([vec(dng0), vec(dng1), vec(dng2), vec(dng3)], axis=0),
            w_a2=dwa2[0, :LR], b_a2=dba2.reshape(QK), gla_norm_g=jnp.sum(dgn, axis=0).reshape(dvh),
            conv_mix_w=jnp.concatenate([vec(t) for t in dcw], axis=0),
            ffn_conv_w=jnp.concatenate([vec(t) for t in dfw], axis=0))
        dmod[l] = jnp.concatenate([vec(t) for t in (dsh1, dsc1, dg1, dsh2, dsc2, dg2)], axis=1).reshape(-1)
    grad_x = dx.reshape(x.shape)

    stack = lambda key, src: jnp.stack([src[l][key] for l in range(depth)])
    small_names = ["norm_g", "w_a2", "b_a2", "gla_norm_g", "conv_mix_w", "ffn_conv_w"]
    gpack = _Pack([jnp.stack(dmod)] + [stack(k, gsmall) for k in small_names])
    (gsm,) = _all_gather([gpack.packed], name="gather_small_grads")
    gs = gpack.unpack(gsm)
    dmod_all = gs[0]
    parts_small = dict(zip(small_names, gs[1:]))

    def my_cols(t, n):
        return lax.dynamic_slice_in_dim(t, me * n, n, axis=t.ndim - 1)

    gW_ada = []
    for l in range(depth):
        dcols = my_cols(dmod_all[:, l], NADA)
        dcols = jnp.concatenate([dcols, jnp.zeros((MP - NDEV, NADA), F32)], axis=0)
        gW_ada.append(bmm(silu_c, dcols, ta=True, out_dtype=F32, name=f"ada_wgrad{l}")[None])
    parts = {}
    parts["b_ada"] = dmod_all
    parts["norm_g"] = my_cols(parts_small["norm_g"], D // NDEV)
    parts["w_a2"] = my_cols(parts_small["w_a2"], QK // NDEV)
    parts["b_a2"] = parts_small["b_a2"]
    parts["gla_norm_g"] = parts_small["gla_norm_g"]
    parts["conv_mix_w"] = my_cols(parts_small["conv_mix_w"], CW // NDEV)
    parts["ffn_conv_w"] = my_cols(parts_small["ffn_conv_w"], FF // NDEV)

    weights = dict(w_ada=(w_ada, m_w_ada, v_w_ada), b_ada=(b_ada, m_b_ada, v_b_ada),
                   norm_g=(norm_g, m_norm_g, v_norm_g), w_in=(w_in, m_w_in, v_w_in), w_a2=(w_a2, m_w_a2, v_w_a2),
                   b_a2=(b_a2, m_b_a2, v_b_a2), gla_norm_g=(gla_norm_g, m_gla_norm_g, v_gla_norm_g),
                   w_out_gla=(w_out_gla, m_w_out_gla, v_w_out_gla), conv_mix_w=(conv_mix_w, m_conv_mix_w, v_conv_mix_w),
                   w_out_conv=(w_out_conv, m_w_out_conv, v_w_out_conv), w_o=(w_o, m_w_o, v_w_o),
                   w_up=(w_up, m_w_up, v_w_up), ffn_conv_w=(ffn_conv_w, m_ffn_conv_w, v_ffn_conv_w),
                   w_down=(w_down, m_w_down, v_w_down))
    done = {}
    for nm in ["w_ada", "w_down", "w_up", "w_o", "w_out_gla", "w_out_conv", "w_in"]:
        w, m, v = weights[nm]
        if nm == "w_ada":
            mine = gW_ada
        else:
            streams = [grad_streams[l, nm] for l in range(depth)]
            rides.need(*streams)
            mine = [s.buf for s in streams]
        done[nm] = _adamw(mine, w, m, v, name=f"adamw_{nm}", rides=rides)
    for nm, (w, m, v) in weights.items():
        if nm not in done:
            done[nm] = _adamw_nd(parts[nm], w, m, v, name=f"adamw_{nm}")
    grads, deltas, new_m, new_v = zip(*[done[nm] for nm in weights])
    return (loss, grad_x, *grads, *deltas, *new_m, *new_v)
```

```python
import functools
import math
from typing import Any, NamedTuple

import jax
import jax.numpy as jnp
from jax import lax
from jax.experimental import pallas as pl
from jax.experimental.pallas import tpu as pltpu

F32 = jnp.float32
BF16 = jnp.bfloat16
MESH_AXES = ("x", "y", "c")
NDEV = 8
EPS = 1e-6
CHUNK = 64
HEADS = 4
GLA_TAU = 16.0
CONV_K = 3
ADAM_LR = 0.001
ADAM_B1 = 0.9
ADAM_B2 = 0.999
ADAM_EPS = 1e-08
ADAM_WD = 0.01
ADAM_STEP = 10

LANES = 128
SUBLANES = 8
HALO = 2 * SUBLANES
VMEM_LIMIT = 56 * 1024 * 1024


def _params(sem=None):
    return pltpu.CompilerParams(dimension_semantics=sem, vmem_limit_bytes=VMEM_LIMIT)


def _pick(n, cap, unit=LANES):
    best = None
    for d in range(unit, min(n, cap) + 1, unit):
        if n % d == 0:
            best = d
    return best if best is not None else n


def _position():
    return lax.axis_index("x"), lax.axis_index("y"), lax.axis_index("c")


def _index(p):
    return 4 * p[0] + 2 * p[1] + p[2]


def _peer(k):
    x, y, c = _position()
    return ((1 - x) if k & 4 else x, (1 - y) if k & 2 else y, (1 - c) if k & 1 else c)


ROW_QUANTUM = 2 * SUBLANES
MIN_ROWS = 128


class _Stream:
    def __init__(self, mode, src, rows, cols, rate, after=None):
        self.mode, self.src, self.R, self.rate, self.after = mode, src, rows, rate, after
        self.shape = (NDEV, rows, cols)
        self.holder = self if after is None else after
        self.buf = None
        self.done = 0


class _Exchange:
    def __init__(self, items):
        self.items = items
        self.ins, self.src_pos, self.alias, self.holders, self.out_pos = [], [], {}, [], []
        for s, _, _ in items:
            if not any(s.holder is h for h in self.holders):
                self.holders.append(s.holder)
        for s, _, _ in items:
            self.out_pos.append([n for n, h in enumerate(self.holders) if h is s.holder][0])
            if s.mode == "gather_b":
                self.src_pos.append(None)
            else:
                self.src_pos.append(len(self.ins))
                self.ins.append(s.src)
        for n, h in enumerate(self.holders):
            if h.buf is not None:
                self.alias[len(self.ins)] = n
                self.ins.append(h.buf)
        self.out_shape = [jax.ShapeDtypeStruct(h.shape, BF16) for h in self.holders]
        n = len(items)
        self.scratch = [pltpu.SemaphoreType.DMA((n, NDEV)), pltpu.SemaphoreType.DMA((n, NDEV)),
                        pltpu.SemaphoreType.DMA((n,))]

    def deliver(self, outs):
        for h, o in zip(self.holders, outs):
            h.buf = o

    def _copies(self, ins, outs, sems):
        send_sems, recv_sems, local_sems = sems
        me = _index(_position())
        slot = lambda k: _index(_peer(k))
        local, sends, arrivals = [], [], []
        for n, (s, r0, rc) in enumerate(self.items):
            out, rows = outs[self.out_pos[n]], pl.ds(r0, rc)
            src = out if self.src_pos[n] is None else ins[self.src_pos[n]]
            if s.mode == "a2a":
                local.append(pltpu.make_async_copy(src.at[me, rows], out.at[me, rows], local_sems.at[n]))
                moves = [(k, src.at[slot(k), rows], out.at[me, rows], _peer(k), out.at[slot(k), rows])
                         for k in range(1, NDEV)]
            elif s.mode == "gather_a":
                local.append(pltpu.make_async_copy(src.at[rows], out.at[me, rows], local_sems.at[n]))
                moves = [(k, src.at[rows], out.at[me, rows], _peer(k), out.at[slot(k), rows]) for k in (1, 2, 4, 6)]
            else:
                moves = [(k, src.at[slot(k), rows], out.at[slot(k), rows], _peer(1), out.at[slot(k ^ 1), rows])
                         for k in (2, 4, 6)]
            for k, src_ref, dst_ref, to, landing in moves:
                common = dict(send_sem=send_sems.at[n, k], recv_sem=recv_sems.at[n, k],
                              device_id=to, device_id_type=pl.DeviceIdType.MESH)
                sends.append(pltpu.make_async_remote_copy(src_ref=src_ref, dst_ref=dst_ref, **common))
                arrivals.append(pltpu.make_async_remote_copy(src_ref=landing, dst_ref=landing, **common))
        return local, sends, arrivals

    def start(self, ins, outs, sems):
        local, sends, _ = self._copies(ins, outs, sems)
        for cp in local + sends:
            cp.start()

    def wait(self, ins, outs, sems):
        local, sends, arrivals = self._copies(ins, outs, sems)
        for cp in arrivals:
            cp.wait_recv()
        for cp in sends:
            cp.wait_send()
        for cp in local:
            cp.wait()


class _Rides:
    def __init__(self):
        self.queue = []
        self.flushes = 0

    def add(self, stream):
        self.queue.append(stream)
        return stream

    def take(self, budget, only=None):
        ready = {id(s): (s.after.done if s.after is not None else s.R) for s in self.queue}
        items, left = [], budget
        for s in list(self.queue):
            if only is not None and not any(s is t for t in only):
                continue
            avail = ready[id(s)] - s.done
            if avail <= 0:
                continue
            rc = avail if left == float("inf") else min(avail, int(left * s.rate) // ROW_QUANTUM * ROW_QUANTUM)
            if rc < min(avail, MIN_ROWS):
                continue
            items.append((s, s.done, rc))
            s.done += rc
            left -= rc / s.rate
            if s.done == s.R:
                self.queue.remove(s)
        return items

    def need(self, *streams):
        while any(s.done < s.R for s in streams):
            items = self.take(float("inf"), only=streams)
            comm = _Exchange(items)
            hbm = pl.BlockSpec(memory_space=pl.ANY)
            nin, nout = len(comm.ins), len(comm.holders)

            def body(*refs, comm=comm, nin=nin, nout=nout):
                comm.start(refs[:nin], refs[nin:nin + nout], refs[nin + nout:])
                comm.wait(refs[:nin], refs[nin:nin + nout], refs[nin + nout:])

            outs = pl.pallas_call(body, in_specs=[hbm] * nin, out_specs=[hbm] * nout, out_shape=comm.out_shape,
                                  scratch_shapes=comm.scratch, input_output_aliases=dict(comm.alias),
                                  name=f"exchange_alone{self.flushes}")(*comm.ins)
            self.flushes += 1
            comm.deliver(outs)


def _call(body, *, grid, in_specs, out_specs, out_shape, scratch_shapes=(), args, name, rides=None, us=0.0):
    sem = ("arbitrary",) * len(grid)
    items = rides.take(us) if rides is not None else []
    if not items:
        return pl.pallas_call(body, grid=grid, in_specs=in_specs, out_specs=out_specs, out_shape=out_shape,
                              scratch_shapes=list(scratch_shapes), compiler_params=_params(sem), name=name)(*args)
    comm = _Exchange(items)
    nin, nout, nscr, cin, cout = len(in_specs), len(out_specs), len(scratch_shapes), len(comm.ins), len(comm.holders)
    hbm = pl.BlockSpec(memory_space=pl.ANY)

    def both(*refs):
        ins, c_ins = refs[:nin], refs[nin:nin + cin]
        pos = nin + cin
        outs, c_outs = refs[pos:pos + nout], refs[pos + nout:pos + nout + cout]
        pos += nout + cout
        scr, sems = refs[pos:pos + nscr], refs[pos + nscr:]
        first = functools.reduce(lambda p, q: p & q, [pl.program_id(d) == 0 for d in range(len(grid))])
        last = functools.reduce(lambda p, q: p & q, [pl.program_id(d) == grid[d] - 1 for d in range(len(grid))])

        @pl.when(first)
        def _():
            comm.start(c_ins, c_outs, sems)

        body(*ins, *outs, *scr)

        @pl.when(last)
        def _():
            comm.wait(c_ins, c_outs, sems)

    res = pl.pallas_call(both, grid=grid, in_specs=list(in_specs) + [hbm] * cin, out_specs=list(out_specs) + [hbm] * cout,
                         out_shape=list(out_shape) + comm.out_shape, scratch_shapes=list(scratch_shapes) + comm.scratch,
                         input_output_aliases={nin + i: nout + o for i, o in comm.alias.items()},
                         compiler_params=_params(sem), name=name)(*args, *comm.ins)
    comm.deliver(res[nout:])
    return res[:nout]


US_GLA, US_GLAPOST, US_MERGE, US_FFN = 100.0, 40.0, 45.0, 110.0
MM_VMEM_BUDGET = 44 * 1024 * 1024
MM_FLOPS_PER_US = 700e6


def _mm(a, b, *, ta=False, tb=False, out_dtype=F32, name, rides=None, out_shards=False):
    if ta:
        K, M = a.shape
    else:
        M, K = a.shape
    sharded = b.ndim == 3
    if sharded:
        ns, n = b.shape[0], b.shape[2]
        Kb, N = (ns * n, b.shape[1]) if tb else (b.shape[1], ns * n)
    elif tb:
        N, Kb = b.shape
    else:
        Kb, N = b.shape
    assert K == Kb, (a.shape, b.shape, ta, tb)
    tm = _pick(M, 1024)
    tn = N // NDEV if out_shards else n if (sharded and not tb) else _pick(N, 1152)
    sa, sb, so = a.dtype.itemsize, b.dtype.itemsize, jnp.dtype(out_dtype).itemsize

    def vmem(tk):
        return 2 * (tm * tk * sa + tk * tn * sb) + 2 * tm * tn * so + (tm * tn * 4 if tk < K else 0)

    if sharded and tb:
        tk = n
    else:
        tk = K
        if vmem(K) > MM_VMEM_BUDGET:
            fits = [d for d in range(LANES, K, LANES) if K % d == 0 and vmem(d) <= MM_VMEM_BUDGET]
            tk = max(fits) if fits else _pick(K, 512)
    nk = K // tk
    dims = (((0 if ta else 1,), (1 if tb else 0,)), ((), ()))

    def dot(a_ref, b_ref):
        return lax.dot_general(a_ref[...].astype(BF16), b_ref[...].astype(BF16), dims, preferred_element_type=F32)

    def body_one(a_ref, b_ref, o_ref):
        o_ref[...] = dot(a_ref, b_ref).astype(o_ref.dtype)

    def body_acc(a_ref, b_ref, o_ref, acc_ref):
        k = pl.program_id(2)

        @pl.when(k == 0)
        def _():
            acc_ref[...] = dot(a_ref, b_ref)

        @pl.when((k > 0) & (k < nk - 1))
        def _():
            acc_ref[...] += dot(a_ref, b_ref)

        @pl.when(k == nk - 1)
        def _():
            o_ref[...] = (acc_ref[...] + dot(a_ref, b_ref)).astype(o_ref.dtype)

    a_spec = (pl.BlockSpec((tk, tm), lambda i, j, k: (k, i)) if ta
              else pl.BlockSpec((tm, tk), lambda i, j, k: (i, k)))
    if sharded:
        b_spec = (pl.BlockSpec((None, tn, tk), lambda i, j, k: (k, j, 0)) if tb
                  else pl.BlockSpec((None, tk, tn), lambda i, j, k: (j, k, 0)))
    else:
        b_spec = (pl.BlockSpec((tn, tk), lambda i, j, k: (j, k)) if tb
                  else pl.BlockSpec((tk, tn), lambda i, j, k: (k, j)))
    if out_shards:
        out_spec = pl.BlockSpec((None, tm, tn), lambda i, j, k: (j, i, 0))
        out_shape = jax.ShapeDtypeStruct((NDEV, M, tn), out_dtype)
    else:
        out_spec = pl.BlockSpec((tm, tn), lambda i, j, k: (i, j))
        out_shape = jax.ShapeDtypeStruct((M, N), out_dtype)
    (out,) = _call(
        body_one if nk == 1 else body_acc, grid=(M // tm, N // tn, nk), in_specs=[a_spec, b_spec],
        out_specs=[out_spec], out_shape=[out_shape],
        scratch_shapes=[] if nk == 1 else [pltpu.VMEM((tm, tn), F32)], args=(a, b), name=name,
        rides=rides, us=2.0 * M * N * K / MM_FLOPS_PER_US)
    return out


class V(NamedTuple):
    arr: Any
    width: int
    base: Any


def _v(arr, width=None, col0=0):
    width = arr.shape[-1] if width is None else width
    assert col0 % width == 0
    return V(arr, width, col0 // width)


def _row_specs(v, tr, nrow8, halo):
    main = pl.BlockSpec((tr, v.width), lambda j, i, b=v.base: (i, b + j))
    if not halo:
        return [main]
    per = tr // HALO
    prev = pl.BlockSpec((HALO, v.width), lambda j, i, b=v.base: (jnp.maximum(i * per - 1, 0), b + j))
    nxt = pl.BlockSpec((HALO, v.width), lambda j, i, b=v.base: (jnp.minimum((i + 1) * per, nrow8 - 1), b + j))
    return [prev, main, nxt]


def _par_spec(p):
    if p.base is None:
        return pl.BlockSpec(p.arr.shape, lambda j, i: (0,) * p.arr.ndim)
    return pl.BlockSpec((p.arr.shape[0], p.width), lambda j, i, b=p.base: (0, b + j))


def _load_rows(refs, i, nrow, halo):
    if not halo:
        return refs[0][...].astype(F32)
    prev, main, nxt = refs
    pv = jnp.where(i > 0, prev[...].astype(F32), 0.0)
    nv = jnp.where(i < nrow - 1, nxt[...].astype(F32), 0.0)
    return jnp.concatenate([pv, main[...].astype(F32), nv], axis=0)


def _center_mask(tr):
    row = lax.broadcasted_iota(jnp.int32, (tr + 2 * HALO, 1), 0)
    return ((row >= HALO) & (row < HALO + tr)).astype(F32)


def _rw_fwd(f, rows, params, outs, *, tr, ncol=1, halo=False, name, rides=None, us=0.0):
    S = rows[0].arr.shape[0]
    tr = min(tr, S)
    nrow = S // tr
    per = 3 if halo else 1
    nr, npar = len(rows), len(params)

    def body(*refs):
        i = pl.program_id(1)
        vals = [_load_rows(refs[per * r: per * (r + 1)], i, nrow, halo) for r in range(nr)]
        pvals = [refs[per * nr + p][...].astype(F32) for p in range(npar)]
        out_refs = refs[per * nr + npar:]
        res = f(*vals, *pvals, _center_mask(tr)) if halo else f(*vals, *pvals)
        for r, o in zip(res, out_refs):
            o[...] = (r[HALO:HALO + tr] if halo else r).astype(o.dtype)

    in_specs = []
    for v in rows:
        in_specs += _row_specs(v, tr, S // HALO, halo)
    in_specs += [_par_spec(p) for p in params]
    args = []
    for v in rows:
        args += [v.arr] * per
    args += [p.arr for p in params]
    return _call(
        body, grid=(ncol, nrow), in_specs=in_specs,
        out_specs=[pl.BlockSpec((tr, tot // ncol), lambda j, i: (i, j)) for tot, _ in outs],
        out_shape=[jax.ShapeDtypeStruct((S, tot), dt) for tot, dt in outs],
        args=args, name=name, rides=rides, us=us)


def _rw_bwd(f, rows, params, douts, *, row_grads, par_grads, adds=None, tr, ncol=1, halo=False, name):
    S = rows[0].arr.shape[0]
    tr = min(tr, S)
    nrow = S // tr
    per = 3 if halo else 1
    adds = adds or {}
    nr, npar, nd = len(rows), len(params), len(douts)
    add_keys = sorted(adds)
    rg_idx = [k for k in range(nr) if row_grads[k]]
    pg_idx = [k for k in range(npar) if par_grads[k]]

    def body(*refs):
        i = pl.program_id(1)
        pos = 0
        vals = [_load_rows(refs[per * r: per * (r + 1)], i, nrow, halo) for r in range(nr)]
        pos = per * nr
        pvals = [refs[pos + p][...].astype(F32) for p in range(npar)]
        pos += npar
        dvals = [_load_rows(refs[pos + per * d: pos + per * (d + 1)], i, nrow, halo) for d in range(nd)]
        pos += per * nd
        add_vals = {k: refs[pos + n][...].astype(F32) for n, k in enumerate(add_keys)}
        pos += len(add_keys)
        rg_refs = refs[pos: pos + len(rg_idx)]
        pg_refs = refs[pos + len(rg_idx):]
        if halo:
            center = _center_mask(tr)
            fun = lambda *a: f(*a, center)
        else:
            fun = f
        _, vjp = jax.vjp(fun, *vals, *pvals)
        grads = vjp(tuple(dvals))
        for ref, k in zip(rg_refs, rg_idx):
            g = grads[k]
            g = g[HALO:HALO + tr] if halo else g
            if k in add_vals:
                g = g + add_vals[k]
            ref[...] = g.astype(ref.dtype)
        for ref, k in zip(pg_refs, pg_idx):
            @pl.when(i == 0)
            def _(ref=ref):
                ref[...] = jnp.zeros_like(ref)
            ref[...] += grads[nr + k]

    in_specs, args = [], []
    for v in list(rows):
        in_specs += _row_specs(v, tr, S // HALO, halo)
        args += [v.arr] * per
    in_specs += [_par_spec(p) for p in params]
    args += [p.arr for p in params]
    for v in douts:
        in_specs += _row_specs(v, tr, S // HALO, halo)
        args += [v.arr] * per
    for k in add_keys:
        in_specs += _row_specs(_v(adds[k], rows[k].width), tr, S // HALO, False)
        args += [adds[k]]
    out_specs, out_shape = [], []
    for k in rg_idx:
        w = rows[k].width
        out_specs.append(pl.BlockSpec((tr, w), lambda j, i: (i, j)))
        out_shape.append(jax.ShapeDtypeStruct((S, ncol * w), row_grads[k]))
    for k in pg_idx:
        p = params[k]
        blk = p.arr.shape if p.base is None else (p.arr.shape[0], p.width)
        out_specs.append(pl.BlockSpec((None,) + tuple(blk), lambda j, i, n=len(blk): (j,) + (0,) * n))
        out_shape.append(jax.ShapeDtypeStruct((ncol,) + tuple(blk), F32))
    res = _call(body, grid=(ncol, nrow), in_specs=in_specs, out_specs=out_specs, out_shape=out_shape, args=args, name=name)
    rg = [None] * nr
    for n, k in enumerate(rg_idx):
        rg[k] = res[n]
    pg = [None] * npar
    for n, k in enumerate(pg_idx):
        pg[k] = res[len(rg_idx) + n]
    return rg, pg


def _sigmoid(x):
    return 1.0 / (1.0 + jnp.exp(-x))


def _rms(x, g):
    return x * lax.rsqrt(jnp.mean(x * x, axis=-1, keepdims=True) + EPS) * g


def _f_normmod(x, g, sc, sh):
    return (_rms(x, g) * (1.0 + sc) + sh,)


def _f_resid(x, y, gate, g):
    return (x + gate * _rms(y, g),)


def _f_merge(ga, gb, ya, yb):
    return (_sigmoid(ga) * ya + _sigmoid(gb) * yb,)


def _f_glapost(o, r, g):
    return (_rms(o, g) * (r * _sigmoid(r)),)


def _roll_rows(x, k):
    return pltpu.roll(x, k % x.shape[0], 0)


@functools.partial(jax.custom_vjp, nondiff_argnums=(1,))
def _shift(x, k):
    return _roll_rows(x, k)


def _shift_fwd(x, k):
    return _roll_rows(x, k), None


def _shift_bwd(k, _, g):
    return (_roll_rows(g, -k),)


_shift.defvjp(_shift_fwd, _shift_bwd)


def _taps(w, center):
    return center * w + (1.0 - center) * lax.stop_gradient(w)


def _conv3(u, w0, w1, w2, center):
    return (_taps(w0, center) * _shift(u, 2) + _taps(w1, center) * _shift(u, 1)) + _taps(w2, center) * u


def _f_convmix(cb, cc, cx, w0, w1, w2, center):
    return (cb * _conv3(cc * cx, w0, w1, w2, center),)


def _gelu_tanh(x):
    return 0.5 * x * (1.0 + jnp.tanh(math.sqrt(2.0 / math.pi) * (x + 0.044715 * (x * x * x))))


def _f_ffn(gate, up, w0, w1, w2, center):
    return (_gelu_tanh(_conv3(gate, w0, w1, w2, center)) * up,)


@jax.custom_vjp
def _bdot(a, b):
    return jnp.dot(a.astype(BF16), b.astype(BF16), preferred_element_type=F32)


def _bdot_fwd(a, b):
    return _bdot(a, b), (a, b)


def _bdot_bwd(res, g):
    a, b = res
    gb = g.astype(BF16)
    da = lax.dot_general(gb, b.astype(BF16), (((1,), (1,)), ((), ())), preferred_element_type=F32)
    db = lax.dot_general(a.astype(BF16), gb, (((0,), (0,)), ((), ())), preferred_element_type=F32)
    return da, db


_bdot.defvjp(_bdot_fwd, _bdot_bwd)


@jax.custom_vjp
def _log_sigmoid(z):
    return jnp.minimum(z, 0.0) - jnp.log(1.0 + jnp.exp(-jnp.abs(z)))


def _log_sigmoid_fwd(z):
    return _log_sigmoid(z), z


def _log_sigmoid_bwd(z, g):
    return (g * _sigmoid(-z),)


_log_sigmoid.defvjp(_log_sigmoid_fwd, _log_sigmoid_bwd)


def _f_loga(lr, w, b):
    return (_log_sigmoid(_bdot(lr, w) + b) / GLA_TAU,)


def _split3(x):
    hi = x.astype(BF16)
    r1 = x - hi.astype(F32)
    mid = r1.astype(BF16)
    lo = (r1 - mid.astype(F32)).astype(BF16)
    return hi, mid, lo


def _dot_exact(a01, x, dims):
    hi, mid, lo = _split3(x)
    d = lambda p: lax.dot_general(a01, p, dims, preferred_element_type=F32)
    return (d(lo) + d(mid)) + d(hi)


def _dot_exact_lhs(x, a01, dims):
    hi, mid, lo = _split3(x)
    d = lambda p: lax.dot_general(p, a01, dims, preferred_element_type=F32)
    return (d(lo) + d(mid)) + d(hi)


GLA_CHUNKS_PER_STEP = 4

NN = (((1,), (0,)), ((), ()))
NT = (((1,), (1,)), ((), ()))
TN = (((0,), (0,)), ((), ()))


def _bf_dot(a, b, dims):
    return lax.dot_general(a.astype(BF16), b.astype(BF16), dims, preferred_element_type=F32)


def _chunk_decay(a, dv):
    L = a.shape[0]
    r = lax.broadcasted_iota(jnp.int32, (L, L), 0)
    c = lax.broadcasted_iota(jnp.int32, (L, L), 1)
    tri = (r >= c).astype(BF16)
    cum = _dot_exact(tri, a, NN)
    ce = cum[L - 1:L, :]
    dec = jnp.exp(_dot_exact_lhs(a, jnp.ones((L, dv), BF16), TN))
    return cum, ce, jnp.exp(ce - cum), dec


def _gla_fwd(P, a, *, qk, vw, name, rides=None, us=0.0):
    S = P.shape[0]
    nc = S // CHUNK
    cps = math.gcd(nc, GLA_CHUNKS_PER_STEP)
    rows = cps * CHUNK
    dk, dv = qk // HEADS, vw // HEADS
    scale = dk ** -0.5

    def body(q_ref, k_ref, v_ref, a_ref, o_ref, st_ref, state):
        @pl.when(pl.program_id(0) == 0)
        def _():
            state[...] = jnp.zeros_like(state)

        for h in range(HEADS):
            ks, vs = slice(h * dk, (h + 1) * dk), slice(h * dv, (h + 1) * dv)
            s = state[h]
            for c in range(cps):
                rs = slice(c * CHUNK, (c + 1) * CHUNK)
                _, _, E, dec = _chunk_decay(a_ref[rs, ks], dv)
                kd = k_ref[rs, ks] * E
                s = dec * s + _bf_dot(kd, v_ref[rs, vs], TN)
                st_ref[c, h] = s
                o_ref[rs, vs] = _bf_dot(q_ref[rs, ks] * scale, s, NN).astype(o_ref.dtype)
            state[h] = s

    return _call(
        body, grid=(nc // cps,),
        in_specs=[pl.BlockSpec((rows, qk), lambda i: (i, 0)),
                  pl.BlockSpec((rows, qk), lambda i: (i, 1)),
                  pl.BlockSpec((rows, vw), lambda i: (i, 2 * qk // vw)),
                  pl.BlockSpec((rows, qk), lambda i: (i, 0))],
        out_specs=[pl.BlockSpec((rows, vw), lambda i: (i, 0)),
                   pl.BlockSpec((cps, HEADS, dk, dv), lambda i: (i, 0, 0, 0))],
        out_shape=[jax.ShapeDtypeStruct((S, vw), BF16), jax.ShapeDtypeStruct((nc, HEADS, dk, dv), F32)],
        scratch_shapes=[pltpu.VMEM((HEADS, dk, dv), F32)], args=(P, P, P, a), name=name, rides=rides, us=us)


def _gla_bwd(P, a, states, do, *, qk, vw, name):
    S = P.shape[0]
    nc = S // CHUNK
    cps = math.gcd(nc, GLA_CHUNKS_PER_STEP)
    rows = cps * CHUNK
    nstep = nc // cps
    dk, dv = qk // HEADS, vw // HEADS
    scale = dk ** -0.5

    def body(q_ref, k_ref, v_ref, a_ref, do_ref, sc_ref, sp_ref, dq_ref, dk_ref, dv_ref, da_ref, ds_carry):
        i = pl.program_id(0)

        @pl.when(i == 0)
        def _():
            ds_carry[...] = jnp.zeros_like(ds_carry)

        has_prev = (i < nstep - 1).astype(F32)
        L = CHUNK
        r = lax.broadcasted_iota(jnp.int32, (L, L), 0)
        c = lax.broadcasted_iota(jnp.int32, (L, L), 1)
        tri_t = (c >= r).astype(BF16)
        last = (lax.broadcasted_iota(jnp.int32, (L, 1), 0) == L - 1).astype(F32)
        for h in range(HEADS):
            ks, vs = slice(h * dk, (h + 1) * dk), slice(h * dv, (h + 1) * dv)
            carry = ds_carry[h]
            for n in reversed(range(cps)):
                rs = slice(n * CHUNK, (n + 1) * CHUNK)
                _, _, E, dec = _chunk_decay(a_ref[rs, ks], dv)
                kk, vv, dov = k_ref[rs, ks], v_ref[rs, vs], do_ref[rs, vs]
                kd = kk * E
                ds = carry + _bf_dot(q_ref[rs, ks] * scale, dov, TN)
                dq_ref[rs, ks] = (_bf_dot(dov, sc_ref[n, h], NT) * scale).astype(dq_ref.dtype)
                dkd = _bf_dot(vv, ds, NT)
                dv_ref[rs, vs] = _bf_dot(kd, ds, NN).astype(dv_ref.dtype)
                dk_ref[rs, ks] = (dkd * E).astype(dk_ref.dtype)
                darg = dkd * kd
                before = sc_ref[n - 1, h] if n > 0 else sp_ref[0, h] * has_prev
                w = ds * before * dec
                dce = _dot_exact(jnp.ones((SUBLANES, dv), BF16), w, NT)[0:1, :]
                dce = dce + jnp.sum(darg, axis=0, keepdims=True)
                dcum = last * dce - darg
                da_ref[rs, ks] = _dot_exact(tri_t, dcum, NN)
                carry = dec * ds
            ds_carry[h] = carry

    rev = lambda i: nstep - 1 - i
    return pl.pallas_call(
        body, grid=(nstep,),
        in_specs=[pl.BlockSpec((rows, qk), lambda i: (rev(i), 0)),
                  pl.BlockSpec((rows, qk), lambda i: (rev(i), 1)),
                  pl.BlockSpec((rows, vw), lambda i: (rev(i), 2 * qk // vw)),
                  pl.BlockSpec((rows, qk), lambda i: (rev(i), 0)),
                  pl.BlockSpec((rows, vw), lambda i: (rev(i), 0)),
                  pl.BlockSpec((cps, HEADS, dk, dv), lambda i: (rev(i), 0, 0, 0)),
                  pl.BlockSpec((1, HEADS, dk, dv), lambda i: (jnp.maximum(rev(i) * cps - 1, 0), 0, 0, 0))],
        out_specs=[pl.BlockSpec((rows, qk), lambda i: (rev(i), 0)),
                   pl.BlockSpec((rows, qk), lambda i: (rev(i), 0)),
                   pl.BlockSpec((rows, vw), lambda i: (rev(i), 0)),
                   pl.BlockSpec((rows, qk), lambda i: (rev(i), 0))],
        out_shape=[jax.ShapeDtypeStruct((S, qk), BF16), jax.ShapeDtypeStruct((S, qk), BF16),
                   jax.ShapeDtypeStruct((S, vw), BF16), jax.ShapeDtypeStruct((S, qk), F32)],
        scratch_shapes=[pltpu.VMEM((HEADS, dk, dv), F32)],
        compiler_params=_params(("arbitrary",)), name=name)(P, P, P, a, do, states, states)


def _loss_head(y, target, *, tr, name):
    S, D = y.shape
    tr = min(tr, S)

    def body(y_ref, t_ref, l_ref, dy_ref):
        @pl.when(pl.program_id(0) == 0)
        def _():
            l_ref[...] = jnp.zeros_like(l_ref)

        e = y_ref[...] - t_ref[...]
        dy_ref[...] = e / D
        l_ref[...] += 0.5 * jnp.sum(jnp.mean(e * e, axis=-1, keepdims=True), axis=0, keepdims=True)

    loss, dy = pl.pallas_call(
        body, grid=(S // tr,),
        in_specs=[pl.BlockSpec((tr, D), lambda i: (i, 0))] * 2,
        out_specs=[pl.BlockSpec((SUBLANES, LANES), lambda i: (0, 0)), pl.BlockSpec((tr, D), lambda i: (i, 0))],
        out_shape=[jax.ShapeDtypeStruct((SUBLANES, LANES), F32), jax.ShapeDtypeStruct((S, D), F32)],
        compiler_params=_params(("arbitrary",)), name=name)(y, target)
    return loss[0, 0], dy


ADAMW_BYTES_PER_US = 2.5e6


def _adamw(parts, w, m, v, *, name, rides=None):
    G, R, C = w.shape
    n = parts[0].shape[0]
    unit = 2 * SUBLANES if parts[0].dtype == BF16 else SUBLANES
    tr = _pick(R, max(unit, (128 * 1024) // C), unit)
    nblk = R // tr
    bc1 = 1.0 - ADAM_B1 ** ADAM_STEP
    bc2 = 1.0 - ADAM_B2 ** ADAM_STEP

    def body(*refs):
        p_refs = refs[:G]
        w_ref, m_ref, v_ref, g_ref, d_ref, mo_ref, vo_ref = refs[G:]
        for q in range(G):
            @pl.when(pl.program_id(0) == q)
            def _(q=q):
                g = p_refs[q][0].astype(F32)
                for s in range(1, n):
                    g = g + p_refs[q][s].astype(F32)
                m2 = ADAM_B1 * m_ref[...] + (1.0 - ADAM_B1) * g
                v2 = ADAM_B2 * v_ref[...] + (1.0 - ADAM_B2) * (g * g)
                g_ref[...] = g
                mo_ref[...] = m2
                vo_ref[...] = v2
                d_ref[...] = -ADAM_LR * ((m2 / bc1) / (jnp.sqrt(v2 / bc2) + ADAM_EPS) + ADAM_WD * w_ref[...])

    def part_spec(q):
        return pl.BlockSpec((n, tr, C), lambda g, i: (0, jnp.where(g == q, i, jnp.where(g < q, 0, nblk - 1)), 0))

    blk = pl.BlockSpec((None, tr, C), lambda g, i: (g, i, 0))
    traffic = G * R * C * (n * parts[0].dtype.itemsize + 7 * 4)
    return _call(
        body, grid=(G, nblk), in_specs=[part_spec(q) for q in range(G)] + [blk, blk, blk],
        out_specs=[blk] * 4, out_shape=[jax.ShapeDtypeStruct((G, R, C), F32)] * 4,
        args=(*parts, w, m, v), name=name, rides=rides, us=traffic / ADAMW_BYTES_PER_US)


def _adamw_nd(parts, w, m, v, *, name):
    shape = w.shape
    C = shape[-1]
    R = math.prod(shape[:-1])
    outs = _adamw([parts.reshape(parts.shape[0], R, C)], w.reshape(1, R, C), m.reshape(1, R, C), v.reshape(1, R, C),
                  name=name)
    return [o.reshape(shape) for o in outs]


def _all_gather(arrs, *, name):
    n = len(arrs)
    HBM = pl.BlockSpec(memory_space=pl.ANY)

    def body(*refs):
        ins, outs = refs[:n], refs[n:2 * n]
        send_sems, recv_sems, local_sems = refs[2 * n:]
        x, y, c = _position()
        me, sibling = (x, y, c), (x, y, 1 - c)
        chips = [(1 - x, y), (x, 1 - y), (1 - x, 1 - y)]

        def copy(t, k, block, to, src=None):
            rows = outs[t].at[_index(block)]
            return pltpu.make_async_remote_copy(
                src_ref=rows if src is None else src, dst_ref=rows,
                send_sem=send_sems.at[t, k], recv_sem=recv_sems.at[t, k],
                device_id=to, device_id_type=pl.DeviceIdType.MESH)

        started = []
        mine = []
        for t in range(n):
            cp = pltpu.make_async_copy(ins[t], outs[t].at[_index(me)], local_sems.at[t])
            cp.start()
            mine.append(cp)
            first = [copy(t, 0, me, sibling, src=ins[t])]
            first += [copy(t, 1 + j, me, (*chip, c), src=ins[t]) for j, chip in enumerate(chips)]
            for cp in first:
                cp.start()
            started += first
        for t in range(n):
            for j, chip in enumerate(chips):
                copy(t, 1 + j, (*chip, c), me).wait_recv()
                cp = copy(t, 4 + j, (*chip, c), sibling)
                cp.start()
                started.append(cp)
        for t in range(n):
            copy(t, 0, sibling, me).wait_recv()
            for j, chip in enumerate(chips):
                copy(t, 4 + j, (*chip, 1 - c), me).wait_recv()
        for cp in started:
            cp.wait_send()
        for cp in mine:
            cp.wait()

    return pl.pallas_call(
        body, in_specs=[HBM] * n, out_specs=[HBM] * n,
        out_shape=[jax.ShapeDtypeStruct((NDEV,) + a.shape, a.dtype) for a in arrs],
        scratch_shapes=[pltpu.SemaphoreType.DMA((n, 7)), pltpu.SemaphoreType.DMA((n, 7)),
                        pltpu.SemaphoreType.DMA((n,))],
        name=name)(*arrs)


class _Pack:
    def __init__(self, arrs):
        self.shapes = [a.shape for a in arrs]
        self.sizes = [math.prod(s) for s in self.shapes]
        total = sum(self.sizes)
        unit = SUBLANES * LANES
        self.padded = -(-total // unit) * unit
        flat = jnp.concatenate([a.reshape(-1).astype(F32) for a in arrs] + [jnp.zeros((self.padded - total,), F32)])
        self.packed = flat.reshape(self.padded // LANES, LANES)

    def unpack(self, gathered):
        flat = gathered.reshape(NDEV, self.padded)
        out, off = [], 0
        for shape, size in zip(self.shapes, self.sizes):
            out.append(flat[:, off:off + size].reshape((NDEV,) + tuple(shape)))
            off += size
        return out


def _cols_from_shards(g):
    g = jnp.moveaxis(g, 0, -2)
    return g.reshape(g.shape[:-2] + (g.shape[-2] * g.shape[-1],))


def _cols_to_shards(w):
    K, N = w.shape
    return jnp.moveaxis(w.reshape(K, NDEV, N // NDEV), 1, 0)


def kernel(x, c, w_ada, b_ada, norm_g, w_in, w_a2, b_a2, gla_norm_g, w_out_gla, conv_mix_w, w_out_conv, w_o, w_up, ffn_conv_w, w_down, loss_target, m_w_ada, m_b_ada, m_norm_g, m_w_in, m_w_a2, m_b_a2, m_gla_norm_g, m_w_out_gla, m_conv_mix_w, m_w_out_conv, m_w_o, m_w_up, m_ffn_conv_w, m_w_down, v_w_ada, v_b_ada, v_norm_g, v_w_in, v_w_a2, v_b_a2, v_gla_norm_g, v_w_out_gla, v_conv_mix_w, v_w_out_conv, v_w_o, v_w_up, v_ffn_conv_w, v_w_down):
    depth = w_ada.shape[0]
    S, D = x.shape[1], x.shape[2]
    QK = w_a2.shape[2] * NDEV
    LR = w_a2.shape[1]
    VW = w_out_gla.shape[1]
    CW = w_out_conv.shape[1]
    FF = w_down.shape[1] * NDEV
    NIN = w_in.shape[2] * NDEV
    NADA = w_ada.shape[2]
    LRP = -(-LR // LANES) * LANES
    off_lr = 2 * QK + 2 * VW
    NP = NIN - LR + LRP
    assert NIN == 2 * QK + 2 * VW + LR + 3 * CW + 2 * D
    xi, yi, ci = _position()
    me = _index((xi, yi, ci))
    xs = x.reshape(S, D)
    target = loss_target.reshape(S, D)
    TR = 256
    TRC = 512
    TC = _pick(CW, 512)
    TCF = _pick(FF, 512)

    small = _Pack([c, norm_g, w_a2, conv_mix_w, ffn_conv_w])
    big_names = ["w_in", "w_out_gla", "w_out_conv", "w_o", "w_up", "w_down"]
    big = dict(w_in=w_in, w_out_gla=w_out_gla, w_out_conv=w_out_conv, w_o=w_o, w_up=w_up, w_down=w_down)

    def matrix(key, g):
        if key == "w_up":
            return g
        if key in ("w_o", "w_down"):
            return g.reshape(g.shape[0] * g.shape[1], g.shape[2])
        if key != "w_in":
            return _cols_from_shards(g)
        pieces = [g[d][:, lo - d * nin_shard:hi - d * nin_shard] for _, d, lo, hi in sorted(in_pieces)]
        return jnp.concatenate(pieces + [jnp.zeros((D, LRP - LR), BF16)], axis=-1)

    nin_shard = NIN // NDEV
    in_pieces = []
    for g_lo, g_hi, to in ((0, off_lr, 0), (off_lr, off_lr + LR, NIN - LR), (off_lr + LR, NIN, off_lr)):
        for d in range(NDEV):
            lo, hi = max(g_lo, d * nin_shard), min(g_hi, (d + 1) * nin_shard)
            if lo < hi:
                in_pieces.append((to + lo - g_lo, d, lo, hi))

    def in_grad_blocks(gW_all):
        blocks = []
        for d in range(NDEV):
            mine = sorted((lo, to, hi) for to, dd, lo, hi in in_pieces if dd == d)
            blocks.append(jnp.concatenate([gW_all[:, to:to + hi - lo] for lo, to, hi in mine], axis=1))
        return jnp.stack(blocks)

    (small_g, w_in0_g) = _all_gather([small.packed, w_in[0].astype(BF16)], name="gather_first")
    c_all, norm_g_s, w_a2_s, conv_w_s, ffn_w_s = small.unpack(small_g)
    c_all = c_all.reshape(NDEV, D)
    norm_g_f = _cols_from_shards(norm_g_s)
    w_a2_f = _cols_from_shards(w_a2_s)
    conv_w_f = _cols_from_shards(conv_w_s)
    ffn_w_f = _cols_from_shards(ffn_w_s)
    w_a2_p = jnp.concatenate([w_a2_f, jnp.zeros((depth, LRP - LR, QK), F32)], axis=1)

    rides = _Rides()
    GATHER_PARAMS_PER_US = 85e3
    FORWARD_PARAMS_PER_US = 800e3
    EXCHANGE_PARAMS_PER_US = 45e3
    weight_streams = {}
    for l in range(depth):
        for key in big_names:
            if (l, key) == (0, "w_in"):
                continue
            shard = big[key][l].astype(BF16)
            rows, cols = shard.shape
            first = rides.add(_Stream("gather_a", shard, rows, cols, GATHER_PARAMS_PER_US / (NDEV * cols)))
            second = rides.add(_Stream("gather_b", None, rows, cols, FORWARD_PARAMS_PER_US / (NDEV * cols), after=first))
            weight_streams[l, key] = (first, second)
    W = {(0, "w_in"): matrix("w_in", w_in0_g)}

    def weight(l, key):
        if (l, key) not in W:
            first, second = weight_streams[l, key]
            rides.need(first, second)
            W[l, key] = matrix(key, first.buf)
        return W[l, key]

    MP = LANES
    c_pad = jnp.concatenate([c_all, jnp.zeros((MP - NDEV, D), F32)], axis=0)
    (silu_c,) = _rw_fwd(lambda t: (t * _sigmoid(t),), [_v(c_pad)], [], [(D, BF16)], tr=MP, name="silu_c")
    mod_cols = _mm(silu_c, w_ada, name="ada_fwd")[:NDEV].reshape(NDEV, depth, NADA)
    (mod_g,) = _all_gather([jnp.moveaxis(mod_cols, 1, 0)], name="gather_mod")
    mod_all = jnp.moveaxis(mod_g, 0, 2).reshape(depth, NDEV, NDEV * NADA)
    mod_me = lax.dynamic_slice_in_dim(mod_all, me, 1, axis=1).reshape(depth, NDEV * NADA)
    (mod_me,) = _rw_fwd(lambda a, b: (a + b,), [_v(mod_me), _v(b_ada)], [], [(NDEV * NADA, F32)],
                        tr=depth, name="mod_bias")

    col = {}
    off = 0
    for nm, wd in (("q", QK), ("k", QK), ("v", VW), ("r", VW), ("cb", CW), ("cc", CW), ("cx", CW),
                   ("ga", D), ("gb", D), ("lr", LRP)):
        col[nm] = off
        off += wd

    saved = []
    xin = xs
    for l in range(depth):
        sh1, sc1, g1, sh2, sc2, g2 = [mod_me[l:l + 1, i * D:(i + 1) * D] for i in range(6)]
        ng = [norm_g_f[l, i:i + 1] for i in range(4)]
        cw = [conv_w_f[l, i:i + 1] for i in range(CONV_K)]
        fw = [ffn_w_f[l, i:i + 1] for i in range(CONV_K)]
        gn = gla_norm_g[l:l + 1]
        ba2 = b_a2[l:l + 1]
        def proj(act, key, name, out_dtype=BF16):
            return _mm(act, weight(l, key), name=name, rides=rides, out_dtype=out_dtype)

        (h1,) = _rw_fwd(_f_normmod, [_v(xin)], [_v(ng[0]), _v(sc1), _v(sh1)], [(D, BF16)], tr=TR, name=f"normmod1_{l}")
        P = proj(h1, "w_in", f"in_proj{l}")
        (a,) = _rw_fwd(_f_loga, [_v(P, LRP, col["lr"])], [V(w_a2_p[l], QK, None), _v(ba2)], [(QK, F32)],
                       tr=TR, name=f"loga{l}")
        o, states = _gla_fwd(P, a, qk=QK, vw=VW, name=f"gla_fwd{l}", rides=rides, us=US_GLA)
        dvh = VW // HEADS
        (ya_in,) = _rw_fwd(_f_glapost, [_v(o, dvh), _v(P, dvh, col["r"])], [V(gn, dvh, None)], [(VW, BF16)],
                           tr=TR, ncol=HEADS, name=f"glapost{l}", rides=rides, us=US_GLAPOST)
        ya = proj(ya_in, "w_out_gla", f"out_gla{l}")
        (yb_in,) = _rw_fwd(_f_convmix, [_v(P, TC, col["cb"]), _v(P, TC, col["cc"]), _v(P, TC, col["cx"])],
                           [_v(w, TC) for w in cw], [(CW, BF16)], tr=TRC, ncol=CW // TC, halo=True, name=f"convmix{l}")
        yb = proj(yb_in, "w_out_conv", f"out_conv{l}")
        (mg,) = _rw_fwd(_f_merge, [_v(P, D, col["ga"]), _v(P, D, col["gb"]), _v(ya), _v(yb)], [], [(D, BF16)],
                        tr=TR, name=f"merge{l}", rides=rides, us=US_MERGE)
        y = proj(mg, "w_o", f"o_proj{l}", F32)
        (x1,) = _rw_fwd(_f_resid, [_v(xin), _v(y)], [_v(g1), _v(ng[1])], [(D, F32)], tr=TR, name=f"resid1_{l}")
        (h2,) = _rw_fwd(_f_normmod, [_v(x1)], [_v(ng[2]), _v(sc2), _v(sh2)], [(D, BF16)], tr=TR, name=f"normmod2_{l}")
        U = proj(h2, "w_up", f"up_proj{l}")
        (ff,) = _rw_fwd(_f_ffn, [_v(U, TCF, 0), _v(U, TCF, FF)], [_v(w, TCF) for w in fw], [(FF, BF16)],
                        tr=TRC, ncol=FF // TCF, halo=True, name=f"ffn{l}", rides=rides, us=US_FFN)
        y2 = proj(ff, "w_down", f"down_proj{l}", F32)
        (x2,) = _rw_fwd(_f_resid, [_v(x1), _v(y2)], [_v(g2), _v(ng[3])], [(D, F32)], tr=TR, name=f"resid2_{l}")
        saved.append(dict(xin=xin, h1=h1, P=P, a=a, o=o, states=states, ya_in=ya_in, ya=ya, yb_in=yb_in, yb=yb,
                          mg=mg, y=y, x1=x1, h2=h2, U=U, ff=ff, y2=y2, mods=(sh1, sc1, g1, sh2, sc2, g2),
                          ng=ng, cw=cw, fw=fw, gn=gn, ba2=ba2))
        xin = x2

    loss_local, dx = _loss_head(xin, target, tr=TR, name="loss_head")
    loss = lax.psum(loss_local, MESH_AXES)

    grad_streams = {}

    def send(l, key, blocks):
        _, rows, cols = blocks.shape
        grad_streams[l, key] = rides.add(_Stream("a2a", blocks, rows, cols, EXCHANGE_PARAMS_PER_US / (NDEV * cols)))

    def bmm(a, b, **kw):
        kw.setdefault("out_dtype", BF16)
        return _mm(a, b, rides=rides, **kw)

    gsmall = [None] * depth
    dmod = [None] * depth
    for l in reversed(range(depth)):
        s = saved[l]
        sh1, sc1, g1, sh2, sc2, g2 = s["mods"]
        ng, cw, fw = s["ng"], s["cw"], s["fw"]
        (dx1_a, dy2), (dg2, dng3) = _rw_bwd(
            _f_resid, [_v(s["x1"]), _v(s["y2"])], [_v(g2), _v(ng[3])], [_v(dx)],
            row_grads=[F32, BF16], par_grads=[True, True], tr=TR, name=f"resid2_bwd{l}")
        gW_down = bmm(s["ff"], dy2, ta=True, out_dtype=BF16, name=f"down_wgrad{l}")
        send(l, "w_down", gW_down.reshape(NDEV, FF // NDEV, D))
        dff = bmm(dy2, weight(l, "w_down"), tb=True, name=f"down_dgrad{l}")
        U = s["U"]
        (dgate, dup), dfw = _rw_bwd(
            _f_ffn, [_v(U, TCF, 0), _v(U, TCF, FF)], [_v(w, TCF) for w in fw], [_v(dff, TCF)],
            row_grads=[BF16, BF16], par_grads=[True] * 3, tr=TRC, ncol=FF // TCF, halo=True, name=f"ffn_bwd{l}")
        dU = jnp.concatenate([dgate, dup], axis=1)
        gW_up = bmm(s["h2"], dU, ta=True, out_dtype=BF16, out_shards=True, name=f"up_wgrad{l}")
        send(l, "w_up", gW_up)
        dh2 = bmm(dU, weight(l, "w_up"), tb=True, name=f"up_dgrad{l}")
        (dx1,), (dng2, dsc2, dsh2) = _rw_bwd(
            _f_normmod, [_v(s["x1"])], [_v(ng[2]), _v(sc2), _v(sh2)], [_v(dh2)],
            row_grads=[F32], par_grads=[True] * 3, adds={0: dx1_a}, tr=TR, name=f"normmod2_bwd{l}")
        (dxin_a, dy), (dg1, dng1) = _rw_bwd(
            _f_resid, [_v(s["xin"]), _v(s["y"])], [_v(g1), _v(ng[1])], [_v(dx1)],
            row_grads=[F32, BF16], par_grads=[True, True], tr=TR, name=f"resid1_bwd{l}")
        gW_o = bmm(s["mg"], dy, ta=True, out_dtype=BF16, name=f"o_wgrad{l}")
        send(l, "w_o", gW_o.reshape(NDEV, D // NDEV, D))
        dmg = bmm(dy, weight(l, "w_o"), tb=True, name=f"o_dgrad{l}")
        P = s["P"]
        (dga, dgb, dya, dyb), _ = _rw_bwd(
            _f_merge, [_v(P, D, col["ga"]), _v(P, D, col["gb"]), _v(s["ya"]), _v(s["yb"])], [], [_v(dmg)],
            row_grads=[BF16] * 4, par_grads=[], tr=TR, name=f"merge_bwd{l}")
        gW_og = bmm(s["ya_in"], dya, ta=True, out_dtype=BF16, name=f"out_gla_wgrad{l}")
        send(l, "w_out_gla", _cols_to_shards(gW_og))
        dya_in = bmm(dya, weight(l, "w_out_gla"), tb=True, name=f"out_gla_dgrad{l}")
        gW_oc = bmm(s["yb_in"], dyb, ta=True, out_dtype=BF16, name=f"out_conv_wgrad{l}")
        send(l, "w_out_conv", _cols_to_shards(gW_oc))
        dyb_in = bmm(dyb, weight(l, "w_out_conv"), tb=True, name=f"out_conv_dgrad{l}")
        (dcb, dcc, dcx), dcw = _rw_bwd(
            _f_convmix, [_v(P, TC, col["cb"]), _v(P, TC, col["cc"]), _v(P, TC, col["cx"])],
            [_v(w, TC) for w in cw], [_v(dyb_in, TC)],
            row_grads=[BF16] * 3, par_grads=[True] * 3, tr=TRC, ncol=CW // TC, halo=True, name=f"convmix_bwd{l}")
        dvh = VW // HEADS
        (do, dr), (dgn,) = _rw_bwd(
            _f_glapost, [_v(s["o"], dvh), _v(P, dvh, col["r"])], [V(s["gn"], dvh, None)], [_v(dya_in, dvh)],
            row_grads=[BF16, BF16], par_grads=[True], tr=TR, ncol=HEADS, name=f"glapost_bwd{l}")
        dq, dk, dv, da = _gla_bwd(P, s["a"], s["states"], do, qk=QK, vw=VW, name=f"gla_bwd{l}")
        (dlr,), (dwa2, dba2) = _rw_bwd(
            _f_loga, [_v(P, LRP, col["lr"])], [V(w_a2_p[l], QK, None), _v(s["ba2"])], [_v(da)],
            row_grads=[BF16], par_grads=[True, True], tr=TR, name=f"loga_bwd{l}")
        dP = jnp.concatenate([dq, dk, dv, dr, dcb, dcc, dcx, dga, dgb, dlr], axis=1)
        gW_all = bmm(s["h1"], dP, ta=True, out_dtype=BF16, name=f"in_wgrad{l}")
        send(l, "w_in", in_grad_blocks(gW_all))
        dh1 = bmm(dP, weight(l, "w_in"), tb=True, name=f"in_dgrad{l}")
        (dx,), (dng0, dsc1, dsh1) = _rw_bwd(
            _f_normmod, [_v(s["xin"])], [_v(ng[0]), _v(sc1), _v(sh1)], [_v(dh1)],
            row_grads=[F32], par_grads=[True] * 3, adds={0: dxin_a}, tr=TR, name=f"normmod1_bwd{l}")
        vec = lambda t: t.reshape(1, -1)
        gsmall[l] = dict(
            norm_g=jnp.concatenate([vec(dng0), vec(dng1), vec(dng2), vec(dng3)], axis=0),
            w_a2=dwa2[0, :LR], b_a2=dba2.reshape(QK), gla_norm_g=jnp.sum(dgn, axis=0).reshape(dvh),
            conv_mix_w=jnp.concatenate([vec(t) for t in dcw], axis=0),
            ffn_conv_w=jnp.concatenate([vec(t) for t in dfw], axis=0))
        dmod[l] = jnp.concatenate([vec(t) for t in (dsh1, dsc1, dg1, dsh2, dsc2, dg2)], axis=1).reshape(-1)
    grad_x = dx.reshape(x.shape)

    stack = lambda key, src: jnp.stack([src[l][key] for l in range(depth)])
    small_names = ["norm_g", "w_a2", "b_a2", "gla_norm_g", "conv_mix_w", "ffn_conv_w"]
    gpack = _Pack([jnp.stack(dmod)] + [stack(k, gsmall) for k in small_names])
    (gsm,) = _all_gather([gpack.packed], name="gather_small_grads")
    gs = gpack.unpack(gsm)
    dmod_all = gs[0]
    parts_small = dict(zip(small_names, gs[1:]))

    def my_cols(t, n):
        return lax.dynamic_slice_in_dim(t, me * n, n, axis=t.ndim - 1)

    gW_ada = []
    for l in range(depth):
        dcols = my_cols(dmod_all[:, l], NADA)
        dcols = jnp.concatenate([dcols, jnp.zeros((MP - NDEV, NADA), F32)], axis=0)
        gW_ada.append(bmm(silu_c, dcols, ta=True, out_dtype=F32, name=f"ada_wgrad{l}")[None])
    parts = {}
    parts["b_ada"] = dmod_all
    parts["norm_g"] = my_cols(parts_small["norm_g"], D // NDEV)
    parts["w_a2"] = my_cols(parts_small["w_a2"], QK // NDEV)
    parts["b_a2"] = parts_small["b_a2"]
    parts["gla_norm_g"] = parts_small["gla_norm_g"]
    parts["conv_mix_w"] = my_cols(parts_small["conv_mix_w"], CW // NDEV)
    parts["ffn_conv_w"] = my_cols(parts_small["ffn_conv_w"], FF // NDEV)

    weights = dict(w_ada=(w_ada, m_w_ada, v_w_ada), b_ada=(b_ada, m_b_ada, v_b_ada),
                   norm_g=(norm_g, m_norm_g, v_norm_g), w_in=(w_in, m_w_in, v_w_in), w_a2=(w_a2, m_w_a2, v_w_a2),
                   b_a2=(b_a2, m_b_a2, v_b_a2), gla_norm_g=(gla_norm_g, m_gla_norm_g, v_gla_norm_g),
                   w_out_gla=(w_out_gla, m_w_out_gla, v_w_out_gla), conv_mix_w=(conv_mix_w, m_conv_mix_w, v_conv_mix_w),
                   w_out_conv=(w_out_conv, m_w_out_conv, v_w_out_conv), w_o=(w_o, m_w_o, v_w_o),
                   w_up=(w_up, m_w_up, v_w_up), ffn_conv_w=(ffn_conv_w, m_ffn_conv_w, v_ffn_conv_w),
                   w_down=(w_down, m_w_down, v_w_down))
    done = {}
    for nm in ["w_ada", "w_down", "w_up", "w_o", "w_out_gla", "w_out_conv", "w_in"]:
        w, m, v = weights[nm]
        if nm == "w_ada":
            mine = gW_ada
        else:
            streams = [grad_streams[l, nm] for l in range(depth)]
            rides.need(*streams)
            mine = [s.buf for s in streams]
        done[nm] = _adamw(mine, w, m, v, name=f"adamw_{nm}", rides=rides)
    for nm, (w, m, v) in weights.items():
        if nm not in done:
            done[nm] = _adamw_nd(parts[nm], w, m, v, name=f"adamw_{nm}")
    grads, deltas, new_m, new_v = zip(*[done[nm] for nm in weights])
    return (loss, grad_x, *grads, *deltas, *new_m, *new_v)
```

```python
import functools
import math
from typing import Any, NamedTuple

import jax
import jax.numpy as jnp
from jax import lax
from jax.experimental import pallas as pl
from jax.experimental.pallas import tpu as pltpu

F32 = jnp.float32
BF16 = jnp.bfloat16
MESH_AXES = ("x", "y", "c")
NDEV = 8
EPS = 1e-6
CHUNK = 64
HEADS = 4
GLA_TAU = 16.0
CONV_K = 3
ADAM_LR = 0.001
ADAM_B1 = 0.9
ADAM_B2 = 0.999
ADAM_EPS = 1e-08
ADAM_WD = 0.01
ADAM_STEP = 10

LANES = 128
SUBLANES = 8
HALO = 2 * SUBLANES
VMEM_LIMIT = 56 * 1024 * 1024


def _params(sem=None):
    return pltpu.CompilerParams(dimension_semantics=sem, vmem_limit_bytes=VMEM_LIMIT)


def _pick(n, cap, unit=LANES):
    best = None
    for d in range(unit, min(n, cap) + 1, unit):
        if n % d == 0:
            best = d
    return best if best is not None else n


def _position():
    return lax.axis_index("x"), lax.axis_index("y"), lax.axis_index("c")


def _index(p):
    return 4 * p[0] + 2 * p[1] + p[2]


def _peer(k):
    x, y, c = _position()
    return ((1 - x) if k & 4 else x, (1 - y) if k & 2 else y, (1 - c) if k & 1 else c)


ROW_QUANTUM = 2 * SUBLANES
MIN_ROWS = 128


class _Stream:
    def __init__(self, mode, src, rows, cols, rate, after=None):
        self.mode, self.src, self.R, self.rate, self.after = mode, src, rows, rate, after
        self.shape = (NDEV, rows, cols)
        self.holder = self if after is None else after
        self.buf = None
        self.done = 0


class _Exchange:
    def __init__(self, items):
        self.items = items
        self.ins, self.src_pos, self.alias, self.holders, self.out_pos = [], [], {}, [], []
        for s, _, _ in items:
            if not any(s.holder is h for h in self.holders):
                self.holders.append(s.holder)
        for s, _, _ in items:
            self.out_pos.append([n for n, h in enumerate(self.holders) if h is s.holder][0])
            if s.mode == "gather_b":
                self.src_pos.append(None)
            else:
                self.src_pos.append(len(self.ins))
                self.ins.append(s.src)
        for n, h in enumerate(self.holders):
            if h.buf is not None:
                self.alias[len(self.ins)] = n
                self.ins.append(h.buf)
        self.out_shape = [jax.ShapeDtypeStruct(h.shape, BF16) for h in self.holders]
        n = len(items)
        self.scratch = [pltpu.SemaphoreType.DMA((n, NDEV)), pltpu.SemaphoreType.DMA((n, NDEV)),
                        pltpu.SemaphoreType.DMA((n,))]

    def deliver(self, outs):
        for h, o in zip(self.holders, outs):
            h.buf = o

    def _copies(self, ins, outs, sems):
        send_sems, recv_sems, local_sems = sems
        me = _index(_position())
        slot = lambda k: _index(_peer(k))
        local, sends, arrivals = [], [], []
        for n, (s, r0, rc) in enumerate(self.items):
            out, rows = outs[self.out_pos[n]], pl.ds(r0, rc)
            src = out if self.src_pos[n] is None else ins[self.src_pos[n]]
            if s.mode == "a2a":
                local.append(pltpu.make_async_copy(src.at[me, rows], out.at[me, rows], local_sems.at[n]))
                moves = [(k, src.at[slot(k), rows], out.at[me, rows], _peer(k), out.at[slot(k), rows])
                         for k in range(1, NDEV)]
            elif s.mode == "gather_a":
                local.append(pltpu.make_async_copy(src.at[rows], out.at[me, rows], local_sems.at[n]))
                moves = [(k, src.at[rows], out.at[me, rows], _peer(k), out.at[slot(k), rows]) for k in (1, 2, 4, 6)]
            else:
                moves = [(k, src.at[slot(k), rows], out.at[slot(k), rows], _peer(1), out.at[slot(k ^ 1), rows])
                         for k in (2, 4, 6)]
            for k, src_ref, dst_ref, to, landing in moves:
                common = dict(send_sem=send_sems.at[n, k], recv_sem=recv_sems.at[n, k],
                              device_id=to, device_id_type=pl.DeviceIdType.MESH)
                sends.append(pltpu.make_async_remote_copy(src_ref=src_ref, dst_ref=dst_ref, **common))
                arrivals.append(pltpu.make_async_remote_copy(src_ref=landing, dst_ref=landing, **common))
        return local, sends, arrivals

    def start(self, ins, outs, sems):
        local, sends, _ = self._copies(ins, outs, sems)
        for cp in local + sends:
            cp.start()

    def wait(self, ins, outs, sems):
        local, sends, arrivals = self._copies(ins, outs, sems)
        for cp in arrivals:
            cp.wait_recv()
        for cp in sends:
            cp.wait_send()
        for cp in local:
            cp.wait()


class _Rides:
    def __init__(self):
        self.queue = []
        self.flushes = 0

    def add(self, stream):
        self.queue.append(stream)
        return stream

    def take(self, budget, only=None):
        ready = {id(s): (s.after.done if s.after is not None else s.R) for s in self.queue}
        items, left = [], budget
        for s in list(self.queue):
            if only is not None and not any(s is t for t in only):
                continue
            avail = ready[id(s)] - s.done
            if avail <= 0:
                continue
            rc = avail if left == float("inf") else min(avail, int(left * s.rate) // ROW_QUANTUM * ROW_QUANTUM)
            if rc < min(avail, MIN_ROWS):
                continue
            items.append((s, s.done, rc))
            s.done += rc
            left -= rc / s.rate
            if s.done == s.R:
                self.queue.remove(s)
        return items

    def need(self, *streams):
        while any(s.done < s.R for s in streams):
            items = self.take(float("inf"), only=streams)
            comm = _Exchange(items)
            hbm = pl.BlockSpec(memory_space=pl.ANY)
            nin, nout = len(comm.ins), len(comm.holders)

            def body(*refs, comm=comm, nin=nin, nout=nout):
                comm.start(refs[:nin], refs[nin:nin + nout], refs[nin + nout:])
                comm.wait(refs[:nin], refs[nin:nin + nout], refs[nin + nout:])

            outs = pl.pallas_call(body, in_specs=[hbm] * nin, out_specs=[hbm] * nout, out_shape=comm.out_shape,
                                  scratch_shapes=comm.scratch, input_output_aliases=dict(comm.alias),
                                  name=f"exchange_alone{self.flushes}")(*comm.ins)
            self.flushes += 1
            comm.deliver(outs)


def _call(body, *, grid, in_specs, out_specs, out_shape, scratch_shapes=(), args, name, rides=None, us=0.0):
    sem = ("arbitrary",) * len(grid)
    items = rides.take(us) if rides is not None else []
    if not items:
        return pl.pallas_call(body, grid=grid, in_specs=in_specs, out_specs=out_specs, out_shape=out_shape,
                              scratch_shapes=list(scratch_shapes), compiler_params=_params(sem), name=name)(*args)
    comm = _Exchange(items)
    nin, nout, nscr, cin, cout = len(in_specs), len(out_specs), len(scratch_shapes), len(comm.ins), len(comm.holders)
    hbm = pl.BlockSpec(memory_space=pl.ANY)

    def both(*refs):
        ins, c_ins = refs[:nin], refs[nin:nin + cin]
        pos = nin + cin
        outs, c_outs = refs[pos:pos + nout], refs[pos + nout:pos + nout + cout]
        pos += nout + cout
        scr, sems = refs[pos:pos + nscr], refs[pos + nscr:]
        first = functools.reduce(lambda p, q: p & q, [pl.program_id(d) == 0 for d in range(len(grid))])
        last = functools.reduce(lambda p, q: p & q, [pl.program_id(d) == grid[d] - 1 for d in range(len(grid))])

        @pl.when(first)
        def _():
            comm.start(c_ins, c_outs, sems)

        body(*ins, *outs, *scr)

        @pl.when(last)
        def _():
            comm.wait(c_ins, c_outs, sems)

    res = pl.pallas_call(both, grid=grid, in_specs=list(in_specs) + [hbm] * cin, out_specs=list(out_specs) + [hbm] * cout,
                         out_shape=list(out_shape) + comm.out_shape, scratch_shapes=list(scratch_shapes) + comm.scratch,
                         input_output_aliases={nin + i: nout + o for i, o in comm.alias.items()},
                         compiler_params=_params(sem), name=name)(*args, *comm.ins)
    comm.deliver(res[nout:])
    return res[:nout]


US_GLA, US_GLAPOST, US_MERGE, US_FFN = 100.0, 40.0, 45.0, 110.0
MM_VMEM_BUDGET = 44 * 1024 * 1024
MM_FLOPS_PER_US = 700e6


def _mm(a, b, *, ta=False, tb=False, out_dtype=F32, name, rides=None, out_shards=False, m_part=(0, 1)):
    if ta:
        K, M = a.shape
    else:
        M, K = a.shape
    M //= m_part[1]
    sharded = b.ndim == 3
    if sharded:
        ns, n = b.shape[0], b.shape[2]
        Kb, N = (ns * n, b.shape[1]) if tb else (b.shape[1], ns * n)
    elif tb:
        N, Kb = b.shape
    else:
        Kb, N = b.shape
    assert K == Kb, (a.shape, b.shape, ta, tb)
    tm = _pick(M, 1024)
    tn = N // NDEV if out_shards else n if (sharded and not tb) else _pick(N, 1152)
    sa, sb, so = a.dtype.itemsize, b.dtype.itemsize, jnp.dtype(out_dtype).itemsize

    def vmem(tk):
        blocks = 2 * (tm * tk * sa + tk * tn * sb) + 2 * tm * tn * so + (tm * tn * 4 if tk < K else 0)
        return blocks + (tm * K * 2 if ta else 0)

    if sharded and tb:
        tk = n
    else:
        tk = K
        if vmem(K) > MM_VMEM_BUDGET:
            fits = [d for d in range(LANES, K, LANES) if K % d == 0 and vmem(d) <= MM_VMEM_BUDGET]
            tk = max(fits) if fits else _pick(K, 512)
    nk = K // tk
    dims = (((1,), (1 if tb else 0,)), ((), ()))

    def dot(a_ref, b_ref, at_ref):
        if ta:
            k = pl.program_id(2)

            @pl.when(pl.program_id(1) == 0)
            def _():
                at_ref[k] = a_ref[...].astype(BF16).T

            lhs = at_ref[k]
        else:
            lhs = a_ref[...].astype(BF16)
        return lax.dot_general(lhs, b_ref[...].astype(BF16), dims, preferred_element_type=F32)

    def body_one(a_ref, b_ref, o_ref, *at_ref):
        o_ref[...] = dot(a_ref, b_ref, *at_ref or (None,)).astype(o_ref.dtype)

    def body_acc(a_ref, b_ref, o_ref, acc_ref, *at_ref):
        k = pl.program_id(2)
        at = (at_ref or (None,))[0]

        @pl.when(k == 0)
        def _():
            acc_ref[...] = dot(a_ref, b_ref, at)

        @pl.when((k > 0) & (k < nk - 1))
        def _():
            acc_ref[...] += dot(a_ref, b_ref, at)

        @pl.when(k == nk - 1)
        def _():
            o_ref[...] = (acc_ref[...] + dot(a_ref, b_ref, at)).astype(o_ref.dtype)

    i0 = m_part[0] * (M // tm)
    a_spec = (pl.BlockSpec((tk, tm), lambda i, j, k: (k, i0 + i)) if ta
              else pl.BlockSpec((tm, tk), lambda i, j, k: (i0 + i, k)))
    if sharded:
        b_spec = (pl.BlockSpec((None, tn, tk), lambda i, j, k: (k, j, 0)) if tb
                  else pl.BlockSpec((None, tk, tn), lambda i, j, k: (j, k, 0)))
    else:
        b_spec = (pl.BlockSpec((tn, tk), lambda i, j, k: (j, k)) if tb
                  else pl.BlockSpec((tk, tn), lambda i, j, k: (k, j)))
    if out_shards:
        out_spec = pl.BlockSpec((None, tm, tn), lambda i, j, k: (j, i, 0))
        out_shape = jax.ShapeDtypeStruct((NDEV, M, tn), out_dtype)
    else:
        out_spec = pl.BlockSpec((tm, tn), lambda i, j, k: (i, j))
        out_shape = jax.ShapeDtypeStruct((M, N), out_dtype)
    (out,) = _call(
        body_one if nk == 1 else body_acc, grid=(M // tm, N // tn, nk), in_specs=[a_spec, b_spec],
        out_specs=[out_spec], out_shape=[out_shape],
        scratch_shapes=([] if nk == 1 else [pltpu.VMEM((tm, tn), F32)]) + ([pltpu.VMEM((nk, tm, tk), BF16)] if ta else []),
        args=(a, b), name=name,
        rides=rides, us=2.0 * M * N * K / MM_FLOPS_PER_US)
    return out


class V(NamedTuple):
    arr: Any
    width: int
    base: Any


def _v(arr, width=None, col0=0):
    width = arr.shape[-1] if width is None else width
    assert col0 % width == 0
    return V(arr, width, col0 // width)


def _row_specs(v, tr, nrow8, halo):
    main = pl.BlockSpec((tr, v.width), lambda j, i, b=v.base: (i, b + j))
    if not halo:
        return [main]
    per = tr // HALO
    prev = pl.BlockSpec((HALO, v.width), lambda j, i, b=v.base: (jnp.maximum(i * per - 1, 0), b + j))
    nxt = pl.BlockSpec((HALO, v.width), lambda j, i, b=v.base: (jnp.minimum((i + 1) * per, nrow8 - 1), b + j))
    return [prev, main, nxt]


def _par_spec(p):
    if p.base is None:
        return pl.BlockSpec(p.arr.shape, lambda j, i: (0,) * p.arr.ndim)
    return pl.BlockSpec((p.arr.shape[0], p.width), lambda j, i, b=p.base: (0, b + j))


def _load_rows(refs, i, nrow, halo):
    if not halo:
        return refs[0][...].astype(F32)
    prev, main, nxt = refs
    pv = jnp.where(i > 0, prev[...].astype(F32), 0.0)
    nv = jnp.where(i < nrow - 1, nxt[...].astype(F32), 0.0)
    return jnp.concatenate([pv, main[...].astype(F32), nv], axis=0)


def _center_mask(tr):
    row = lax.broadcasted_iota(jnp.int32, (tr + 2 * HALO, 1), 0)
    return ((row >= HALO) & (row < HALO + tr)).astype(F32)


def _rw_fwd(f, rows, params, outs, *, tr, ncol=1, halo=False, name, rides=None, us=0.0):
    S = rows[0].arr.shape[0]
    tr = min(tr, S)
    nrow = S // tr
    per = 3 if halo else 1
    nr, npar = len(rows), len(params)

    def body(*refs):
        i = pl.program_id(1)
        vals = [_load_rows(refs[per * r: per * (r + 1)], i, nrow, halo) for r in range(nr)]
        pvals = [refs[per * nr + p][...].astype(F32) for p in range(npar)]
        out_refs = refs[per * nr + npar:]
        res = f(*vals, *pvals, _center_mask(tr)) if halo else f(*vals, *pvals)
        for r, o in zip(res, out_refs):
            o[...] = (r[HALO:HALO + tr] if halo else r).astype(o.dtype)

    in_specs = []
    for v in rows:
        in_specs += _row_specs(v, tr, S // HALO, halo)
    in_specs += [_par_spec(p) for p in params]
    args = []
    for v in rows:
        args += [v.arr] * per
    args += [p.arr for p in params]
    return _call(
        body, grid=(ncol, nrow), in_specs=in_specs,
        out_specs=[pl.BlockSpec((tr, tot // ncol), lambda j, i: (i, j)) for tot, _ in outs],
        out_shape=[jax.ShapeDtypeStruct((S, tot), dt) for tot, dt in outs],
        args=args, name=name, rides=rides, us=us)


def _rw_bwd(f, rows, params, douts, *, row_grads, par_grads, adds=None, tr, ncol=1, halo=False, name):
    S = rows[0].arr.shape[0]
    tr = min(tr, S)
    nrow = S // tr
    per = 3 if halo else 1
    adds = adds or {}
    nr, npar, nd = len(rows), len(params), len(douts)
    add_keys = sorted(adds)
    rg_idx = [k for k in range(nr) if row_grads[k]]
    pg_idx = [k for k in range(npar) if par_grads[k]]

    def body(*refs):
        i = pl.program_id(1)
        pos = 0
        vals = [_load_rows(refs[per * r: per * (r + 1)], i, nrow, halo) for r in range(nr)]
        pos = per * nr
        pvals = [refs[pos + p][...].astype(F32) for p in range(npar)]
        pos += npar
        dvals = [_load_rows(refs[pos + per * d: pos + per * (d + 1)], i, nrow, halo) for d in range(nd)]
        pos += per * nd
        add_vals = {k: refs[pos + n][...].astype(F32) for n, k in enumerate(add_keys)}
        pos += len(add_keys)
        rg_refs = refs[pos: pos + len(rg_idx)]
        pg_refs = refs[pos + len(rg_idx):]
        if halo:
            center = _center_mask(tr)
            fun = lambda *a: f(*a, center)
        else:
            fun = f
        _, vjp = jax.vjp(fun, *vals, *pvals)
        grads = vjp(tuple(dvals))
        for ref, k in zip(rg_refs, rg_idx):
            g = grads[k]
            g = g[HALO:HALO + tr] if halo else g
            if k in add_vals:
                g = g + add_vals[k]
            ref[...] = g.astype(ref.dtype)
        for ref, k in zip(pg_refs, pg_idx):
            @pl.when(i == 0)
            def _(ref=ref):
                ref[...] = jnp.zeros_like(ref)
            ref[...] += grads[nr + k]

    in_specs, args = [], []
    for v in list(rows):
        in_specs += _row_specs(v, tr, S // HALO, halo)
        args += [v.arr] * per
    in_specs += [_par_spec(p) for p in params]
    args += [p.arr for p in params]
    for v in douts:
        in_specs += _row_specs(v, tr, S // HALO, halo)
        args += [v.arr] * per
    for k in add_keys:
        in_specs += _row_specs(_v(adds[k], rows[k].width), tr, S // HALO, False)
        args += [adds[k]]
    out_specs, out_shape = [], []
    for k in rg_idx:
        w = rows[k].width
        out_specs.append(pl.BlockSpec((tr, w), lambda j, i: (i, j)))
        out_shape.append(jax.ShapeDtypeStruct((S, ncol * w), row_grads[k]))
    for k in pg_idx:
        p = params[k]
        blk = p.arr.shape if p.base is None else (p.arr.shape[0], p.width)
        out_specs.append(pl.BlockSpec((None,) + tuple(blk), lambda j, i, n=len(blk): (j,) + (0,) * n))
        out_shape.append(jax.ShapeDtypeStruct((ncol,) + tuple(blk), F32))
    res = _call(body, grid=(ncol, nrow), in_specs=in_specs, out_specs=out_specs, out_shape=out_shape, args=args, name=name)
    rg = [None] * nr
    for n, k in enumerate(rg_idx):
        rg[k] = res[n]
    pg = [None] * npar
    for n, k in enumerate(pg_idx):
        pg[k] = res[len(rg_idx) + n]
    return rg, pg


def _sigmoid(x):
    return 1.0 / (1.0 + jnp.exp(-x))


def _rms(x, g):
    return x * lax.rsqrt(jnp.mean(x * x, axis=-1, keepdims=True) + EPS) * g


def _f_normmod(x, g, sc, sh):
    return (_rms(x, g) * (1.0 + sc) + sh,)


def _f_resid(x, y, gate, g):
    return (x + gate * _rms(y, g),)


def _f_merge(ga, gb, ya, yb):
    return (_sigmoid(ga) * ya + _sigmoid(gb) * yb,)


def _f_glapost(o, r, g):
    return (_rms(o, g) * (r * _sigmoid(r)),)


def _roll_rows(x, k):
    return pltpu.roll(x, k % x.shape[0], 0)


@functools.partial(jax.custom_vjp, nondiff_argnums=(1,))
def _shift(x, k):
    return _roll_rows(x, k)


def _shift_fwd(x, k):
    return _roll_rows(x, k), None


def _shift_bwd(k, _, g):
    return (_roll_rows(g, -k),)


_shift.defvjp(_shift_fwd, _shift_bwd)


def _taps(w, center):
    return jnp.where(center > 0.0, w, lax.stop_gradient(w))


def _conv3(u, w0, w1, w2, center):
    return (_taps(w0, center) * _shift(u, 2) + _taps(w1, center) * _shift(u, 1)) + _taps(w2, center) * u


def _f_convmix(cb, cc, cx, w0, w1, w2, center):
    return (cb * _conv3(cc * cx, w0, w1, w2, center),)


def _gelu_tanh(x):
    return 0.5 * x * (1.0 + jnp.tanh(math.sqrt(2.0 / math.pi) * (x + 0.044715 * (x * x * x))))


def _f_ffn(gate, up, w0, w1, w2, center):
    return (_gelu_tanh(_conv3(gate, w0, w1, w2, center)) * up,)


@jax.custom_vjp
def _bdot(a, b):
    return jnp.dot(a.astype(BF16), b.astype(BF16), preferred_element_type=F32)


def _bdot_fwd(a, b):
    return _bdot(a, b), (a, b)


def _bdot_bwd(res, g):
    a, b = res
    gb = g.astype(BF16)
    da = lax.dot_general(gb, b.astype(BF16), (((1,), (1,)), ((), ())), preferred_element_type=F32)
    db = lax.dot_general(a.astype(BF16), gb, (((0,), (0,)), ((), ())), preferred_element_type=F32)
    return da, db


_bdot.defvjp(_bdot_fwd, _bdot_bwd)


@jax.custom_vjp
def _log_sigmoid(z):
    return jnp.minimum(z, 0.0) - jnp.log(1.0 + jnp.exp(-jnp.abs(z)))


def _log_sigmoid_fwd(z):
    return _log_sigmoid(z), z


def _log_sigmoid_bwd(z, g):
    return (g * _sigmoid(-z),)


_log_sigmoid.defvjp(_log_sigmoid_fwd, _log_sigmoid_bwd)


def _f_loga(lr, w, b):
    return (_log_sigmoid(_bdot(lr, w) + b) / GLA_TAU,)


def _split3(x):
    hi = x.astype(BF16)
    r1 = x - hi.astype(F32)
    mid = r1.astype(BF16)
    lo = (r1 - mid.astype(F32)).astype(BF16)
    return hi, mid, lo


def _dot_exact(a01, x, dims):
    hi, mid, lo = _split3(x)
    d = lambda p: lax.dot_general(a01, p, dims, preferred_element_type=F32)
    return (d(lo) + d(mid)) + d(hi)


def _dot_exact_lhs(x, a01, dims):
    hi, mid, lo = _split3(x)
    d = lambda p: lax.dot_general(p, a01, dims, preferred_element_type=F32)
    return (d(lo) + d(mid)) + d(hi)


GLA_CHUNKS_PER_STEP = 4

NN = (((1,), (0,)), ((), ()))
NT = (((1,), (1,)), ((), ()))
TN = (((0,), (0,)), ((), ()))


def _bf_dot(a, b, dims):
    return lax.dot_general(a.astype(BF16), b.astype(BF16), dims, preferred_element_type=F32)


def _chunk_decay(a, dv):
    L = a.shape[0]
    r = lax.broadcasted_iota(jnp.int32, (L, L), 0)
    c = lax.broadcasted_iota(jnp.int32, (L, L), 1)
    tri = (r >= c).astype(BF16)
    cum = _dot_exact(tri, a, NN)
    ce = cum[L - 1:L, :]
    dec = jnp.exp(_dot_exact_lhs(a, jnp.ones((L, dv), BF16), TN))
    return cum, ce, jnp.exp(ce - cum), dec


def _gla_fwd(P, a, *, qk, vw, name, rides=None, us=0.0):
    S = P.shape[0]
    nc = S // CHUNK
    cps = math.gcd(nc, GLA_CHUNKS_PER_STEP)
    rows = cps * CHUNK
    dk, dv = qk // HEADS, vw // HEADS
    scale = dk ** -0.5

    def body(q_ref, k_ref, v_ref, a_ref, o_ref, st_ref, state):
        @pl.when(pl.program_id(0) == 0)
        def _():
            state[...] = jnp.zeros_like(state)

        for h in range(HEADS):
            ks, vs = slice(h * dk, (h + 1) * dk), slice(h * dv, (h + 1) * dv)
            s = state[h]
            for c in range(cps):
                rs = slice(c * CHUNK, (c + 1) * CHUNK)
                _, _, E, dec = _chunk_decay(a_ref[rs, ks], dv)
                kd = k_ref[rs, ks] * E
                s = dec * s + _bf_dot(kd, v_ref[rs, vs], TN)
                st_ref[c, h] = s
                o_ref[rs, vs] = _bf_dot(q_ref[rs, ks] * scale, s, NN).astype(o_ref.dtype)
            state[h] = s

    return _call(
        body, grid=(nc // cps,),
        in_specs=[pl.BlockSpec((rows, qk), lambda i: (i, 0)),
                  pl.BlockSpec((rows, qk), lambda i: (i, 1)),
                  pl.BlockSpec((rows, vw), lambda i: (i, 2 * qk // vw)),
                  pl.BlockSpec((rows, qk), lambda i: (i, 0))],
        out_specs=[pl.BlockSpec((rows, vw), lambda i: (i, 0)),
                   pl.BlockSpec((cps, HEADS, dk, dv), lambda i: (i, 0, 0, 0))],
        out_shape=[jax.ShapeDtypeStruct((S, vw), BF16), jax.ShapeDtypeStruct((nc, HEADS, dk, dv), F32)],
        scratch_shapes=[pltpu.VMEM((HEADS, dk, dv), F32)], args=(P, P, P, a), name=name, rides=rides, us=us)


def _gla_bwd(P, a, states, do, *, qk, vw, name):
    S = P.shape[0]
    nc = S // CHUNK
    cps = math.gcd(nc, GLA_CHUNKS_PER_STEP)
    rows = cps * CHUNK
    nstep = nc // cps
    dk, dv = qk // HEADS, vw // HEADS
    scale = dk ** -0.5

    def body(q_ref, k_ref, v_ref, a_ref, do_ref, sc_ref, sp_ref, dq_ref, dk_ref, dv_ref, da_ref, ds_carry):
        i = pl.program_id(0)

        @pl.when(i == 0)
        def _():
            ds_carry[...] = jnp.zeros_like(ds_carry)

        has_prev = (i < nstep - 1).astype(F32)
        L = CHUNK
        r = lax.broadcasted_iota(jnp.int32, (L, L), 0)
        c = lax.broadcasted_iota(jnp.int32, (L, L), 1)
        tri_t = (c >= r).astype(BF16)
        last = (lax.broadcasted_iota(jnp.int32, (L, 1), 0) == L - 1).astype(F32)
        for h in range(HEADS):
            ks, vs = slice(h * dk, (h + 1) * dk), slice(h * dv, (h + 1) * dv)
            carry = ds_carry[h]
            for n in reversed(range(cps)):
                rs = slice(n * CHUNK, (n + 1) * CHUNK)
                _, _, E, dec = _chunk_decay(a_ref[rs, ks], dv)
                kk, vv, dov = k_ref[rs, ks], v_ref[rs, vs], do_ref[rs, vs]
                kd = kk * E
                ds = carry + _bf_dot(q_ref[rs, ks] * scale, dov, TN)
                dq_ref[rs, ks] = (_bf_dot(dov, sc_ref[n, h], NT) * scale).astype(dq_ref.dtype)
                dkd = _bf_dot(vv, ds, NT)
                dv_ref[rs, vs] = _bf_dot(kd, ds, NN).astype(dv_ref.dtype)
                dk_ref[rs, ks] = (dkd * E).astype(dk_ref.dtype)
                darg = dkd * kd
                before = sc_ref[n - 1, h] if n > 0 else sp_ref[0, h] * has_prev
                w = ds * before * dec
                dce = _dot_exact(jnp.ones((SUBLANES, dv), BF16), w, NT)[0:1, :]
                dce = dce + jnp.sum(darg, axis=0, keepdims=True)
                dcum = last * dce - darg
                da_ref[rs, ks] = _dot_exact(tri_t, dcum, NN)
                carry = dec * ds
            ds_carry[h] = carry

    rev = lambda i: nstep - 1 - i
    return pl.pallas_call(
        body, grid=(nstep,),
        in_specs=[pl.BlockSpec((rows, qk), lambda i: (rev(i), 0)),
                  pl.BlockSpec((rows, qk), lambda i: (rev(i), 1)),
                  pl.BlockSpec((rows, vw), lambda i: (rev(i), 2 * qk // vw)),
                  pl.BlockSpec((rows, qk), lambda i: (rev(i), 0)),
                  pl.BlockSpec((rows, vw), lambda i: (rev(i), 0)),
                  pl.BlockSpec((cps, HEADS, dk, dv), lambda i: (rev(i), 0, 0, 0)),
                  pl.BlockSpec((1, HEADS, dk, dv), lambda i: (jnp.maximum(rev(i) * cps - 1, 0), 0, 0, 0))],
        out_specs=[pl.BlockSpec((rows, qk), lambda i: (rev(i), 0)),
                   pl.BlockSpec((rows, qk), lambda i: (rev(i), 0)),
                   pl.BlockSpec((rows, vw), lambda i: (rev(i), 0)),
                   pl.BlockSpec((rows, qk), lambda i: (rev(i), 0))],
        out_shape=[jax.ShapeDtypeStruct((S, qk), BF16), jax.ShapeDtypeStruct((S, qk), BF16),
                   jax.ShapeDtypeStruct((S, vw), BF16), jax.ShapeDtypeStruct((S, qk), F32)],
        scratch_shapes=[pltpu.VMEM((HEADS, dk, dv), F32)],
        compiler_params=_params(("arbitrary",)), name=name)(P, P, P, a, do, states, states)


def _loss_head(y, target, *, tr, name):
    S, D = y.shape
    tr = min(tr, S)

    def body(y_ref, t_ref, l_ref, dy_ref):
        @pl.when(pl.program_id(0) == 0)
        def _():
            l_ref[...] = jnp.zeros_like(l_ref)

        e = y_ref[...] - t_ref[...]
        dy_ref[...] = e / D
        l_ref[...] += 0.5 * jnp.sum(jnp.mean(e * e, axis=-1, keepdims=True), axis=0, keepdims=True)

    loss, dy = pl.pallas_call(
        body, grid=(S // tr,),
        in_specs=[pl.BlockSpec((tr, D), lambda i: (i, 0))] * 2,
        out_specs=[pl.BlockSpec((SUBLANES, LANES), lambda i: (0, 0)), pl.BlockSpec((tr, D), lambda i: (i, 0))],
        out_shape=[jax.ShapeDtypeStruct((SUBLANES, LANES), F32), jax.ShapeDtypeStruct((S, D), F32)],
        compiler_params=_params(("arbitrary",)), name=name)(y, target)
    return loss[0, 0], dy


ADAMW_BYTES_PER_US = 2.5e6


def _adamw(parts, w, m, v, *, name, rides=None):
    G, R, C = w.shape
    n = parts[0].shape[0]
    unit = 2 * SUBLANES if parts[0].dtype == BF16 else SUBLANES
    tr = _pick(R, max(unit, (128 * 1024) // C), unit)
    nblk = R // tr
    bc1 = 1.0 - ADAM_B1 ** ADAM_STEP
    bc2 = 1.0 - ADAM_B2 ** ADAM_STEP

    def body(*refs):
        p_refs = refs[:G]
        w_ref, m_ref, v_ref, g_ref, d_ref, mo_ref, vo_ref = refs[G:]
        for q in range(G):
            @pl.when(pl.program_id(0) == q)
            def _(q=q):
                g = p_refs[q][0].astype(F32)
                for s in range(1, n):
                    g = g + p_refs[q][s].astype(F32)
                m2 = ADAM_B1 * m_ref[...] + (1.0 - ADAM_B1) * g
                v2 = ADAM_B2 * v_ref[...] + (1.0 - ADAM_B2) * (g * g)
                g_ref[...] = g
                mo_ref[...] = m2
                vo_ref[...] = v2
                d_ref[...] = -ADAM_LR * ((m2 / bc1) / (jnp.sqrt(v2 / bc2) + ADAM_EPS) + ADAM_WD * w_ref[...])

    def part_spec(q):
        return pl.BlockSpec((n, tr, C), lambda g, i: (0, jnp.where(g == q, i, jnp.where(g < q, 0, nblk - 1)), 0))

    blk = pl.BlockSpec((None, tr, C), lambda g, i: (g, i, 0))
    traffic = G * R * C * (n * parts[0].dtype.itemsize + 7 * 4)
    return _call(
        body, grid=(G, nblk), in_specs=[part_spec(q) for q in range(G)] + [blk, blk, blk],
        out_specs=[blk] * 4, out_shape=[jax.ShapeDtypeStruct((G, R, C), F32)] * 4,
        args=(*parts, w, m, v), name=name, rides=rides, us=traffic / ADAMW_BYTES_PER_US)


def _adamw_nd(parts, w, m, v, *, name):
    shape = w.shape
    C = shape[-1]
    R = math.prod(shape[:-1])
    outs = _adamw([parts.reshape(parts.shape[0], R, C)], w.reshape(1, R, C), m.reshape(1, R, C), v.reshape(1, R, C),
                  name=name)
    return [o.reshape(shape) for o in outs]


def _all_gather(arrs, *, name):
    n = len(arrs)
    HBM = pl.BlockSpec(memory_space=pl.ANY)

    def body(*refs):
        ins, outs = refs[:n], refs[n:2 * n]
        send_sems, recv_sems, local_sems = refs[2 * n:]
        x, y, c = _position()
        me, sibling = (x, y, c), (x, y, 1 - c)
        chips = [(1 - x, y), (x, 1 - y), (1 - x, 1 - y)]

        def copy(t, k, block, to, src=None):
            rows = outs[t].at[_index(block)]
            return pltpu.make_async_remote_copy(
                src_ref=rows if src is None else src, dst_ref=rows,
                send_sem=send_sems.at[t, k], recv_sem=recv_sems.at[t, k],
                device_id=to, device_id_type=pl.DeviceIdType.MESH)

        started = []
        mine = []
        for t in range(n):
            cp = pltpu.make_async_copy(ins[t], outs[t].at[_index(me)], local_sems.at[t])
            cp.start()
            mine.append(cp)
            first = [copy(t, 0, me, sibling, src=ins[t])]
            first += [copy(t, 1 + j, me, (*chip, c), src=ins[t]) for j, chip in enumerate(chips)]
            for cp in first:
                cp.start()
            started += first
        for t in range(n):
            for j, chip in enumerate(chips):
                copy(t, 1 + j, (*chip, c), me).wait_recv()
                cp = copy(t, 4 + j, (*chip, c), sibling)
                cp.start()
                started.append(cp)
        for t in range(n):
            copy(t, 0, sibling, me).wait_recv()
            for j, chip in enumerate(chips):
                copy(t, 4 + j, (*chip, 1 - c), me).wait_recv()
        for cp in started:
            cp.wait_send()
        for cp in mine:
            cp.wait()

    return pl.pallas_call(
        body, in_specs=[HBM] * n, out_specs=[HBM] * n,
        out_shape=[jax.ShapeDtypeStruct((NDEV,) + a.shape, a.dtype) for a in arrs],
        scratch_shapes=[pltpu.SemaphoreType.DMA((n, 7)), pltpu.SemaphoreType.DMA((n, 7)),
                        pltpu.SemaphoreType.DMA((n,))],
        name=name)(*arrs)


class _Pack:
    def __init__(self, arrs):
        self.shapes = [a.shape for a in arrs]
        self.sizes = [math.prod(s) for s in self.shapes]
        total = sum(self.sizes)
        unit = SUBLANES * LANES
        self.padded = -(-total // unit) * unit
        flat = jnp.concatenate([a.reshape(-1).astype(F32) for a in arrs] + [jnp.zeros((self.padded - total,), F32)])
        self.packed = flat.reshape(self.padded // LANES, LANES)

    def unpack(self, gathered):
        flat = gathered.reshape(NDEV, self.padded)
        out, off = [], 0
        for shape, size in zip(self.shapes, self.sizes):
            out.append(flat[:, off:off + size].reshape((NDEV,) + tuple(shape)))
            off += size
        return out


def _cols_from_shards(g):
    g = jnp.moveaxis(g, 0, -2)
    return g.reshape(g.shape[:-2] + (g.shape[-2] * g.shape[-1],))


def _cols_to_shards(w):
    K, N = w.shape
    return jnp.moveaxis(w.reshape(K, NDEV, N // NDEV), 1, 0)


def kernel(x, c, w_ada, b_ada, norm_g, w_in, w_a2, b_a2, gla_norm_g, w_out_gla, conv_mix_w, w_out_conv, w_o, w_up, ffn_conv_w, w_down, loss_target, m_w_ada, m_b_ada, m_norm_g, m_w_in, m_w_a2, m_b_a2, m_gla_norm_g, m_w_out_gla, m_conv_mix_w, m_w_out_conv, m_w_o, m_w_up, m_ffn_conv_w, m_w_down, v_w_ada, v_b_ada, v_norm_g, v_w_in, v_w_a2, v_b_a2, v_gla_norm_g, v_w_out_gla, v_conv_mix_w, v_w_out_conv, v_w_o, v_w_up, v_ffn_conv_w, v_w_down):
    depth = w_ada.shape[0]
    S, D = x.shape[1], x.shape[2]
    QK = w_a2.shape[2] * NDEV
    LR = w_a2.shape[1]
    VW = w_out_gla.shape[1]
    CW = w_out_conv.shape[1]
    FF = w_down.shape[1] * NDEV
    NIN = w_in.shape[2] * NDEV
    NADA = w_ada.shape[2]
    LRP = -(-LR // LANES) * LANES
    off_lr = 2 * QK + 2 * VW
    NP = NIN - LR + LRP
    assert NIN == 2 * QK + 2 * VW + LR + 3 * CW + 2 * D
    xi, yi, ci = _position()
    me = _index((xi, yi, ci))
    xs = x.reshape(S, D)
    target = loss_target.reshape(S, D)
    TR = 256
    TRC = 512
    TC = _pick(CW, 512)
    TCF = _pick(FF, 512)

    small = _Pack([c, norm_g, w_a2, conv_mix_w, ffn_conv_w])
    big_names = ["w_in", "w_out_gla", "w_out_conv", "w_o", "w_up", "w_down"]
    big = dict(w_in=w_in, w_out_gla=w_out_gla, w_out_conv=w_out_conv, w_o=w_o, w_up=w_up, w_down=w_down)

    def matrix(key, g):
        if key == "w_up":
            return g
        if key in ("w_o", "w_down"):
            return g.reshape(g.shape[0] * g.shape[1], g.shape[2])
        if key != "w_in":
            return _cols_from_shards(g)
        pieces = [g[d][:, lo - d * nin_shard:hi - d * nin_shard] for _, d, lo, hi in sorted(in_pieces)]
        return jnp.concatenate(pieces + [jnp.zeros((D, LRP - LR), BF16)], axis=-1)

    nin_shard = NIN // NDEV
    in_pieces = []
    for g_lo, g_hi, to in ((0, off_lr, 0), (off_lr, off_lr + LR, NIN - LR), (off_lr + LR, NIN, off_lr)):
        for d in range(NDEV):
            lo, hi = max(g_lo, d * nin_shard), min(g_hi, (d + 1) * nin_shard)
            if lo < hi:
                in_pieces.append((to + lo - g_lo, d, lo, hi))

    def in_grad_blocks(gW_all):
        blocks = []
        for d in range(NDEV):
            mine = sorted((lo, to, hi) for to, dd, lo, hi in in_pieces if dd == d)
            blocks.append(jnp.concatenate([gW_all[:, to:to + hi - lo] for lo, to, hi in mine], axis=1))
        return jnp.stack(blocks)

    (small_g, w_in0_g) = _all_gather([small.packed, w_in[0].astype(BF16)], name="gather_first")
    c_all, norm_g_s, w_a2_s, conv_w_s, ffn_w_s = small.unpack(small_g)
    c_all = c_all.reshape(NDEV, D)
    norm_g_f = _cols_from_shards(norm_g_s)
    w_a2_f = _cols_from_shards(w_a2_s)
    conv_w_f = _cols_from_shards(conv_w_s)
    ffn_w_f = _cols_from_shards(ffn_w_s)
    w_a2_p = jnp.concatenate([w_a2_f, jnp.zeros((depth, LRP - LR, QK), F32)], axis=1)

    rides = _Rides()
    GATHER_PARAMS_PER_US = 80e3
    FORWARD_PARAMS_PER_US = 800e3
    EXCHANGE_PARAMS_PER_US = 43e3
    weight_streams = {}
    for l in range(depth):
        for key in big_names:
            if (l, key) == (0, "w_in"):
                continue
            shard = big[key][l].astype(BF16)
            rows, cols = shard.shape
            first = rides.add(_Stream("gather_a", shard, rows, cols, GATHER_PARAMS_PER_US / (NDEV * cols)))
            second = rides.add(_Stream("gather_b", None, rows, cols, FORWARD_PARAMS_PER_US / (NDEV * cols), after=first))
            weight_streams[l, key] = (first, second)
    W = {(0, "w_in"): matrix("w_in", w_in0_g)}

    def weight(l, key):
        if (l, key) not in W:
            first, second = weight_streams[l, key]
            rides.need(first, second)
            W[l, key] = matrix(key, first.buf)
        return W[l, key]

    MP = LANES
    c_pad = jnp.concatenate([c_all, jnp.zeros((MP - NDEV, D), F32)], axis=0)
    (silu_c,) = _rw_fwd(lambda t: (t * _sigmoid(t),), [_v(c_pad)], [], [(D, BF16)], tr=MP, name="silu_c")
    mod_cols = _mm(silu_c, w_ada, name="ada_fwd")[:NDEV].reshape(NDEV, depth, NADA)
    (mod_g,) = _all_gather([jnp.moveaxis(mod_cols, 1, 0)], name="gather_mod")
    mod_all = jnp.moveaxis(mod_g, 0, 2).reshape(depth, NDEV, NDEV * NADA)
    mod_me = lax.dynamic_slice_in_dim(mod_all, me, 1, axis=1).reshape(depth, NDEV * NADA)
    (mod_me,) = _rw_fwd(lambda a, b: (a + b,), [_v(mod_me), _v(b_ada)], [], [(NDEV * NADA, F32)],
                        tr=depth, name="mod_bias")

    col = {}
    off = 0
    for nm, wd in (("q", QK), ("k", QK), ("v", VW), ("r", VW), ("cb", CW), ("cc", CW), ("cx", CW),
                   ("ga", D), ("gb", D), ("lr", LRP)):
        col[nm] = off
        off += wd

    saved = []
    xin = xs
    for l in range(depth):
        sh1, sc1, g1, sh2, sc2, g2 = [mod_me[l:l + 1, i * D:(i + 1) * D] for i in range(6)]
        ng = [norm_g_f[l, i:i + 1] for i in range(4)]
        cw = [conv_w_f[l, i:i + 1] for i in range(CONV_K)]
        fw = [ffn_w_f[l, i:i + 1] for i in range(CONV_K)]
        gn = gla_norm_g[l:l + 1]
        ba2 = b_a2[l:l + 1]
        def proj(act, key, name, out_dtype=BF16):
            return _mm(act, weight(l, key), name=name, rides=rides, out_dtype=out_dtype)

        (h1,) = _rw_fwd(_f_normmod, [_v(xin)], [_v(ng[0]), _v(sc1), _v(sh1)], [(D, BF16)], tr=TR, name=f"normmod1_{l}")
        P = proj(h1, "w_in", f"in_proj{l}")
        (a,) = _rw_fwd(_f_loga, [_v(P, LRP, col["lr"])], [V(w_a2_p[l], QK, None), _v(ba2)], [(QK, F32)],
                       tr=TR, name=f"loga{l}")
        o, states = _gla_fwd(P, a, qk=QK, vw=VW, name=f"gla_fwd{l}", rides=rides, us=US_GLA)
        dvh = VW // HEADS
        (ya_in,) = _rw_fwd(_f_glapost, [_v(o, dvh), _v(P, dvh, col["r"])], [V(gn, dvh, None)], [(VW, BF16)],
                           tr=TR, ncol=HEADS, name=f"glapost{l}", rides=rides, us=US_GLAPOST)
        ya = proj(ya_in, "w_out_gla", f"out_gla{l}")
        (yb_in,) = _rw_fwd(_f_convmix, [_v(P, TC, col["cb"]), _v(P, TC, col["cc"]), _v(P, TC, col["cx"])],
                           [_v(w, TC) for w in cw], [(CW, BF16)], tr=TRC, ncol=CW // TC, halo=True, name=f"convmix{l}")
        yb = proj(yb_in, "w_out_conv", f"out_conv{l}")
        (mg,) = _rw_fwd(_f_merge, [_v(P, D, col["ga"]), _v(P, D, col["gb"]), _v(ya), _v(yb)], [], [(D, BF16)],
                        tr=TR, name=f"merge{l}", rides=rides, us=US_MERGE)
        y = proj(mg, "w_o", f"o_proj{l}", F32)
        (x1,) = _rw_fwd(_f_resid, [_v(xin), _v(y)], [_v(g1), _v(ng[1])], [(D, F32)], tr=TR, name=f"resid1_{l}")
        (h2,) = _rw_fwd(_f_normmod, [_v(x1)], [_v(ng[2]), _v(sc2), _v(sh2)], [(D, BF16)], tr=TR, name=f"normmod2_{l}")
        U = proj(h2, "w_up", f"up_proj{l}")
        (ff,) = _rw_fwd(_f_ffn, [_v(U, TCF, 0), _v(U, TCF, FF)], [_v(w, TCF) for w in fw], [(FF, BF16)],
                        tr=TRC, ncol=FF // TCF, halo=True, name=f"ffn{l}", rides=rides, us=US_FFN)
        y2 = proj(ff, "w_down", f"down_proj{l}", F32)
        (x2,) = _rw_fwd(_f_resid, [_v(x1), _v(y2)], [_v(g2), _v(ng[3])], [(D, F32)], tr=TR, name=f"resid2_{l}")
        saved.append(dict(xin=xin, h1=h1, P=P, a=a, o=o, states=states, ya_in=ya_in, ya=ya, yb_in=yb_in, yb=yb,
                          mg=mg, y=y, x1=x1, h2=h2, U=U, ff=ff, y2=y2, mods=(sh1, sc1, g1, sh2, sc2, g2),
                          ng=ng, cw=cw, fw=fw, gn=gn, ba2=ba2))
        xin = x2

    loss_local, dx = _loss_head(xin, target, tr=TR, name="loss_head")
    loss = lax.psum(loss_local, MESH_AXES)

    grad_streams = {}
    IN_WGRAD_PARTS = 2

    def send(l, key, blocks):
        _, rows, cols = blocks.shape
        stream = rides.add(_Stream("a2a", blocks, rows, cols, EXCHANGE_PARAMS_PER_US / (NDEV * cols)))
        grad_streams.setdefault((l, key), []).append(stream)

    def bmm(a, b, **kw):
        kw.setdefault("out_dtype", BF16)
        return _mm(a, b, rides=rides, **kw)

    gsmall = [None] * depth
    dmod = [None] * depth
    for l in reversed(range(depth)):
        s = saved[l]
        sh1, sc1, g1, sh2, sc2, g2 = s["mods"]
        ng, cw, fw = s["ng"], s["cw"], s["fw"]
        (dx1_a, dy2), (dg2, dng3) = _rw_bwd(
            _f_resid, [_v(s["x1"]), _v(s["y2"])], [_v(g2), _v(ng[3])], [_v(dx)],
            row_grads=[F32, BF16], par_grads=[True, True], tr=TR, name=f"resid2_bwd{l}")
        gW_down = bmm(s["ff"], dy2, ta=True, out_dtype=BF16, name=f"down_wgrad{l}")
        send(l, "w_down", gW_down.reshape(NDEV, FF // NDEV, D))
        dff = bmm(dy2, weight(l, "w_down"), tb=True, name=f"down_dgrad{l}")
        U = s["U"]
        (dgate, dup), dfw = _rw_bwd(
            _f_ffn, [_v(U, TCF, 0), _v(U, TCF, FF)], [_v(w, TCF) for w in fw], [_v(dff, TCF)],
            row_grads=[BF16, BF16], par_grads=[True] * 3, tr=TRC, ncol=FF // TCF, halo=True, name=f"ffn_bwd{l}")
        dU = jnp.concatenate([dgate, dup], axis=1)
        gW_up = bmm(s["h2"], dU, ta=True, out_dtype=BF16, out_shards=True, name=f"up_wgrad{l}")
        send(l, "w_up", gW_up)
        dh2 = bmm(dU, weight(l, "w_up"), tb=True, name=f"up_dgrad{l}")
        (dx1,), (dng2, dsc2, dsh2) = _rw_bwd(
            _f_normmod, [_v(s["x1"])], [_v(ng[2]), _v(sc2), _v(sh2)], [_v(dh2)],
            row_grads=[F32], par_grads=[True] * 3, adds={0: dx1_a}, tr=TR, name=f"normmod2_bwd{l}")
        (dxin_a, dy), (dg1, dng1) = _rw_bwd(
            _f_resid, [_v(s["xin"]), _v(s["y"])], [_v(g1), _v(ng[1])], [_v(dx1)],
            row_grads=[F32, BF16], par_grads=[True, True], tr=TR, name=f"resid1_bwd{l}")
        gW_o = bmm(s["mg"], dy, ta=True, out_dtype=BF16, name=f"o_wgrad{l}")
        send(l, "w_o", gW_o.reshape(NDEV, D // NDEV, D))
        dmg = bmm(dy, weight(l, "w_o"), tb=True, name=f"o_dgrad{l}")
        P = s["P"]
        (dga, dgb, dya, dyb), _ = _rw_bwd(
            _f_merge, [_v(P, D, col["ga"]), _v(P, D, col["gb"]), _v(s["ya"]), _v(s["yb"])], [], [_v(dmg)],
            row_grads=[BF16] * 4, par_grads=[], tr=TR, name=f"merge_bwd{l}")
        gW_og = bmm(s["ya_in"], dya, ta=True, out_dtype=BF16, name=f"out_gla_wgrad{l}")
        send(l, "w_out_gla", _cols_to_shards(gW_og))
        dya_in = bmm(dya, weight(l, "w_out_gla"), tb=True, name=f"out_gla_dgrad{l}")
        gW_oc = bmm(s["yb_in"], dyb, ta=True, out_dtype=BF16, name=f"out_conv_wgrad{l}")
        send(l, "w_out_conv", _cols_to_shards(gW_oc))
        dyb_in = bmm(dyb, weight(l, "w_out_conv"), tb=True, name=f"out_conv_dgrad{l}")
        (dcb, dcc, dcx), dcw = _rw_bwd(
            _f_convmix, [_v(P, TC, col["cb"]), _v(P, TC, col["cc"]), _v(P, TC, col["cx"])],
            [_v(w, TC) for w in cw], [_v(dyb_in, TC)],
            row_grads=[BF16] * 3, par_grads=[True] * 3, tr=TRC, ncol=CW // TC, halo=True, name=f"convmix_bwd{l}")
        dvh = VW // HEADS
        (do, dr), (dgn,) = _rw_bwd(
            _f_glapost, [_v(s["o"], dvh), _v(P, dvh, col["r"])], [V(s["gn"], dvh, None)], [_v(dya_in, dvh)],
            row_grads=[BF16, BF16], par_grads=[True], tr=TR, ncol=HEADS, name=f"glapost_bwd{l}")
        dq, dk, dv, da = _gla_bwd(P, s["a"], s["states"], do, qk=QK, vw=VW, name=f"gla_bwd{l}")
        (dlr,), (dwa2, dba2) = _rw_bwd(
            _f_loga, [_v(P, LRP, col["lr"])], [V(w_a2_p[l], QK, None), _v(s["ba2"])], [_v(da)],
            row_grads=[BF16], par_grads=[True, True], tr=TR, name=f"loga_bwd{l}")
        dP = jnp.concatenate([dq, dk, dv, dr, dcb, dcc, dcx, dga, dgb, dlr], axis=1)
        for part in range(IN_WGRAD_PARTS):
            gW_all = bmm(s["h1"], dP, ta=True, out_dtype=BF16, m_part=(part, IN_WGRAD_PARTS), name=f"in_wgrad{l}_{part}")
            send(l, "w_in", in_grad_blocks(gW_all))
        dh1 = bmm(dP, weight(l, "w_in"), tb=True, name=f"in_dgrad{l}")
        (dx,), (dng0, dsc1, dsh1) = _rw_bwd(
            _f_normmod, [_v(s["xin"])], [_v(ng[0]), _v(sc1), _v(sh1)], [_v(dh1)],
            row_grads=[F32], par_grads=[True] * 3, adds={0: dxin_a}, tr=TR, name=f"normmod1_bwd{l}")
        vec = lambda t: t.reshape(1, -1)
        gsmall[l] = dict(
            norm_g=jnp.concatenate([vec(dng0), vec(dng1), vec(dng2), vec(dng3)], axis=0),
            w_a2=dwa2[0, :LR], b_a2=dba2.reshape(QK), gla_norm_g=jnp.sum(dgn, axis=0).reshape(dvh),
            conv_mix_w=jnp.concatenate([vec(t) for t in dcw], axis=0),
            ffn_conv_w=jnp.concatenate([vec(t) for t in dfw], axis=0))
        dmod[l] = jnp.concatenate([vec(t) for t in (dsh1, dsc1, dg1, dsh2, dsc2, dg2)], axis=1).reshape(-1)
    grad_x = dx.reshape(x.shape)

    stack = lambda key, src: jnp.stack([src[l][key] for l in range(depth)])
    small_names = ["norm_g", "w_a2", "b_a2", "gla_norm_g", "conv_mix_w", "ffn_conv_w"]
    gpack = _Pack([jnp.stack(dmod)] + [stack(k, gsmall) for k in small_names])
    (gsm,) = _all_gather([gpack.packed], name="gather_small_grads")
    gs = gpack.unpack(gsm)
    dmod_all = gs[0]
    parts_small = dict(zip(small_names, gs[1:]))

    def my_cols(t, n):
        return lax.dynamic_slice_in_dim(t, me * n, n, axis=t.ndim - 1)

    gW_ada = []
    for l in range(depth):
        dcols = my_cols(dmod_all[:, l], NADA)
        dcols = jnp.concatenate([dcols, jnp.zeros((MP - NDEV, NADA), F32)], axis=0)
        gW_ada.append(bmm(silu_c, dcols, ta=True, out_dtype=F32, name=f"ada_wgrad{l}")[None])
    parts = {}
    parts["b_ada"] = dmod_all
    parts["norm_g"] = my_cols(parts_small["norm_g"], D // NDEV)
    parts["w_a2"] = my_cols(parts_small["w_a2"], QK // NDEV)
    parts["b_a2"] = parts_small["b_a2"]
    parts["gla_norm_g"] = parts_small["gla_norm_g"]
    parts["conv_mix_w"] = my_cols(parts_small["conv_mix_w"], CW // NDEV)
    parts["ffn_conv_w"] = my_cols(parts_small["ffn_conv_w"], FF // NDEV)

    weights = dict(w_ada=(w_ada, m_w_ada, v_w_ada), b_ada=(b_ada, m_b_ada, v_b_ada),
                   norm_g=(norm_g, m_norm_g, v_norm_g), w_in=(w_in, m_w_in, v_w_in), w_a2=(w_a2, m_w_a2, v_w_a2),
                   b_a2=(b_a2, m_b_a2, v_b_a2), gla_norm_g=(gla_norm_g, m_gla_norm_g, v_gla_norm_g),
                   w_out_gla=(w_out_gla, m_w_out_gla, v_w_out_gla), conv_mix_w=(conv_mix_w, m_conv_mix_w, v_conv_mix_w),
                   w_out_conv=(w_out_conv, m_w_out_conv, v_w_out_conv), w_o=(w_o, m_w_o, v_w_o),
                   w_up=(w_up, m_w_up, v_w_up), ffn_conv_w=(ffn_conv_w, m_ffn_conv_w, v_ffn_conv_w),
                   w_down=(w_down, m_w_down, v_w_down))
    done = {}
    for nm in ["w_ada", "w_down", "w_up", "w_o", "w_out_gla", "w_out_conv", "w_in"]:
        w, m, v = weights[nm]
        if nm == "w_ada":
            done[nm] = _adamw(gW_ada, w, m, v, name=f"adamw_{nm}")
            continue
        streams = [s for l in range(depth) for s in grad_streams[l, nm]]
        rides.need(*streams)
        groups, rows = len(streams), w.shape[1] * depth // len(streams)
        split = lambda t: t.reshape(groups, rows, t.shape[2])
        outs = _adamw([s.buf for s in streams], split(w), split(m), split(v), name=f"adamw_{nm}")
        done[nm] = [o.reshape(w.shape) for o in outs]
    for nm, (w, m, v) in weights.items():
        if nm not in done:
            done[nm] = _adamw_nd(parts[nm], w, m, v, name=f"adamw_{nm}")
    grads, deltas, new_m, new_v = zip(*[done[nm] for nm in weights])
    return (loss, grad_x, *grads, *deltas, *new_m, *new_v)
```

```python
import functools
import math
from typing import Any, NamedTuple

import jax
import jax.numpy as jnp
from jax import lax
from jax.experimental import pallas as pl
from jax.experimental.pallas import tpu as pltpu

F32 = jnp.float32
BF16 = jnp.bfloat16
MESH_AXES = ("x", "y", "c")
NDEV = 8
EPS = 1e-6
CHUNK = 64
HEADS = 4
GLA_TAU = 16.0
CONV_K = 3
ADAM_LR = 0.001
ADAM_B1 = 0.9
ADAM_B2 = 0.999
ADAM_EPS = 1e-08
ADAM_WD = 0.01
ADAM_STEP = 10

LANES = 128
SUBLANES = 8
HALO = 2 * SUBLANES
VMEM_LIMIT = 56 * 1024 * 1024


def _params(sem=None):
    return pltpu.CompilerParams(dimension_semantics=sem, vmem_limit_bytes=VMEM_LIMIT)


def _pick(n, cap, unit=LANES):
    best = None
    for d in range(unit, min(n, cap) + 1, unit):
        if n % d == 0:
            best = d
    return best if best is not None else n


def _position():
    return lax.axis_index("x"), lax.axis_index("y"), lax.axis_index("c")


def _index(p):
    return 4 * p[0] + 2 * p[1] + p[2]


def _peer(k):
    x, y, c = _position()
    return ((1 - x) if k & 4 else x, (1 - y) if k & 2 else y, (1 - c) if k & 1 else c)


ROW_QUANTUM = 2 * SUBLANES
MIN_ROWS = 128


class _Stream:
    def __init__(self, mode, src, rows, cols, rate, after=None):
        self.mode, self.src, self.R, self.rate, self.after = mode, src, rows, rate, after
        self.shape = (NDEV, rows, cols)
        self.holder = self if after is None else after
        self.buf = None
        self.done = 0


class _Exchange:
    def __init__(self, items):
        self.items = items
        self.ins, self.src_pos, self.alias, self.holders, self.out_pos = [], [], {}, [], []
        for s, _, _ in items:
            if not any(s.holder is h for h in self.holders):
                self.holders.append(s.holder)
        for s, _, _ in items:
            self.out_pos.append([n for n, h in enumerate(self.holders) if h is s.holder][0])
            if s.mode == "gather_b":
                self.src_pos.append(None)
            else:
                self.src_pos.append(len(self.ins))
                self.ins.append(s.src)
        for n, h in enumerate(self.holders):
            if h.buf is not None:
                self.alias[len(self.ins)] = n
                self.ins.append(h.buf)
        self.out_shape = [jax.ShapeDtypeStruct(h.shape, BF16) for h in self.holders]
        n = len(items)
        self.scratch = [pltpu.SemaphoreType.DMA((n, NDEV)), pltpu.SemaphoreType.DMA((n, NDEV)),
                        pltpu.SemaphoreType.DMA((n,))]

    def deliver(self, outs):
        for h, o in zip(self.holders, outs):
            h.buf = o

    def _copies(self, ins, outs, sems):
        send_sems, recv_sems, local_sems = sems
        me = _index(_position())
        slot = lambda k: _index(_peer(k))
        local, sends, arrivals = [], [], []
        for n, (s, r0, rc) in enumerate(self.items):
            out, rows = outs[self.out_pos[n]], pl.ds(r0, rc)
            src = out if self.src_pos[n] is None else ins[self.src_pos[n]]
            if s.mode == "a2a":
                local.append(pltpu.make_async_copy(src.at[me, rows], out.at[me, rows], local_sems.at[n]))
                moves = [(k, src.at[slot(k), rows], out.at[me, rows], _peer(k), out.at[slot(k), rows])
                         for k in range(1, NDEV)]
            elif s.mode == "gather_a":
                local.append(pltpu.make_async_copy(src.at[rows], out.at[me, rows], local_sems.at[n]))
                moves = [(k, src.at[rows], out.at[me, rows], _peer(k), out.at[slot(k), rows]) for k in (1, 2, 4, 6)]
            else:
                moves = [(k, src.at[slot(k), rows], out.at[slot(k), rows], _peer(1), out.at[slot(k ^ 1), rows])
                         for k in (2, 4, 6)]
            for k, src_ref, dst_ref, to, landing in moves:
                common = dict(send_sem=send_sems.at[n, k], recv_sem=recv_sems.at[n, k],
                              device_id=to, device_id_type=pl.DeviceIdType.MESH)
                sends.append(pltpu.make_async_remote_copy(src_ref=src_ref, dst_ref=dst_ref, **common))
                arrivals.append(pltpu.make_async_remote_copy(src_ref=landing, dst_ref=landing, **common))
        return local, sends, arrivals

    def start(self, ins, outs, sems):
        local, sends, _ = self._copies(ins, outs, sems)
        for cp in local + sends:
            cp.start()

    def wait(self, ins, outs, sems):
        local, sends, arrivals = self._copies(ins, outs, sems)
        for cp in arrivals:
            cp.wait_recv()
        for cp in sends:
            cp.wait_send()
        for cp in local:
            cp.wait()


class _Rides:
    def __init__(self):
        self.queue = []
        self.flushes = 0

    def add(self, stream):
        self.queue.append(stream)
        return stream

    def take(self, budget, only=None):
        ready = {id(s): (s.after.done if s.after is not None else s.R) for s in self.queue}
        items, left = [], budget
        for s in list(self.queue):
            if only is not None and not any(s is t for t in only):
                continue
            avail = ready[id(s)] - s.done
            if avail <= 0:
                continue
            rc = avail if left == float("inf") else min(avail, int(left * s.rate) // ROW_QUANTUM * ROW_QUANTUM)
            if rc < min(avail, MIN_ROWS):
                continue
            if avail - rc <= MIN_ROWS:
                rc = avail
            items.append((s, s.done, rc))
            s.done += rc
            left -= rc / s.rate
            if s.done == s.R:
                self.queue.remove(s)
        return items

    def need(self, *streams):
        while any(s.done < s.R for s in streams):
            items = self.take(float("inf"), only=streams)
            comm = _Exchange(items)
            hbm = pl.BlockSpec(memory_space=pl.ANY)
            nin, nout = len(comm.ins), len(comm.holders)

            def body(*refs, comm=comm, nin=nin, nout=nout):
                comm.start(refs[:nin], refs[nin:nin + nout], refs[nin + nout:])
                comm.wait(refs[:nin], refs[nin:nin + nout], refs[nin + nout:])

            outs = pl.pallas_call(body, in_specs=[hbm] * nin, out_specs=[hbm] * nout, out_shape=comm.out_shape,
                                  scratch_shapes=comm.scratch, input_output_aliases=dict(comm.alias),
                                  name=f"exchange_alone{self.flushes}")(*comm.ins)
            self.flushes += 1
            comm.deliver(outs)


def _call(body, *, grid, in_specs, out_specs, out_shape, scratch_shapes=(), args, name, rides=None, us=0.0):
    sem = ("arbitrary",) * len(grid)
    items = rides.take(us) if rides is not None else []
    if not items:
        return pl.pallas_call(body, grid=grid, in_specs=in_specs, out_specs=out_specs, out_shape=out_shape,
                              scratch_shapes=list(scratch_shapes), compiler_params=_params(sem), name=name)(*args)
    comm = _Exchange(items)
    nin, nout, nscr, cin, cout = len(in_specs), len(out_specs), len(scratch_shapes), len(comm.ins), len(comm.holders)
    hbm = pl.BlockSpec(memory_space=pl.ANY)

    def both(*refs):
        ins, c_ins = refs[:nin], refs[nin:nin + cin]
        pos = nin + cin
        outs, c_outs = refs[pos:pos + nout], refs[pos + nout:pos + nout + cout]
        pos += nout + cout
        scr, sems = refs[pos:pos + nscr], refs[pos + nscr:]
        first = functools.reduce(lambda p, q: p & q, [pl.program_id(d) == 0 for d in range(len(grid))])
        last = functools.reduce(lambda p, q: p & q, [pl.program_id(d) == grid[d] - 1 for d in range(len(grid))])

        @pl.when(first)
        def _():
            comm.start(c_ins, c_outs, sems)

        body(*ins, *outs, *scr)

        @pl.when(last)
        def _():
            comm.wait(c_ins, c_outs, sems)

    res = pl.pallas_call(both, grid=grid, in_specs=list(in_specs) + [hbm] * cin, out_specs=list(out_specs) + [hbm] * cout,
                         out_shape=list(out_shape) + comm.out_shape, scratch_shapes=list(scratch_shapes) + comm.scratch,
                         input_output_aliases={nin + i: nout + o for i, o in comm.alias.items()},
                         compiler_params=_params(sem), name=name)(*args, *comm.ins)
    comm.deliver(res[nout:])
    return res[:nout]


US_GLA, US_GLAPOST, US_MERGE, US_FFN, US_RESID, US_NORMMOD = 100.0, 40.0, 45.0, 110.0, 40.0, 24.0
US_FFN_BWD, US_GLA_BWD, US_MERGE_BWD, US_RESID_BWD, US_NORMMOD_BWD, US_GLAPOST_BWD = 215.0, 118.0, 57.0, 57.0, 54.0, 52.0
MM_VMEM_BUDGET = 44 * 1024 * 1024
MM_FLOPS_PER_US = 780e6


def _mm(a, b, *, ta=False, tb=False, out_dtype=F32, name, rides=None, out_shards=False, m_part=(0, 1)):
    if ta:
        K, M = a.shape
    else:
        M, K = a.shape
    M //= m_part[1]
    sharded = b.ndim == 3
    if sharded:
        ns, n = b.shape[0], b.shape[2]
        Kb, N = (ns * n, b.shape[1]) if tb else (b.shape[1], ns * n)
    elif tb:
        N, Kb = b.shape
    else:
        Kb, N = b.shape
    assert K == Kb, (a.shape, b.shape, ta, tb)
    tm = _pick(M, 1024)
    tn = N // NDEV if out_shards else n if (sharded and not tb) else _pick(N, 1152)
    sa, sb, so = a.dtype.itemsize, b.dtype.itemsize, jnp.dtype(out_dtype).itemsize

    def vmem(tk):
        blocks = 2 * (tm * tk * sa + tk * tn * sb) + 2 * tm * tn * so + (tm * tn * 4 if tk < K else 0)
        return blocks + (tm * K * 2 if ta else 0)

    if sharded and tb:
        tk = n
    else:
        tk = K
        if vmem(K) > MM_VMEM_BUDGET:
            fits = [d for d in range(LANES, K, LANES) if K % d == 0 and vmem(d) <= MM_VMEM_BUDGET]
            tk = max(fits) if fits else _pick(K, 512)
    nk = K // tk
    dims = (((1,), (1 if tb else 0,)), ((), ()))

    def dot(a_ref, b_ref, at_ref):
        if ta:
            k = pl.program_id(2)

            @pl.when(pl.program_id(1) == 0)
            def _():
                at_ref[k] = a_ref[...].astype(BF16).T

            lhs = at_ref[k]
        else:
            lhs = a_ref[...].astype(BF16)
        return lax.dot_general(lhs, b_ref[...].astype(BF16), dims, preferred_element_type=F32)

    def body_one(a_ref, b_ref, o_ref, *at_ref):
        o_ref[...] = dot(a_ref, b_ref, *at_ref or (None,)).astype(o_ref.dtype)

    def body_acc(a_ref, b_ref, o_ref, acc_ref, *at_ref):
        k = pl.program_id(2)
        at = (at_ref or (None,))[0]

        @pl.when(k == 0)
        def _():
            acc_ref[...] = dot(a_ref, b_ref, at)

        @pl.when((k > 0) & (k < nk - 1))
        def _():
            acc_ref[...] += dot(a_ref, b_ref, at)

        @pl.when(k == nk - 1)
        def _():
            o_ref[...] = (acc_ref[...] + dot(a_ref, b_ref, at)).astype(o_ref.dtype)

    i0 = m_part[0] * (M // tm)
    a_spec = (pl.BlockSpec((tk, tm), lambda i, j, k: (k, i0 + i)) if ta
              else pl.BlockSpec((tm, tk), lambda i, j, k: (i0 + i, k)))
    if sharded:
        b_spec = (pl.BlockSpec((None, tn, tk), lambda i, j, k: (k, j, 0)) if tb
                  else pl.BlockSpec((None, tk, tn), lambda i, j, k: (j, k, 0)))
    else:
        b_spec = (pl.BlockSpec((tn, tk), lambda i, j, k: (j, k)) if tb
                  else pl.BlockSpec((tk, tn), lambda i, j, k: (k, j)))
    if out_shards:
        out_spec = pl.BlockSpec((None, tm, tn), lambda i, j, k: (j, i, 0))
        out_shape = jax.ShapeDtypeStruct((NDEV, M, tn), out_dtype)
    else:
        out_spec = pl.BlockSpec((tm, tn), lambda i, j, k: (i, j))
        out_shape = jax.ShapeDtypeStruct((M, N), out_dtype)
    (out,) = _call(
        body_one if nk == 1 else body_acc, grid=(M // tm, N // tn, nk), in_specs=[a_spec, b_spec],
        out_specs=[out_spec], out_shape=[out_shape],
        scratch_shapes=([] if nk == 1 else [pltpu.VMEM((tm, tn), F32)]) + ([pltpu.VMEM((nk, tm, tk), BF16)] if ta else []),
        args=(a, b), name=name,
        rides=rides, us=2.0 * M * N * K / MM_FLOPS_PER_US)
    return out


class V(NamedTuple):
    arr: Any
    width: int
    base: Any


def _v(arr, width=None, col0=0):
    width = arr.shape[-1] if width is None else width
    assert col0 % width == 0
    return V(arr, width, col0 // width)


def _row_specs(v, tr, nrow8, halo):
    main = pl.BlockSpec((tr, v.width), lambda j, i, b=v.base: (i, b + j))
    if not halo:
        return [main]
    per = tr // HALO
    prev = pl.BlockSpec((HALO, v.width), lambda j, i, b=v.base: (jnp.maximum(i * per - 1, 0), b + j))
    nxt = pl.BlockSpec((HALO, v.width), lambda j, i, b=v.base: (jnp.minimum((i + 1) * per, nrow8 - 1), b + j))
    return [prev, main, nxt]


def _par_spec(p):
    if p.base is None:
        return pl.BlockSpec(p.arr.shape, lambda j, i: (0,) * p.arr.ndim)
    return pl.BlockSpec((p.arr.shape[0], p.width), lambda j, i, b=p.base: (0, b + j))


def _load_rows(refs, i, nrow, halo):
    if not halo:
        return refs[0][...].astype(F32)
    prev, main, nxt = refs
    pv = jnp.where(i > 0, prev[...].astype(F32), 0.0)
    nv = jnp.where(i < nrow - 1, nxt[...].astype(F32), 0.0)
    return jnp.concatenate([pv, main[...].astype(F32), nv], axis=0)


def _center_mask(tr):
    row = lax.broadcasted_iota(jnp.int32, (tr + 2 * HALO, 1), 0)
    return ((row >= HALO) & (row < HALO + tr)).astype(F32)


def _rw_fwd(f, rows, params, outs, *, tr, ncol=1, halo=False, name, rides=None, us=0.0):
    S = rows[0].arr.shape[0]
    tr = min(tr, S)
    nrow = S // tr
    per = 3 if halo else 1
    nr, npar = len(rows), len(params)

    def body(*refs):
        i = pl.program_id(1)
        vals = [_load_rows(refs[per * r: per * (r + 1)], i, nrow, halo) for r in range(nr)]
        pvals = [refs[per * nr + p][...].astype(F32) for p in range(npar)]
        out_refs = refs[per * nr + npar:]
        res = f(*vals, *pvals, _center_mask(tr)) if halo else f(*vals, *pvals)
        for r, o in zip(res, out_refs):
            o[...] = (r[HALO:HALO + tr] if halo else r).astype(o.dtype)

    in_specs = []
    for v in rows:
        in_specs += _row_specs(v, tr, S // HALO, halo)
    in_specs += [_par_spec(p) for p in params]
    args = []
    for v in rows:
        args += [v.arr] * per
    args += [p.arr for p in params]
    return _call(
        body, grid=(ncol, nrow), in_specs=in_specs,
        out_specs=[pl.BlockSpec((tr, tot // ncol), lambda j, i: (i, j)) for tot, _ in outs],
        out_shape=[jax.ShapeDtypeStruct((S, tot), dt) for tot, dt in outs],
        args=args, name=name, rides=rides, us=us)


def _rw_bwd(f, rows, params, douts, *, row_grads, par_grads, adds=None, tr, ncol=1, halo=False, name,
            rides=None, us=0.0):
    S = rows[0].arr.shape[0]
    tr = min(tr, S)
    nrow = S // tr
    per = 3 if halo else 1
    adds = adds or {}
    nr, npar, nd = len(rows), len(params), len(douts)
    add_keys = sorted(adds)
    rg_idx = [k for k in range(nr) if row_grads[k]]
    pg_idx = [k for k in range(npar) if par_grads[k]]

    def body(*refs):
        i = pl.program_id(1)
        pos = 0
        vals = [_load_rows(refs[per * r: per * (r + 1)], i, nrow, halo) for r in range(nr)]
        pos = per * nr
        pvals = [refs[pos + p][...].astype(F32) for p in range(npar)]
        pos += npar
        dvals = [_load_rows(refs[pos + per * d: pos + per * (d + 1)], i, nrow, halo) for d in range(nd)]
        pos += per * nd
        add_vals = {k: refs[pos + n][...].astype(F32) for n, k in enumerate(add_keys)}
        pos += len(add_keys)
        rg_refs = refs[pos: pos + len(rg_idx)]
        pg_refs = refs[pos + len(rg_idx):]
        if halo:
            center = _center_mask(tr)
            fun = lambda *a: f(*a, center)
        else:
            fun = f
        _, vjp = jax.vjp(fun, *vals, *pvals)
        grads = vjp(tuple(dvals))
        for ref, k in zip(rg_refs, rg_idx):
            g = grads[k]
            g = g[HALO:HALO + tr] if halo else g
            if k in add_vals:
                g = g + add_vals[k]
            ref[...] = g.astype(ref.dtype)
        for ref, k in zip(pg_refs, pg_idx):
            @pl.when(i == 0)
            def _(ref=ref):
                ref[...] = jnp.zeros_like(ref)
            ref[...] += grads[nr + k]

    in_specs, args = [], []
    for v in list(rows):
        in_specs += _row_specs(v, tr, S // HALO, halo)
        args += [v.arr] * per
    in_specs += [_par_spec(p) for p in params]
    args += [p.arr for p in params]
    for v in douts:
        in_specs += _row_specs(v, tr, S // HALO, halo)
        args += [v.arr] * per
    for k in add_keys:
        in_specs += _row_specs(_v(adds[k], rows[k].width), tr, S // HALO, False)
        args += [adds[k]]
    out_specs, out_shape = [], []
    for k in rg_idx:
        w = rows[k].width
        out_specs.append(pl.BlockSpec((tr, w), lambda j, i: (i, j)))
        out_shape.append(jax.ShapeDtypeStruct((S, ncol * w), row_grads[k]))
    for k in pg_idx:
        p = params[k]
        blk = p.arr.shape if p.base is None else (p.arr.shape[0], p.width)
        out_specs.append(pl.BlockSpec((None,) + tuple(blk), lambda j, i, n=len(blk): (j,) + (0,) * n))
        out_shape.append(jax.ShapeDtypeStruct((ncol,) + tuple(blk), F32))
    res = _call(body, grid=(ncol, nrow), in_specs=in_specs, out_specs=out_specs, out_shape=out_shape, args=args, name=name,
                rides=rides, us=us)
    rg = [None] * nr
    for n, k in enumerate(rg_idx):
        rg[k] = res[n]
    pg = [None] * npar
    for n, k in enumerate(pg_idx):
        pg[k] = res[len(rg_idx) + n]
    return rg, pg


def _sigmoid(x):
    return 1.0 / (1.0 + jnp.exp(-x))


def _rms(x, g):
    return x * lax.rsqrt(jnp.mean(x * x, axis=-1, keepdims=True) + EPS) * g


def _f_normmod(x, g, sc, sh):
    return (_rms(x, g) * (1.0 + sc) + sh,)


def _f_resid(x, y, gate, g):
    return (x + gate * _rms(y, g),)


def _f_merge(ga, gb, ya, yb):
    return (_sigmoid(ga) * ya + _sigmoid(gb) * yb,)


def _f_glapost(o, r, g):
    return (_rms(o, g) * (r * _sigmoid(r)),)


def _roll_rows(x, k):
    return pltpu.roll(x, k % x.shape[0], 0)


@functools.partial(jax.custom_vjp, nondiff_argnums=(1,))
def _shift(x, k):
    return _roll_rows(x, k)


def _shift_fwd(x, k):
    return _roll_rows(x, k), None


def _shift_bwd(k, _, g):
    return (_roll_rows(g, -k),)


_shift.defvjp(_shift_fwd, _shift_bwd)


def _taps(w, center):
    return jnp.where(center > 0.0, w, lax.stop_gradient(w))


def _conv3(u, w0, w1, w2, center):
    return (_taps(w0, center) * _shift(u, 2) + _taps(w1, center) * _shift(u, 1)) + _taps(w2, center) * u


def _f_convmix(cb, cc, cx, w0, w1, w2, center):
    return (cb * _conv3(cc * cx, w0, w1, w2, center),)


def _gelu_tanh(x):
    return 0.5 * x * (1.0 + jnp.tanh(math.sqrt(2.0 / math.pi) * (x + 0.044715 * (x * x * x))))


def _f_ffn(gate, up, w0, w1, w2, center):
    return (_gelu_tanh(_conv3(gate, w0, w1, w2, center)) * up,)


@jax.custom_vjp
def _bdot(a, b):
    return jnp.dot(a.astype(BF16), b.astype(BF16), preferred_element_type=F32)


def _bdot_fwd(a, b):
    return _bdot(a, b), (a, b)


def _bdot_bwd(res, g):
    a, b = res
    gb = g.astype(BF16)
    da = lax.dot_general(gb, b.astype(BF16), (((1,), (1,)), ((), ())), preferred_element_type=F32)
    db = lax.dot_general(a.astype(BF16), gb, (((0,), (0,)), ((), ())), preferred_element_type=F32)
    return da, db


_bdot.defvjp(_bdot_fwd, _bdot_bwd)


@jax.custom_vjp
def _log_sigmoid(z):
    return jnp.minimum(z, 0.0) - jnp.log(1.0 + jnp.exp(-jnp.abs(z)))


def _log_sigmoid_fwd(z):
    return _log_sigmoid(z), z


def _log_sigmoid_bwd(z, g):
    return (g * _sigmoid(-z),)


_log_sigmoid.defvjp(_log_sigmoid_fwd, _log_sigmoid_bwd)


def _f_loga(lr, w, b):
    return (_log_sigmoid(_bdot(lr, w) + b) / GLA_TAU,)


def _split3(x):
    hi = x.astype(BF16)
    r1 = x - hi.astype(F32)
    mid = r1.astype(BF16)
    lo = (r1 - mid.astype(F32)).astype(BF16)
    return hi, mid, lo


def _dot_exact(a01, x, dims):
    hi, mid, lo = _split3(x)
    d = lambda p: lax.dot_general(a01, p, dims, preferred_element_type=F32)
    return (d(lo) + d(mid)) + d(hi)


def _dot_exact_lhs(x, a01, dims):
    hi, mid, lo = _split3(x)
    d = lambda p: lax.dot_general(p, a01, dims, preferred_element_type=F32)
    return (d(lo) + d(mid)) + d(hi)


GLA_CHUNKS_PER_STEP = 4

NN = (((1,), (0,)), ((), ()))
NT = (((1,), (1,)), ((), ()))
TN = (((0,), (0,)), ((), ()))


def _bf_dot(a, b, dims):
    return lax.dot_general(a.astype(BF16), b.astype(BF16), dims, preferred_element_type=F32)


def _chunk_decay(a, dv):
    L = a.shape[0]
    r = lax.broadcasted_iota(jnp.int32, (L, L), 0)
    c = lax.broadcasted_iota(jnp.int32, (L, L), 1)
    tri = (r >= c).astype(BF16)
    cum = _dot_exact(tri, a, NN)
    ce = cum[L - 1:L, :]
    dec = jnp.exp(_dot_exact_lhs(a, jnp.ones((L, dv), BF16), TN))
    return cum, ce, jnp.exp(ce - cum), dec


def _gla_fwd(P, a, *, qk, vw, name, rides=None, us=0.0):
    S = P.shape[0]
    nc = S // CHUNK
    cps = math.gcd(nc, GLA_CHUNKS_PER_STEP)
    rows = cps * CHUNK
    dk, dv = qk // HEADS, vw // HEADS
    scale = dk ** -0.5

    def body(q_ref, k_ref, v_ref, a_ref, o_ref, st_ref, state):
        @pl.when(pl.program_id(0) == 0)
        def _():
            state[...] = jnp.zeros_like(state)

        for h in range(HEADS):
            ks, vs = slice(h * dk, (h + 1) * dk), slice(h * dv, (h + 1) * dv)
            s = state[h]
            for c in range(cps):
                rs = slice(c * CHUNK, (c + 1) * CHUNK)
                _, _, E, dec = _chunk_decay(a_ref[rs, ks], dv)
                kd = k_ref[rs, ks] * E
                s = dec * s + _bf_dot(kd, v_ref[rs, vs], TN)
                st_ref[c, h] = s
                o_ref[rs, vs] = _bf_dot(q_ref[rs, ks] * scale, s, NN).astype(o_ref.dtype)
            state[h] = s

    return _call(
        body, grid=(nc // cps,),
        in_specs=[pl.BlockSpec((rows, qk), lambda i: (i, 0)),
                  pl.BlockSpec((rows, qk), lambda i: (i, 1)),
                  pl.BlockSpec((rows, vw), lambda i: (i, 2 * qk // vw)),
                  pl.BlockSpec((rows, qk), lambda i: (i, 0))],
        out_specs=[pl.BlockSpec((rows, vw), lambda i: (i, 0)),
                   pl.BlockSpec((cps, HEADS, dk, dv), lambda i: (i, 0, 0, 0))],
        out_shape=[jax.ShapeDtypeStruct((S, vw), BF16), jax.ShapeDtypeStruct((nc, HEADS, dk, dv), F32)],
        scratch_shapes=[pltpu.VMEM((HEADS, dk, dv), F32)], args=(P, P, P, a), name=name, rides=rides, us=us)


def _gla_bwd(P, a, states, do, *, qk, vw, name, rides=None, us=0.0):
    S = P.shape[0]
    nc = S // CHUNK
    cps = math.gcd(nc, GLA_CHUNKS_PER_STEP)
    rows = cps * CHUNK
    nstep = nc // cps
    dk, dv = qk // HEADS, vw // HEADS
    scale = dk ** -0.5

    def body(q_ref, k_ref, v_ref, a_ref, do_ref, sc_ref, sp_ref, dq_ref, dk_ref, dv_ref, da_ref, ds_carry):
        i = pl.program_id(0)

        @pl.when(i == 0)
        def _():
            ds_carry[...] = jnp.zeros_like(ds_carry)

        has_prev = (i < nstep - 1).astype(F32)
        L = CHUNK
        r = lax.broadcasted_iota(jnp.int32, (L, L), 0)
        c = lax.broadcasted_iota(jnp.int32, (L, L), 1)
        tri_t = (c >= r).astype(BF16)
        last = (lax.broadcasted_iota(jnp.int32, (L, 1), 0) == L - 1).astype(F32)
        for h in range(HEADS):
            ks, vs = slice(h * dk, (h + 1) * dk), slice(h * dv, (h + 1) * dv)
            carry = ds_carry[h]
            for n in reversed(range(cps)):
                rs = slice(n * CHUNK, (n + 1) * CHUNK)
                _, _, E, dec = _chunk_decay(a_ref[rs, ks], dv)
                kk, vv, dov = k_ref[rs, ks], v_ref[rs, vs], do_ref[rs, vs]
                kd = kk * E
                ds = carry + _bf_dot(q_ref[rs, ks] * scale, dov, TN)
                dq_ref[rs, ks] = (_bf_dot(dov, sc_ref[n, h], NT) * scale).astype(dq_ref.dtype)
                dkd = _bf_dot(vv, ds, NT)
                dv_ref[rs, vs] = _bf_dot(kd, ds, NN).astype(dv_ref.dtype)
                dk_ref[rs, ks] = (dkd * E).astype(dk_ref.dtype)
                darg = dkd * kd
                before = sc_ref[n - 1, h] if n > 0 else sp_ref[0, h] * has_prev
                w = ds * before * dec
                dce = _dot_exact(jnp.ones((SUBLANES, dv), BF16), w, NT)[0:1, :]
                dce = dce + jnp.sum(darg, axis=0, keepdims=True)
                dcum = last * dce - darg
                da_ref[rs, ks] = _dot_exact(tri_t, dcum, NN)
                carry = dec * ds
            ds_carry[h] = carry

    rev = lambda i: nstep - 1 - i
    return _call(
        body, grid=(nstep,),
        in_specs=[pl.BlockSpec((rows, qk), lambda i: (rev(i), 0)),
                  pl.BlockSpec((rows, qk), lambda i: (rev(i), 1)),
                  pl.BlockSpec((rows, vw), lambda i: (rev(i), 2 * qk // vw)),
                  pl.BlockSpec((rows, qk), lambda i: (rev(i), 0)),
                  pl.BlockSpec((rows, vw), lambda i: (rev(i), 0)),
                  pl.BlockSpec((cps, HEADS, dk, dv), lambda i: (rev(i), 0, 0, 0)),
                  pl.BlockSpec((1, HEADS, dk, dv), lambda i: (jnp.maximum(rev(i) * cps - 1, 0), 0, 0, 0))],
        out_specs=[pl.BlockSpec((rows, qk), lambda i: (rev(i), 0)),
                   pl.BlockSpec((rows, qk), lambda i: (rev(i), 0)),
                   pl.BlockSpec((rows, vw), lambda i: (rev(i), 0)),
                   pl.BlockSpec((rows, qk), lambda i: (rev(i), 0))],
        out_shape=[jax.ShapeDtypeStruct((S, qk), BF16), jax.ShapeDtypeStruct((S, qk), BF16),
                   jax.ShapeDtypeStruct((S, vw), BF16), jax.ShapeDtypeStruct((S, qk), F32)],
        scratch_shapes=[pltpu.VMEM((HEADS, dk, dv), F32)], args=(P, P, P, a, do, states, states), name=name,
        rides=rides, us=us)


def _loss_head(y, target, *, tr, name):
    S, D = y.shape
    tr = min(tr, S)

    def body(y_ref, t_ref, l_ref, dy_ref):
        @pl.when(pl.program_id(0) == 0)
        def _():
            l_ref[...] = jnp.zeros_like(l_ref)

        e = y_ref[...] - t_ref[...]
        dy_ref[...] = e / D
        l_ref[...] += 0.5 * jnp.sum(jnp.mean(e * e, axis=-1, keepdims=True), axis=0, keepdims=True)

    loss, dy = pl.pallas_call(
        body, grid=(S // tr,),
        in_specs=[pl.BlockSpec((tr, D), lambda i: (i, 0))] * 2,
        out_specs=[pl.BlockSpec((SUBLANES, LANES), lambda i: (0, 0)), pl.BlockSpec((tr, D), lambda i: (i, 0))],
        out_shape=[jax.ShapeDtypeStruct((SUBLANES, LANES), F32), jax.ShapeDtypeStruct((S, D), F32)],
        compiler_params=_params(("arbitrary",)), name=name)(y, target)
    return loss[0, 0], dy


ADAMW_BYTES_PER_US = 2.5e6


def _adamw(parts, w, m, v, *, name, rides=None):
    G, R, C = w.shape
    n = parts[0].shape[0]
    unit = 2 * SUBLANES if parts[0].dtype == BF16 else SUBLANES
    tr = _pick(R, max(unit, (128 * 1024) // C), unit)
    nblk = R // tr
    bc1 = 1.0 - ADAM_B1 ** ADAM_STEP
    bc2 = 1.0 - ADAM_B2 ** ADAM_STEP

    def body(*refs):
        p_refs = refs[:G]
        w_ref, m_ref, v_ref, g_ref, d_ref, mo_ref, vo_ref = refs[G:]
        for q in range(G):
            @pl.when(pl.program_id(0) == q)
            def _(q=q):
                g = p_refs[q][0].astype(F32)
                for s in range(1, n):
                    g = g + p_refs[q][s].astype(F32)
                m2 = ADAM_B1 * m_ref[...] + (1.0 - ADAM_B1) * g
                v2 = ADAM_B2 * v_ref[...] + (1.0 - ADAM_B2) * (g * g)
                g_ref[...] = g
                mo_ref[...] = m2
                vo_ref[...] = v2
                d_ref[...] = -ADAM_LR * ((m2 / bc1) / (jnp.sqrt(v2 / bc2) + ADAM_EPS) + ADAM_WD * w_ref[...])

    def part_spec(q):
        return pl.BlockSpec((n, tr, C), lambda g, i: (0, jnp.where(g == q, i, jnp.where(g < q, 0, nblk - 1)), 0))

    blk = pl.BlockSpec((None, tr, C), lambda g, i: (g, i, 0))
    traffic = G * R * C * (n * parts[0].dtype.itemsize + 7 * 4)
    return _call(
        body, grid=(G, nblk), in_specs=[part_spec(q) for q in range(G)] + [blk, blk, blk],
        out_specs=[blk] * 4, out_shape=[jax.ShapeDtypeStruct((G, R, C), F32)] * 4,
        args=(*parts, w, m, v), name=name, rides=rides, us=traffic / ADAMW_BYTES_PER_US)


def _adamw_nd(parts, w, m, v, *, name):
    shape = w.shape
    C = shape[-1]
    R = math.prod(shape[:-1])
    outs = _adamw([parts.reshape(parts.shape[0], R, C)], w.reshape(1, R, C), m.reshape(1, R, C), v.reshape(1, R, C),
                  name=name)
    return [o.reshape(shape) for o in outs]


def _all_gather(arrs, *, name):
    n = len(arrs)
    HBM = pl.BlockSpec(memory_space=pl.ANY)

    def body(*refs):
        ins, outs = refs[:n], refs[n:2 * n]
        send_sems, recv_sems, local_sems = refs[2 * n:]
        x, y, c = _position()
        me, sibling = (x, y, c), (x, y, 1 - c)
        chips = [(1 - x, y), (x, 1 - y), (1 - x, 1 - y)]

        def copy(t, k, block, to, src=None):
            rows = outs[t].at[_index(block)]
            return pltpu.make_async_remote_copy(
                src_ref=rows if src is None else src, dst_ref=rows,
                send_sem=send_sems.at[t, k], recv_sem=recv_sems.at[t, k],
                device_id=to, device_id_type=pl.DeviceIdType.MESH)

        started = []
        mine = []
        for t in range(n):
            cp = pltpu.make_async_copy(ins[t], outs[t].at[_index(me)], local_sems.at[t])
            cp.start()
            mine.append(cp)
            first = [copy(t, 0, me, sibling, src=ins[t])]
            first += [copy(t, 1 + j, me, (*chip, c), src=ins[t]) for j, chip in enumerate(chips)]
            for cp in first:
                cp.start()
            started += first
        for t in range(n):
            for j, chip in enumerate(chips):
                copy(t, 1 + j, (*chip, c), me).wait_recv()
                cp = copy(t, 4 + j, (*chip, c), sibling)
                cp.start()
                started.append(cp)
        for t in range(n):
            copy(t, 0, sibling, me).wait_recv()
            for j, chip in enumerate(chips):
                copy(t, 4 + j, (*chip, 1 - c), me).wait_recv()
        for cp in started:
            cp.wait_send()
        for cp in mine:
            cp.wait()

    return pl.pallas_call(
        body, in_specs=[HBM] * n, out_specs=[HBM] * n,
        out_shape=[jax.ShapeDtypeStruct((NDEV,) + a.shape, a.dtype) for a in arrs],
        scratch_shapes=[pltpu.SemaphoreType.DMA((n, 7)), pltpu.SemaphoreType.DMA((n, 7)),
                        pltpu.SemaphoreType.DMA((n,))],
        name=name)(*arrs)


class _Pack:
    def __init__(self, arrs):
        self.shapes = [a.shape for a in arrs]
        self.sizes = [math.prod(s) for s in self.shapes]
        total = sum(self.sizes)
        unit = SUBLANES * LANES
        self.padded = -(-total // unit) * unit
        flat = jnp.concatenate([a.reshape(-1).astype(F32) for a in arrs] + [jnp.zeros((self.padded - total,), F32)])
        self.packed = flat.reshape(self.padded // LANES, LANES)

    def unpack(self, gathered):
        flat = gathered.reshape(NDEV, self.padded)
        out, off = [], 0
        for shape, size in zip(self.shapes, self.sizes):
            out.append(flat[:, off:off + size].reshape((NDEV,) + tuple(shape)))
            off += size
        return out


def _cols_from_shards(g):
    g = jnp.moveaxis(g, 0, -2)
    return g.reshape(g.shape[:-2] + (g.shape[-2] * g.shape[-1],))


def _cols_to_shards(w):
    K, N = w.shape
    return jnp.moveaxis(w.reshape(K, NDEV, N // NDEV), 1, 0)


def kernel(x, c, w_ada, b_ada, norm_g, w_in, w_a2, b_a2, gla_norm_g, w_out_gla, conv_mix_w, w_out_conv, w_o, w_up, ffn_conv_w, w_down, loss_target, m_w_ada, m_b_ada, m_norm_g, m_w_in, m_w_a2, m_b_a2, m_gla_norm_g, m_w_out_gla, m_conv_mix_w, m_w_out_conv, m_w_o, m_w_up, m_ffn_conv_w, m_w_down, v_w_ada, v_b_ada, v_norm_g, v_w_in, v_w_a2, v_b_a2, v_gla_norm_g, v_w_out_gla, v_conv_mix_w, v_w_out_conv, v_w_o, v_w_up, v_ffn_conv_w, v_w_down):
    depth = w_ada.shape[0]
    S, D = x.shape[1], x.shape[2]
    QK = w_a2.shape[2] * NDEV
    LR = w_a2.shape[1]
    VW = w_out_gla.shape[1]
    CW = w_out_conv.shape[1]
    FF = w_down.shape[1] * NDEV
    NIN = w_in.shape[2] * NDEV
    NADA = w_ada.shape[2]
    LRP = -(-LR // LANES) * LANES
    off_lr = 2 * QK + 2 * VW
    NP = NIN - LR + LRP
    assert NIN == 2 * QK + 2 * VW + LR + 3 * CW + 2 * D
    xi, yi, ci = _position()
    me = _index((xi, yi, ci))
    xs = x.reshape(S, D)
    target = loss_target.reshape(S, D)
    TR = 256
    TRC = 512
    TC = _pick(CW, 512)
    TCF = _pick(FF, 512)

    small = _Pack([c, norm_g, w_a2, conv_mix_w, ffn_conv_w])
    big_names = ["w_in", "w_out_gla", "w_out_conv", "w_o", "w_up", "w_down"]
    big = dict(w_in=w_in, w_out_gla=w_out_gla, w_out_conv=w_out_conv, w_o=w_o, w_up=w_up, w_down=w_down)

    def matrix(key, g):
        if key == "w_up":
            return g
        if key in ("w_o", "w_down"):
            return g.reshape(g.shape[0] * g.shape[1], g.shape[2])
        if key != "w_in":
            return _cols_from_shards(g)
        pieces = [g[d][:, lo - d * nin_shard:hi - d * nin_shard] for _, d, lo, hi in sorted(in_pieces)]
        return jnp.concatenate(pieces + [jnp.zeros((D, LRP - LR), BF16)], axis=-1)

    nin_shard = NIN // NDEV
    in_pieces = []
    for g_lo, g_hi, to in ((0, off_lr, 0), (off_lr, off_lr + LR, NIN - LR), (off_lr + LR, NIN, off_lr)):
        for d in range(NDEV):
            lo, hi = max(g_lo, d * nin_shard), min(g_hi, (d + 1) * nin_shard)
            if lo < hi:
                in_pieces.append((to + lo - g_lo, d, lo, hi))

    def in_grad_blocks(gW_all):
        blocks = []
        for d in range(NDEV):
            mine = sorted((lo, to, hi) for to, dd, lo, hi in in_pieces if dd == d)
            blocks.append(jnp.concatenate([gW_all[:, to:to + hi - lo] for lo, to, hi in mine], axis=1))
        return jnp.stack(blocks)

    (small_g, w_in0_g) = _all_gather([small.packed, w_in[0].astype(BF16)], name="gather_first")
    c_all, norm_g_s, w_a2_s, conv_w_s, ffn_w_s = small.unpack(small_g)
    c_all = c_all.reshape(NDEV, D)
    norm_g_f = _cols_from_shards(norm_g_s)
    w_a2_f = _cols_from_shards(w_a2_s)
    conv_w_f = _cols_from_shards(conv_w_s)
    ffn_w_f = _cols_from_shards(ffn_w_s)
    w_a2_p = jnp.concatenate([w_a2_f, jnp.zeros((depth, LRP - LR, QK), F32)], axis=1)

    rides = _Rides()
    GATHER_PARAMS_PER_US = 80e3
    FORWARD_PARAMS_PER_US = 800e3
    EXCHANGE_PARAMS_PER_US = 45e3
    weight_streams = {}
    for l in range(depth):
        for key in big_names:
            if (l, key) == (0, "w_in"):
                continue
            shard = big[key][l].astype(BF16)
            rows, cols = shard.shape
            first = rides.add(_Stream("gather_a", shard, rows, cols, GATHER_PARAMS_PER_US / (NDEV * cols)))
            second = rides.add(_Stream("gather_b", None, rows, cols, FORWARD_PARAMS_PER_US / (NDEV * cols), after=first))
            weight_streams[l, key] = (first, second)
    W = {(0, "w_in"): matrix("w_in", w_in0_g)}

    def weight(l, key):
        if (l, key) not in W:
            first, second = weight_streams[l, key]
            rides.need(first, second)
            W[l, key] = matrix(key, first.buf)
        return W[l, key]

    MP = LANES
    c_pad = jnp.concatenate([c_all, jnp.zeros((MP - NDEV, D), F32)], axis=0)
    (silu_c,) = _rw_fwd(lambda t: (t * _sigmoid(t),), [_v(c_pad)], [], [(D, BF16)], tr=MP, name="silu_c")
    mod_cols = _mm(silu_c, w_ada, name="ada_fwd")[:NDEV].reshape(NDEV, depth, NADA)
    (mod_g,) = _all_gather([jnp.moveaxis(mod_cols, 1, 0)], name="gather_mod")
    mod_all = jnp.moveaxis(mod_g, 0, 2).reshape(depth, NDEV, NDEV * NADA)
    mod_me = lax.dynamic_slice_in_dim(mod_all, me, 1, axis=1).reshape(depth, NDEV * NADA)
    (mod_me,) = _rw_fwd(lambda a, b: (a + b,), [_v(mod_me), _v(b_ada)], [], [(NDEV * NADA, F32)],
                        tr=depth, name="mod_bias")

    col = {}
    off = 0
    for nm, wd in (("q", QK), ("k", QK), ("v", VW), ("r", VW), ("cb", CW), ("cc", CW), ("cx", CW),
                   ("ga", D), ("gb", D), ("lr", LRP)):
        col[nm] = off
        off += wd

    saved = []
    xin = xs
    for l in range(depth):
        sh1, sc1, g1, sh2, sc2, g2 = [mod_me[l:l + 1, i * D:(i + 1) * D] for i in range(6)]
        ng = [norm_g_f[l, i:i + 1] for i in range(4)]
        cw = [conv_w_f[l, i:i + 1] for i in range(CONV_K)]
        fw = [ffn_w_f[l, i:i + 1] for i in range(CONV_K)]
        gn = gla_norm_g[l:l + 1]
        ba2 = b_a2[l:l + 1]
        def proj(act, key, name, out_dtype=BF16):
            return _mm(act, weight(l, key), name=name, rides=rides, out_dtype=out_dtype)

        (h1,) = _rw_fwd(_f_normmod, [_v(xin)], [_v(ng[0]), _v(sc1), _v(sh1)], [(D, BF16)], tr=TR, name=f"normmod1_{l}",
                        rides=rides, us=US_NORMMOD)
        P = proj(h1, "w_in", f"in_proj{l}")
        (a,) = _rw_fwd(_f_loga, [_v(P, LRP, col["lr"])], [V(w_a2_p[l], QK, None), _v(ba2)], [(QK, F32)],
                       tr=TR, name=f"loga{l}")
        o, states = _gla_fwd(P, a, qk=QK, vw=VW, name=f"gla_fwd{l}", rides=rides, us=US_GLA)
        dvh = VW // HEADS
        (ya_in,) = _rw_fwd(_f_glapost, [_v(o, dvh), _v(P, dvh, col["r"])], [V(gn, dvh, None)], [(VW, BF16)],
                           tr=TR, ncol=HEADS, name=f"glapost{l}", rides=rides, us=US_GLAPOST)
        ya = proj(ya_in, "w_out_gla", f"out_gla{l}")
        (yb_in,) = _rw_fwd(_f_convmix, [_v(P, TC, col["cb"]), _v(P, TC, col["cc"]), _v(P, TC, col["cx"])],
                           [_v(w, TC) for w in cw], [(CW, BF16)], tr=TRC, ncol=CW // TC, halo=True, name=f"convmix{l}")
        yb = proj(yb_in, "w_out_conv", f"out_conv{l}")
        (mg,) = _rw_fwd(_f_merge, [_v(P, D, col["ga"]), _v(P, D, col["gb"]), _v(ya), _v(yb)], [], [(D, BF16)],
                        tr=TR, name=f"merge{l}", rides=rides, us=US_MERGE)
        y = proj(mg, "w_o", f"o_proj{l}", F32)
        (x1,) = _rw_fwd(_f_resid, [_v(xin), _v(y)], [_v(g1), _v(ng[1])], [(D, F32)], tr=TR, name=f"resid1_{l}",
                        rides=rides, us=US_RESID)
        (h2,) = _rw_fwd(_f_normmod, [_v(x1)], [_v(ng[2]), _v(sc2), _v(sh2)], [(D, BF16)], tr=TR, name=f"normmod2_{l}",
                        rides=rides, us=US_NORMMOD)
        U = proj(h2, "w_up", f"up_proj{l}")
        (ff,) = _rw_fwd(_f_ffn, [_v(U, TCF, 0), _v(U, TCF, FF)], [_v(w, TCF) for w in fw], [(FF, BF16)],
                        tr=TRC, ncol=FF // TCF, halo=True, name=f"ffn{l}", rides=rides, us=US_FFN)
        y2 = proj(ff, "w_down", f"down_proj{l}", F32)
        (x2,) = _rw_fwd(_f_resid, [_v(x1), _v(y2)], [_v(g2), _v(ng[3])], [(D, F32)], tr=TR, name=f"resid2_{l}",
                        rides=rides, us=US_RESID)
        saved.append(dict(xin=xin, h1=h1, P=P, a=a, o=o, states=states, ya_in=ya_in, ya=ya, yb_in=yb_in, yb=yb,
                          mg=mg, y=y, x1=x1, h2=h2, U=U, ff=ff, y2=y2, mods=(sh1, sc1, g1, sh2, sc2, g2),
                          ng=ng, cw=cw, fw=fw, gn=gn, ba2=ba2))
        xin = x2

    loss_local, dx = _loss_head(xin, target, tr=TR, name="loss_head")
    loss = lax.psum(loss_local, MESH_AXES)

    grad_streams = {}
    IN_WGRAD_PARTS = 4

    def send(l, key, blocks):
        _, rows, cols = blocks.shape
        stream = rides.add(_Stream("a2a", blocks, rows, cols, EXCHANGE_PARAMS_PER_US / (NDEV * cols)))
        grad_streams.setdefault((l, key), []).append(stream)

    def bmm(a, b, **kw):
        kw.setdefault("out_dtype", BF16)
        return _mm(a, b, rides=rides, **kw)

    gsmall = [None] * depth
    dmod = [None] * depth
    for l in reversed(range(depth)):
        s = saved[l]
        sh1, sc1, g1, sh2, sc2, g2 = s["mods"]
        ng, cw, fw = s["ng"], s["cw"], s["fw"]
        (dx1_a, dy2), (dg2, dng3) = _rw_bwd(
            _f_resid, [_v(s["x1"]), _v(s["y2"])], [_v(g2), _v(ng[3])], [_v(dx)],
            row_grads=[F32, BF16], par_grads=[True, True], tr=TR, name=f"resid2_bwd{l}", rides=rides, us=US_RESID_BWD)
        gW_down = bmm(s["ff"], dy2, ta=True, out_dtype=BF16, name=f"down_wgrad{l}")
        send(l, "w_down", gW_down.reshape(NDEV, FF // NDEV, D))
        dff = bmm(dy2, weight(l, "w_down"), tb=True, name=f"down_dgrad{l}")
        U = s["U"]
        (dgate, dup), dfw = _rw_bwd(
            _f_ffn, [_v(U, TCF, 0), _v(U, TCF, FF)], [_v(w, TCF) for w in fw], [_v(dff, TCF)],
            row_grads=[BF16, BF16], par_grads=[True] * 3, tr=TRC, ncol=FF // TCF, halo=True, name=f"ffn_bwd{l}",
            rides=rides, us=US_FFN_BWD)
        dU = jnp.concatenate([dgate, dup], axis=1)
        gW_up = bmm(s["h2"], dU, ta=True, out_dtype=BF16, out_shards=True, name=f"up_wgrad{l}")
        send(l, "w_up", gW_up)
        dh2 = bmm(dU, weight(l, "w_up"), tb=True, name=f"up_dgrad{l}")
        (dx1,), (dng2, dsc2, dsh2) = _rw_bwd(
            _f_normmod, [_v(s["x1"])], [_v(ng[2]), _v(sc2), _v(sh2)], [_v(dh2)],
            row_grads=[F32], par_grads=[True] * 3, adds={0: dx1_a}, tr=TR, name=f"normmod2_bwd{l}",
            rides=rides, us=US_NORMMOD_BWD)
        (dxin_a, dy), (dg1, dng1) = _rw_bwd(
            _f_resid, [_v(s["xin"]), _v(s["y"])], [_v(g1), _v(ng[1])], [_v(dx1)],
            row_grads=[F32, BF16], par_grads=[True, True], tr=TR, name=f"resid1_bwd{l}", rides=rides, us=US_RESID_BWD)
        gW_o = bmm(s["mg"], dy, ta=True, out_dtype=BF16, name=f"o_wgrad{l}")
        send(l, "w_o", gW_o.reshape(NDEV, D // NDEV, D))
        dmg = bmm(dy, weight(l, "w_o"), tb=True, name=f"o_dgrad{l}")
        P = s["P"]
        (dga, dgb, dya, dyb), _ = _rw_bwd(
            _f_merge, [_v(P, D, col["ga"]), _v(P, D, col["gb"]), _v(s["ya"]), _v(s["yb"])], [], [_v(dmg)],
            row_grads=[BF16] * 4, par_grads=[], tr=TR, name=f"merge_bwd{l}", rides=rides, us=US_MERGE_BWD)
        gW_og = bmm(s["ya_in"], dya, ta=True, out_dtype=BF16, name=f"out_gla_wgrad{l}")
        send(l, "w_out_gla", _cols_to_shards(gW_og))
        dya_in = bmm(dya, weight(l, "w_out_gla"), tb=True, name=f"out_gla_dgrad{l}")
        gW_oc = bmm(s["yb_in"], dyb, ta=True, out_dtype=BF16, name=f"out_conv_wgrad{l}")
        send(l, "w_out_conv", _cols_to_shards(gW_oc))
        dyb_in = bmm(dyb, weight(l, "w_out_conv"), tb=True, name=f"out_conv_dgrad{l}")
        (dcb, dcc, dcx), dcw = _rw_bwd(
            _f_convmix, [_v(P, TC, col["cb"]), _v(P, TC, col["cc"]), _v(P, TC, col["cx"])],
            [_v(w, TC) for w in cw], [_v(dyb_in, TC)],
            row_grads=[BF16] * 3, par_grads=[True] * 3, tr=TRC, ncol=CW // TC, halo=True, name=f"convmix_bwd{l}")
        dvh = VW // HEADS
        (do, dr), (dgn,) = _rw_bwd(
            _f_glapost, [_v(s["o"], dvh), _v(P, dvh, col["r"])], [V(s["gn"], dvh, None)], [_v(dya_in, dvh)],
            row_grads=[BF16, BF16], par_grads=[True], tr=TR, ncol=HEADS, name=f"glapost_bwd{l}",
            rides=rides, us=US_GLAPOST_BWD)
        dq, dk, dv, da = _gla_bwd(P, s["a"], s["states"], do, qk=QK, vw=VW, name=f"gla_bwd{l}",
                                  rides=rides, us=US_GLA_BWD)
        (dlr,), (dwa2, dba2) = _rw_bwd(
            _f_loga, [_v(P, LRP, col["lr"])], [V(w_a2_p[l], QK, None), _v(s["ba2"])], [_v(da)],
            row_grads=[BF16], par_grads=[True, True], tr=TR, name=f"loga_bwd{l}")
        dP = jnp.concatenate([dq, dk, dv, dr, dcb, dcc, dcx, dga, dgb, dlr], axis=1)
        for part in range(IN_WGRAD_PARTS):
            gW_all = bmm(s["h1"], dP, ta=True, out_dtype=BF16, m_part=(part, IN_WGRAD_PARTS), name=f"in_wgrad{l}_{part}")
            send(l, "w_in", in_grad_blocks(gW_all))
        dh1 = bmm(dP, weight(l, "w_in"), tb=True, name=f"in_dgrad{l}")
        (dx,), (dng0, dsc1, dsh1) = _rw_bwd(
            _f_normmod, [_v(s["xin"])], [_v(ng[0]), _v(sc1), _v(sh1)], [_v(dh1)],
            row_grads=[F32], par_grads=[True] * 3, adds={0: dxin_a}, tr=TR, name=f"normmod1_bwd{l}",
            rides=rides, us=US_NORMMOD_BWD)
        vec = lambda t: t.reshape(1, -1)
        gsmall[l] = dict(
            norm_g=jnp.concatenate([vec(dng0), vec(dng1), vec(dng2), vec(dng3)], axis=0),
            w_a2=dwa2[0, :LR], b_a2=dba2.reshape(QK), gla_norm_g=jnp.sum(dgn, axis=0).reshape(dvh),
            conv_mix_w=jnp.concatenate([vec(t) for t in dcw], axis=0),
            ffn_conv_w=jnp.concatenate([vec(t) for t in dfw], axis=0))
        dmod[l] = jnp.concatenate([vec(t) for t in (dsh1, dsc1, dg1, dsh2, dsc2, dg2)], axis=1).reshape(-1)
    grad_x = dx.reshape(x.shape)

    stack = lambda key, src: jnp.stack([src[l][key] for l in range(depth)])
    small_names = ["norm_g", "w_a2", "b_a2", "gla_norm_g", "conv_mix_w", "ffn_conv_w"]
    gpack = _Pack([jnp.stack(dmod)] + [stack(k, gsmall) for k in small_names])
    (gsm,) = _all_gather([gpack.packed], name="gather_small_grads")
    gs = gpack.unpack(gsm)
    dmod_all = gs[0]
    parts_small = dict(zip(small_names, gs[1:]))

    def my_cols(t, n):
        return lax.dynamic_slice_in_dim(t, me * n, n, axis=t.ndim - 1)

    gW_ada = []
    for l in range(depth):
        dcols = my_cols(dmod_all[:, l], NADA)
        dcols = jnp.concatenate([dcols, jnp.zeros((MP - NDEV, NADA), F32)], axis=0)
        gW_ada.append(bmm(silu_c, dcols, ta=True, out_dtype=F32, name=f"ada_wgrad{l}")[None])
    parts = {}
    parts["b_ada"] = dmod_all
    parts["norm_g"] = my_cols(parts_small["norm_g"], D // NDEV)
    parts["w_a2"] = my_cols(parts_small["w_a2"], QK // NDEV)
    parts["b_a2"] = parts_small["b_a2"]
    parts["gla_norm_g"] = parts_small["gla_norm_g"]
    parts["conv_mix_w"] = my_cols(parts_small["conv_mix_w"], CW // NDEV)
    parts["ffn_conv_w"] = my_cols(parts_small["ffn_conv_w"], FF // NDEV)

    weights = dict(w_ada=(w_ada, m_w_ada, v_w_ada), b_ada=(b_ada, m_b_ada, v_b_ada),
                   norm_g=(norm_g, m_norm_g, v_norm_g), w_in=(w_in, m_w_in, v_w_in), w_a2=(w_a2, m_w_a2, v_w_a2),
                   b_a2=(b_a2, m_b_a2, v_b_a2), gla_norm_g=(gla_norm_g, m_gla_norm_g, v_gla_norm_g),
                   w_out_gla=(w_out_gla, m_w_out_gla, v_w_out_gla), conv_mix_w=(conv_mix_w, m_conv_mix_w, v_conv_mix_w),
                   w_out_conv=(w_out_conv, m_w_out_conv, v_w_out_conv), w_o=(w_o, m_w_o, v_w_o),
                   w_up=(w_up, m_w_up, v_w_up), ffn_conv_w=(ffn_conv_w, m_ffn_conv_w, v_ffn_conv_w),
                   w_down=(w_down, m_w_down, v_w_down))
    done = {}
    for nm in ["w_ada", "w_down", "w_up", "w_o", "w_out_gla", "w_out_conv", "w_in"]:
        w, m, v = weights[nm]
        if nm == "w_ada":
            done[nm] = _adamw(gW_ada, w, m, v, name=f"adamw_{nm}")
            continue
        streams = [s for l in range(depth) for s in grad_streams[l, nm]]
        rides.need(*streams)
        groups, rows = len(streams), w.shape[1] * depth // len(streams)
        split = lambda t: t.reshape(groups, rows, t.shape[2])
        outs = _adamw([s.buf for s in streams], split(w), split(m), split(v), name=f"adamw_{nm}")
        done[nm] = [o.reshape(w.shape) for o in outs]
    for nm, (w, m, v) in weights.items():
        if nm not in done:
            done[nm] = _adamw_nd(parts[nm], w, m, v, name=f"adamw_{nm}")
    grads, deltas, new_m, new_v = zip(*[done[nm] for nm in weights])
    return (loss, grad_x, *grads, *deltas, *new_m, *new_v)
```

```python
import functools
import math
from typing import Any, NamedTuple

import jax
import jax.numpy as jnp
from jax import lax
from jax.experimental import pallas as pl
from jax.experimental.pallas import tpu as pltpu

F32 = jnp.float32
BF16 = jnp.bfloat16
MESH_AXES = ("x", "y", "c")
NDEV = 8
EPS = 1e-6
CHUNK = 64
HEADS = 4
GLA_TAU = 16.0
CONV_K = 3
ADAM_LR = 0.001
ADAM_B1 = 0.9
ADAM_B2 = 0.999
ADAM_EPS = 1e-08
ADAM_WD = 0.01
ADAM_STEP = 10

LANES = 128
SUBLANES = 8
HALO = 2 * SUBLANES
VMEM_LIMIT = 56 * 1024 * 1024


def _params(sem=None):
    return pltpu.CompilerParams(dimension_semantics=sem, vmem_limit_bytes=VMEM_LIMIT)


def _pick(n, cap, unit=LANES):
    best = None
    for d in range(unit, min(n, cap) + 1, unit):
        if n % d == 0:
            best = d
    return best if best is not None else n


def _position():
    return lax.axis_index("x"), lax.axis_index("y"), lax.axis_index("c")


def _index(p):
    return 4 * p[0] + 2 * p[1] + p[2]


def _peer(k):
    x, y, c = _position()
    return ((1 - x) if k & 4 else x, (1 - y) if k & 2 else y, (1 - c) if k & 1 else c)


ROW_QUANTUM = 2 * SUBLANES
MIN_ROWS = 128
MIN_ICI_CARRIER_US = 50.0


class _Stream:
    def __init__(self, mode, src, rows, cols, rate, after=None, on_done=None):
        self.mode, self.src, self.R, self.rate, self.after = mode, src, rows, rate, after
        self.shape = (NDEV if mode.startswith("gather") else NDEV // 2, rows, cols)
        self.holder = self if after is None else after
        self.buf = None
        self.done = 0
        self.on_done = on_done


class _Exchange:
    def __init__(self, items):
        self.items = items
        self.ins, self.src_pos, self.alias, self.holders, self.out_pos = [], [], {}, [], []
        for s, _, _ in items:
            if not any(s.holder is h for h in self.holders):
                self.holders.append(s.holder)
        for s, _, _ in items:
            self.out_pos.append([n for n, h in enumerate(self.holders) if h is s.holder][0])
            if s.mode == "gather_b":
                self.src_pos.append(None)
            else:
                self.src_pos.append(len(self.ins))
                self.ins.append(s.src)
        for n, h in enumerate(self.holders):
            if h.buf is not None:
                self.alias[len(self.ins)] = n
                self.ins.append(h.buf)
        self.out_shape = [jax.ShapeDtypeStruct(h.shape, BF16) for h in self.holders]
        n = len(items)
        self.scratch = [pltpu.SemaphoreType.DMA((n, NDEV)), pltpu.SemaphoreType.DMA((n, NDEV)),
                        pltpu.SemaphoreType.DMA((n,))]

    def deliver(self, outs):
        for h, o in zip(self.holders, outs):
            h.buf = o

    def _copies(self, ins, outs, sems):
        send_sems, recv_sems, local_sems = sems
        me = _index(_position())
        slot = lambda k: _index(_peer(k))
        local, sends, arrivals = [], [], []
        for n, (s, r0, rc) in enumerate(self.items):
            out, rows = outs[self.out_pos[n]], pl.ds(r0, rc)
            src = out if self.src_pos[n] is None else ins[self.src_pos[n]]
            if s.mode == "pair":
                other = 1 - _position()[2]
                moves = [(q, src.at[2 * q + other, rows], out.at[q, rows], _peer(1), out.at[q, rows])
                         for q in range(NDEV // 2)]
            elif s.mode == "chips":
                chip = lambda k: slot(k) // 2
                local.append(pltpu.make_async_copy(src.at[me // 2, rows], out.at[me // 2, rows], local_sems.at[n]))
                moves = [(k, src.at[chip(k), rows], out.at[me // 2, rows], _peer(k), out.at[chip(k), rows])
                         for k in (2, 4, 6)]
            elif s.mode == "gather_a":
                local.append(pltpu.make_async_copy(src.at[rows], out.at[me, rows], local_sems.at[n]))
                moves = [(k, src.at[rows], out.at[me, rows], _peer(k), out.at[slot(k), rows]) for k in (1, 2, 4, 6)]
            else:
                moves = [(k, src.at[slot(k), rows], out.at[slot(k), rows], _peer(1), out.at[slot(k ^ 1), rows])
                         for k in (2, 4, 6)]
            for k, src_ref, dst_ref, to, landing in moves:
                common = dict(send_sem=send_sems.at[n, k], recv_sem=recv_sems.at[n, k],
                              device_id=to, device_id_type=pl.DeviceIdType.MESH)
                sends.append(pltpu.make_async_remote_copy(src_ref=src_ref, dst_ref=dst_ref, **common))
                arrivals.append(pltpu.make_async_remote_copy(src_ref=landing, dst_ref=landing, **common))
        return local, sends, arrivals

    def start(self, ins, outs, sems):
        local, sends, _ = self._copies(ins, outs, sems)
        for cp in local + sends:
            cp.start()

    def wait(self, ins, outs, sems):
        local, sends, arrivals = self._copies(ins, outs, sems)
        for cp in arrivals:
            cp.wait_recv()
        for cp in sends:
            cp.wait_send()
        for cp in local:
            cp.wait()


class _Rides:
    def __init__(self):
        self.queue = []
        self.finished = []
        self.flushes = 0
        self.ici_share = 1.0
        self.min_ici_us = 0.0

    def add(self, stream):
        self.queue.append(stream)
        return stream

    def take(self, budget, only=None, ici=True):
        ready = {id(s): (s.after.done if s.after is not None else s.R) for s in self.queue}
        left = {"ici": budget * self.ici_share if ici else 0.0, "d2d": budget}
        items = []
        for s in list(self.queue):
            if only is not None and not any(s is t for t in only):
                continue
            link = "d2d" if s.mode in ("gather_b", "pair") else "ici"
            avail = ready[id(s)] - s.done
            if avail <= 0:
                continue
            rc = avail if budget == float("inf") else min(avail, int(left[link] * s.rate) // ROW_QUANTUM * ROW_QUANTUM)
            if rc < min(avail, MIN_ROWS):
                continue
            if avail - rc <= 2 * MIN_ROWS:
                rc = avail
            items.append((s, s.done, rc))
            s.done += rc
            left[link] -= rc / s.rate
            if s.done == s.R:
                self.queue.remove(s)
                self.finished.append(s)
        return items

    def fire(self):
        finished, self.finished = self.finished, []
        for s in finished:
            if s.on_done is not None:
                s.on_done(s)

    def need(self, *streams):
        while any(s.done < s.R for s in streams):
            items = self.take(float("inf"), only=streams)
            comm = _Exchange(items)
            hbm = pl.BlockSpec(memory_space=pl.ANY)
            nin, nout = len(comm.ins), len(comm.holders)

            def body(*refs, comm=comm, nin=nin, nout=nout):
                comm.start(refs[:nin], refs[nin:nin + nout], refs[nin + nout:])
                comm.wait(refs[:nin], refs[nin:nin + nout], refs[nin + nout:])

            outs = pl.pallas_call(body, in_specs=[hbm] * nin, out_specs=[hbm] * nout, out_shape=comm.out_shape,
                                  scratch_shapes=comm.scratch, input_output_aliases=dict(comm.alias),
                                  name=f"exchange_alone{self.flushes}")(*comm.ins)
            self.flushes += 1
            comm.deliver(outs)
            self.fire()


def _call(body, *, grid, in_specs, out_specs, out_shape, scratch_shapes=(), args, name, rides=None, us=0.0):
    sem = ("arbitrary",) * len(grid)
    items = rides.take(us, ici=us >= rides.min_ici_us) if rides is not None else []
    if not items:
        return pl.pallas_call(body, grid=grid, in_specs=in_specs, out_specs=out_specs, out_shape=out_shape,
                              scratch_shapes=list(scratch_shapes), compiler_params=_params(sem), name=name)(*args)
    comm = _Exchange(items)
    nin, nout, nscr, cin, cout = len(in_specs), len(out_specs), len(scratch_shapes), len(comm.ins), len(comm.holders)
    hbm = pl.BlockSpec(memory_space=pl.ANY)

    def both(*refs):
        ins, c_ins = refs[:nin], refs[nin:nin + cin]
        pos = nin + cin
        outs, c_outs = refs[pos:pos + nout], refs[pos + nout:pos + nout + cout]
        pos += nout + cout
        scr, sems = refs[pos:pos + nscr], refs[pos + nscr:]
        first = functools.reduce(lambda p, q: p & q, [pl.program_id(d) == 0 for d in range(len(grid))])
        last = functools.reduce(lambda p, q: p & q, [pl.program_id(d) == grid[d] - 1 for d in range(len(grid))])

        @pl.when(first)
        def _():
            comm.start(c_ins, c_outs, sems)

        body(*ins, *outs, *scr)

        @pl.when(last)
        def _():
            comm.wait(c_ins, c_outs, sems)

    res = pl.pallas_call(both, grid=grid, in_specs=list(in_specs) + [hbm] * cin, out_specs=list(out_specs) + [hbm] * cout,
                         out_shape=list(out_shape) + comm.out_shape, scratch_shapes=list(scratch_shapes) + comm.scratch,
                         input_output_aliases={nin + i: nout + o for i, o in comm.alias.items()},
                         compiler_params=_params(sem), name=name)(*args, *comm.ins)
    comm.deliver(res[nout:])
    rides.fire()
    return res[:nout]


US_GLA, US_GLAPOST, US_MERGE, US_FFN, US_RESID, US_NORMMOD = 100.0, 40.0, 45.0, 110.0, 40.0, 24.0
US_FFN_BWD, US_GLA_BWD, US_MERGE_BWD, US_RESID_BWD, US_NORMMOD_BWD, US_GLAPOST_BWD = 215.0, 118.0, 57.0, 57.0, 54.0, 52.0
MM_VMEM_BUDGET = 44 * 1024 * 1024
MM_FLOPS_PER_US = 780e6


def _mm(a, b, *, ta=False, tb=False, out_dtype=F32, name, rides=None, out_shards=False, m_part=(0, 1)):
    if ta:
        K, M = a.shape
    else:
        M, K = a.shape
    M //= m_part[1]
    sharded = b.ndim == 3
    if sharded:
        ns, n = b.shape[0], b.shape[2]
        Kb, N = (ns * n, b.shape[1]) if tb else (b.shape[1], ns * n)
    elif tb:
        N, Kb = b.shape
    else:
        Kb, N = b.shape
    assert K == Kb, (a.shape, b.shape, ta, tb)
    tm = _pick(M, 1024)
    tn = N // NDEV if out_shards else n if (sharded and not tb) else _pick(N, 1152)
    sa, sb, so = a.dtype.itemsize, b.dtype.itemsize, jnp.dtype(out_dtype).itemsize

    def vmem(tk):
        blocks = 2 * (tm * tk * sa + tk * tn * sb) + 2 * tm * tn * so + (tm * tn * 4 if tk < K else 0)
        return blocks + (tm * K * 2 if ta else 0)

    if sharded and tb:
        tk = n
    else:
        tk = K
        if vmem(K) > MM_VMEM_BUDGET:
            fits = [d for d in range(LANES, K, LANES) if K % d == 0 and vmem(d) <= MM_VMEM_BUDGET]
            tk = max(fits) if fits else _pick(K, 512)
    nk = K // tk
    dims = (((1,), (1 if tb else 0,)), ((), ()))

    def dot(a_ref, b_ref, at_ref):
        if ta:
            k = pl.program_id(2)

            @pl.when(pl.program_id(1) == 0)
            def _():
                at_ref[k] = a_ref[...].astype(BF16).T

            lhs = at_ref[k]
        else:
            lhs = a_ref[...].astype(BF16)
        return lax.dot_general(lhs, b_ref[...].astype(BF16), dims, preferred_element_type=F32)

    def body_one(a_ref, b_ref, o_ref, *at_ref):
        o_ref[...] = dot(a_ref, b_ref, *at_ref or (None,)).astype(o_ref.dtype)

    def body_acc(a_ref, b_ref, o_ref, acc_ref, *at_ref):
        k = pl.program_id(2)
        at = (at_ref or (None,))[0]

        @pl.when(k == 0)
        def _():
            acc_ref[...] = dot(a_ref, b_ref, at)

        @pl.when((k > 0) & (k < nk - 1))
        def _():
            acc_ref[...] += dot(a_ref, b_ref, at)

        @pl.when(k == nk - 1)
        def _():
            o_ref[...] = (acc_ref[...] + dot(a_ref, b_ref, at)).astype(o_ref.dtype)

    i0 = m_part[0] * (M // tm)
    a_spec = (pl.BlockSpec((tk, tm), lambda i, j, k: (k, i0 + i)) if ta
              else pl.BlockSpec((tm, tk), lambda i, j, k: (i0 + i, k)))
    if sharded:
        b_spec = (pl.BlockSpec((None, tn, tk), lambda i, j, k: (k, j, 0)) if tb
                  else pl.BlockSpec((None, tk, tn), lambda i, j, k: (j, k, 0)))
    else:
        b_spec = (pl.BlockSpec((tn, tk), lambda i, j, k: (j, k)) if tb
                  else pl.BlockSpec((tk, tn), lambda i, j, k: (k, j)))
    if out_shards:
        out_spec = pl.BlockSpec((None, tm, tn), lambda i, j, k: (j, i, 0))
        out_shape = jax.ShapeDtypeStruct((NDEV, M, tn), out_dtype)
    else:
        out_spec = pl.BlockSpec((tm, tn), lambda i, j, k: (i, j))
        out_shape = jax.ShapeDtypeStruct((M, N), out_dtype)
    (out,) = _call(
        body_one if nk == 1 else body_acc, grid=(M // tm, N // tn, nk), in_specs=[a_spec, b_spec],
        out_specs=[out_spec], out_shape=[out_shape],
        scratch_shapes=([] if nk == 1 else [pltpu.VMEM((tm, tn), F32)]) + ([pltpu.VMEM((nk, tm, tk), BF16)] if ta else []),
        args=(a, b), name=name,
        rides=rides, us=2.0 * M * N * K / MM_FLOPS_PER_US)
    return out


class V(NamedTuple):
    arr: Any
    width: int
    base: Any


def _v(arr, width=None, col0=0):
    width = arr.shape[-1] if width is None else width
    assert col0 % width == 0
    return V(arr, width, col0 // width)


def _row_specs(v, tr, nrow8, halo):
    main = pl.BlockSpec((tr, v.width), lambda j, i, b=v.base: (i, b + j))
    if not halo:
        return [main]
    per = tr // HALO
    prev = pl.BlockSpec((HALO, v.width), lambda j, i, b=v.base: (jnp.maximum(i * per - 1, 0), b + j))
    nxt = pl.BlockSpec((HALO, v.width), lambda j, i, b=v.base: (jnp.minimum((i + 1) * per, nrow8 - 1), b + j))
    return [prev, main, nxt]


def _par_spec(p):
    if p.base is None:
        return pl.BlockSpec(p.arr.shape, lambda j, i: (0,) * p.arr.ndim)
    return pl.BlockSpec((p.arr.shape[0], p.width), lambda j, i, b=p.base: (0, b + j))


def _load_rows(refs, i, nrow, halo):
    if not halo:
        return refs[0][...].astype(F32)
    prev, main, nxt = refs
    pv = jnp.where(i > 0, prev[...].astype(F32), 0.0)
    nv = jnp.where(i < nrow - 1, nxt[...].astype(F32), 0.0)
    return jnp.concatenate([pv, main[...].astype(F32), nv], axis=0)


def _center_mask(tr):
    row = lax.broadcasted_iota(jnp.int32, (tr + 2 * HALO, 1), 0)
    return ((row >= HALO) & (row < HALO + tr)).astype(F32)


def _rw_fwd(f, rows, params, outs, *, tr, ncol=1, halo=False, name, rides=None, us=0.0):
    S = rows[0].arr.shape[0]
    tr = min(tr, S)
    nrow = S // tr
    per = 3 if halo else 1
    nr, npar = len(rows), len(params)

    def body(*refs):
        i = pl.program_id(1)
        vals = [_load_rows(refs[per * r: per * (r + 1)], i, nrow, halo) for r in range(nr)]
        pvals = [refs[per * nr + p][...].astype(F32) for p in range(npar)]
        out_refs = refs[per * nr + npar:]
        res = f(*vals, *pvals, _center_mask(tr)) if halo else f(*vals, *pvals)
        for r, o in zip(res, out_refs):
            o[...] = (r[HALO:HALO + tr] if halo else r).astype(o.dtype)

    in_specs = []
    for v in rows:
        in_specs += _row_specs(v, tr, S // HALO, halo)
    in_specs += [_par_spec(p) for p in params]
    args = []
    for v in rows:
        args += [v.arr] * per
    args += [p.arr for p in params]
    return _call(
        body, grid=(ncol, nrow), in_specs=in_specs,
        out_specs=[pl.BlockSpec((tr, tot // ncol), lambda j, i: (i, j)) for tot, _ in outs],
        out_shape=[jax.ShapeDtypeStruct((S, tot), dt) for tot, dt in outs],
        args=args, name=name, rides=rides, us=us)


def _rw_bwd(f, rows, params, douts, *, row_grads, par_grads, adds=None, tr, ncol=1, halo=False, name,
            rides=None, us=0.0):
    S = rows[0].arr.shape[0]
    tr = min(tr, S)
    nrow = S // tr
    per = 3 if halo else 1
    adds = adds or {}
    nr, npar, nd = len(rows), len(params), len(douts)
    add_keys = sorted(adds)
    rg_idx = [k for k in range(nr) if row_grads[k]]
    pg_idx = [k for k in range(npar) if par_grads[k]]

    def body(*refs):
        i = pl.program_id(1)
        pos = 0
        vals = [_load_rows(refs[per * r: per * (r + 1)], i, nrow, halo) for r in range(nr)]
        pos = per * nr
        pvals = [refs[pos + p][...].astype(F32) for p in range(npar)]
        pos += npar
        dvals = [_load_rows(refs[pos + per * d: pos + per * (d + 1)], i, nrow, halo) for d in range(nd)]
        pos += per * nd
        add_vals = {k: refs[pos + n][...].astype(F32) for n, k in enumerate(add_keys)}
        pos += len(add_keys)
        rg_refs = refs[pos: pos + len(rg_idx)]
        pg_refs = refs[pos + len(rg_idx):]
        if halo:
            center = _center_mask(tr)
            fun = lambda *a: f(*a, center)
        else:
            fun = f
        _, vjp = jax.vjp(fun, *vals, *pvals)
        grads = vjp(tuple(dvals))
        for ref, k in zip(rg_refs, rg_idx):
            g = grads[k]
            g = g[HALO:HALO + tr] if halo else g
            if k in add_vals:
                g = g + add_vals[k]
            ref[...] = g.astype(ref.dtype)
        for ref, k in zip(pg_refs, pg_idx):
            @pl.when(i == 0)
            def _(ref=ref):
                ref[...] = jnp.zeros_like(ref)
            ref[...] += grads[nr + k]

    in_specs, args = [], []
    for v in list(rows):
        in_specs += _row_specs(v, tr, S // HALO, halo)
        args += [v.arr] * per
    in_specs += [_par_spec(p) for p in params]
    args += [p.arr for p in params]
    for v in douts:
        in_specs += _row_specs(v, tr, S // HALO, halo)
        args += [v.arr] * per
    for k in add_keys:
        in_specs += _row_specs(_v(adds[k], rows[k].width), tr, S // HALO, False)
        args += [adds[k]]
    out_specs, out_shape = [], []
    for k in rg_idx:
        w = rows[k].width
        out_specs.append(pl.BlockSpec((tr, w), lambda j, i: (i, j)))
        out_shape.append(jax.ShapeDtypeStruct((S, ncol * w), row_grads[k]))
    for k in pg_idx:
        p = params[k]
        blk = p.arr.shape if p.base is None else (p.arr.shape[0], p.width)
        out_specs.append(pl.BlockSpec((None,) + tuple(blk), lambda j, i, n=len(blk): (j,) + (0,) * n))
        out_shape.append(jax.ShapeDtypeStruct((ncol,) + tuple(blk), F32))
    res = _call(body, grid=(ncol, nrow), in_specs=in_specs, out_specs=out_specs, out_shape=out_shape, args=args, name=name,
                rides=rides, us=us)
    rg = [None] * nr
    for n, k in enumerate(rg_idx):
        rg[k] = res[n]
    pg = [None] * npar
    for n, k in enumerate(pg_idx):
        pg[k] = res[len(rg_idx) + n]
    return rg, pg


def _sigmoid(x):
    return 1.0 / (1.0 + jnp.exp(-x))


def _rms(x, g):
    return x * lax.rsqrt(jnp.mean(x * x, axis=-1, keepdims=True) + EPS) * g


def _f_normmod(x, g, sc, sh):
    return (_rms(x, g) * (1.0 + sc) + sh,)


def _f_resid(x, y, gate, g):
    return (x + gate * _rms(y, g),)


def _f_merge(ga, gb, ya, yb):
    return (_sigmoid(ga) * ya + _sigmoid(gb) * yb,)


def _f_glapost(o, r, g):
    return (_rms(o, g) * (r * _sigmoid(r)),)


def _roll_rows(x, k):
    return pltpu.roll(x, k % x.shape[0], 0)


@functools.partial(jax.custom_vjp, nondiff_argnums=(1,))
def _shift(x, k):
    return _roll_rows(x, k)


def _shift_fwd(x, k):
    return _roll_rows(x, k), None


def _shift_bwd(k, _, g):
    return (_roll_rows(g, -k),)


_shift.defvjp(_shift_fwd, _shift_bwd)


def _taps(w, center):
    return jnp.where(center > 0.0, w, lax.stop_gradient(w))


def _conv3(u, w0, w1, w2, center):
    return (_taps(w0, center) * _shift(u, 2) + _taps(w1, center) * _shift(u, 1)) + _taps(w2, center) * u


def _f_convmix(cb, cc, cx, w0, w1, w2, center):
    return (cb * _conv3(cc * cx, w0, w1, w2, center),)


def _gelu_tanh(x):
    return 0.5 * x * (1.0 + jnp.tanh(math.sqrt(2.0 / math.pi) * (x + 0.044715 * (x * x * x))))


def _f_ffn(gate, up, w0, w1, w2, center):
    return (_gelu_tanh(_conv3(gate, w0, w1, w2, center)) * up,)


@jax.custom_vjp
def _bdot(a, b):
    return jnp.dot(a.astype(BF16), b.astype(BF16), preferred_element_type=F32)


def _bdot_fwd(a, b):
    return _bdot(a, b), (a, b)


def _bdot_bwd(res, g):
    a, b = res
    gb = g.astype(BF16)
    da = lax.dot_general(gb, b.astype(BF16), (((1,), (1,)), ((), ())), preferred_element_type=F32)
    db = lax.dot_general(a.astype(BF16), gb, (((0,), (0,)), ((), ())), preferred_element_type=F32)
    return da, db


_bdot.defvjp(_bdot_fwd, _bdot_bwd)


@jax.custom_vjp
def _log_sigmoid(z):
    return jnp.minimum(z, 0.0) - jnp.log(1.0 + jnp.exp(-jnp.abs(z)))


def _log_sigmoid_fwd(z):
    return _log_sigmoid(z), z


def _log_sigmoid_bwd(z, g):
    return (g * _sigmoid(-z),)


_log_sigmoid.defvjp(_log_sigmoid_fwd, _log_sigmoid_bwd)


def _f_loga(lr, w, b):
    return (_log_sigmoid(_bdot(lr, w) + b) / GLA_TAU,)


def _split3(x):
    hi = x.astype(BF16)
    r1 = x - hi.astype(F32)
    mid = r1.astype(BF16)
    lo = (r1 - mid.astype(F32)).astype(BF16)
    return hi, mid, lo


def _dot_exact(a01, x, dims):
    hi, mid, lo = _split3(x)
    d = lambda p: lax.dot_general(a01, p, dims, preferred_element_type=F32)
    return (d(lo) + d(mid)) + d(hi)


def _dot_exact_lhs(x, a01, dims):
    hi, mid, lo = _split3(x)
    d = lambda p: lax.dot_general(p, a01, dims, preferred_element_type=F32)
    return (d(lo) + d(mid)) + d(hi)


GLA_CHUNKS_PER_STEP = 4

NN = (((1,), (0,)), ((), ()))
NT = (((1,), (1,)), ((), ()))
TN = (((0,), (0,)), ((), ()))


def _bf_dot(a, b, dims):
    return lax.dot_general(a.astype(BF16), b.astype(BF16), dims, preferred_element_type=F32)


def _chunk_decay(a, dv):
    L = a.shape[0]
    r = lax.broadcasted_iota(jnp.int32, (L, L), 0)
    c = lax.broadcasted_iota(jnp.int32, (L, L), 1)
    tri = (r >= c).astype(BF16)
    cum = _dot_exact(tri, a, NN)
    ce = cum[L - 1:L, :]
    dec = jnp.exp(_dot_exact_lhs(a, jnp.ones((L, dv), BF16), TN))
    return cum, ce, jnp.exp(ce - cum), dec


def _gla_fwd(P, a, *, qk, vw, name, rides=None, us=0.0):
    S = P.shape[0]
    nc = S // CHUNK
    cps = math.gcd(nc, GLA_CHUNKS_PER_STEP)
    rows = cps * CHUNK
    dk, dv = qk // HEADS, vw // HEADS
    scale = dk ** -0.5

    def body(q_ref, k_ref, v_ref, a_ref, o_ref, st_ref, state):
        @pl.when(pl.program_id(0) == 0)
        def _():
            state[...] = jnp.zeros_like(state)

        for h in range(HEADS):
            ks, vs = slice(h * dk, (h + 1) * dk), slice(h * dv, (h + 1) * dv)
            s = state[h]
            for c in range(cps):
                rs = slice(c * CHUNK, (c + 1) * CHUNK)
                _, _, E, dec = _chunk_decay(a_ref[rs, ks], dv)
                kd = k_ref[rs, ks] * E
                s = dec * s + _bf_dot(kd, v_ref[rs, vs], TN)
                st_ref[c, h] = s
                o_ref[rs, vs] = _bf_dot(q_ref[rs, ks] * scale, s, NN).astype(o_ref.dtype)
            state[h] = s

    return _call(
        body, grid=(nc // cps,),
        in_specs=[pl.BlockSpec((rows, qk), lambda i: (i, 0)),
                  pl.BlockSpec((rows, qk), lambda i: (i, 1)),
                  pl.BlockSpec((rows, vw), lambda i: (i, 2 * qk // vw)),
                  pl.BlockSpec((rows, qk), lambda i: (i, 0))],
        out_specs=[pl.BlockSpec((rows, vw), lambda i: (i, 0)),
                   pl.BlockSpec((cps, HEADS, dk, dv), lambda i: (i, 0, 0, 0))],
        out_shape=[jax.ShapeDtypeStruct((S, vw), BF16), jax.ShapeDtypeStruct((nc, HEADS, dk, dv), F32)],
        scratch_shapes=[pltpu.VMEM((HEADS, dk, dv), F32)], args=(P, P, P, a), name=name, rides=rides, us=us)


def _gla_bwd(P, a, states, do, *, qk, vw, name, rides=None, us=0.0):
    S = P.shape[0]
    nc = S // CHUNK
    cps = math.gcd(nc, GLA_CHUNKS_PER_STEP)
    rows = cps * CHUNK
    nstep = nc // cps
    dk, dv = qk // HEADS, vw // HEADS
    scale = dk ** -0.5

    def body(q_ref, k_ref, v_ref, a_ref, do_ref, sc_ref, sp_ref, dq_ref, dk_ref, dv_ref, da_ref, ds_carry):
        i = pl.program_id(0)

        @pl.when(i == 0)
        def _():
            ds_carry[...] = jnp.zeros_like(ds_carry)

        has_prev = (i < nstep - 1).astype(F32)
        L = CHUNK
        r = lax.broadcasted_iota(jnp.int32, (L, L), 0)
        c = lax.broadcasted_iota(jnp.int32, (L, L), 1)
        tri_t = (c >= r).astype(BF16)
        last = (lax.broadcasted_iota(jnp.int32, (L, 1), 0) == L - 1).astype(F32)
        for h in range(HEADS):
            ks, vs = slice(h * dk, (h + 1) * dk), slice(h * dv, (h + 1) * dv)
            carry = ds_carry[h]
            for n in reversed(range(cps)):
                rs = slice(n * CHUNK, (n + 1) * CHUNK)
                _, _, E, dec = _chunk_decay(a_ref[rs, ks], dv)
                kk, vv, dov = k_ref[rs, ks], v_ref[rs, vs], do_ref[rs, vs]
                kd = kk * E
                ds = carry + _bf_dot(q_ref[rs, ks] * scale, dov, TN)
                dq_ref[rs, ks] = (_bf_dot(dov, sc_ref[n, h], NT) * scale).astype(dq_ref.dtype)
                dkd = _bf_dot(vv, ds, NT)
                dv_ref[rs, vs] = _bf_dot(kd, ds, NN).astype(dv_ref.dtype)
                dk_ref[rs, ks] = (dkd * E).astype(dk_ref.dtype)
                darg = dkd * kd
                before = sc_ref[n - 1, h] if n > 0 else sp_ref[0, h] * has_prev
                w = ds * before * dec
                dce = _dot_exact(jnp.ones((SUBLANES, dv), BF16), w, NT)[0:1, :]
                dce = dce + jnp.sum(darg, axis=0, keepdims=True)
                dcum = last * dce - darg
                da_ref[rs, ks] = _dot_exact(tri_t, dcum, NN)
                carry = dec * ds
            ds_carry[h] = carry

    rev = lambda i: nstep - 1 - i
    return _call(
        body, grid=(nstep,),
        in_specs=[pl.BlockSpec((rows, qk), lambda i: (rev(i), 0)),
                  pl.BlockSpec((rows, qk), lambda i: (rev(i), 1)),
                  pl.BlockSpec((rows, vw), lambda i: (rev(i), 2 * qk // vw)),
                  pl.BlockSpec((rows, qk), lambda i: (rev(i), 0)),
                  pl.BlockSpec((rows, vw), lambda i: (rev(i), 0)),
                  pl.BlockSpec((cps, HEADS, dk, dv), lambda i: (rev(i), 0, 0, 0)),
                  pl.BlockSpec((1, HEADS, dk, dv), lambda i: (jnp.maximum(rev(i) * cps - 1, 0), 0, 0, 0))],
        out_specs=[pl.BlockSpec((rows, qk), lambda i: (rev(i), 0)),
                   pl.BlockSpec((rows, qk), lambda i: (rev(i), 0)),
                   pl.BlockSpec((rows, vw), lambda i: (rev(i), 0)),
                   pl.BlockSpec((rows, qk), lambda i: (rev(i), 0))],
        out_shape=[jax.ShapeDtypeStruct((S, qk), BF16), jax.ShapeDtypeStruct((S, qk), BF16),
                   jax.ShapeDtypeStruct((S, vw), BF16), jax.ShapeDtypeStruct((S, qk), F32)],
        scratch_shapes=[pltpu.VMEM((HEADS, dk, dv), F32)], args=(P, P, P, a, do, states, states), name=name,
        rides=rides, us=us)


def _loss_head(y, target, *, tr, name):
    S, D = y.shape
    tr = min(tr, S)

    def body(y_ref, t_ref, l_ref, dy_ref):
        @pl.when(pl.program_id(0) == 0)
        def _():
            l_ref[...] = jnp.zeros_like(l_ref)

        e = y_ref[...] - t_ref[...]
        dy_ref[...] = e / D
        l_ref[...] += 0.5 * jnp.sum(jnp.mean(e * e, axis=-1, keepdims=True), axis=0, keepdims=True)

    loss, dy = pl.pallas_call(
        body, grid=(S // tr,),
        in_specs=[pl.BlockSpec((tr, D), lambda i: (i, 0))] * 2,
        out_specs=[pl.BlockSpec((SUBLANES, LANES), lambda i: (0, 0)), pl.BlockSpec((tr, D), lambda i: (i, 0))],
        out_shape=[jax.ShapeDtypeStruct((SUBLANES, LANES), F32), jax.ShapeDtypeStruct((S, D), F32)],
        compiler_params=_params(("arbitrary",)), name=name)(y, target)
    return loss[0, 0], dy


ADAMW_BYTES_PER_US = 2.5e6


def _adamw(parts, w, m, v, *, name, rides=None):
    G, R, C = w.shape
    n = parts[0].shape[0]
    unit = 2 * SUBLANES if parts[0].dtype == BF16 else SUBLANES
    tr = _pick(R, max(unit, (128 * 1024) // C), unit)
    nblk = R // tr
    bc1 = 1.0 - ADAM_B1 ** ADAM_STEP
    bc2 = 1.0 - ADAM_B2 ** ADAM_STEP

    def body(*refs):
        p_refs = refs[:G]
        w_ref, m_ref, v_ref, g_ref, d_ref, mo_ref, vo_ref = refs[G:]
        for q in range(G):
            @pl.when(pl.program_id(0) == q)
            def _(q=q):
                g = p_refs[q][0].astype(F32)
                for s in range(1, n):
                    g = g + p_refs[q][s].astype(F32)
                m2 = ADAM_B1 * m_ref[...] + (1.0 - ADAM_B1) * g
                v2 = ADAM_B2 * v_ref[...] + (1.0 - ADAM_B2) * (g * g)
                g_ref[...] = g
                mo_ref[...] = m2
                vo_ref[...] = v2
                d_ref[...] = -ADAM_LR * ((m2 / bc1) / (jnp.sqrt(v2 / bc2) + ADAM_EPS) + ADAM_WD * w_ref[...])

    def part_spec(q):
        return pl.BlockSpec((n, tr, C), lambda g, i: (0, jnp.where(g == q, i, jnp.where(g < q, 0, nblk - 1)), 0))

    blk = pl.BlockSpec((None, tr, C), lambda g, i: (g, i, 0))
    traffic = G * R * C * (n * parts[0].dtype.itemsize + 7 * 4)
    return _call(
        body, grid=(G, nblk), in_specs=[part_spec(q) for q in range(G)] + [blk, blk, blk],
        out_specs=[blk] * 4, out_shape=[jax.ShapeDtypeStruct((G, R, C), F32)] * 4,
        args=(*parts, w, m, v), name=name, rides=rides, us=traffic / ADAMW_BYTES_PER_US)


def _pair_add(blocks, received, core, *, name):
    _, R, C = blocks.shape
    tr = _pick(R, max(ROW_QUANTUM, (256 * 1024) // C), ROW_QUANTUM)

    def body(core_ref, mine_ref, theirs_ref, o_ref):
        o_ref[...] = (mine_ref[...].astype(F32) + theirs_ref[...].astype(F32)).astype(o_ref.dtype)

    grid_spec = pltpu.PrefetchScalarGridSpec(
        num_scalar_prefetch=1, grid=(NDEV // 2, R // tr),
        in_specs=[pl.BlockSpec((None, tr, C), lambda q, i, core_ref: (2 * q + core_ref[0], i, 0)),
                  pl.BlockSpec((None, tr, C), lambda q, i, core_ref: (q, i, 0))],
        out_specs=pl.BlockSpec((None, tr, C), lambda q, i, core_ref: (q, i, 0)))
    return pl.pallas_call(body, grid_spec=grid_spec, out_shape=jax.ShapeDtypeStruct((NDEV // 2, R, C), BF16),
                          compiler_params=_params(("arbitrary", "arbitrary")), name=name)(
        jnp.reshape(core, (1,)).astype(jnp.int32), blocks, received)


def _adamw_nd(parts, w, m, v, *, name):
    shape = w.shape
    C = shape[-1]
    R = math.prod(shape[:-1])
    outs = _adamw([parts.reshape(parts.shape[0], R, C)], w.reshape(1, R, C), m.reshape(1, R, C), v.reshape(1, R, C),
                  name=name)
    return [o.reshape(shape) for o in outs]


def _all_gather(arrs, *, name):
    n = len(arrs)
    HBM = pl.BlockSpec(memory_space=pl.ANY)

    def body(*refs):
        ins, outs = refs[:n], refs[n:2 * n]
        send_sems, recv_sems, local_sems = refs[2 * n:]
        x, y, c = _position()
        me, sibling = (x, y, c), (x, y, 1 - c)
        chips = [(1 - x, y), (x, 1 - y), (1 - x, 1 - y)]

        def copy(t, k, block, to, src=None):
            rows = outs[t].at[_index(block)]
            return pltpu.make_async_remote_copy(
                src_ref=rows if src is None else src, dst_ref=rows,
                send_sem=send_sems.at[t, k], recv_sem=recv_sems.at[t, k],
                device_id=to, device_id_type=pl.DeviceIdType.MESH)

        started = []
        mine = []
        for t in range(n):
            cp = pltpu.make_async_copy(ins[t], outs[t].at[_index(me)], local_sems.at[t])
            cp.start()
            mine.append(cp)
            first = [copy(t, 0, me, sibling, src=ins[t])]
            first += [copy(t, 1 + j, me, (*chip, c), src=ins[t]) for j, chip in enumerate(chips)]
            for cp in first:
                cp.start()
            started += first
        for t in range(n):
            for j, chip in enumerate(chips):
                copy(t, 1 + j, (*chip, c), me).wait_recv()
                cp = copy(t, 4 + j, (*chip, c), sibling)
                cp.start()
                started.append(cp)
        for t in range(n):
            copy(t, 0, sibling, me).wait_recv()
            for j, chip in enumerate(chips):
                copy(t, 4 + j, (*chip, 1 - c), me).wait_recv()
        for cp in started:
            cp.wait_send()
        for cp in mine:
            cp.wait()

    return pl.pallas_call(
        body, in_specs=[HBM] * n, out_specs=[HBM] * n,
        out_shape=[jax.ShapeDtypeStruct((NDEV,) + a.shape, a.dtype) for a in arrs],
        scratch_shapes=[pltpu.SemaphoreType.DMA((n, 7)), pltpu.SemaphoreType.DMA((n, 7)),
                        pltpu.SemaphoreType.DMA((n,))],
        name=name)(*arrs)


class _Pack:
    def __init__(self, arrs):
        self.shapes = [a.shape for a in arrs]
        self.sizes = [math.prod(s) for s in self.shapes]
        total = sum(self.sizes)
        unit = SUBLANES * LANES
        self.padded = -(-total // unit) * unit
        flat = jnp.concatenate([a.reshape(-1).astype(F32) for a in arrs] + [jnp.zeros((self.padded - total,), F32)])
        self.packed = flat.reshape(self.padded // LANES, LANES)

    def unpack(self, gathered):
        flat = gathered.reshape(NDEV, self.padded)
        out, off = [], 0
        for shape, size in zip(self.shapes, self.sizes):
            out.append(flat[:, off:off + size].reshape((NDEV,) + tuple(shape)))
            off += size
        return out


def _cols_from_shards(g):
    g = jnp.moveaxis(g, 0, -2)
    return g.reshape(g.shape[:-2] + (g.shape[-2] * g.shape[-1],))


def _cols_to_shards(w):
    K, N = w.shape
    return jnp.moveaxis(w.reshape(K, NDEV, N // NDEV), 1, 0)


def kernel(x, c, w_ada, b_ada, norm_g, w_in, w_a2, b_a2, gla_norm_g, w_out_gla, conv_mix_w, w_out_conv, w_o, w_up, ffn_conv_w, w_down, loss_target, m_w_ada, m_b_ada, m_norm_g, m_w_in, m_w_a2, m_b_a2, m_gla_norm_g, m_w_out_gla, m_conv_mix_w, m_w_out_conv, m_w_o, m_w_up, m_ffn_conv_w, m_w_down, v_w_ada, v_b_ada, v_norm_g, v_w_in, v_w_a2, v_b_a2, v_gla_norm_g, v_w_out_gla, v_conv_mix_w, v_w_out_conv, v_w_o, v_w_up, v_ffn_conv_w, v_w_down):
    depth = w_ada.shape[0]
    S, D = x.shape[1], x.shape[2]
    QK = w_a2.shape[2] * NDEV
    LR = w_a2.shape[1]
    VW = w_out_gla.shape[1]
    CW = w_out_conv.shape[1]
    FF = w_down.shape[1] * NDEV
    NIN = w_in.shape[2] * NDEV
    NADA = w_ada.shape[2]
    LRP = -(-LR // LANES) * LANES
    off_lr = 2 * QK + 2 * VW
    NP = NIN - LR + LRP
    assert NIN == 2 * QK + 2 * VW + LR + 3 * CW + 2 * D
    xi, yi, ci = _position()
    me = _index((xi, yi, ci))
    xs = x.reshape(S, D)
    target = loss_target.reshape(S, D)
    TR = 256
    TRC = 512
    TC = _pick(CW, 512)
    TCF = _pick(FF, 512)

    small = _Pack([c, norm_g, w_a2, conv_mix_w, ffn_conv_w])
    big_names = ["w_in", "w_out_gla", "w_out_conv", "w_o", "w_up", "w_down"]
    big = dict(w_in=w_in, w_out_gla=w_out_gla, w_out_conv=w_out_conv, w_o=w_o, w_up=w_up, w_down=w_down)

    def matrix(key, g):
        if key == "w_up":
            return g
        if key in ("w_o", "w_down"):
            return g.reshape(g.shape[0] * g.shape[1], g.shape[2])
        if key != "w_in":
            return _cols_from_shards(g)
        pieces = [g[d][:, lo - d * nin_shard:hi - d * nin_shard] for _, d, lo, hi in sorted(in_pieces)]
        return jnp.concatenate(pieces + [jnp.zeros((D, LRP - LR), BF16)], axis=-1)

    nin_shard = NIN // NDEV
    in_pieces = []
    for g_lo, g_hi, to in ((0, off_lr, 0), (off_lr, off_lr + LR, NIN - LR), (off_lr + LR, NIN, off_lr)):
        for d in range(NDEV):
            lo, hi = max(g_lo, d * nin_shard), min(g_hi, (d + 1) * nin_shard)
            if lo < hi:
                in_pieces.append((to + lo - g_lo, d, lo, hi))

    def in_grad_blocks(gW_all):
        blocks = []
        for d in range(NDEV):
            mine = sorted((lo, to, hi) for to, dd, lo, hi in in_pieces if dd == d)
            blocks.append(jnp.concatenate([gW_all[:, to:to + hi - lo] for lo, to, hi in mine], axis=1))
        return jnp.stack(blocks)

    (small_g, w_in0_g) = _all_gather([small.packed, w_in[0].astype(BF16)], name="gather_first")
    c_all, norm_g_s, w_a2_s, conv_w_s, ffn_w_s = small.unpack(small_g)
    c_all = c_all.reshape(NDEV, D)
    norm_g_f = _cols_from_shards(norm_g_s)
    w_a2_f = _cols_from_shards(w_a2_s)
    conv_w_f = _cols_from_shards(conv_w_s)
    ffn_w_f = _cols_from_shards(ffn_w_s)
    w_a2_p = jnp.concatenate([w_a2_f, jnp.zeros((depth, LRP - LR, QK), F32)], axis=1)

    rides = _Rides()
    GATHER_PARAMS_PER_US = 80e3
    FORWARD_PARAMS_PER_US = 800e3
    EXCHANGE_PARAMS_PER_US = 45e3
    PAIR_PARAMS_PER_US = 500e3
    BACKWARD_ICI_SHARE = 0.8
    weight_streams = {}
    for l in range(depth):
        for key in big_names:
            if (l, key) == (0, "w_in"):
                continue
            shard = big[key][l].astype(BF16)
            rows, cols = shard.shape
            first = rides.add(_Stream("gather_a", shard, rows, cols, GATHER_PARAMS_PER_US / (NDEV * cols)))
            second = rides.add(_Stream("gather_b", None, rows, cols, FORWARD_PARAMS_PER_US / (NDEV * cols), after=first))
            weight_streams[l, key] = (first, second)
    W = {(0, "w_in"): matrix("w_in", w_in0_g)}

    def weight(l, key):
        if (l, key) not in W:
            first, second = weight_streams[l, key]
            rides.need(first, second)
            W[l, key] = matrix(key, first.buf)
        return W[l, key]

    MP = LANES
    c_pad = jnp.concatenate([c_all, jnp.zeros((MP - NDEV, D), F32)], axis=0)
    (silu_c,) = _rw_fwd(lambda t: (t * _sigmoid(t),), [_v(c_pad)], [], [(D, BF16)], tr=MP, name="silu_c")
    mod_cols = _mm(silu_c, w_ada, name="ada_fwd")[:NDEV].reshape(NDEV, depth, NADA)
    (mod_g,) = _all_gather([jnp.moveaxis(mod_cols, 1, 0)], name="gather_mod")
    mod_all = jnp.moveaxis(mod_g, 0, 2).reshape(depth, NDEV, NDEV * NADA)
    mod_me = lax.dynamic_slice_in_dim(mod_all, me, 1, axis=1).reshape(depth, NDEV * NADA)
    (mod_me,) = _rw_fwd(lambda a, b: (a + b,), [_v(mod_me), _v(b_ada)], [], [(NDEV * NADA, F32)],
                        tr=depth, name="mod_bias")

    col = {}
    off = 0
    for nm, wd in (("q", QK), ("k", QK), ("v", VW), ("r", VW), ("cb", CW), ("cc", CW), ("cx", CW),
                   ("ga", D), ("gb", D), ("lr", LRP)):
        col[nm] = off
        off += wd

    saved = []
    xin = xs
    for l in range(depth):
        sh1, sc1, g1, sh2, sc2, g2 = [mod_me[l:l + 1, i * D:(i + 1) * D] for i in range(6)]
        ng = [norm_g_f[l, i:i + 1] for i in range(4)]
        cw = [conv_w_f[l, i:i + 1] for i in range(CONV_K)]
        fw = [ffn_w_f[l, i:i + 1] for i in range(CONV_K)]
        gn = gla_norm_g[l:l + 1]
        ba2 = b_a2[l:l + 1]
        def proj(act, key, name, out_dtype=BF16):
            return _mm(act, weight(l, key), name=name, rides=rides, out_dtype=out_dtype)

        (h1,) = _rw_fwd(_f_normmod, [_v(xin)], [_v(ng[0]), _v(sc1), _v(sh1)], [(D, BF16)], tr=TR, name=f"normmod1_{l}",
                        rides=rides, us=US_NORMMOD)
        P = proj(h1, "w_in", f"in_proj{l}")
        (a,) = _rw_fwd(_f_loga, [_v(P, LRP, col["lr"])], [V(w_a2_p[l], QK, None), _v(ba2)], [(QK, F32)],
                       tr=TR, name=f"loga{l}")
        o, states = _gla_fwd(P, a, qk=QK, vw=VW, name=f"gla_fwd{l}", rides=rides, us=US_GLA)
        dvh = VW // HEADS
        (ya_in,) = _rw_fwd(_f_glapost, [_v(o, dvh), _v(P, dvh, col["r"])], [V(gn, dvh, None)], [(VW, BF16)],
                           tr=TR, ncol=HEADS, name=f"glapost{l}", rides=rides, us=US_GLAPOST)
        ya = proj(ya_in, "w_out_gla", f"out_gla{l}")
        (yb_in,) = _rw_fwd(_f_convmix, [_v(P, TC, col["cb"]), _v(P, TC, col["cc"]), _v(P, TC, col["cx"])],
                           [_v(w, TC) for w in cw], [(CW, BF16)], tr=TRC, ncol=CW // TC, halo=True, name=f"convmix{l}")
        yb = proj(yb_in, "w_out_conv", f"out_conv{l}")
        (mg,) = _rw_fwd(_f_merge, [_v(P, D, col["ga"]), _v(P, D, col["gb"]), _v(ya), _v(yb)], [], [(D, BF16)],
                        tr=TR, name=f"merge{l}", rides=rides, us=US_MERGE)
        y = proj(mg, "w_o", f"o_proj{l}", F32)
        (x1,) = _rw_fwd(_f_resid, [_v(xin), _v(y)], [_v(g1), _v(ng[1])], [(D, F32)], tr=TR, name=f"resid1_{l}",
                        rides=rides, us=US_RESID)
        (h2,) = _rw_fwd(_f_normmod, [_v(x1)], [_v(ng[2]), _v(sc2), _v(sh2)], [(D, BF16)], tr=TR, name=f"normmod2_{l}",
                        rides=rides, us=US_NORMMOD)
        U = proj(h2, "w_up", f"up_proj{l}")
        (ff,) = _rw_fwd(_f_ffn, [_v(U, TCF, 0), _v(U, TCF, FF)], [_v(w, TCF) for w in fw], [(FF, BF16)],
                        tr=TRC, ncol=FF // TCF, halo=True, name=f"ffn{l}", rides=rides, us=US_FFN)
        y2 = proj(ff, "w_down", f"down_proj{l}", F32)
        (x2,) = _rw_fwd(_f_resid, [_v(x1), _v(y2)], [_v(g2), _v(ng[3])], [(D, F32)], tr=TR, name=f"resid2_{l}",
                        rides=rides, us=US_RESID)
        saved.append(dict(xin=xin, h1=h1, P=P, a=a, o=o, states=states, ya_in=ya_in, ya=ya, yb_in=yb_in, yb=yb,
                          mg=mg, y=y, x1=x1, h2=h2, U=U, ff=ff, y2=y2, mods=(sh1, sc1, g1, sh2, sc2, g2),
                          ng=ng, cw=cw, fw=fw, gn=gn, ba2=ba2))
        xin = x2

    loss_local, dx = _loss_head(xin, target, tr=TR, name="loss_head")
    loss = lax.psum(loss_local, MESH_AXES)

    grad_streams = {}
    IN_WGRAD_PARTS = 4
    UP_WGRAD_PARTS = 2

    rides.ici_share = BACKWARD_ICI_SHARE
    rides.min_ici_us = MIN_ICI_CARRIER_US

    def send(l, key, blocks):
        _, rows, cols = blocks.shape
        entry = {}
        tag = f"{key}{l}_{len(grad_streams.get((l, key), []))}"

        def swapped(pair):
            sums = _pair_add(blocks, pair.buf, ci, name=f"pair_add_{tag}")
            entry["chips"] = rides.add(_Stream("chips", sums, rows, cols, 2 * EXCHANGE_PARAMS_PER_US / (NDEV * cols)))

        entry["pair"] = rides.add(_Stream("pair", blocks, rows, cols, PAIR_PARAMS_PER_US / (NDEV * cols), on_done=swapped))
        grad_streams.setdefault((l, key), []).append(entry)

    def bmm(a, b, **kw):
        kw.setdefault("out_dtype", BF16)
        return _mm(a, b, rides=rides, **kw)

    gsmall = [None] * depth
    dmod = [None] * depth
    for l in reversed(range(depth)):
        s = saved[l]
        sh1, sc1, g1, sh2, sc2, g2 = s["mods"]
        ng, cw, fw = s["ng"], s["cw"], s["fw"]
        (dx1_a, dy2), (dg2, dng3) = _rw_bwd(
            _f_resid, [_v(s["x1"]), _v(s["y2"])], [_v(g2), _v(ng[3])], [_v(dx)],
            row_grads=[F32, BF16], par_grads=[True, True], tr=TR, name=f"resid2_bwd{l}", rides=rides, us=US_RESID_BWD)
        gW_down = bmm(s["ff"], dy2, ta=True, out_dtype=BF16, name=f"down_wgrad{l}")
        send(l, "w_down", gW_down.reshape(NDEV, FF // NDEV, D))
        dff = bmm(dy2, weight(l, "w_down"), tb=True, name=f"down_dgrad{l}")
        U = s["U"]
        (dgate, dup), dfw = _rw_bwd(
            _f_ffn, [_v(U, TCF, 0), _v(U, TCF, FF)], [_v(w, TCF) for w in fw], [_v(dff, TCF)],
            row_grads=[BF16, BF16], par_grads=[True] * 3, tr=TRC, ncol=FF // TCF, halo=True, name=f"ffn_bwd{l}",
            rides=rides, us=US_FFN_BWD)
        dU = jnp.concatenate([dgate, dup], axis=1)
        for part in range(UP_WGRAD_PARTS):
            gW_up = bmm(s["h2"], dU, ta=True, out_dtype=BF16, out_shards=True, m_part=(part, UP_WGRAD_PARTS),
                        name=f"up_wgrad{l}_{part}")
            send(l, "w_up", gW_up)
        dh2 = bmm(dU, weight(l, "w_up"), tb=True, name=f"up_dgrad{l}")
        (dx1,), (dng2, dsc2, dsh2) = _rw_bwd(
            _f_normmod, [_v(s["x1"])], [_v(ng[2]), _v(sc2), _v(sh2)], [_v(dh2)],
            row_grads=[F32], par_grads=[True] * 3, adds={0: dx1_a}, tr=TR, name=f"normmod2_bwd{l}",
            rides=rides, us=US_NORMMOD_BWD)
        (dxin_a, dy), (dg1, dng1) = _rw_bwd(
            _f_resid, [_v(s["xin"]), _v(s["y"])], [_v(g1), _v(ng[1])], [_v(dx1)],
            row_grads=[F32, BF16], par_grads=[True, True], tr=TR, name=f"resid1_bwd{l}", rides=rides, us=US_RESID_BWD)
        gW_o = bmm(s["mg"], dy, ta=True, out_dtype=BF16, name=f"o_wgrad{l}")
        send(l, "w_o", gW_o.reshape(NDEV, D // NDEV, D))
        dmg = bmm(dy, weight(l, "w_o"), tb=True, name=f"o_dgrad{l}")
        P = s["P"]
        (dga, dgb, dya, dyb), _ = _rw_bwd(
            _f_merge, [_v(P, D, col["ga"]), _v(P, D, col["gb"]), _v(s["ya"]), _v(s["yb"])], [], [_v(dmg)],
            row_grads=[BF16] * 4, par_grads=[], tr=TR, name=f"merge_bwd{l}", rides=rides, us=US_MERGE_BWD)
        gW_og = bmm(s["ya_in"], dya, ta=True, out_dtype=BF16, name=f"out_gla_wgrad{l}")
        send(l, "w_out_gla", _cols_to_shards(gW_og))
        dya_in = bmm(dya, weight(l, "w_out_gla"), tb=True, name=f"out_gla_dgrad{l}")
        gW_oc = bmm(s["yb_in"], dyb, ta=True, out_dtype=BF16, name=f"out_conv_wgrad{l}")
        send(l, "w_out_conv", _cols_to_shards(gW_oc))
        dyb_in = bmm(dyb, weight(l, "w_out_conv"), tb=True, name=f"out_conv_dgrad{l}")
        (dcb, dcc, dcx), dcw = _rw_bwd(
            _f_convmix, [_v(P, TC, col["cb"]), _v(P, TC, col["cc"]), _v(P, TC, col["cx"])],
            [_v(w, TC) for w in cw], [_v(dyb_in, TC)],
            row_grads=[BF16] * 3, par_grads=[True] * 3, tr=TRC, ncol=CW // TC, halo=True, name=f"convmix_bwd{l}")
        dvh = VW // HEADS
        (do, dr), (dgn,) = _rw_bwd(
            _f_glapost, [_v(s["o"], dvh), _v(P, dvh, col["r"])], [V(s["gn"], dvh, None)], [_v(dya_in, dvh)],
            row_grads=[BF16, BF16], par_grads=[True], tr=TR, ncol=HEADS, name=f"glapost_bwd{l}",
            rides=rides, us=US_GLAPOST_BWD)
        dq, dk, dv, da = _gla_bwd(P, s["a"], s["states"], do, qk=QK, vw=VW, name=f"gla_bwd{l}",
                                  rides=rides, us=US_GLA_BWD)
        (dlr,), (dwa2, dba2) = _rw_bwd(
            _f_loga, [_v(P, LRP, col["lr"])], [V(w_a2_p[l], QK, None), _v(s["ba2"])], [_v(da)],
            row_grads=[BF16], par_grads=[True, True], tr=TR, name=f"loga_bwd{l}")
        dP = jnp.concatenate([dq, dk, dv, dr, dcb, dcc, dcx, dga, dgb, dlr], axis=1)
        for part in range(IN_WGRAD_PARTS):
            gW_all = bmm(s["h1"], dP, ta=True, out_dtype=BF16, m_part=(part, IN_WGRAD_PARTS), name=f"in_wgrad{l}_{part}")
            send(l, "w_in", in_grad_blocks(gW_all))
        dh1 = bmm(dP, weight(l, "w_in"), tb=True, name=f"in_dgrad{l}")
        (dx,), (dng0, dsc1, dsh1) = _rw_bwd(
            _f_normmod, [_v(s["xin"])], [_v(ng[0]), _v(sc1), _v(sh1)], [_v(dh1)],
            row_grads=[F32], par_grads=[True] * 3, adds={0: dxin_a}, tr=TR, name=f"normmod1_bwd{l}",
            rides=rides, us=US_NORMMOD_BWD)
        vec = lambda t: t.reshape(1, -1)
        gsmall[l] = dict(
            norm_g=jnp.concatenate([vec(dng0), vec(dng1), vec(dng2), vec(dng3)], axis=0),
            w_a2=dwa2[0, :LR], b_a2=dba2.reshape(QK), gla_norm_g=jnp.sum(dgn, axis=0).reshape(dvh),
            conv_mix_w=jnp.concatenate([vec(t) for t in dcw], axis=0),
            ffn_conv_w=jnp.concatenate([vec(t) for t in dfw], axis=0))
        dmod[l] = jnp.concatenate([vec(t) for t in (dsh1, dsc1, dg1, dsh2, dsc2, dg2)], axis=1).reshape(-1)
    grad_x = dx.reshape(x.shape)

    stack = lambda key, src: jnp.stack([src[l][key] for l in range(depth)])
    small_names = ["norm_g", "w_a2", "b_a2", "gla_norm_g", "conv_mix_w", "ffn_conv_w"]
    gpack = _Pack([jnp.stack(dmod)] + [stack(k, gsmall) for k in small_names])
    (gsm,) = _all_gather([gpack.packed], name="gather_small_grads")
    gs = gpack.unpack(gsm)
    dmod_all = gs[0]
    parts_small = dict(zip(small_names, gs[1:]))

    def my_cols(t, n):
        return lax.dynamic_slice_in_dim(t, me * n, n, axis=t.ndim - 1)

    gW_ada = []
    for l in range(depth):
        dcols = my_cols(dmod_all[:, l], NADA)
        dcols = jnp.concatenate([dcols, jnp.zeros((MP - NDEV, NADA), F32)], axis=0)
        gW_ada.append(bmm(silu_c, dcols, ta=True, out_dtype=F32, name=f"ada_wgrad{l}")[None])
    parts = {}
    parts["b_ada"] = dmod_all
    parts["norm_g"] = my_cols(parts_small["norm_g"], D // NDEV)
    parts["w_a2"] = my_cols(parts_small["w_a2"], QK // NDEV)
    parts["b_a2"] = parts_small["b_a2"]
    parts["gla_norm_g"] = parts_small["gla_norm_g"]
    parts["conv_mix_w"] = my_cols(parts_small["conv_mix_w"], CW // NDEV)
    parts["ffn_conv_w"] = my_cols(parts_small["ffn_conv_w"], FF // NDEV)

    weights = dict(w_ada=(w_ada, m_w_ada, v_w_ada), b_ada=(b_ada, m_b_ada, v_b_ada),
                   norm_g=(norm_g, m_norm_g, v_norm_g), w_in=(w_in, m_w_in, v_w_in), w_a2=(w_a2, m_w_a2, v_w_a2),
                   b_a2=(b_a2, m_b_a2, v_b_a2), gla_norm_g=(gla_norm_g, m_gla_norm_g, v_gla_norm_g),
                   w_out_gla=(w_out_gla, m_w_out_gla, v_w_out_gla), conv_mix_w=(conv_mix_w, m_conv_mix_w, v_conv_mix_w),
                   w_out_conv=(w_out_conv, m_w_out_conv, v_w_out_conv), w_o=(w_o, m_w_o, v_w_o),
                   w_up=(w_up, m_w_up, v_w_up), ffn_conv_w=(ffn_conv_w, m_ffn_conv_w, v_ffn_conv_w),
                   w_down=(w_down, m_w_down, v_w_down))
    done = {}
    for nm in ["w_ada", "w_down", "w_up", "w_o", "w_out_gla", "w_out_conv", "w_in"]:
        w, m, v = weights[nm]
        if nm == "w_ada":
            done[nm] = _adamw(gW_ada, w, m, v, name=f"adamw_{nm}")
            continue
        entries = [e for l in range(depth) for e in grad_streams[l, nm]]
        rides.need(*[e["pair"] for e in entries])
        streams = [e["chips"] for e in entries]
        rides.need(*streams)
        groups, rows = len(streams), w.shape[1] * depth // len(streams)
        split = lambda t: t.reshape(groups, rows, t.shape[2])
        outs = _adamw([s.buf for s in streams], split(w), split(m), split(v), name=f"adamw_{nm}")
        done[nm] = [o.reshape(w.shape) for o in outs]
    for nm, (w, m, v) in weights.items():
        if nm not in done:
            done[nm] = _adamw_nd(parts[nm], w, m, v, name=f"adamw_{nm}")
    grads, deltas, new_m, new_v = zip(*[done[nm] for nm in weights])
    return (loss, grad_x, *grads, *deltas, *new_m, *new_v)
```

```python
import functools
import math
from typing import Any, NamedTuple

import jax
import jax.numpy as jnp
from jax import lax
from jax.experimental import pallas as pl
from jax.experimental.pallas import tpu as pltpu

F32 = jnp.float32
BF16 = jnp.bfloat16
MESH_AXES = ("x", "y", "c")
NDEV = 8
EPS = 1e-6
CHUNK = 64
HEADS = 4
GLA_TAU = 16.0
CONV_K = 3
ADAM_LR = 0.001
ADAM_B1 = 0.9
ADAM_B2 = 0.999
ADAM_EPS = 1e-08
ADAM_WD = 0.01
ADAM_STEP = 10

LANES = 128
SUBLANES = 8
HALO = 2 * SUBLANES
VMEM_LIMIT = 56 * 1024 * 1024


def _params(sem=None):
    return pltpu.CompilerParams(dimension_semantics=sem, vmem_limit_bytes=VMEM_LIMIT)


def _pick(n, cap, unit=LANES):
    best = None
    for d in range(unit, min(n, cap) + 1, unit):
        if n % d == 0:
            best = d
    return best if best is not None else n


def _position():
    return lax.axis_index("x"), lax.axis_index("y"), lax.axis_index("c")


def _index(p):
    return 4 * p[0] + 2 * p[1] + p[2]


def _peer(k):
    x, y, c = _position()
    return ((1 - x) if k & 4 else x, (1 - y) if k & 2 else y, (1 - c) if k & 1 else c)


ROW_QUANTUM = 2 * SUBLANES
MIN_ROWS = 128
MIN_ICI_CARRIER_US = 50.0


class _Stream:
    def __init__(self, mode, src, rows, cols, rate, after=None, on_done=None):
        self.mode, self.src, self.R, self.rate, self.after = mode, src, rows, rate, after
        self.shape = (NDEV if mode.startswith("gather") else NDEV // 2, rows, cols)
        self.holder = self if after is None else after
        self.buf = None
        self.done = 0
        self.on_done = on_done


class _Exchange:
    def __init__(self, items):
        self.items = items
        self.ins, self.src_pos, self.alias, self.holders, self.out_pos = [], [], {}, [], []
        for s, _, _ in items:
            if not any(s.holder is h for h in self.holders):
                self.holders.append(s.holder)
        for s, _, _ in items:
            self.out_pos.append([n for n, h in enumerate(self.holders) if h is s.holder][0])
            if s.mode == "gather_b":
                self.src_pos.append(None)
            else:
                self.src_pos.append(len(self.ins))
                self.ins.append(s.src)
        for n, h in enumerate(self.holders):
            if h.buf is not None:
                self.alias[len(self.ins)] = n
                self.ins.append(h.buf)
        self.out_shape = [jax.ShapeDtypeStruct(h.shape, BF16) for h in self.holders]
        n = len(items)
        self.scratch = [pltpu.SemaphoreType.DMA((n, NDEV)), pltpu.SemaphoreType.DMA((n, NDEV)),
                        pltpu.SemaphoreType.DMA((n,))]

    def deliver(self, outs):
        for h, o in zip(self.holders, outs):
            h.buf = o

    def _copies(self, ins, outs, sems):
        send_sems, recv_sems, local_sems = sems
        me = _index(_position())
        slot = lambda k: _index(_peer(k))
        local, sends, arrivals = [], [], []
        for n, (s, r0, rc) in enumerate(self.items):
            out, rows = outs[self.out_pos[n]], pl.ds(r0, rc)
            src = out if self.src_pos[n] is None else ins[self.src_pos[n]]
            if s.mode == "pair":
                other = 1 - _position()[2]
                moves = [(q, src.at[2 * q + other, rows], out.at[q, rows], _peer(1), out.at[q, rows])
                         for q in range(NDEV // 2)]
            elif s.mode == "chips":
                chip = lambda k: slot(k) // 2
                local.append(pltpu.make_async_copy(src.at[me // 2, rows], out.at[me // 2, rows], local_sems.at[n]))
                moves = [(k, src.at[chip(k), rows], out.at[me // 2, rows], _peer(k), out.at[chip(k), rows])
                         for k in (2, 4, 6)]
            elif s.mode == "gather_a":
                local.append(pltpu.make_async_copy(src.at[rows], out.at[me, rows], local_sems.at[n]))
                moves = [(k, src.at[rows], out.at[me, rows], _peer(k), out.at[slot(k), rows]) for k in (1, 2, 4, 6)]
            else:
                moves = [(k, src.at[slot(k), rows], out.at[slot(k), rows], _peer(1), out.at[slot(k ^ 1), rows])
                         for k in (2, 4, 6)]
            for k, src_ref, dst_ref, to, landing in moves:
                common = dict(send_sem=send_sems.at[n, k], recv_sem=recv_sems.at[n, k],
                              device_id=to, device_id_type=pl.DeviceIdType.MESH)
                sends.append(pltpu.make_async_remote_copy(src_ref=src_ref, dst_ref=dst_ref, **common))
                arrivals.append(pltpu.make_async_remote_copy(src_ref=landing, dst_ref=landing, **common))
        return local, sends, arrivals

    def start(self, ins, outs, sems):
        local, sends, _ = self._copies(ins, outs, sems)
        for cp in local + sends:
            cp.start()

    def wait(self, ins, outs, sems):
        local, sends, arrivals = self._copies(ins, outs, sems)
        for cp in arrivals:
            cp.wait_recv()
        for cp in sends:
            cp.wait_send()
        for cp in local:
            cp.wait()


class _Rides:
    def __init__(self):
        self.queue = []
        self.finished = []
        self.flushes = 0
        self.ici_share = 1.0
        self.min_ici_us = 0.0

    def add(self, stream):
        self.queue.append(stream)
        return stream

    def take(self, budget, only=None, ici=True):
        ready = {id(s): (s.after.done if s.after is not None else s.R) for s in self.queue}
        left = {"ici": budget * self.ici_share if ici else 0.0, "d2d": budget}
        items = []
        for s in list(self.queue):
            if only is not None and not any(s is t for t in only):
                continue
            link = "d2d" if s.mode in ("gather_b", "pair") else "ici"
            avail = ready[id(s)] - s.done
            if avail <= 0:
                continue
            rc = avail if budget == float("inf") else min(avail, int(left[link] * s.rate) // ROW_QUANTUM * ROW_QUANTUM)
            if rc < min(avail, MIN_ROWS):
                continue
            if avail - rc <= 2 * MIN_ROWS:
                rc = avail
            items.append((s, s.done, rc))
            s.done += rc
            left[link] -= rc / s.rate
            if s.done == s.R:
                self.queue.remove(s)
                self.finished.append(s)
        return items

    def fire(self):
        finished, self.finished = self.finished, []
        for s in finished:
            if s.on_done is not None:
                s.on_done(s)

    def need(self, *streams):
        while any(s.done < s.R for s in streams):
            items = self.take(float("inf"), only=streams)
            comm = _Exchange(items)
            hbm = pl.BlockSpec(memory_space=pl.ANY)
            nin, nout = len(comm.ins), len(comm.holders)

            def body(*refs, comm=comm, nin=nin, nout=nout):
                comm.start(refs[:nin], refs[nin:nin + nout], refs[nin + nout:])
                comm.wait(refs[:nin], refs[nin:nin + nout], refs[nin + nout:])

            outs = pl.pallas_call(body, in_specs=[hbm] * nin, out_specs=[hbm] * nout, out_shape=comm.out_shape,
                                  scratch_shapes=comm.scratch, input_output_aliases=dict(comm.alias),
                                  name=f"exchange_alone{self.flushes}")(*comm.ins)
            self.flushes += 1
            comm.deliver(outs)
            self.fire()


def _call(body, *, grid, in_specs, out_specs, out_shape, scratch_shapes=(), args, name, rides=None, us=0.0):
    sem = ("arbitrary",) * len(grid)
    items = rides.take(us, ici=us >= rides.min_ici_us) if rides is not None else []
    if not items:
        return pl.pallas_call(body, grid=grid, in_specs=in_specs, out_specs=out_specs, out_shape=out_shape,
                              scratch_shapes=list(scratch_shapes), compiler_params=_params(sem), name=name)(*args)
    comm = _Exchange(items)
    nin, nout, nscr, cin, cout = len(in_specs), len(out_specs), len(scratch_shapes), len(comm.ins), len(comm.holders)
    hbm = pl.BlockSpec(memory_space=pl.ANY)

    def both(*refs):
        ins, c_ins = refs[:nin], refs[nin:nin + cin]
        pos = nin + cin
        outs, c_outs = refs[pos:pos + nout], refs[pos + nout:pos + nout + cout]
        pos += nout + cout
        scr, sems = refs[pos:pos + nscr], refs[pos + nscr:]
        first = functools.reduce(lambda p, q: p & q, [pl.program_id(d) == 0 for d in range(len(grid))])
        last = functools.reduce(lambda p, q: p & q, [pl.program_id(d) == grid[d] - 1 for d in range(len(grid))])

        @pl.when(first)
        def _():
            comm.start(c_ins, c_outs, sems)

        body(*ins, *outs, *scr)

        @pl.when(last)
        def _():
            comm.wait(c_ins, c_outs, sems)

    res = pl.pallas_call(both, grid=grid, in_specs=list(in_specs) + [hbm] * cin, out_specs=list(out_specs) + [hbm] * cout,
                         out_shape=list(out_shape) + comm.out_shape, scratch_shapes=list(scratch_shapes) + comm.scratch,
                         input_output_aliases={nin + i: nout + o for i, o in comm.alias.items()},
                         compiler_params=_params(sem), name=name)(*args, *comm.ins)
    comm.deliver(res[nout:])
    rides.fire()
    return res[:nout]


US_GLA, US_GLAPOST, US_MERGE, US_FFN, US_RESID, US_NORMMOD = 100.0, 40.0, 45.0, 110.0, 40.0, 24.0
US_FFN_BWD, US_GLA_BWD, US_MERGE_BWD, US_RESID_BWD, US_NORMMOD_BWD, US_GLAPOST_BWD = 215.0, 118.0, 57.0, 57.0, 54.0, 52.0
MM_VMEM_BUDGET = 44 * 1024 * 1024
MM_FLOPS_PER_US = 780e6


def _mm(a, b, *, ta=False, tb=False, out_dtype=F32, name, rides=None, out_shards=False, m_rows=None):
    if ta:
        K, M = a.shape
    else:
        M, K = a.shape
    m_first, M = (0, M) if m_rows is None else m_rows
    sharded = b.ndim == 3
    if sharded:
        ns, n = b.shape[0], b.shape[2]
        Kb, N = (ns * n, b.shape[1]) if tb else (b.shape[1], ns * n)
    elif tb:
        N, Kb = b.shape
    else:
        Kb, N = b.shape
    assert K == Kb, (a.shape, b.shape, ta, tb)
    tm = _pick(M, 1024)
    tn = N // NDEV if out_shards else n if (sharded and not tb) else _pick(N, 1152)
    sa, sb, so = a.dtype.itemsize, b.dtype.itemsize, jnp.dtype(out_dtype).itemsize

    def vmem(tk):
        blocks = 2 * (tm * tk * sa + tk * tn * sb) + 2 * tm * tn * so + (tm * tn * 4 if tk < K else 0)
        return blocks + (tm * K * 2 if ta else 0)

    if sharded and tb:
        tk = n
    else:
        tk = K
        if vmem(K) > MM_VMEM_BUDGET:
            fits = [d for d in range(LANES, K, LANES) if K % d == 0 and vmem(d) <= MM_VMEM_BUDGET]
            tk = max(fits) if fits else _pick(K, 512)
    nk = K // tk
    dims = (((1,), (1 if tb else 0,)), ((), ()))

    def dot(a_ref, b_ref, at_ref):
        if ta:
            k = pl.program_id(2)

            @pl.when(pl.program_id(1) == 0)
            def _():
                at_ref[k] = a_ref[...].astype(BF16).T

            lhs = at_ref[k]
        else:
            lhs = a_ref[...].astype(BF16)
        return lax.dot_general(lhs, b_ref[...].astype(BF16), dims, preferred_element_type=F32)

    def body_one(a_ref, b_ref, o_ref, *at_ref):
        o_ref[...] = dot(a_ref, b_ref, *at_ref or (None,)).astype(o_ref.dtype)

    def body_acc(a_ref, b_ref, o_ref, acc_ref, *at_ref):
        k = pl.program_id(2)
        at = (at_ref or (None,))[0]

        @pl.when(k == 0)
        def _():
            acc_ref[...] = dot(a_ref, b_ref, at)

        @pl.when((k > 0) & (k < nk - 1))
        def _():
            acc_ref[...] += dot(a_ref, b_ref, at)

        @pl.when(k == nk - 1)
        def _():
            o_ref[...] = (acc_ref[...] + dot(a_ref, b_ref, at)).astype(o_ref.dtype)

    assert m_first % tm == 0
    i0 = m_first // tm
    a_spec = (pl.BlockSpec((tk, tm), lambda i, j, k: (k, i0 + i)) if ta
              else pl.BlockSpec((tm, tk), lambda i, j, k: (i0 + i, k)))
    if sharded:
        b_spec = (pl.BlockSpec((None, tn, tk), lambda i, j, k: (k, j, 0)) if tb
                  else pl.BlockSpec((None, tk, tn), lambda i, j, k: (j, k, 0)))
    else:
        b_spec = (pl.BlockSpec((tn, tk), lambda i, j, k: (j, k)) if tb
                  else pl.BlockSpec((tk, tn), lambda i, j, k: (k, j)))
    if out_shards:
        out_spec = pl.BlockSpec((None, tm, tn), lambda i, j, k: (j, i, 0))
        out_shape = jax.ShapeDtypeStruct((NDEV, M, tn), out_dtype)
    else:
        out_spec = pl.BlockSpec((tm, tn), lambda i, j, k: (i, j))
        out_shape = jax.ShapeDtypeStruct((M, N), out_dtype)
    (out,) = _call(
        body_one if nk == 1 else body_acc, grid=(M // tm, N // tn, nk), in_specs=[a_spec, b_spec],
        out_specs=[out_spec], out_shape=[out_shape],
        scratch_shapes=([] if nk == 1 else [pltpu.VMEM((tm, tn), F32)]) + ([pltpu.VMEM((nk, tm, tk), BF16)] if ta else []),
        args=(a, b), name=name,
        rides=rides, us=2.0 * M * N * K / MM_FLOPS_PER_US)
    return out


class V(NamedTuple):
    arr: Any
    width: int
    base: Any


def _v(arr, width=None, col0=0):
    width = arr.shape[-1] if width is None else width
    assert col0 % width == 0
    return V(arr, width, col0 // width)


def _row_specs(v, tr, nrow8, halo):
    main = pl.BlockSpec((tr, v.width), lambda j, i, b=v.base: (i, b + j))
    if not halo:
        return [main]
    per = tr // HALO
    prev = pl.BlockSpec((HALO, v.width), lambda j, i, b=v.base: (jnp.maximum(i * per - 1, 0), b + j))
    nxt = pl.BlockSpec((HALO, v.width), lambda j, i, b=v.base: (jnp.minimum((i + 1) * per, nrow8 - 1), b + j))
    return [prev, main, nxt]


def _par_spec(p):
    if p.base is None:
        return pl.BlockSpec(p.arr.shape, lambda j, i: (0,) * p.arr.ndim)
    return pl.BlockSpec((p.arr.shape[0], p.width), lambda j, i, b=p.base: (0, b + j))


def _load_rows(refs, i, nrow, halo):
    if not halo:
        return refs[0][...].astype(F32)
    prev, main, nxt = refs
    pv = jnp.where(i > 0, prev[...].astype(F32), 0.0)
    nv = jnp.where(i < nrow - 1, nxt[...].astype(F32), 0.0)
    return jnp.concatenate([pv, main[...].astype(F32), nv], axis=0)


def _center_mask(tr):
    row = lax.broadcasted_iota(jnp.int32, (tr + 2 * HALO, 1), 0)
    return ((row >= HALO) & (row < HALO + tr)).astype(F32)


def _rw_fwd(f, rows, params, outs, *, tr, ncol=1, halo=False, name, rides=None, us=0.0):
    S = rows[0].arr.shape[0]
    tr = min(tr, S)
    nrow = S // tr
    per = 3 if halo else 1
    nr, npar = len(rows), len(params)

    def body(*refs):
        i = pl.program_id(1)
        vals = [_load_rows(refs[per * r: per * (r + 1)], i, nrow, halo) for r in range(nr)]
        pvals = [refs[per * nr + p][...].astype(F32) for p in range(npar)]
        out_refs = refs[per * nr + npar:]
        res = f(*vals, *pvals, _center_mask(tr)) if halo else f(*vals, *pvals)
        for r, o in zip(res, out_refs):
            o[...] = (r[HALO:HALO + tr] if halo else r).astype(o.dtype)

    in_specs = []
    for v in rows:
        in_specs += _row_specs(v, tr, S // HALO, halo)
    in_specs += [_par_spec(p) for p in params]
    args = []
    for v in rows:
        args += [v.arr] * per
    args += [p.arr for p in params]
    return _call(
        body, grid=(ncol, nrow), in_specs=in_specs,
        out_specs=[pl.BlockSpec((tr, tot // ncol), lambda j, i: (i, j)) for tot, _ in outs],
        out_shape=[jax.ShapeDtypeStruct((S, tot), dt) for tot, dt in outs],
        args=args, name=name, rides=rides, us=us)


def _rw_bwd(f, rows, params, douts, *, row_grads, par_grads, adds=None, tr, ncol=1, halo=False, name,
            rides=None, us=0.0):
    S = rows[0].arr.shape[0]
    tr = min(tr, S)
    nrow = S // tr
    per = 3 if halo else 1
    adds = adds or {}
    nr, npar, nd = len(rows), len(params), len(douts)
    add_keys = sorted(adds)
    rg_idx = [k for k in range(nr) if row_grads[k]]
    pg_idx = [k for k in range(npar) if par_grads[k]]

    def body(*refs):
        i = pl.program_id(1)
        pos = 0
        vals = [_load_rows(refs[per * r: per * (r + 1)], i, nrow, halo) for r in range(nr)]
        pos = per * nr
        pvals = [refs[pos + p][...].astype(F32) for p in range(npar)]
        pos += npar
        dvals = [_load_rows(refs[pos + per * d: pos + per * (d + 1)], i, nrow, halo) for d in range(nd)]
        pos += per * nd
        add_vals = {k: refs[pos + n][...].astype(F32) for n, k in enumerate(add_keys)}
        pos += len(add_keys)
        rg_refs = refs[pos: pos + len(rg_idx)]
        pg_refs = refs[pos + len(rg_idx):]
        if halo:
            center = _center_mask(tr)
            fun = lambda *a: f(*a, center)
        else:
            fun = f
        _, vjp = jax.vjp(fun, *vals, *pvals)
        grads = vjp(tuple(dvals))
        for ref, k in zip(rg_refs, rg_idx):
            g = grads[k]
            g = g[HALO:HALO + tr] if halo else g
            if k in add_vals:
                g = g + add_vals[k]
            ref[...] = g.astype(ref.dtype)
        for ref, k in zip(pg_refs, pg_idx):
            @pl.when(i == 0)
            def _(ref=ref):
                ref[...] = jnp.zeros_like(ref)
            ref[...] += grads[nr + k]

    in_specs, args = [], []
    for v in list(rows):
        in_specs += _row_specs(v, tr, S // HALO, halo)
        args += [v.arr] * per
    in_specs += [_par_spec(p) for p in params]
    args += [p.arr for p in params]
    for v in douts:
        in_specs += _row_specs(v, tr, S // HALO, halo)
        args += [v.arr] * per
    for k in add_keys:
        in_specs += _row_specs(_v(adds[k], rows[k].width), tr, S // HALO, False)
        args += [adds[k]]
    out_specs, out_shape = [], []
    for k in rg_idx:
        w = rows[k].width
        out_specs.append(pl.BlockSpec((tr, w), lambda j, i: (i, j)))
        out_shape.append(jax.ShapeDtypeStruct((S, ncol * w), row_grads[k]))
    for k in pg_idx:
        p = params[k]
        blk = p.arr.shape if p.base is None else (p.arr.shape[0], p.width)
        out_specs.append(pl.BlockSpec((None,) + tuple(blk), lambda j, i, n=len(blk): (j,) + (0,) * n))
        out_shape.append(jax.ShapeDtypeStruct((ncol,) + tuple(blk), F32))
    res = _call(body, grid=(ncol, nrow), in_specs=in_specs, out_specs=out_specs, out_shape=out_shape, args=args, name=name,
                rides=rides, us=us)
    rg = [None] * nr
    for n, k in enumerate(rg_idx):
        rg[k] = res[n]
    pg = [None] * npar
    for n, k in enumerate(pg_idx):
        pg[k] = res[len(rg_idx) + n]
    return rg, pg


def _sigmoid(x):
    return 1.0 / (1.0 + jnp.exp(-x))


def _rms(x, g):
    return x * lax.rsqrt(jnp.mean(x * x, axis=-1, keepdims=True) + EPS) * g


def _f_normmod(x, g, sc, sh):
    return (_rms(x, g) * (1.0 + sc) + sh,)


def _f_resid(x, y, gate, g):
    return (x + gate * _rms(y, g),)


def _f_merge(ga, gb, ya, yb):
    return (_sigmoid(ga) * ya + _sigmoid(gb) * yb,)


def _f_glapost(o, r, g):
    return (_rms(o, g) * (r * _sigmoid(r)),)


def _roll_rows(x, k):
    return pltpu.roll(x, k % x.shape[0], 0)


@functools.partial(jax.custom_vjp, nondiff_argnums=(1,))
def _shift(x, k):
    return _roll_rows(x, k)


def _shift_fwd(x, k):
    return _roll_rows(x, k), None


def _shift_bwd(k, _, g):
    return (_roll_rows(g, -k),)


_shift.defvjp(_shift_fwd, _shift_bwd)


def _taps(w, center):
    return jnp.where(center > 0.0, w, lax.stop_gradient(w))


def _conv3(u, w0, w1, w2, center):
    return (_taps(w0, center) * _shift(u, 2) + _taps(w1, center) * _shift(u, 1)) + _taps(w2, center) * u


def _f_convmix(cb, cc, cx, w0, w1, w2, center):
    return (cb * _conv3(cc * cx, w0, w1, w2, center),)


def _gelu_tanh(x):
    return 0.5 * x * (1.0 + jnp.tanh(math.sqrt(2.0 / math.pi) * (x + 0.044715 * (x * x * x))))


def _f_ffn(gate, up, w0, w1, w2, center):
    return (_gelu_tanh(_conv3(gate, w0, w1, w2, center)) * up,)


@jax.custom_vjp
def _bdot(a, b):
    return jnp.dot(a.astype(BF16), b.astype(BF16), preferred_element_type=F32)


def _bdot_fwd(a, b):
    return _bdot(a, b), (a, b)


def _bdot_bwd(res, g):
    a, b = res
    gb = g.astype(BF16)
    da = lax.dot_general(gb, b.astype(BF16), (((1,), (1,)), ((), ())), preferred_element_type=F32)
    db = lax.dot_general(a.astype(BF16), gb, (((0,), (0,)), ((), ())), preferred_element_type=F32)
    return da, db


_bdot.defvjp(_bdot_fwd, _bdot_bwd)


@jax.custom_vjp
def _log_sigmoid(z):
    return jnp.minimum(z, 0.0) - jnp.log(1.0 + jnp.exp(-jnp.abs(z)))


def _log_sigmoid_fwd(z):
    return _log_sigmoid(z), z


def _log_sigmoid_bwd(z, g):
    return (g * _sigmoid(-z),)


_log_sigmoid.defvjp(_log_sigmoid_fwd, _log_sigmoid_bwd)


def _f_loga(lr, w, b):
    return (_log_sigmoid(_bdot(lr, w) + b) / GLA_TAU,)


def _split3(x):
    hi = x.astype(BF16)
    r1 = x - hi.astype(F32)
    mid = r1.astype(BF16)
    lo = (r1 - mid.astype(F32)).astype(BF16)
    return hi, mid, lo


def _dot_exact(a01, x, dims):
    hi, mid, lo = _split3(x)
    d = lambda p: lax.dot_general(a01, p, dims, preferred_element_type=F32)
    return (d(lo) + d(mid)) + d(hi)


def _dot_exact_lhs(x, a01, dims):
    hi, mid, lo = _split3(x)
    d = lambda p: lax.dot_general(p, a01, dims, preferred_element_type=F32)
    return (d(lo) + d(mid)) + d(hi)


GLA_CHUNKS_PER_STEP = 4

NN = (((1,), (0,)), ((), ()))
NT = (((1,), (1,)), ((), ()))
TN = (((0,), (0,)), ((), ()))


def _bf_dot(a, b, dims):
    return lax.dot_general(a.astype(BF16), b.astype(BF16), dims, preferred_element_type=F32)


def _chunk_decay(a, dv):
    L = a.shape[0]
    r = lax.broadcasted_iota(jnp.int32, (L, L), 0)
    c = lax.broadcasted_iota(jnp.int32, (L, L), 1)
    tri = (r >= c).astype(BF16)
    cum = _dot_exact(tri, a, NN)
    ce = cum[L - 1:L, :]
    dec = jnp.exp(_dot_exact_lhs(a, jnp.ones((L, dv), BF16), TN))
    return cum, ce, jnp.exp(ce - cum), dec


def _gla_fwd(P, a, *, qk, vw, name, rides=None, us=0.0):
    S = P.shape[0]
    nc = S // CHUNK
    cps = math.gcd(nc, GLA_CHUNKS_PER_STEP)
    rows = cps * CHUNK
    dk, dv = qk // HEADS, vw // HEADS
    scale = dk ** -0.5

    def body(q_ref, k_ref, v_ref, a_ref, o_ref, st_ref, state):
        @pl.when(pl.program_id(0) == 0)
        def _():
            state[...] = jnp.zeros_like(state)

        for h in range(HEADS):
            ks, vs = slice(h * dk, (h + 1) * dk), slice(h * dv, (h + 1) * dv)
            s = state[h]
            for c in range(cps):
                rs = slice(c * CHUNK, (c + 1) * CHUNK)
                _, _, E, dec = _chunk_decay(a_ref[rs, ks], dv)
                kd = k_ref[rs, ks] * E
                s = dec * s + _bf_dot(kd, v_ref[rs, vs], TN)
                st_ref[c, h] = s
                o_ref[rs, vs] = _bf_dot(q_ref[rs, ks] * scale, s, NN).astype(o_ref.dtype)
            state[h] = s

    return _call(
        body, grid=(nc // cps,),
        in_specs=[pl.BlockSpec((rows, qk), lambda i: (i, 0)),
                  pl.BlockSpec((rows, qk), lambda i: (i, 1)),
                  pl.BlockSpec((rows, vw), lambda i: (i, 2 * qk // vw)),
                  pl.BlockSpec((rows, qk), lambda i: (i, 0))],
        out_specs=[pl.BlockSpec((rows, vw), lambda i: (i, 0)),
                   pl.BlockSpec((cps, HEADS, dk, dv), lambda i: (i, 0, 0, 0))],
        out_shape=[jax.ShapeDtypeStruct((S, vw), BF16), jax.ShapeDtypeStruct((nc, HEADS, dk, dv), F32)],
        scratch_shapes=[pltpu.VMEM((HEADS, dk, dv), F32)], args=(P, P, P, a), name=name, rides=rides, us=us)


def _gla_bwd(P, a, states, do, *, qk, vw, name, rides=None, us=0.0):
    S = P.shape[0]
    nc = S // CHUNK
    cps = math.gcd(nc, GLA_CHUNKS_PER_STEP)
    rows = cps * CHUNK
    nstep = nc // cps
    dk, dv = qk // HEADS, vw // HEADS
    scale = dk ** -0.5

    def body(q_ref, k_ref, v_ref, a_ref, do_ref, sc_ref, sp_ref, dq_ref, dk_ref, dv_ref, da_ref, ds_carry):
        i = pl.program_id(0)

        @pl.when(i == 0)
        def _():
            ds_carry[...] = jnp.zeros_like(ds_carry)

        has_prev = (i < nstep - 1).astype(F32)
        L = CHUNK
        r = lax.broadcasted_iota(jnp.int32, (L, L), 0)
        c = lax.broadcasted_iota(jnp.int32, (L, L), 1)
        tri_t = (c >= r).astype(BF16)
        last = (lax.broadcasted_iota(jnp.int32, (L, 1), 0) == L - 1).astype(F32)
        for h in range(HEADS):
            ks, vs = slice(h * dk, (h + 1) * dk), slice(h * dv, (h + 1) * dv)
            carry = ds_carry[h]
            for n in reversed(range(cps)):
                rs = slice(n * CHUNK, (n + 1) * CHUNK)
                _, _, E, dec = _chunk_decay(a_ref[rs, ks], dv)
                kk, vv, dov = k_ref[rs, ks], v_ref[rs, vs], do_ref[rs, vs]
                kd = kk * E
                ds = carry + _bf_dot(q_ref[rs, ks] * scale, dov, TN)
                dq_ref[rs, ks] = (_bf_dot(dov, sc_ref[n, h], NT) * scale).astype(dq_ref.dtype)
                dkd = _bf_dot(vv, ds, NT)
                dv_ref[rs, vs] = _bf_dot(kd, ds, NN).astype(dv_ref.dtype)
                dk_ref[rs, ks] = (dkd * E).astype(dk_ref.dtype)
                darg = dkd * kd
                before = sc_ref[n - 1, h] if n > 0 else sp_ref[0, h] * has_prev
                w = ds * before * dec
                dce = _dot_exact(jnp.ones((SUBLANES, dv), BF16), w, NT)[0:1, :]
                dce = dce + jnp.sum(darg, axis=0, keepdims=True)
                dcum = last * dce - darg
                da_ref[rs, ks] = _dot_exact(tri_t, dcum, NN)
                carry = dec * ds
            ds_carry[h] = carry

    rev = lambda i: nstep - 1 - i
    return _call(
        body, grid=(nstep,),
        in_specs=[pl.BlockSpec((rows, qk), lambda i: (rev(i), 0)),
                  pl.BlockSpec((rows, qk), lambda i: (rev(i), 1)),
                  pl.BlockSpec((rows, vw), lambda i: (rev(i), 2 * qk // vw)),
                  pl.BlockSpec((rows, qk), lambda i: (rev(i), 0)),
                  pl.BlockSpec((rows, vw), lambda i: (rev(i), 0)),
                  pl.BlockSpec((cps, HEADS, dk, dv), lambda i: (rev(i), 0, 0, 0)),
                  pl.BlockSpec((1, HEADS, dk, dv), lambda i: (jnp.maximum(rev(i) * cps - 1, 0), 0, 0, 0))],
        out_specs=[pl.BlockSpec((rows, qk), lambda i: (rev(i), 0)),
                   pl.BlockSpec((rows, qk), lambda i: (rev(i), 0)),
                   pl.BlockSpec((rows, vw), lambda i: (rev(i), 0)),
                   pl.BlockSpec((rows, qk), lambda i: (rev(i), 0))],
        out_shape=[jax.ShapeDtypeStruct((S, qk), BF16), jax.ShapeDtypeStruct((S, qk), BF16),
                   jax.ShapeDtypeStruct((S, vw), BF16), jax.ShapeDtypeStruct((S, qk), F32)],
        scratch_shapes=[pltpu.VMEM((HEADS, dk, dv), F32)], args=(P, P, P, a, do, states, states), name=name,
        rides=rides, us=us)


def _loss_head(y, target, *, tr, name):
    S, D = y.shape
    tr = min(tr, S)

    def body(y_ref, t_ref, l_ref, dy_ref):
        @pl.when(pl.program_id(0) == 0)
        def _():
            l_ref[...] = jnp.zeros_like(l_ref)

        e = y_ref[...] - t_ref[...]
        dy_ref[...] = e / D
        l_ref[...] += 0.5 * jnp.sum(jnp.mean(e * e, axis=-1, keepdims=True), axis=0, keepdims=True)

    loss, dy = pl.pallas_call(
        body, grid=(S // tr,),
        in_specs=[pl.BlockSpec((tr, D), lambda i: (i, 0))] * 2,
        out_specs=[pl.BlockSpec((SUBLANES, LANES), lambda i: (0, 0)), pl.BlockSpec((tr, D), lambda i: (i, 0))],
        out_shape=[jax.ShapeDtypeStruct((SUBLANES, LANES), F32), jax.ShapeDtypeStruct((S, D), F32)],
        compiler_params=_params(("arbitrary",)), name=name)(y, target)
    return loss[0, 0], dy


ADAMW_BYTES_PER_US = 2.5e6


def _adamw(parts, w, m, v, *, name, rides=None, row0=None):
    G, R, C = w.shape
    n = parts[0].shape[0]
    row0 = [0] * G if row0 is None else row0
    unit = 2 * SUBLANES if parts[0].dtype == BF16 else SUBLANES
    tr = _pick(R, max(unit, (128 * 1024) // C), unit)
    nblk = R // tr
    assert all(r % tr == 0 for r in row0)
    bc1 = 1.0 - ADAM_B1 ** ADAM_STEP
    bc2 = 1.0 - ADAM_B2 ** ADAM_STEP

    def body(*refs):
        p_refs = refs[:G]
        w_ref, m_ref, v_ref, g_ref, d_ref, mo_ref, vo_ref = refs[G:]
        for q in range(G):
            @pl.when(pl.program_id(0) == q)
            def _(q=q):
                g = p_refs[q][0].astype(F32)
                for s in range(1, n):
                    g = g + p_refs[q][s].astype(F32)
                m2 = ADAM_B1 * m_ref[...] + (1.0 - ADAM_B1) * g
                v2 = ADAM_B2 * v_ref[...] + (1.0 - ADAM_B2) * (g * g)
                g_ref[...] = g
                mo_ref[...] = m2
                vo_ref[...] = v2
                d_ref[...] = -ADAM_LR * ((m2 / bc1) / (jnp.sqrt(v2 / bc2) + ADAM_EPS) + ADAM_WD * w_ref[...])

    def part_spec(q):
        first = row0[q] // tr
        return pl.BlockSpec((n, tr, C), lambda g, i: (0, first + jnp.where(g == q, i, jnp.where(g < q, 0, nblk - 1)), 0))

    blk = pl.BlockSpec((None, tr, C), lambda g, i: (g, i, 0))
    traffic = G * R * C * (n * parts[0].dtype.itemsize + 7 * 4)
    return _call(
        body, grid=(G, nblk), in_specs=[part_spec(q) for q in range(G)] + [blk, blk, blk],
        out_specs=[blk] * 4, out_shape=[jax.ShapeDtypeStruct((G, R, C), F32)] * 4,
        args=(*parts, w, m, v), name=name, rides=rides, us=traffic / ADAMW_BYTES_PER_US)


def _pair_add(blocks, received, core, *, name):
    _, R, C = blocks.shape
    tr = _pick(R, max(ROW_QUANTUM, (1024 * 1024) // C), ROW_QUANTUM)

    def body(core_ref, mine_ref, theirs_ref, o_ref):
        o_ref[...] = (mine_ref[...].astype(F32) + theirs_ref[...].astype(F32)).astype(o_ref.dtype)

    grid_spec = pltpu.PrefetchScalarGridSpec(
        num_scalar_prefetch=1, grid=(NDEV // 2, R // tr),
        in_specs=[pl.BlockSpec((None, tr, C), lambda q, i, core_ref: (2 * q + core_ref[0], i, 0)),
                  pl.BlockSpec((None, tr, C), lambda q, i, core_ref: (q, i, 0))],
        out_specs=pl.BlockSpec((None, tr, C), lambda q, i, core_ref: (q, i, 0)))
    return pl.pallas_call(body, grid_spec=grid_spec, out_shape=jax.ShapeDtypeStruct((NDEV // 2, R, C), BF16),
                          compiler_params=_params(("arbitrary", "arbitrary")), name=name)(
        jnp.reshape(core, (1,)).astype(jnp.int32), blocks, received)


def _adamw_nd(parts, w, m, v, *, name):
    shape = w.shape
    C = shape[-1]
    R = math.prod(shape[:-1])
    outs = _adamw([parts.reshape(parts.shape[0], R, C)], w.reshape(1, R, C), m.reshape(1, R, C), v.reshape(1, R, C),
                  name=name)
    return [o.reshape(shape) for o in outs]


def _all_gather(arrs, *, name):
    n = len(arrs)
    HBM = pl.BlockSpec(memory_space=pl.ANY)

    def body(*refs):
        ins, outs = refs[:n], refs[n:2 * n]
        send_sems, recv_sems, local_sems = refs[2 * n:]
        x, y, c = _position()
        me, sibling = (x, y, c), (x, y, 1 - c)
        chips = [(1 - x, y), (x, 1 - y), (1 - x, 1 - y)]

        def copy(t, k, block, to, src=None):
            rows = outs[t].at[_index(block)]
            return pltpu.make_async_remote_copy(
                src_ref=rows if src is None else src, dst_ref=rows,
                send_sem=send_sems.at[t, k], recv_sem=recv_sems.at[t, k],
                device_id=to, device_id_type=pl.DeviceIdType.MESH)

        started = []
        mine = []
        for t in range(n):
            cp = pltpu.make_async_copy(ins[t], outs[t].at[_index(me)], local_sems.at[t])
            cp.start()
            mine.append(cp)
            first = [copy(t, 0, me, sibling, src=ins[t])]
            first += [copy(t, 1 + j, me, (*chip, c), src=ins[t]) for j, chip in enumerate(chips)]
            for cp in first:
                cp.start()
            started += first
        for t in range(n):
            for j, chip in enumerate(chips):
                copy(t, 1 + j, (*chip, c), me).wait_recv()
                cp = copy(t, 4 + j, (*chip, c), sibling)
                cp.start()
                started.append(cp)
        for t in range(n):
            copy(t, 0, sibling, me).wait_recv()
            for j, chip in enumerate(chips):
                copy(t, 4 + j, (*chip, 1 - c), me).wait_recv()
        for cp in started:
            cp.wait_send()
        for cp in mine:
            cp.wait()

    return pl.pallas_call(
        body, in_specs=[HBM] * n, out_specs=[HBM] * n,
        out_shape=[jax.ShapeDtypeStruct((NDEV,) + a.shape, a.dtype) for a in arrs],
        scratch_shapes=[pltpu.SemaphoreType.DMA((n, 7)), pltpu.SemaphoreType.DMA((n, 7)),
                        pltpu.SemaphoreType.DMA((n,))],
        name=name)(*arrs)


class _Pack:
    def __init__(self, arrs):
        self.shapes = [a.shape for a in arrs]
        self.sizes = [math.prod(s) for s in self.shapes]
        total = sum(self.sizes)
        unit = SUBLANES * LANES
        self.padded = -(-total // unit) * unit
        flat = jnp.concatenate([a.reshape(-1).astype(F32) for a in arrs] + [jnp.zeros((self.padded - total,), F32)])
        self.packed = flat.reshape(self.padded // LANES, LANES)

    def unpack(self, gathered):
        flat = gathered.reshape(NDEV, self.padded)
        out, off = [], 0
        for shape, size in zip(self.shapes, self.sizes):
            out.append(flat[:, off:off + size].reshape((NDEV,) + tuple(shape)))
            off += size
        return out


def _cols_from_shards(g):
    g = jnp.moveaxis(g, 0, -2)
    return g.reshape(g.shape[:-2] + (g.shape[-2] * g.shape[-1],))


def _cols_to_shards(w):
    K, N = w.shape
    return jnp.moveaxis(w.reshape(K, NDEV, N // NDEV), 1, 0)


def kernel(x, c, w_ada, b_ada, norm_g, w_in, w_a2, b_a2, gla_norm_g, w_out_gla, conv_mix_w, w_out_conv, w_o, w_up, ffn_conv_w, w_down, loss_target, m_w_ada, m_b_ada, m_norm_g, m_w_in, m_w_a2, m_b_a2, m_gla_norm_g, m_w_out_gla, m_conv_mix_w, m_w_out_conv, m_w_o, m_w_up, m_ffn_conv_w, m_w_down, v_w_ada, v_b_ada, v_norm_g, v_w_in, v_w_a2, v_b_a2, v_gla_norm_g, v_w_out_gla, v_conv_mix_w, v_w_out_conv, v_w_o, v_w_up, v_ffn_conv_w, v_w_down):
    depth = w_ada.shape[0]
    S, D = x.shape[1], x.shape[2]
    QK = w_a2.shape[2] * NDEV
    LR = w_a2.shape[1]
    VW = w_out_gla.shape[1]
    CW = w_out_conv.shape[1]
    FF = w_down.shape[1] * NDEV
    NIN = w_in.shape[2] * NDEV
    NADA = w_ada.shape[2]
    LRP = -(-LR // LANES) * LANES
    off_lr = 2 * QK + 2 * VW
    NP = NIN - LR + LRP
    assert NIN == 2 * QK + 2 * VW + LR + 3 * CW + 2 * D
    xi, yi, ci = _position()
    me = _index((xi, yi, ci))
    xs = x.reshape(S, D)
    target = loss_target.reshape(S, D)
    TR = 256
    TRC = 512
    TC = _pick(CW, 512)
    TCF = _pick(FF, 512)

    small = _Pack([c, norm_g, w_a2, conv_mix_w, ffn_conv_w])
    big_names = ["w_in", "w_out_gla", "w_out_conv", "w_o", "w_up", "w_down"]
    big = dict(w_in=w_in, w_out_gla=w_out_gla, w_out_conv=w_out_conv, w_o=w_o, w_up=w_up, w_down=w_down)

    def matrix(key, g):
        if key == "w_up":
            return g
        if key in ("w_o", "w_down"):
            return g.reshape(g.shape[0] * g.shape[1], g.shape[2])
        if key != "w_in":
            return _cols_from_shards(g)
        pieces = [g[d][:, lo - d * nin_shard:hi - d * nin_shard] for _, d, lo, hi in sorted(in_pieces)]
        return jnp.concatenate(pieces + [jnp.zeros((D, LRP - LR), BF16)], axis=-1)

    nin_shard = NIN // NDEV
    in_pieces = []
    for g_lo, g_hi, to in ((0, off_lr, 0), (off_lr, off_lr + LR, NIN - LR), (off_lr + LR, NIN, off_lr)):
        for d in range(NDEV):
            lo, hi = max(g_lo, d * nin_shard), min(g_hi, (d + 1) * nin_shard)
            if lo < hi:
                in_pieces.append((to + lo - g_lo, d, lo, hi))

    def in_grad_blocks(gW_all):
        blocks = []
        for d in range(NDEV):
            mine = sorted((lo, to, hi) for to, dd, lo, hi in in_pieces if dd == d)
            blocks.append(jnp.concatenate([gW_all[:, to:to + hi - lo] for lo, to, hi in mine], axis=1))
        return jnp.stack(blocks)

    (small_g, w_in0_g) = _all_gather([small.packed, w_in[0].astype(BF16)], name="gather_first")
    c_all, norm_g_s, w_a2_s, conv_w_s, ffn_w_s = small.unpack(small_g)
    c_all = c_all.reshape(NDEV, D)
    norm_g_f = _cols_from_shards(norm_g_s)
    w_a2_f = _cols_from_shards(w_a2_s)
    conv_w_f = _cols_from_shards(conv_w_s)
    ffn_w_f = _cols_from_shards(ffn_w_s)
    w_a2_p = jnp.concatenate([w_a2_f, jnp.zeros((depth, LRP - LR, QK), F32)], axis=1)

    rides = _Rides()
    GATHER_PARAMS_PER_US = 80e3
    FORWARD_PARAMS_PER_US = 800e3
    EXCHANGE_PARAMS_PER_US = 45e3
    PAIR_PARAMS_PER_US = 500e3
    BACKWARD_ICI_SHARE = 0.8
    weight_streams = {}
    for l in range(depth):
        for key in big_names:
            if (l, key) == (0, "w_in"):
                continue
            shard = big[key][l].astype(BF16)
            rows, cols = shard.shape
            first = rides.add(_Stream("gather_a", shard, rows, cols, GATHER_PARAMS_PER_US / (NDEV * cols)))
            second = rides.add(_Stream("gather_b", None, rows, cols, FORWARD_PARAMS_PER_US / (NDEV * cols), after=first))
            weight_streams[l, key] = (first, second)
    W = {(0, "w_in"): matrix("w_in", w_in0_g)}

    def weight(l, key):
        if (l, key) not in W:
            first, second = weight_streams[l, key]
            rides.need(first, second)
            W[l, key] = matrix(key, first.buf)
        return W[l, key]

    MP = LANES
    c_pad = jnp.concatenate([c_all, jnp.zeros((MP - NDEV, D), F32)], axis=0)
    (silu_c,) = _rw_fwd(lambda t: (t * _sigmoid(t),), [_v(c_pad)], [], [(D, BF16)], tr=MP, name="silu_c")
    mod_cols = _mm(silu_c, w_ada, name="ada_fwd")[:NDEV].reshape(NDEV, depth, NADA)
    (mod_g,) = _all_gather([jnp.moveaxis(mod_cols, 1, 0)], name="gather_mod")
    mod_all = jnp.moveaxis(mod_g, 0, 2).reshape(depth, NDEV, NDEV * NADA)
    mod_me = lax.dynamic_slice_in_dim(mod_all, me, 1, axis=1).reshape(depth, NDEV * NADA)
    (mod_me,) = _rw_fwd(lambda a, b: (a + b,), [_v(mod_me), _v(b_ada)], [], [(NDEV * NADA, F32)],
                        tr=depth, name="mod_bias")

    col = {}
    off = 0
    for nm, wd in (("q", QK), ("k", QK), ("v", VW), ("r", VW), ("cb", CW), ("cc", CW), ("cx", CW),
                   ("ga", D), ("gb", D), ("lr", LRP)):
        col[nm] = off
        off += wd

    saved = []
    xin = xs
    for l in range(depth):
        sh1, sc1, g1, sh2, sc2, g2 = [mod_me[l:l + 1, i * D:(i + 1) * D] for i in range(6)]
        ng = [norm_g_f[l, i:i + 1] for i in range(4)]
        cw = [conv_w_f[l, i:i + 1] for i in range(CONV_K)]
        fw = [ffn_w_f[l, i:i + 1] for i in range(CONV_K)]
        gn = gla_norm_g[l:l + 1]
        ba2 = b_a2[l:l + 1]
        def proj(act, key, name, out_dtype=BF16):
            return _mm(act, weight(l, key), name=name, rides=rides, out_dtype=out_dtype)

        (h1,) = _rw_fwd(_f_normmod, [_v(xin)], [_v(ng[0]), _v(sc1), _v(sh1)], [(D, BF16)], tr=TR, name=f"normmod1_{l}",
                        rides=rides, us=US_NORMMOD)
        P = proj(h1, "w_in", f"in_proj{l}")
        (a,) = _rw_fwd(_f_loga, [_v(P, LRP, col["lr"])], [V(w_a2_p[l], QK, None), _v(ba2)], [(QK, F32)],
                       tr=TR, name=f"loga{l}")
        o, states = _gla_fwd(P, a, qk=QK, vw=VW, name=f"gla_fwd{l}", rides=rides, us=US_GLA)
        dvh = VW // HEADS
        (ya_in,) = _rw_fwd(_f_glapost, [_v(o, dvh), _v(P, dvh, col["r"])], [V(gn, dvh, None)], [(VW, BF16)],
                           tr=TR, ncol=HEADS, name=f"glapost{l}", rides=rides, us=US_GLAPOST)
        ya = proj(ya_in, "w_out_gla", f"out_gla{l}")
        (yb_in,) = _rw_fwd(_f_convmix, [_v(P, TC, col["cb"]), _v(P, TC, col["cc"]), _v(P, TC, col["cx"])],
                           [_v(w, TC) for w in cw], [(CW, BF16)], tr=TRC, ncol=CW // TC, halo=True, name=f"convmix{l}")
        yb = proj(yb_in, "w_out_conv", f"out_conv{l}")
        (mg,) = _rw_fwd(_f_merge, [_v(P, D, col["ga"]), _v(P, D, col["gb"]), _v(ya), _v(yb)], [], [(D, BF16)],
                        tr=TR, name=f"merge{l}", rides=rides, us=US_MERGE)
        y = proj(mg, "w_o", f"o_proj{l}", F32)
        (x1,) = _rw_fwd(_f_resid, [_v(xin), _v(y)], [_v(g1), _v(ng[1])], [(D, F32)], tr=TR, name=f"resid1_{l}",
                        rides=rides, us=US_RESID)
        (h2,) = _rw_fwd(_f_normmod, [_v(x1)], [_v(ng[2]), _v(sc2), _v(sh2)], [(D, BF16)], tr=TR, name=f"normmod2_{l}",
                        rides=rides, us=US_NORMMOD)
        U = proj(h2, "w_up", f"up_proj{l}")
        (ff,) = _rw_fwd(_f_ffn, [_v(U, TCF, 0), _v(U, TCF, FF)], [_v(w, TCF) for w in fw], [(FF, BF16)],
                        tr=TRC, ncol=FF // TCF, halo=True, name=f"ffn{l}", rides=rides, us=US_FFN)
        y2 = proj(ff, "w_down", f"down_proj{l}", F32)
        (x2,) = _rw_fwd(_f_resid, [_v(x1), _v(y2)], [_v(g2), _v(ng[3])], [(D, F32)], tr=TR, name=f"resid2_{l}",
                        rides=rides, us=US_RESID)
        saved.append(dict(xin=xin, h1=h1, P=P, a=a, o=o, states=states, ya_in=ya_in, ya=ya, yb_in=yb_in, yb=yb,
                          mg=mg, y=y, x1=x1, h2=h2, U=U, ff=ff, y2=y2, mods=(sh1, sc1, g1, sh2, sc2, g2),
                          ng=ng, cw=cw, fw=fw, gn=gn, ba2=ba2))
        xin = x2

    loss_local, dx = _loss_head(xin, target, tr=TR, name="loss_head")
    loss = lax.psum(loss_local, MESH_AXES)

    grad_streams = {}
    IN_WGRAD_ROWS = [(0, 3 * D // 4), (3 * D // 4, D // 4)]
    UP_WGRAD_ROWS = [(0, D // 2), (D // 2, D // 2)]

    rides.ici_share = BACKWARD_ICI_SHARE
    rides.min_ici_us = MIN_ICI_CARRIER_US

    def send(l, key, blocks):
        _, rows, cols = blocks.shape
        entry = {}
        tag = f"{key}{l}_{len(grad_streams.get((l, key), []))}"

        def swapped(pair):
            sums = _pair_add(blocks, pair.buf, ci, name=f"pair_add_{tag}")
            entry["chips"] = rides.add(_Stream("chips", sums, rows, cols, 2 * EXCHANGE_PARAMS_PER_US / (NDEV * cols)))

        entry["pair"] = rides.add(_Stream("pair", blocks, rows, cols, PAIR_PARAMS_PER_US / (NDEV * cols), on_done=swapped))
        grad_streams.setdefault((l, key), []).append(entry)

    def bmm(a, b, **kw):
        kw.setdefault("out_dtype", BF16)
        return _mm(a, b, rides=rides, **kw)

    gsmall = [None] * depth
    dmod = [None] * depth
    for l in reversed(range(depth)):
        s = saved[l]
        sh1, sc1, g1, sh2, sc2, g2 = s["mods"]
        ng, cw, fw = s["ng"], s["cw"], s["fw"]
        (dx1_a, dy2), (dg2, dng3) = _rw_bwd(
            _f_resid, [_v(s["x1"]), _v(s["y2"])], [_v(g2), _v(ng[3])], [_v(dx)],
            row_grads=[F32, BF16], par_grads=[True, True], tr=TR, name=f"resid2_bwd{l}", rides=rides, us=US_RESID_BWD)
        gW_down = bmm(s["ff"], dy2, ta=True, out_dtype=BF16, name=f"down_wgrad{l}")
        send(l, "w_down", gW_down.reshape(NDEV, FF // NDEV, D))
        dff = bmm(dy2, weight(l, "w_down"), tb=True, name=f"down_dgrad{l}")
        U = s["U"]
        (dgate, dup), dfw = _rw_bwd(
            _f_ffn, [_v(U, TCF, 0), _v(U, TCF, FF)], [_v(w, TCF) for w in fw], [_v(dff, TCF)],
            row_grads=[BF16, BF16], par_grads=[True] * 3, tr=TRC, ncol=FF // TCF, halo=True, name=f"ffn_bwd{l}",
            rides=rides, us=US_FFN_BWD)
        dU = jnp.concatenate([dgate, dup], axis=1)
        for part, rows in enumerate(UP_WGRAD_ROWS):
            gW_up = bmm(s["h2"], dU, ta=True, out_dtype=BF16, out_shards=True, m_rows=rows, name=f"up_wgrad{l}_{part}")
            send(l, "w_up", gW_up)
        dh2 = bmm(dU, weight(l, "w_up"), tb=True, name=f"up_dgrad{l}")
        (dx1,), (dng2, dsc2, dsh2) = _rw_bwd(
            _f_normmod, [_v(s["x1"])], [_v(ng[2]), _v(sc2), _v(sh2)], [_v(dh2)],
            row_grads=[F32], par_grads=[True] * 3, adds={0: dx1_a}, tr=TR, name=f"normmod2_bwd{l}",
            rides=rides, us=US_NORMMOD_BWD)
        (dxin_a, dy), (dg1, dng1) = _rw_bwd(
            _f_resid, [_v(s["xin"]), _v(s["y"])], [_v(g1), _v(ng[1])], [_v(dx1)],
            row_grads=[F32, BF16], par_grads=[True, True], tr=TR, name=f"resid1_bwd{l}", rides=rides, us=US_RESID_BWD)
        gW_o = bmm(s["mg"], dy, ta=True, out_dtype=BF16, name=f"o_wgrad{l}")
        send(l, "w_o", gW_o.reshape(NDEV, D // NDEV, D))
        dmg = bmm(dy, weight(l, "w_o"), tb=True, name=f"o_dgrad{l}")
        P = s["P"]
        (dga, dgb, dya, dyb), _ = _rw_bwd(
            _f_merge, [_v(P, D, col["ga"]), _v(P, D, col["gb"]), _v(s["ya"]), _v(s["yb"])], [], [_v(dmg)],
            row_grads=[BF16] * 4, par_grads=[], tr=TR, name=f"merge_bwd{l}", rides=rides, us=US_MERGE_BWD)
        gW_og = bmm(s["ya_in"], dya, ta=True, out_dtype=BF16, name=f"out_gla_wgrad{l}")
        send(l, "w_out_gla", _cols_to_shards(gW_og))
        dya_in = bmm(dya, weight(l, "w_out_gla"), tb=True, name=f"out_gla_dgrad{l}")
        gW_oc = bmm(s["yb_in"], dyb, ta=True, out_dtype=BF16, name=f"out_conv_wgrad{l}")
        send(l, "w_out_conv", _cols_to_shards(gW_oc))
        dyb_in = bmm(dyb, weight(l, "w_out_conv"), tb=True, name=f"out_conv_dgrad{l}")
        (dcb, dcc, dcx), dcw = _rw_bwd(
            _f_convmix, [_v(P, TC, col["cb"]), _v(P, TC, col["cc"]), _v(P, TC, col["cx"])],
            [_v(w, TC) for w in cw], [_v(dyb_in, TC)],
            row_grads=[BF16] * 3, par_grads=[True] * 3, tr=TRC, ncol=CW // TC, halo=True, name=f"convmix_bwd{l}")
        dvh = VW // HEADS
        (do, dr), (dgn,) = _rw_bwd(
            _f_glapost, [_v(s["o"], dvh), _v(P, dvh, col["r"])], [V(s["gn"], dvh, None)], [_v(dya_in, dvh)],
            row_grads=[BF16, BF16], par_grads=[True], tr=TR, ncol=HEADS, name=f"glapost_bwd{l}",
            rides=rides, us=US_GLAPOST_BWD)
        dq, dk, dv, da = _gla_bwd(P, s["a"], s["states"], do, qk=QK, vw=VW, name=f"gla_bwd{l}",
                                  rides=rides, us=US_GLA_BWD)
        (dlr,), (dwa2, dba2) = _rw_bwd(
            _f_loga, [_v(P, LRP, col["lr"])], [V(w_a2_p[l], QK, None), _v(s["ba2"])], [_v(da)],
            row_grads=[BF16], par_grads=[True, True], tr=TR, name=f"loga_bwd{l}")
        dP = jnp.concatenate([dq, dk, dv, dr, dcb, dcc, dcx, dga, dgb, dlr], axis=1)
        for part, rows in enumerate(IN_WGRAD_ROWS):
            gW_all = bmm(s["h1"], dP, ta=True, out_dtype=BF16, m_rows=rows, name=f"in_wgrad{l}_{part}")
            send(l, "w_in", in_grad_blocks(gW_all))
        dh1 = bmm(dP, weight(l, "w_in"), tb=True, name=f"in_dgrad{l}")
        (dx,), (dng0, dsc1, dsh1) = _rw_bwd(
            _f_normmod, [_v(s["xin"])], [_v(ng[0]), _v(sc1), _v(sh1)], [_v(dh1)],
            row_grads=[F32], par_grads=[True] * 3, adds={0: dxin_a}, tr=TR, name=f"normmod1_bwd{l}",
            rides=rides, us=US_NORMMOD_BWD)
        vec = lambda t: t.reshape(1, -1)
        gsmall[l] = dict(
            norm_g=jnp.concatenate([vec(dng0), vec(dng1), vec(dng2), vec(dng3)], axis=0),
            w_a2=dwa2[0, :LR], b_a2=dba2.reshape(QK), gla_norm_g=jnp.sum(dgn, axis=0).reshape(dvh),
            conv_mix_w=jnp.concatenate([vec(t) for t in dcw], axis=0),
            ffn_conv_w=jnp.concatenate([vec(t) for t in dfw], axis=0))
        dmod[l] = jnp.concatenate([vec(t) for t in (dsh1, dsc1, dg1, dsh2, dsc2, dg2)], axis=1).reshape(-1)
    grad_x = dx.reshape(x.shape)

    stack = lambda key, src: jnp.stack([src[l][key] for l in range(depth)])
    small_names = ["norm_g", "w_a2", "b_a2", "gla_norm_g", "conv_mix_w", "ffn_conv_w"]
    gpack = _Pack([jnp.stack(dmod)] + [stack(k, gsmall) for k in small_names])
    (gsm,) = _all_gather([gpack.packed], name="gather_small_grads")
    gs = gpack.unpack(gsm)
    dmod_all = gs[0]
    parts_small = dict(zip(small_names, gs[1:]))

    def my_cols(t, n):
        return lax.dynamic_slice_in_dim(t, me * n, n, axis=t.ndim - 1)

    gW_ada = []
    for l in range(depth):
        dcols = my_cols(dmod_all[:, l], NADA)
        dcols = jnp.concatenate([dcols, jnp.zeros((MP - NDEV, NADA), F32)], axis=0)
        gW_ada.append(bmm(silu_c, dcols, ta=True, out_dtype=F32, name=f"ada_wgrad{l}")[None])
    parts = {}
    parts["b_ada"] = dmod_all
    parts["norm_g"] = my_cols(parts_small["norm_g"], D // NDEV)
    parts["w_a2"] = my_cols(parts_small["w_a2"], QK // NDEV)
    parts["b_a2"] = parts_small["b_a2"]
    parts["gla_norm_g"] = parts_small["gla_norm_g"]
    parts["conv_mix_w"] = my_cols(parts_small["conv_mix_w"], CW // NDEV)
    parts["ffn_conv_w"] = my_cols(parts_small["ffn_conv_w"], FF // NDEV)

    weights = dict(w_ada=(w_ada, m_w_ada, v_w_ada), b_ada=(b_ada, m_b_ada, v_b_ada),
                   norm_g=(norm_g, m_norm_g, v_norm_g), w_in=(w_in, m_w_in, v_w_in), w_a2=(w_a2, m_w_a2, v_w_a2),
                   b_a2=(b_a2, m_b_a2, v_b_a2), gla_norm_g=(gla_norm_g, m_gla_norm_g, v_gla_norm_g),
                   w_out_gla=(w_out_gla, m_w_out_gla, v_w_out_gla), conv_mix_w=(conv_mix_w, m_conv_mix_w, v_conv_mix_w),
                   w_out_conv=(w_out_conv, m_w_out_conv, v_w_out_conv), w_o=(w_o, m_w_o, v_w_o),
                   w_up=(w_up, m_w_up, v_w_up), ffn_conv_w=(ffn_conv_w, m_ffn_conv_w, v_ffn_conv_w),
                   w_down=(w_down, m_w_down, v_w_down))
    done = {}
    for nm in ["w_ada", "w_down", "w_up", "w_o", "w_out_gla", "w_out_conv", "w_in"]:
        w, m, v = weights[nm]
        if nm == "w_ada":
            done[nm] = _adamw(gW_ada, w, m, v, name=f"adamw_{nm}")
            continue
        entries = [e for l in range(depth) for e in grad_streams[l, nm]]
        rides.need(*[e["pair"] for e in entries])
        streams = [e["chips"] for e in entries]
        rides.need(*streams)
        rows = math.gcd(*[s.R for s in streams])
        bufs, row0 = [], []
        for s in streams:
            for first in range(0, s.R, rows):
                bufs.append(s.buf)
                row0.append(first)
        split = lambda t: t.reshape(len(bufs), rows, t.shape[2])
        outs = _adamw(bufs, split(w), split(m), split(v), name=f"adamw_{nm}", row0=row0)
        done[nm] = [o.reshape(w.shape) for o in outs]
    for nm, (w, m, v) in weights.items():
        if nm not in done:
            done[nm] = _adamw_nd(parts[nm], w, m, v, name=f"adamw_{nm}")
    grads, deltas, new_m, new_v = zip(*[done[nm] for nm in weights])
    return (loss, grad_x, *grads, *deltas, *new_m, *new_v)
```

```python
import functools
import math
from typing import Any, NamedTuple

import jax
import jax.numpy as jnp
from jax import lax
from jax.experimental import pallas as pl
from jax.experimental.pallas import tpu as pltpu

F32 = jnp.float32
BF16 = jnp.bfloat16
MESH_AXES = ("x", "y", "c")
NDEV = 8
EPS = 1e-6
CHUNK = 64
HEADS = 4
GLA_TAU = 16.0
CONV_K = 3
ADAM_LR = 0.001
ADAM_B1 = 0.9
ADAM_B2 = 0.999
ADAM_EPS = 1e-08
ADAM_WD = 0.01
ADAM_STEP = 10

LANES = 128
SUBLANES = 8
HALO = 2 * SUBLANES
VMEM_LIMIT = 56 * 1024 * 1024


def _params(sem=None):
    return pltpu.CompilerParams(dimension_semantics=sem, vmem_limit_bytes=VMEM_LIMIT)


def _pick(n, cap, unit=LANES):
    best = None
    for d in range(unit, min(n, cap) + 1, unit):
        if n % d == 0:
            best = d
    return best if best is not None else n


def _position():
    return lax.axis_index("x"), lax.axis_index("y"), lax.axis_index("c")


def _index(p):
    return 4 * p[0] + 2 * p[1] + p[2]


def _peer(k):
    x, y, c = _position()
    return ((1 - x) if k & 4 else x, (1 - y) if k & 2 else y, (1 - c) if k & 1 else c)


ROW_QUANTUM = 2 * SUBLANES
MIN_ROWS = 128
MIN_ICI_CARRIER_US = 50.0


class _Stream:
    def __init__(self, mode, src, rows, cols, rate, after=None, on_done=None):
        self.mode, self.src, self.R, self.rate, self.after = mode, src, rows, rate, after
        self.shape = (NDEV if mode.startswith("gather") else NDEV // 2, rows, cols)
        self.holder = self if after is None else after
        self.buf = None
        self.done = 0
        self.on_done = on_done


class _Exchange:
    def __init__(self, items):
        self.items = items
        self.ins, self.src_pos, self.alias, self.holders, self.out_pos = [], [], {}, [], []
        for s, _, _ in items:
            if not any(s.holder is h for h in self.holders):
                self.holders.append(s.holder)
        for s, _, _ in items:
            self.out_pos.append([n for n, h in enumerate(self.holders) if h is s.holder][0])
            if s.mode == "gather_b":
                self.src_pos.append(None)
            else:
                self.src_pos.append(len(self.ins))
                self.ins.append(s.src)
        for n, h in enumerate(self.holders):
            if h.buf is not None:
                self.alias[len(self.ins)] = n
                self.ins.append(h.buf)
        self.out_shape = [jax.ShapeDtypeStruct(h.shape, BF16) for h in self.holders]
        n = len(items)
        self.scratch = [pltpu.SemaphoreType.DMA((n, NDEV)), pltpu.SemaphoreType.DMA((n, NDEV)),
                        pltpu.SemaphoreType.DMA((n,))]

    def deliver(self, outs):
        for h, o in zip(self.holders, outs):
            h.buf = o

    def _copies(self, ins, outs, sems):
        send_sems, recv_sems, local_sems = sems
        me = _index(_position())
        slot = lambda k: _index(_peer(k))
        local, sends, arrivals = [], [], []
        for n, (s, r0, rc) in enumerate(self.items):
            out, rows = outs[self.out_pos[n]], pl.ds(r0, rc)
            src = out if self.src_pos[n] is None else ins[self.src_pos[n]]
            if s.mode == "pair":
                other = 1 - _position()[2]
                moves = [(q, src.at[2 * q + other, rows], out.at[q, rows], _peer(1), out.at[q, rows])
                         for q in range(NDEV // 2)]
            elif s.mode == "chips":
                chip = lambda k: slot(k) // 2
                local.append(pltpu.make_async_copy(src.at[me // 2, rows], out.at[me // 2, rows], local_sems.at[n]))
                moves = [(k, src.at[chip(k), rows], out.at[me // 2, rows], _peer(k), out.at[chip(k), rows])
                         for k in (2, 4, 6)]
            elif s.mode == "gather_a":
                local.append(pltpu.make_async_copy(src.at[rows], out.at[me, rows], local_sems.at[n]))
                moves = [(k, src.at[rows], out.at[me, rows], _peer(k), out.at[slot(k), rows]) for k in (1, 2, 4, 6)]
            else:
                moves = [(k, src.at[slot(k), rows], out.at[slot(k), rows], _peer(1), out.at[slot(k ^ 1), rows])
                         for k in (2, 4, 6)]
            for k, src_ref, dst_ref, to, landing in moves:
                common = dict(send_sem=send_sems.at[n, k], recv_sem=recv_sems.at[n, k],
                              device_id=to, device_id_type=pl.DeviceIdType.MESH)
                sends.append(pltpu.make_async_remote_copy(src_ref=src_ref, dst_ref=dst_ref, **common))
                arrivals.append(pltpu.make_async_remote_copy(src_ref=landing, dst_ref=landing, **common))
        return local, sends, arrivals

    def start(self, ins, outs, sems):
        local, sends, _ = self._copies(ins, outs, sems)
        for cp in local + sends:
            cp.start()

    def wait(self, ins, outs, sems):
        local, sends, arrivals = self._copies(ins, outs, sems)
        for cp in arrivals:
            cp.wait_recv()
        for cp in sends:
            cp.wait_send()
        for cp in local:
            cp.wait()


class _Rides:
    def __init__(self):
        self.queue = []
        self.finished = []
        self.flushes = 0
        self.ici_share = 1.0
        self.min_ici_us = 0.0

    def add(self, stream):
        self.queue.append(stream)
        return stream

    def take(self, budget, only=None, ici=True):
        ready = {id(s): (s.after.done if s.after is not None else s.R) for s in self.queue}
        left = {"ici": budget * self.ici_share if ici else 0.0, "d2d": budget}
        items = []
        for s in list(self.queue):
            if only is not None and not any(s is t for t in only):
                continue
            link = "d2d" if s.mode in ("gather_b", "pair") else "ici"
            avail = ready[id(s)] - s.done
            if avail <= 0:
                continue
            rc = avail if budget == float("inf") else min(avail, int(left[link] * s.rate) // ROW_QUANTUM * ROW_QUANTUM)
            if rc < min(avail, MIN_ROWS):
                continue
            if avail - rc <= 2 * MIN_ROWS:
                rc = avail
            items.append((s, s.done, rc))
            s.done += rc
            left[link] -= rc / s.rate
            if s.done == s.R:
                self.queue.remove(s)
                self.finished.append(s)
        return items

    def fire(self):
        finished, self.finished = self.finished, []
        for s in finished:
            if s.on_done is not None:
                s.on_done(s)

    def need(self, *streams):
        while any(s.done < s.R for s in streams):
            items = self.take(float("inf"), only=streams)
            comm = _Exchange(items)
            hbm = pl.BlockSpec(memory_space=pl.ANY)
            nin, nout = len(comm.ins), len(comm.holders)

            def body(*refs, comm=comm, nin=nin, nout=nout):
                comm.start(refs[:nin], refs[nin:nin + nout], refs[nin + nout:])
                comm.wait(refs[:nin], refs[nin:nin + nout], refs[nin + nout:])

            outs = pl.pallas_call(body, in_specs=[hbm] * nin, out_specs=[hbm] * nout, out_shape=comm.out_shape,
                                  scratch_shapes=comm.scratch, input_output_aliases=dict(comm.alias),
                                  name=f"exchange_alone{self.flushes}")(*comm.ins)
            self.flushes += 1
            comm.deliver(outs)
            self.fire()


def _call(body, *, grid, in_specs, out_specs, out_shape, scratch_shapes=(), args, name, rides=None, us=0.0):
    sem = ("arbitrary",) * len(grid)
    items = rides.take(us, ici=us >= rides.min_ici_us) if rides is not None else []
    if not items:
        return pl.pallas_call(body, grid=grid, in_specs=in_specs, out_specs=out_specs, out_shape=out_shape,
                              scratch_shapes=list(scratch_shapes), compiler_params=_params(sem), name=name)(*args)
    comm = _Exchange(items)
    nin, nout, nscr, cin, cout = len(in_specs), len(out_specs), len(scratch_shapes), len(comm.ins), len(comm.holders)
    hbm = pl.BlockSpec(memory_space=pl.ANY)

    def both(*refs):
        ins, c_ins = refs[:nin], refs[nin:nin + cin]
        pos = nin + cin
        outs, c_outs = refs[pos:pos + nout], refs[pos + nout:pos + nout + cout]
        pos += nout + cout
        scr, sems = refs[pos:pos + nscr], refs[pos + nscr:]
        first = functools.reduce(lambda p, q: p & q, [pl.program_id(d) == 0 for d in range(len(grid))])
        last = functools.reduce(lambda p, q: p & q, [pl.program_id(d) == grid[d] - 1 for d in range(len(grid))])

        @pl.when(first)
        def _():
            comm.start(c_ins, c_outs, sems)

        body(*ins, *outs, *scr)

        @pl.when(last)
        def _():
            comm.wait(c_ins, c_outs, sems)

    res = pl.pallas_call(both, grid=grid, in_specs=list(in_specs) + [hbm] * cin, out_specs=list(out_specs) + [hbm] * cout,
                         out_shape=list(out_shape) + comm.out_shape, scratch_shapes=list(scratch_shapes) + comm.scratch,
                         input_output_aliases={nin + i: nout + o for i, o in comm.alias.items()},
                         compiler_params=_params(sem), name=name)(*args, *comm.ins)
    comm.deliver(res[nout:])
    rides.fire()
    return res[:nout]


US_GLA, US_GLAPOST, US_MERGE, US_FFN, US_RESID, US_NORMMOD = 100.0, 40.0, 45.0, 110.0, 40.0, 24.0
US_FFN_BWD, US_GLA_BWD, US_MERGE_BWD, US_RESID_BWD, US_NORMMOD_BWD, US_GLAPOST_BWD = 215.0, 118.0, 57.0, 57.0, 54.0, 52.0
MM_VMEM_BUDGET = 44 * 1024 * 1024
MM_FLOPS_PER_US = 780e6


def _mm(a, b, *, ta=False, tb=False, out_dtype=F32, name, rides=None, out_shards=False, m_rows=None):
    if ta:
        K, M = a.shape
    else:
        M, K = a.shape
    m_first, M = (0, M) if m_rows is None else m_rows
    sharded = b.ndim == 3
    if sharded:
        ns, n = b.shape[0], b.shape[2]
        Kb, N = (ns * n, b.shape[1]) if tb else (b.shape[1], ns * n)
    elif tb:
        N, Kb = b.shape
    else:
        Kb, N = b.shape
    assert K == Kb, (a.shape, b.shape, ta, tb)
    tm = _pick(M, 1024)
    tn = N // NDEV if out_shards else n if (sharded and not tb) else _pick(N, 1152)
    sa, sb, so = a.dtype.itemsize, b.dtype.itemsize, jnp.dtype(out_dtype).itemsize

    def vmem(tk):
        blocks = 2 * (tm * tk * sa + tk * tn * sb) + 2 * tm * tn * so + (tm * tn * 4 if tk < K else 0)
        return blocks + (tm * K * 2 if ta else 0)

    if sharded and tb:
        tk = n
    else:
        tk = K
        if vmem(K) > MM_VMEM_BUDGET:
            fits = [d for d in range(LANES, K, LANES) if K % d == 0 and vmem(d) <= MM_VMEM_BUDGET]
            tk = max(fits) if fits else _pick(K, 512)
    nk = K // tk
    dims = (((1,), (1 if tb else 0,)), ((), ()))

    def dot(a_ref, b_ref, at_ref):
        if ta:
            k = pl.program_id(2)

            @pl.when(pl.program_id(1) == 0)
            def _():
                at_ref[k] = a_ref[...].astype(BF16).T

            lhs = at_ref[k]
        else:
            lhs = a_ref[...].astype(BF16)
        return lax.dot_general(lhs, b_ref[...].astype(BF16), dims, preferred_element_type=F32)

    def body_one(a_ref, b_ref, o_ref, *at_ref):
        o_ref[...] = dot(a_ref, b_ref, *at_ref or (None,)).astype(o_ref.dtype)

    def body_acc(a_ref, b_ref, o_ref, acc_ref, *at_ref):
        k = pl.program_id(2)
        at = (at_ref or (None,))[0]

        @pl.when(k == 0)
        def _():
            acc_ref[...] = dot(a_ref, b_ref, at)

        @pl.when((k > 0) & (k < nk - 1))
        def _():
            acc_ref[...] += dot(a_ref, b_ref, at)

        @pl.when(k == nk - 1)
        def _():
            o_ref[...] = (acc_ref[...] + dot(a_ref, b_ref, at)).astype(o_ref.dtype)

    assert m_first % tm == 0
    i0 = m_first // tm
    a_spec = (pl.BlockSpec((tk, tm), lambda i, j, k: (k, i0 + i)) if ta
              else pl.BlockSpec((tm, tk), lambda i, j, k: (i0 + i, k)))
    if sharded:
        b_spec = (pl.BlockSpec((None, tn, tk), lambda i, j, k: (k, j, 0)) if tb
                  else pl.BlockSpec((None, tk, tn), lambda i, j, k: (j, k, 0)))
    else:
        b_spec = (pl.BlockSpec((tn, tk), lambda i, j, k: (j, k)) if tb
                  else pl.BlockSpec((tk, tn), lambda i, j, k: (k, j)))
    if out_shards:
        out_spec = pl.BlockSpec((None, tm, tn), lambda i, j, k: (j, i, 0))
        out_shape = jax.ShapeDtypeStruct((NDEV, M, tn), out_dtype)
    else:
        out_spec = pl.BlockSpec((tm, tn), lambda i, j, k: (i, j))
        out_shape = jax.ShapeDtypeStruct((M, N), out_dtype)
    (out,) = _call(
        body_one if nk == 1 else body_acc, grid=(M // tm, N // tn, nk), in_specs=[a_spec, b_spec],
        out_specs=[out_spec], out_shape=[out_shape],
        scratch_shapes=([] if nk == 1 else [pltpu.VMEM((tm, tn), F32)]) + ([pltpu.VMEM((nk, tm, tk), BF16)] if ta else []),
        args=(a, b), name=name,
        rides=rides, us=2.0 * M * N * K / MM_FLOPS_PER_US)
    return out


class V(NamedTuple):
    arr: Any
    width: int
    base: Any


def _v(arr, width=None, col0=0):
    width = arr.shape[-1] if width is None else width
    assert col0 % width == 0
    return V(arr, width, col0 // width)


def _row_specs(v, tr, nrow8, halo):
    main = pl.BlockSpec((tr, v.width), lambda j, i, b=v.base: (i, b + j))
    if not halo:
        return [main]
    per = tr // HALO
    prev = pl.BlockSpec((HALO, v.width), lambda j, i, b=v.base: (jnp.maximum(i * per - 1, 0), b + j))
    nxt = pl.BlockSpec((HALO, v.width), lambda j, i, b=v.base: (jnp.minimum((i + 1) * per, nrow8 - 1), b + j))
    return [prev, main, nxt]


def _par_spec(p):
    if p.base is None:
        return pl.BlockSpec(p.arr.shape, lambda j, i: (0,) * p.arr.ndim)
    return pl.BlockSpec((p.arr.shape[0], p.width), lambda j, i, b=p.base: (0, b + j))


def _load_rows(refs, i, nrow, halo):
    if not halo:
        return refs[0][...].astype(F32)
    prev, main, nxt = refs
    pv = jnp.where(i > 0, prev[...].astype(F32), 0.0)
    nv = jnp.where(i < nrow - 1, nxt[...].astype(F32), 0.0)
    return jnp.concatenate([pv, main[...].astype(F32), nv], axis=0)


def _center_mask(tr):
    row = lax.broadcasted_iota(jnp.int32, (tr + 2 * HALO, 1), 0)
    return ((row >= HALO) & (row < HALO + tr)).astype(F32)


def _rw_fwd(f, rows, params, outs, *, tr, ncol=1, halo=False, name, rides=None, us=0.0):
    S = rows[0].arr.shape[0]
    tr = min(tr, S)
    nrow = S // tr
    per = 3 if halo else 1
    nr, npar = len(rows), len(params)

    def body(*refs):
        i = pl.program_id(1)
        vals = [_load_rows(refs[per * r: per * (r + 1)], i, nrow, halo) for r in range(nr)]
        pvals = [refs[per * nr + p][...].astype(F32) for p in range(npar)]
        out_refs = refs[per * nr + npar:]
        res = f(*vals, *pvals, _center_mask(tr)) if halo else f(*vals, *pvals)
        for r, o in zip(res, out_refs):
            o[...] = (r[HALO:HALO + tr] if halo else r).astype(o.dtype)

    in_specs = []
    for v in rows:
        in_specs += _row_specs(v, tr, S // HALO, halo)
    in_specs += [_par_spec(p) for p in params]
    args = []
    for v in rows:
        args += [v.arr] * per
    args += [p.arr for p in params]
    return _call(
        body, grid=(ncol, nrow), in_specs=in_specs,
        out_specs=[pl.BlockSpec((tr, tot // ncol), lambda j, i: (i, j)) for tot, _ in outs],
        out_shape=[jax.ShapeDtypeStruct((S, tot), dt) for tot, dt in outs],
        args=args, name=name, rides=rides, us=us)


def _rw_bwd(f, rows, params, douts, *, row_grads, par_grads, adds=None, tr, ncol=1, halo=False, name,
            rides=None, us=0.0):
    S = rows[0].arr.shape[0]
    tr = min(tr, S)
    nrow = S // tr
    per = 3 if halo else 1
    adds = adds or {}
    nr, npar, nd = len(rows), len(params), len(douts)
    add_keys = sorted(adds)
    rg_idx = [k for k in range(nr) if row_grads[k]]
    pg_idx = [k for k in range(npar) if par_grads[k]]

    def body(*refs):
        i = pl.program_id(1)
        pos = 0
        vals = [_load_rows(refs[per * r: per * (r + 1)], i, nrow, halo) for r in range(nr)]
        pos = per * nr
        pvals = [refs[pos + p][...].astype(F32) for p in range(npar)]
        pos += npar
        dvals = [_load_rows(refs[pos + per * d: pos + per * (d + 1)], i, nrow, halo) for d in range(nd)]
        pos += per * nd
        add_vals = {k: refs[pos + n][...].astype(F32) for n, k in enumerate(add_keys)}
        pos += len(add_keys)
        rg_refs = refs[pos: pos + len(rg_idx)]
        pg_refs = refs[pos + len(rg_idx):]
        if halo:
            center = _center_mask(tr)
            fun = lambda *a: f(*a, center)
        else:
            fun = f
        _, vjp = jax.vjp(fun, *vals, *pvals)
        grads = vjp(tuple(dvals))
        for ref, k in zip(rg_refs, rg_idx):
            g = grads[k]
            g = g[HALO:HALO + tr] if halo else g
            if k in add_vals:
                g = g + add_vals[k]
            ref[...] = g.astype(ref.dtype)
        for ref, k in zip(pg_refs, pg_idx):
            @pl.when(i == 0)
            def _(ref=ref):
                ref[...] = jnp.zeros_like(ref)
            ref[...] += grads[nr + k]

    in_specs, args = [], []
    for v in list(rows):
        in_specs += _row_specs(v, tr, S // HALO, halo)
        args += [v.arr] * per
    in_specs += [_par_spec(p) for p in params]
    args += [p.arr for p in params]
    for v in douts:
        in_specs += _row_specs(v, tr, S // HALO, halo)
        args += [v.arr] * per
    for k in add_keys:
        in_specs += _row_specs(_v(adds[k], rows[k].width), tr, S // HALO, False)
        args += [adds[k]]
    out_specs, out_shape = [], []
    for k in rg_idx:
        w = rows[k].width
        out_specs.append(pl.BlockSpec((tr, w), lambda j, i: (i, j)))
        out_shape.append(jax.ShapeDtypeStruct((S, ncol * w), row_grads[k]))
    for k in pg_idx:
        p = params[k]
        blk = p.arr.shape if p.base is None else (p.arr.shape[0], p.width)
        out_specs.append(pl.BlockSpec((None,) + tuple(blk), lambda j, i, n=len(blk): (j,) + (0,) * n))
        out_shape.append(jax.ShapeDtypeStruct((ncol,) + tuple(blk), F32))
    res = _call(body, grid=(ncol, nrow), in_specs=in_specs, out_specs=out_specs, out_shape=out_shape, args=args, name=name,
                rides=rides, us=us)
    rg = [None] * nr
    for n, k in enumerate(rg_idx):
        rg[k] = res[n]
    pg = [None] * npar
    for n, k in enumerate(pg_idx):
        pg[k] = res[len(rg_idx) + n]
    return rg, pg


def _sigmoid(x):
    return 1.0 / (1.0 + jnp.exp(-x))


def _rms(x, g):
    return x * lax.rsqrt(jnp.mean(x * x, axis=-1, keepdims=True) + EPS) * g


def _f_normmod(x, g, sc, sh):
    return (_rms(x, g) * (1.0 + sc) + sh,)


def _f_resid(x, y, gate, g):
    return (x + gate * _rms(y, g),)


def _f_merge(ga, gb, ya, yb):
    return (_sigmoid(ga) * ya + _sigmoid(gb) * yb,)


def _f_glapost(o, r, g):
    return (_rms(o, g) * (r * _sigmoid(r)),)


def _roll_rows(x, k):
    return pltpu.roll(x, k % x.shape[0], 0)


@functools.partial(jax.custom_vjp, nondiff_argnums=(1,))
def _shift(x, k):
    return _roll_rows(x, k)


def _shift_fwd(x, k):
    return _roll_rows(x, k), None


def _shift_bwd(k, _, g):
    return (_roll_rows(g, -k),)


_shift.defvjp(_shift_fwd, _shift_bwd)


def _taps(w, center):
    return jnp.where(center > 0.0, w, lax.stop_gradient(w))


def _conv3(u, w0, w1, w2, center):
    return (_taps(w0, center) * _shift(u, 2) + _taps(w1, center) * _shift(u, 1)) + _taps(w2, center) * u


def _f_convmix(cb, cc, cx, w0, w1, w2, center):
    return (cb * _conv3(cc * cx, w0, w1, w2, center),)


def _gelu_tanh(x):
    return 0.5 * x * (1.0 + jnp.tanh(math.sqrt(2.0 / math.pi) * (x + 0.044715 * (x * x * x))))


def _f_ffn(gate, up, w0, w1, w2, center):
    return (_gelu_tanh(_conv3(gate, w0, w1, w2, center)) * up,)


@jax.custom_vjp
def _bdot(a, b):
    return jnp.dot(a.astype(BF16), b.astype(BF16), preferred_element_type=F32)


def _bdot_fwd(a, b):
    return _bdot(a, b), (a, b)


def _bdot_bwd(res, g):
    a, b = res
    gb = g.astype(BF16)
    da = lax.dot_general(gb, b.astype(BF16), (((1,), (1,)), ((), ())), preferred_element_type=F32)
    db = lax.dot_general(a.astype(BF16), gb, (((0,), (0,)), ((), ())), preferred_element_type=F32)
    return da, db


_bdot.defvjp(_bdot_fwd, _bdot_bwd)


@jax.custom_vjp
def _log_sigmoid(z):
    return jnp.minimum(z, 0.0) - jnp.log(1.0 + jnp.exp(-jnp.abs(z)))


def _log_sigmoid_fwd(z):
    return _log_sigmoid(z), z


def _log_sigmoid_bwd(z, g):
    return (g * _sigmoid(-z),)


_log_sigmoid.defvjp(_log_sigmoid_fwd, _log_sigmoid_bwd)


def _f_loga(lr, w, b):
    return (_log_sigmoid(_bdot(lr, w) + b) / GLA_TAU,)


def _split3(x):
    hi = x.astype(BF16)
    r1 = x - hi.astype(F32)
    mid = r1.astype(BF16)
    lo = (r1 - mid.astype(F32)).astype(BF16)
    return hi, mid, lo


def _dot_exact(a01, x, dims):
    hi, mid, lo = _split3(x)
    d = lambda p: lax.dot_general(a01, p, dims, preferred_element_type=F32)
    return (d(lo) + d(mid)) + d(hi)


def _dot_exact_lhs(x, a01, dims):
    hi, mid, lo = _split3(x)
    d = lambda p: lax.dot_general(p, a01, dims, preferred_element_type=F32)
    return (d(lo) + d(mid)) + d(hi)


GLA_CHUNKS_PER_STEP = 8

NN = (((1,), (0,)), ((), ()))
NT = (((1,), (1,)), ((), ()))
TN = (((0,), (0,)), ((), ()))


def _bf_dot(a, b, dims):
    return lax.dot_general(a.astype(BF16), b.astype(BF16), dims, preferred_element_type=F32)


def _chunk_decay(a, dv):
    L = a.shape[0]
    r = lax.broadcasted_iota(jnp.int32, (L, L), 0)
    c = lax.broadcasted_iota(jnp.int32, (L, L), 1)
    tri = (r >= c).astype(BF16)
    cum = _dot_exact(tri, a, NN)
    ce = cum[L - 1:L, :]
    dec = jnp.exp(_dot_exact_lhs(a, jnp.ones((L, dv), BF16), TN))
    return cum, ce, jnp.exp(ce - cum), dec


def _gla_fwd(P, a, *, qk, vw, name, rides=None, us=0.0):
    S = P.shape[0]
    nc = S // CHUNK
    cps = math.gcd(nc, GLA_CHUNKS_PER_STEP)
    rows = cps * CHUNK
    dk, dv = qk // HEADS, vw // HEADS
    scale = dk ** -0.5

    def body(q_ref, k_ref, v_ref, a_ref, o_ref, st_ref, state):
        @pl.when(pl.program_id(0) == 0)
        def _():
            state[...] = jnp.zeros_like(state)

        for h in range(HEADS):
            ks, vs = slice(h * dk, (h + 1) * dk), slice(h * dv, (h + 1) * dv)
            s = state[h]
            for c in range(cps):
                rs = slice(c * CHUNK, (c + 1) * CHUNK)
                _, _, E, dec = _chunk_decay(a_ref[rs, ks], dv)
                kd = k_ref[rs, ks] * E
                s = dec * s + _bf_dot(kd, v_ref[rs, vs], TN)
                st_ref[c, h] = s
                o_ref[rs, vs] = _bf_dot(q_ref[rs, ks] * scale, s, NN).astype(o_ref.dtype)
            state[h] = s

    return _call(
        body, grid=(nc // cps,),
        in_specs=[pl.BlockSpec((rows, qk), lambda i: (i, 0)),
                  pl.BlockSpec((rows, qk), lambda i: (i, 1)),
                  pl.BlockSpec((rows, vw), lambda i: (i, 2 * qk // vw)),
                  pl.BlockSpec((rows, qk), lambda i: (i, 0))],
        out_specs=[pl.BlockSpec((rows, vw), lambda i: (i, 0)),
                   pl.BlockSpec((cps, HEADS, dk, dv), lambda i: (i, 0, 0, 0))],
        out_shape=[jax.ShapeDtypeStruct((S, vw), BF16), jax.ShapeDtypeStruct((nc, HEADS, dk, dv), F32)],
        scratch_shapes=[pltpu.VMEM((HEADS, dk, dv), F32)], args=(P, P, P, a), name=name, rides=rides, us=us)


def _gla_bwd(P, a, states, do, *, qk, vw, name, rides=None, us=0.0):
    S = P.shape[0]
    nc = S // CHUNK
    cps = math.gcd(nc, GLA_CHUNKS_PER_STEP)
    rows = cps * CHUNK
    nstep = nc // cps
    dk, dv = qk // HEADS, vw // HEADS
    scale = dk ** -0.5

    def body(q_ref, k_ref, v_ref, a_ref, do_ref, sc_ref, sp_ref, dq_ref, dk_ref, dv_ref, da_ref, ds_carry):
        i = pl.program_id(0)

        @pl.when(i == 0)
        def _():
            ds_carry[...] = jnp.zeros_like(ds_carry)

        has_prev = (i < nstep - 1).astype(F32)
        L = CHUNK
        r = lax.broadcasted_iota(jnp.int32, (L, L), 0)
        c = lax.broadcasted_iota(jnp.int32, (L, L), 1)
        tri_t = (c >= r).astype(BF16)
        last = (lax.broadcasted_iota(jnp.int32, (L, 1), 0) == L - 1).astype(F32)
        for h in range(HEADS):
            ks, vs = slice(h * dk, (h + 1) * dk), slice(h * dv, (h + 1) * dv)
            carry = ds_carry[h]
            for n in reversed(range(cps)):
                rs = slice(n * CHUNK, (n + 1) * CHUNK)
                _, _, E, dec = _chunk_decay(a_ref[rs, ks], dv)
                kk, vv, dov = k_ref[rs, ks], v_ref[rs, vs], do_ref[rs, vs]
                kd = kk * E
                ds = carry + _bf_dot(q_ref[rs, ks] * scale, dov, TN)
                dq_ref[rs, ks] = (_bf_dot(dov, sc_ref[n, h], NT) * scale).astype(dq_ref.dtype)
                dkd = _bf_dot(vv, ds, NT)
                dv_ref[rs, vs] = _bf_dot(kd, ds, NN).astype(dv_ref.dtype)
                dk_ref[rs, ks] = (dkd * E).astype(dk_ref.dtype)
                darg = dkd * kd
                before = sc_ref[n - 1, h] if n > 0 else sp_ref[0, h] * has_prev
                w = ds * before * dec
                dce = _dot_exact(jnp.ones((SUBLANES, dv), BF16), w, NT)[0:1, :]
                dce = dce + jnp.sum(darg, axis=0, keepdims=True)
                dcum = last * dce - darg
                da_ref[rs, ks] = _dot_exact(tri_t, dcum, NN)
                carry = dec * ds
            ds_carry[h] = carry

    rev = lambda i: nstep - 1 - i
    return _call(
        body, grid=(nstep,),
        in_specs=[pl.BlockSpec((rows, qk), lambda i: (rev(i), 0)),
                  pl.BlockSpec((rows, qk), lambda i: (rev(i), 1)),
                  pl.BlockSpec((rows, vw), lambda i: (rev(i), 2 * qk // vw)),
                  pl.BlockSpec((rows, qk), lambda i: (rev(i), 0)),
                  pl.BlockSpec((rows, vw), lambda i: (rev(i), 0)),
                  pl.BlockSpec((cps, HEADS, dk, dv), lambda i: (rev(i), 0, 0, 0)),
                  pl.BlockSpec((1, HEADS, dk, dv), lambda i: (jnp.maximum(rev(i) * cps - 1, 0), 0, 0, 0))],
        out_specs=[pl.BlockSpec((rows, qk), lambda i: (rev(i), 0)),
                   pl.BlockSpec((rows, qk), lambda i: (rev(i), 0)),
                   pl.BlockSpec((rows, vw), lambda i: (rev(i), 0)),
                   pl.BlockSpec((rows, qk), lambda i: (rev(i), 0))],
        out_shape=[jax.ShapeDtypeStruct((S, qk), BF16), jax.ShapeDtypeStruct((S, qk), BF16),
                   jax.ShapeDtypeStruct((S, vw), BF16), jax.ShapeDtypeStruct((S, qk), F32)],
        scratch_shapes=[pltpu.VMEM((HEADS, dk, dv), F32)], args=(P, P, P, a, do, states, states), name=name,
        rides=rides, us=us)


def _loss_head(y, target, *, tr, name):
    S, D = y.shape
    tr = min(tr, S)

    def body(y_ref, t_ref, l_ref, dy_ref):
        @pl.when(pl.program_id(0) == 0)
        def _():
            l_ref[...] = jnp.zeros_like(l_ref)

        e = y_ref[...] - t_ref[...]
        dy_ref[...] = e / D
        l_ref[...] += 0.5 * jnp.sum(jnp.mean(e * e, axis=-1, keepdims=True), axis=0, keepdims=True)

    loss, dy = pl.pallas_call(
        body, grid=(S // tr,),
        in_specs=[pl.BlockSpec((tr, D), lambda i: (i, 0))] * 2,
        out_specs=[pl.BlockSpec((SUBLANES, LANES), lambda i: (0, 0)), pl.BlockSpec((tr, D), lambda i: (i, 0))],
        out_shape=[jax.ShapeDtypeStruct((SUBLANES, LANES), F32), jax.ShapeDtypeStruct((S, D), F32)],
        compiler_params=_params(("arbitrary",)), name=name)(y, target)
    return loss[0, 0], dy


ADAMW_BYTES_PER_US = 2.5e6


def _adamw(parts, w, m, v, *, name, rides=None, row0=None):
    G, R, C = w.shape
    n = parts[0].shape[0]
    row0 = [0] * G if row0 is None else row0
    unit = 2 * SUBLANES if parts[0].dtype == BF16 else SUBLANES
    tr = _pick(R, max(unit, (256 * 1024) // C), unit)
    nblk = R // tr
    assert all(r % tr == 0 for r in row0)
    bc1 = 1.0 - ADAM_B1 ** ADAM_STEP
    bc2 = 1.0 - ADAM_B2 ** ADAM_STEP

    def body(*refs):
        p_refs = refs[:G]
        w_ref, m_ref, v_ref, g_ref, d_ref, mo_ref, vo_ref = refs[G:]
        for q in range(G):
            @pl.when(pl.program_id(0) == q)
            def _(q=q):
                g = p_refs[q][0].astype(F32)
                for s in range(1, n):
                    g = g + p_refs[q][s].astype(F32)
                m2 = ADAM_B1 * m_ref[...] + (1.0 - ADAM_B1) * g
                v2 = ADAM_B2 * v_ref[...] + (1.0 - ADAM_B2) * (g * g)
                g_ref[...] = g
                mo_ref[...] = m2
                vo_ref[...] = v2
                d_ref[...] = -ADAM_LR * ((m2 / bc1) / (jnp.sqrt(v2 / bc2) + ADAM_EPS) + ADAM_WD * w_ref[...])

    def part_spec(q):
        first = row0[q] // tr
        return pl.BlockSpec((n, tr, C), lambda g, i: (0, first + jnp.where(g == q, i, jnp.where(g < q, 0, nblk - 1)), 0))

    blk = pl.BlockSpec((None, tr, C), lambda g, i: (g, i, 0))
    traffic = G * R * C * (n * parts[0].dtype.itemsize + 7 * 4)
    return _call(
        body, grid=(G, nblk), in_specs=[part_spec(q) for q in range(G)] + [blk, blk, blk],
        out_specs=[blk] * 4, out_shape=[jax.ShapeDtypeStruct((G, R, C), F32)] * 4,
        args=(*parts, w, m, v), name=name, rides=rides, us=traffic / ADAMW_BYTES_PER_US)


def _pair_add(blocks, received, core, *, name):
    _, R, C = blocks.shape
    tr = _pick(R, max(ROW_QUANTUM, (1024 * 1024) // C), ROW_QUANTUM)

    def body(core_ref, mine_ref, theirs_ref, o_ref):
        o_ref[...] = (mine_ref[...].astype(F32) + theirs_ref[...].astype(F32)).astype(o_ref.dtype)

    grid_spec = pltpu.PrefetchScalarGridSpec(
        num_scalar_prefetch=1, grid=(NDEV // 2, R // tr),
        in_specs=[pl.BlockSpec((None, tr, C), lambda q, i, core_ref: (2 * q + core_ref[0], i, 0)),
                  pl.BlockSpec((None, tr, C), lambda q, i, core_ref: (q, i, 0))],
        out_specs=pl.BlockSpec((None, tr, C), lambda q, i, core_ref: (q, i, 0)))
    return pl.pallas_call(body, grid_spec=grid_spec, out_shape=jax.ShapeDtypeStruct((NDEV // 2, R, C), BF16),
                          compiler_params=_params(("arbitrary", "arbitrary")), name=name)(
        jnp.reshape(core, (1,)).astype(jnp.int32), blocks, received)


def _adamw_nd(parts, w, m, v, *, name):
    shape = w.shape
    C = shape[-1]
    R = math.prod(shape[:-1])
    outs = _adamw([parts.reshape(parts.shape[0], R, C)], w.reshape(1, R, C), m.reshape(1, R, C), v.reshape(1, R, C),
                  name=name)
    return [o.reshape(shape) for o in outs]


def _all_gather(arrs, *, name):
    n = len(arrs)
    HBM = pl.BlockSpec(memory_space=pl.ANY)

    def body(*refs):
        ins, outs = refs[:n], refs[n:2 * n]
        send_sems, recv_sems, local_sems = refs[2 * n:]
        x, y, c = _position()
        me, sibling = (x, y, c), (x, y, 1 - c)
        chips = [(1 - x, y), (x, 1 - y), (1 - x, 1 - y)]

        def copy(t, k, block, to, src=None):
            rows = outs[t].at[_index(block)]
            return pltpu.make_async_remote_copy(
                src_ref=rows if src is None else src, dst_ref=rows,
                send_sem=send_sems.at[t, k], recv_sem=recv_sems.at[t, k],
                device_id=to, device_id_type=pl.DeviceIdType.MESH)

        started = []
        mine = []
        for t in range(n):
            cp = pltpu.make_async_copy(ins[t], outs[t].at[_index(me)], local_sems.at[t])
            cp.start()
            mine.append(cp)
            first = [copy(t, 0, me, sibling, src=ins[t])]
            first += [copy(t, 1 + j, me, (*chip, c), src=ins[t]) for j, chip in enumerate(chips)]
            for cp in first:
                cp.start()
            started += first
        for t in range(n):
            for j, chip in enumerate(chips):
                copy(t, 1 + j, (*chip, c), me).wait_recv()
                cp = copy(t, 4 + j, (*chip, c), sibling)
                cp.start()
                started.append(cp)
        for t in range(n):
            copy(t, 0, sibling, me).wait_recv()
            for j, chip in enumerate(chips):
                copy(t, 4 + j, (*chip, 1 - c), me).wait_recv()
        for cp in started:
            cp.wait_send()
        for cp in mine:
            cp.wait()

    return pl.pallas_call(
        body, in_specs=[HBM] * n, out_specs=[HBM] * n,
        out_shape=[jax.ShapeDtypeStruct((NDEV,) + a.shape, a.dtype) for a in arrs],
        scratch_shapes=[pltpu.SemaphoreType.DMA((n, 7)), pltpu.SemaphoreType.DMA((n, 7)),
                        pltpu.SemaphoreType.DMA((n,))],
        name=name)(*arrs)


class _Pack:
    def __init__(self, arrs):
        self.shapes = [a.shape for a in arrs]
        self.sizes = [math.prod(s) for s in self.shapes]
        total = sum(self.sizes)
        unit = SUBLANES * LANES
        self.padded = -(-total // unit) * unit
        flat = jnp.concatenate([a.reshape(-1).astype(F32) for a in arrs] + [jnp.zeros((self.padded - total,), F32)])
        self.packed = flat.reshape(self.padded // LANES, LANES)

    def unpack(self, gathered):
        flat = gathered.reshape(NDEV, self.padded)
        out, off = [], 0
        for shape, size in zip(self.shapes, self.sizes):
            out.append(flat[:, off:off + size].reshape((NDEV,) + tuple(shape)))
            off += size
        return out


def _cols_from_shards(g):
    g = jnp.moveaxis(g, 0, -2)
    return g.reshape(g.shape[:-2] + (g.shape[-2] * g.shape[-1],))


def _cols_to_shards(w):
    K, N = w.shape
    return jnp.moveaxis(w.reshape(K, NDEV, N // NDEV), 1, 0)


def kernel(x, c, w_ada, b_ada, norm_g, w_in, w_a2, b_a2, gla_norm_g, w_out_gla, conv_mix_w, w_out_conv, w_o, w_up, ffn_conv_w, w_down, loss_target, m_w_ada, m_b_ada, m_norm_g, m_w_in, m_w_a2, m_b_a2, m_gla_norm_g, m_w_out_gla, m_conv_mix_w, m_w_out_conv, m_w_o, m_w_up, m_ffn_conv_w, m_w_down, v_w_ada, v_b_ada, v_norm_g, v_w_in, v_w_a2, v_b_a2, v_gla_norm_g, v_w_out_gla, v_conv_mix_w, v_w_out_conv, v_w_o, v_w_up, v_ffn_conv_w, v_w_down):
    depth = w_ada.shape[0]
    S, D = x.shape[1], x.shape[2]
    QK = w_a2.shape[2] * NDEV
    LR = w_a2.shape[1]
    VW = w_out_gla.shape[1]
    CW = w_out_conv.shape[1]
    FF = w_down.shape[1] * NDEV
    NIN = w_in.shape[2] * NDEV
    NADA = w_ada.shape[2]
    LRP = -(-LR // LANES) * LANES
    off_lr = 2 * QK + 2 * VW
    NP = NIN - LR + LRP
    assert NIN == 2 * QK + 2 * VW + LR + 3 * CW + 2 * D
    xi, yi, ci = _position()
    me = _index((xi, yi, ci))
    xs = x.reshape(S, D)
    target = loss_target.reshape(S, D)
    TR = 256
    TRC = 512
    TC = _pick(CW, 512)
    TCF = _pick(FF, 512)

    small = _Pack([c, norm_g, w_a2, conv_mix_w, ffn_conv_w])
    big_names = ["w_in", "w_out_gla", "w_out_conv", "w_o", "w_up", "w_down"]
    big = dict(w_in=w_in, w_out_gla=w_out_gla, w_out_conv=w_out_conv, w_o=w_o, w_up=w_up, w_down=w_down)

    def matrix(key, g):
        if key == "w_up":
            return g
        if key in ("w_o", "w_down"):
            return g.reshape(g.shape[0] * g.shape[1], g.shape[2])
        if key != "w_in":
            return _cols_from_shards(g)
        pieces = [g[d][:, lo - d * nin_shard:hi - d * nin_shard] for _, d, lo, hi in sorted(in_pieces)]
        return jnp.concatenate(pieces + [jnp.zeros((D, LRP - LR), BF16)], axis=-1)

    nin_shard = NIN // NDEV
    in_pieces = []
    for g_lo, g_hi, to in ((0, off_lr, 0), (off_lr, off_lr + LR, NIN - LR), (off_lr + LR, NIN, off_lr)):
        for d in range(NDEV):
            lo, hi = max(g_lo, d * nin_shard), min(g_hi, (d + 1) * nin_shard)
            if lo < hi:
                in_pieces.append((to + lo - g_lo, d, lo, hi))

    def in_grad_blocks(gW_all):
        blocks = []
        for d in range(NDEV):
            mine = sorted((lo, to, hi) for to, dd, lo, hi in in_pieces if dd == d)
            blocks.append(jnp.concatenate([gW_all[:, to:to + hi - lo] for lo, to, hi in mine], axis=1))
        return jnp.stack(blocks)

    (small_g, w_in0_g) = _all_gather([small.packed, w_in[0].astype(BF16)], name="gather_first")
    c_all, norm_g_s, w_a2_s, conv_w_s, ffn_w_s = small.unpack(small_g)
    c_all = c_all.reshape(NDEV, D)
    norm_g_f = _cols_from_shards(norm_g_s)
    w_a2_f = _cols_from_shards(w_a2_s)
    conv_w_f = _cols_from_shards(conv_w_s)
    ffn_w_f = _cols_from_shards(ffn_w_s)
    w_a2_p = jnp.concatenate([w_a2_f, jnp.zeros((depth, LRP - LR, QK), F32)], axis=1)

    rides = _Rides()
    GATHER_PARAMS_PER_US = 80e3
    FORWARD_PARAMS_PER_US = 800e3
    EXCHANGE_PARAMS_PER_US = 45e3
    PAIR_PARAMS_PER_US = 500e3
    BACKWARD_ICI_SHARE = 0.8
    weight_streams = {}
    for l in range(depth):
        for key in big_names:
            if (l, key) == (0, "w_in"):
                continue
            shard = big[key][l].astype(BF16)
            rows, cols = shard.shape
            first = rides.add(_Stream("gather_a", shard, rows, cols, GATHER_PARAMS_PER_US / (NDEV * cols)))
            second = rides.add(_Stream("gather_b", None, rows, cols, FORWARD_PARAMS_PER_US / (NDEV * cols), after=first))
            weight_streams[l, key] = (first, second)
    W = {(0, "w_in"): matrix("w_in", w_in0_g)}

    def weight(l, key):
        if (l, key) not in W:
            first, second = weight_streams[l, key]
            rides.need(first, second)
            W[l, key] = matrix(key, first.buf)
        return W[l, key]

    MP = LANES
    c_pad = jnp.concatenate([c_all, jnp.zeros((MP - NDEV, D), F32)], axis=0)
    (silu_c,) = _rw_fwd(lambda t: (t * _sigmoid(t),), [_v(c_pad)], [], [(D, BF16)], tr=MP, name="silu_c")
    mod_cols = _mm(silu_c, w_ada, name="ada_fwd")[:NDEV].reshape(NDEV, depth, NADA)
    (mod_g,) = _all_gather([jnp.moveaxis(mod_cols, 1, 0)], name="gather_mod")
    mod_all = jnp.moveaxis(mod_g, 0, 2).reshape(depth, NDEV, NDEV * NADA)
    mod_me = lax.dynamic_slice_in_dim(mod_all, me, 1, axis=1).reshape(depth, NDEV * NADA)
    (mod_me,) = _rw_fwd(lambda a, b: (a + b,), [_v(mod_me), _v(b_ada)], [], [(NDEV * NADA, F32)],
                        tr=depth, name="mod_bias")

    col = {}
    off = 0
    for nm, wd in (("q", QK), ("k", QK), ("v", VW), ("r", VW), ("cb", CW), ("cc", CW), ("cx", CW),
                   ("ga", D), ("gb", D), ("lr", LRP)):
        col[nm] = off
        off += wd

    saved = []
    xin = xs
    for l in range(depth):
        sh1, sc1, g1, sh2, sc2, g2 = [mod_me[l:l + 1, i * D:(i + 1) * D] for i in range(6)]
        ng = [norm_g_f[l, i:i + 1] for i in range(4)]
        cw = [conv_w_f[l, i:i + 1] for i in range(CONV_K)]
        fw = [ffn_w_f[l, i:i + 1] for i in range(CONV_K)]
        gn = gla_norm_g[l:l + 1]
        ba2 = b_a2[l:l + 1]
        def proj(act, key, name, out_dtype=BF16):
            return _mm(act, weight(l, key), name=name, rides=rides, out_dtype=out_dtype)

        (h1,) = _rw_fwd(_f_normmod, [_v(xin)], [_v(ng[0]), _v(sc1), _v(sh1)], [(D, BF16)], tr=TR, name=f"normmod1_{l}",
                        rides=rides, us=US_NORMMOD)
        P = proj(h1, "w_in", f"in_proj{l}")
        (a,) = _rw_fwd(_f_loga, [_v(P, LRP, col["lr"])], [V(w_a2_p[l], QK, None), _v(ba2)], [(QK, F32)],
                       tr=TR, name=f"loga{l}")
        o, states = _gla_fwd(P, a, qk=QK, vw=VW, name=f"gla_fwd{l}", rides=rides, us=US_GLA)
        dvh = VW // HEADS
        (ya_in,) = _rw_fwd(_f_glapost, [_v(o, dvh), _v(P, dvh, col["r"])], [V(gn, dvh, None)], [(VW, BF16)],
                           tr=TR, ncol=HEADS, name=f"glapost{l}", rides=rides, us=US_GLAPOST)
        ya = proj(ya_in, "w_out_gla", f"out_gla{l}")
        (yb_in,) = _rw_fwd(_f_convmix, [_v(P, TC, col["cb"]), _v(P, TC, col["cc"]), _v(P, TC, col["cx"])],
                           [_v(w, TC) for w in cw], [(CW, BF16)], tr=TRC, ncol=CW // TC, halo=True, name=f"convmix{l}")
        yb = proj(yb_in, "w_out_conv", f"out_conv{l}")
        (mg,) = _rw_fwd(_f_merge, [_v(P, D, col["ga"]), _v(P, D, col["gb"]), _v(ya), _v(yb)], [], [(D, BF16)],
                        tr=TR, name=f"merge{l}", rides=rides, us=US_MERGE)
        y = proj(mg, "w_o", f"o_proj{l}", F32)
        (x1,) = _rw_fwd(_f_resid, [_v(xin), _v(y)], [_v(g1), _v(ng[1])], [(D, F32)], tr=TR, name=f"resid1_{l}",
                        rides=rides, us=US_RESID)
        (h2,) = _rw_fwd(_f_normmod, [_v(x1)], [_v(ng[2]), _v(sc2), _v(sh2)], [(D, BF16)], tr=TR, name=f"normmod2_{l}",
                        rides=rides, us=US_NORMMOD)
        U = proj(h2, "w_up", f"up_proj{l}")
        (ff,) = _rw_fwd(_f_ffn, [_v(U, TCF, 0), _v(U, TCF, FF)], [_v(w, TCF) for w in fw], [(FF, BF16)],
                        tr=TRC, ncol=FF // TCF, halo=True, name=f"ffn{l}", rides=rides, us=US_FFN)
        y2 = proj(ff, "w_down", f"down_proj{l}", F32)
        (x2,) = _rw_fwd(_f_resid, [_v(x1), _v(y2)], [_v(g2), _v(ng[3])], [(D, F32)], tr=TR, name=f"resid2_{l}",
                        rides=rides, us=US_RESID)
        saved.append(dict(xin=xin, h1=h1, P=P, a=a, o=o, states=states, ya_in=ya_in, ya=ya, yb_in=yb_in, yb=yb,
                          mg=mg, y=y, x1=x1, h2=h2, U=U, ff=ff, y2=y2, mods=(sh1, sc1, g1, sh2, sc2, g2),
                          ng=ng, cw=cw, fw=fw, gn=gn, ba2=ba2))
        xin = x2

    loss_local, dx = _loss_head(xin, target, tr=TR, name="loss_head")
    loss = lax.psum(loss_local, MESH_AXES)

    grad_streams = {}
    IN_WGRAD_ROWS = [(0, 3 * D // 4), (3 * D // 4, D // 4)]
    UP_WGRAD_ROWS = [(0, D // 2), (D // 2, D // 2)]

    rides.ici_share = BACKWARD_ICI_SHARE
    rides.min_ici_us = MIN_ICI_CARRIER_US

    def send(l, key, blocks):
        _, rows, cols = blocks.shape
        entry = {}
        tag = f"{key}{l}_{len(grad_streams.get((l, key), []))}"

        def swapped(pair):
            sums = _pair_add(blocks, pair.buf, ci, name=f"pair_add_{tag}")
            entry["chips"] = rides.add(_Stream("chips", sums, rows, cols, 2 * EXCHANGE_PARAMS_PER_US / (NDEV * cols)))

        entry["pair"] = rides.add(_Stream("pair", blocks, rows, cols, PAIR_PARAMS_PER_US / (NDEV * cols), on_done=swapped))
        grad_streams.setdefault((l, key), []).append(entry)

    def bmm(a, b, **kw):
        kw.setdefault("out_dtype", BF16)
        return _mm(a, b, rides=rides, **kw)

    gsmall = [None] * depth
    dmod = [None] * depth
    for l in reversed(range(depth)):
        s = saved[l]
        sh1, sc1, g1, sh2, sc2, g2 = s["mods"]
        ng, cw, fw = s["ng"], s["cw"], s["fw"]
        (dx1_a, dy2), (dg2, dng3) = _rw_bwd(
            _f_resid, [_v(s["x1"]), _v(s["y2"])], [_v(g2), _v(ng[3])], [_v(dx)],
            row_grads=[F32, BF16], par_grads=[True, True], tr=TR, name=f"resid2_bwd{l}", rides=rides, us=US_RESID_BWD)
        gW_down = bmm(s["ff"], dy2, ta=True, out_dtype=BF16, name=f"down_wgrad{l}")
        send(l, "w_down", gW_down.reshape(NDEV, FF // NDEV, D))
        dff = bmm(dy2, weight(l, "w_down"), tb=True, name=f"down_dgrad{l}")
        U = s["U"]
        (dgate, dup), dfw = _rw_bwd(
            _f_ffn, [_v(U, TCF, 0), _v(U, TCF, FF)], [_v(w, TCF) for w in fw], [_v(dff, TCF)],
            row_grads=[BF16, BF16], par_grads=[True] * 3, tr=TRC, ncol=FF // TCF, halo=True, name=f"ffn_bwd{l}",
            rides=rides, us=US_FFN_BWD)
        dU = jnp.concatenate([dgate, dup], axis=1)
        for part, rows in enumerate(UP_WGRAD_ROWS):
            gW_up = bmm(s["h2"], dU, ta=True, out_dtype=BF16, out_shards=True, m_rows=rows, name=f"up_wgrad{l}_{part}")
            send(l, "w_up", gW_up)
        dh2 = bmm(dU, weight(l, "w_up"), tb=True, name=f"up_dgrad{l}")
        (dx1,), (dng2, dsc2, dsh2) = _rw_bwd(
            _f_normmod, [_v(s["x1"])], [_v(ng[2]), _v(sc2), _v(sh2)], [_v(dh2)],
            row_grads=[F32], par_grads=[True] * 3, adds={0: dx1_a}, tr=TR, name=f"normmod2_bwd{l}",
            rides=rides, us=US_NORMMOD_BWD)
        (dxin_a, dy), (dg1, dng1) = _rw_bwd(
            _f_resid, [_v(s["xin"]), _v(s["y"])], [_v(g1), _v(ng[1])], [_v(dx1)],
            row_grads=[F32, BF16], par_grads=[True, True], tr=TR, name=f"resid1_bwd{l}", rides=rides, us=US_RESID_BWD)
        gW_o = bmm(s["mg"], dy, ta=True, out_dtype=BF16, name=f"o_wgrad{l}")
        send(l, "w_o", gW_o.reshape(NDEV, D // NDEV, D))
        dmg = bmm(dy, weight(l, "w_o"), tb=True, name=f"o_dgrad{l}")
        P = s["P"]
        (dga, dgb, dya, dyb), _ = _rw_bwd(
            _f_merge, [_v(P, D, col["ga"]), _v(P, D, col["gb"]), _v(s["ya"]), _v(s["yb"])], [], [_v(dmg)],
            row_grads=[BF16] * 4, par_grads=[], tr=TR, name=f"merge_bwd{l}", rides=rides, us=US_MERGE_BWD)
        gW_og = bmm(s["ya_in"], dya, ta=True, out_dtype=BF16, name=f"out_gla_wgrad{l}")
        send(l, "w_out_gla", _cols_to_shards(gW_og))
        dya_in = bmm(dya, weight(l, "w_out_gla"), tb=True, name=f"out_gla_dgrad{l}")
        gW_oc = bmm(s["yb_in"], dyb, ta=True, out_dtype=BF16, name=f"out_conv_wgrad{l}")
        send(l, "w_out_conv", _cols_to_shards(gW_oc))
        dyb_in = bmm(dyb, weight(l, "w_out_conv"), tb=True, name=f"out_conv_dgrad{l}")
        (dcb, dcc, dcx), dcw = _rw_bwd(
            _f_convmix, [_v(P, TC, col["cb"]), _v(P, TC, col["cc"]), _v(P, TC, col["cx"])],
            [_v(w, TC) for w in cw], [_v(dyb_in, TC)],
            row_grads=[BF16] * 3, par_grads=[True] * 3, tr=TRC, ncol=CW // TC, halo=True, name=f"convmix_bwd{l}")
        dvh = VW // HEADS
        (do, dr), (dgn,) = _rw_bwd(
            _f_glapost, [_v(s["o"], dvh), _v(P, dvh, col["r"])], [V(s["gn"], dvh, None)], [_v(dya_in, dvh)],
            row_grads=[BF16, BF16], par_grads=[True], tr=TR, ncol=HEADS, name=f"glapost_bwd{l}",
            rides=rides, us=US_GLAPOST_BWD)
        dq, dk, dv, da = _gla_bwd(P, s["a"], s["states"], do, qk=QK, vw=VW, name=f"gla_bwd{l}",
                                  rides=rides, us=US_GLA_BWD)
        (dlr,), (dwa2, dba2) = _rw_bwd(
            _f_loga, [_v(P, LRP, col["lr"])], [V(w_a2_p[l], QK, None), _v(s["ba2"])], [_v(da)],
            row_grads=[BF16], par_grads=[True, True], tr=TR, name=f"loga_bwd{l}")
        dP = jnp.concatenate([dq, dk, dv, dr, dcb, dcc, dcx, dga, dgb, dlr], axis=1)
        for part, rows in enumerate(IN_WGRAD_ROWS):
            gW_all = bmm(s["h1"], dP, ta=True, out_dtype=BF16, m_rows=rows, name=f"in_wgrad{l}_{part}")
            send(l, "w_in", in_grad_blocks(gW_all))
        dh1 = bmm(dP, weight(l, "w_in"), tb=True, name=f"in_dgrad{l}")
        (dx,), (dng0, dsc1, dsh1) = _rw_bwd(
            _f_normmod, [_v(s["xin"])], [_v(ng[0]), _v(sc1), _v(sh1)], [_v(dh1)],
            row_grads=[F32], par_grads=[True] * 3, adds={0: dxin_a}, tr=TR, name=f"normmod1_bwd{l}",
            rides=rides, us=US_NORMMOD_BWD)
        vec = lambda t: t.reshape(1, -1)
        gsmall[l] = dict(
            norm_g=jnp.concatenate([vec(dng0), vec(dng1), vec(dng2), vec(dng3)], axis=0),
            w_a2=dwa2[0, :LR], b_a2=dba2.reshape(QK), gla_norm_g=jnp.sum(dgn, axis=0).reshape(dvh),
            conv_mix_w=jnp.concatenate([vec(t) for t in dcw], axis=0),
            ffn_conv_w=jnp.concatenate([vec(t) for t in dfw], axis=0))
        dmod[l] = jnp.concatenate([vec(t) for t in (dsh1, dsc1, dg1, dsh2, dsc2, dg2)], axis=1).reshape(-1)
    grad_x = dx.reshape(x.shape)

    stack = lambda key, src: jnp.stack([src[l][key] for l in range(depth)])
    small_names = ["norm_g", "w_a2", "b_a2", "gla_norm_g", "conv_mix_w", "ffn_conv_w"]
    gpack = _Pack([jnp.stack(dmod)] + [stack(k, gsmall) for k in small_names])
    (gsm,) = _all_gather([gpack.packed], name="gather_small_grads")
    gs = gpack.unpack(gsm)
    dmod_all = gs[0]
    parts_small = dict(zip(small_names, gs[1:]))

    def my_cols(t, n):
        return lax.dynamic_slice_in_dim(t, me * n, n, axis=t.ndim - 1)

    gW_ada = []
    for l in range(depth):
        dcols = my_cols(dmod_all[:, l], NADA)
        dcols = jnp.concatenate([dcols, jnp.zeros((MP - NDEV, NADA), F32)], axis=0)
        gW_ada.append(bmm(silu_c, dcols, ta=True, out_dtype=F32, name=f"ada_wgrad{l}")[None])
    parts = {}
    parts["b_ada"] = dmod_all
    parts["norm_g"] = my_cols(parts_small["norm_g"], D // NDEV)
    parts["w_a2"] = my_cols(parts_small["w_a2"], QK // NDEV)
    parts["b_a2"] = parts_small["b_a2"]
    parts["gla_norm_g"] = parts_small["gla_norm_g"]
    parts["conv_mix_w"] = my_cols(parts_small["conv_mix_w"], CW // NDEV)
    parts["ffn_conv_w"] = my_cols(parts_small["ffn_conv_w"], FF // NDEV)

    weights = dict(w_ada=(w_ada, m_w_ada, v_w_ada), b_ada=(b_ada, m_b_ada, v_b_ada),
                   norm_g=(norm_g, m_norm_g, v_norm_g), w_in=(w_in, m_w_in, v_w_in), w_a2=(w_a2, m_w_a2, v_w_a2),
                   b_a2=(b_a2, m_b_a2, v_b_a2), gla_norm_g=(gla_norm_g, m_gla_norm_g, v_gla_norm_g),
                   w_out_gla=(w_out_gla, m_w_out_gla, v_w_out_gla), conv_mix_w=(conv_mix_w, m_conv_mix_w, v_conv_mix_w),
                   w_out_conv=(w_out_conv, m_w_out_conv, v_w_out_conv), w_o=(w_o, m_w_o, v_w_o),
                   w_up=(w_up, m_w_up, v_w_up), ffn_conv_w=(ffn_conv_w, m_ffn_conv_w, v_ffn_conv_w),
                   w_down=(w_down, m_w_down, v_w_down))
    done = {}
    for nm in ["w_ada", "w_down", "w_up", "w_o", "w_out_gla", "w_out_conv", "w_in"]:
        w, m, v = weights[nm]
        if nm == "w_ada":
            done[nm] = _adamw(gW_ada, w, m, v, name=f"adamw_{nm}")
            continue
        entries = [e for l in range(depth) for e in grad_streams[l, nm]]
        rides.need(*[e["pair"] for e in entries])
        streams = [e["chips"] for e in entries]
        rides.need(*streams)
        rows = math.gcd(*[s.R for s in streams])
        bufs, row0 = [], []
        for s in streams:
            for first in range(0, s.R, rows):
                bufs.append(s.buf)
                row0.append(first)
        split = lambda t: t.reshape(len(bufs), rows, t.shape[2])
        outs = _adamw(bufs, split(w), split(m), split(v), name=f"adamw_{nm}", row0=row0)
        done[nm] = [o.reshape(w.shape) for o in outs]
    for nm, (w, m, v) in weights.items():
        if nm not in done:
            done[nm] = _adamw_nd(parts[nm], w, m, v, name=f"adamw_{nm}")
    grads, deltas, new_m, new_v = zip(*[done[nm] for nm in weights])
    return (loss, grad_x, *grads, *deltas, *new_m, *new_v)
```

```python
import functools
import math
from typing import Any, NamedTuple

import jax
import jax.numpy as jnp
from jax import lax
from jax.experimental import pallas as pl
from jax.experimental.pallas import tpu as pltpu

F32 = jnp.float32
BF16 = jnp.bfloat16
MESH_AXES = ("x", "y", "c")
NDEV = 8
EPS = 1e-6
CHUNK = 64
HEADS = 4
GLA_TAU = 16.0
CONV_K = 3
ADAM_LR = 0.001
ADAM_B1 = 0.9
ADAM_B2 = 0.999
ADAM_EPS = 1e-08
ADAM_WD = 0.01
ADAM_STEP = 10

LANES = 128
SUBLANES = 8
HALO = 2 * SUBLANES
VMEM_LIMIT = 56 * 1024 * 1024


def _params(sem=None):
    return pltpu.CompilerParams(dimension_semantics=sem, vmem_limit_bytes=VMEM_LIMIT)


def _pick(n, cap, unit=LANES):
    best = None
    for d in range(unit, min(n, cap) + 1, unit):
        if n % d == 0:
            best = d
    return best if best is not None else n


def _position():
    return lax.axis_index("x"), lax.axis_index("y"), lax.axis_index("c")


def _index(p):
    return 4 * p[0] + 2 * p[1] + p[2]


def _peer(k):
    x, y, c = _position()
    return ((1 - x) if k & 4 else x, (1 - y) if k & 2 else y, (1 - c) if k & 1 else c)


ROW_QUANTUM = 2 * SUBLANES
MIN_ROWS = 128
MIN_ICI_CARRIER_US = 50.0


class _Stream:
    def __init__(self, mode, src, rows, cols, rate, after=None, on_done=None):
        self.mode, self.src, self.R, self.rate, self.after = mode, src, rows, rate, after
        self.shape = (NDEV if mode.startswith("gather") else NDEV // 2, rows, cols)
        self.holder = self if after is None else after
        self.buf = None
        self.done = 0
        self.on_done = on_done


class _Exchange:
    def __init__(self, items):
        self.items = items
        self.ins, self.src_pos, self.alias, self.holders, self.out_pos = [], [], {}, [], []
        for s, _, _ in items:
            if not any(s.holder is h for h in self.holders):
                self.holders.append(s.holder)
        for s, _, _ in items:
            self.out_pos.append([n for n, h in enumerate(self.holders) if h is s.holder][0])
            if s.mode == "gather_b":
                self.src_pos.append(None)
            else:
                self.src_pos.append(len(self.ins))
                self.ins.append(s.src)
        for n, h in enumerate(self.holders):
            if h.buf is not None:
                self.alias[len(self.ins)] = n
                self.ins.append(h.buf)
        self.out_shape = [jax.ShapeDtypeStruct(h.shape, BF16) for h in self.holders]
        n = len(items)
        self.scratch = [pltpu.SemaphoreType.DMA((n, NDEV)), pltpu.SemaphoreType.DMA((n, NDEV)),
                        pltpu.SemaphoreType.DMA((n,))]

    def deliver(self, outs):
        for h, o in zip(self.holders, outs):
            h.buf = o

    def _copies(self, ins, outs, sems):
        send_sems, recv_sems, local_sems = sems
        me = _index(_position())
        slot = lambda k: _index(_peer(k))
        local, sends, arrivals = [], [], []
        for n, (s, r0, rc) in enumerate(self.items):
            out, rows = outs[self.out_pos[n]], pl.ds(r0, rc)
            src = out if self.src_pos[n] is None else ins[self.src_pos[n]]
            if s.mode == "pair":
                other = 1 - _position()[2]
                moves = [(q, src.at[2 * q + other, rows], out.at[q, rows], _peer(1), out.at[q, rows])
                         for q in range(NDEV // 2)]
            elif s.mode == "chips":
                chip = lambda k: slot(k) // 2
                local.append(pltpu.make_async_copy(src.at[me // 2, rows], out.at[me // 2, rows], local_sems.at[n]))
                moves = [(k, src.at[chip(k), rows], out.at[me // 2, rows], _peer(k), out.at[chip(k), rows])
                         for k in (2, 4, 6)]
            elif s.mode == "gather_a":
                local.append(pltpu.make_async_copy(src.at[rows], out.at[me, rows], local_sems.at[n]))
                moves = [(k, src.at[rows], out.at[me, rows], _peer(k), out.at[slot(k), rows]) for k in (1, 2, 4, 6)]
            else:
                moves = [(k, src.at[slot(k), rows], out.at[slot(k), rows], _peer(1), out.at[slot(k ^ 1), rows])
                         for k in (2, 4, 6)]
            for k, src_ref, dst_ref, to, landing in moves:
                common = dict(send_sem=send_sems.at[n, k], recv_sem=recv_sems.at[n, k],
                              device_id=to, device_id_type=pl.DeviceIdType.MESH)
                sends.append(pltpu.make_async_remote_copy(src_ref=src_ref, dst_ref=dst_ref, **common))
                arrivals.append(pltpu.make_async_remote_copy(src_ref=landing, dst_ref=landing, **common))
        return local, sends, arrivals

    def start(self, ins, outs, sems):
        local, sends, _ = self._copies(ins, outs, sems)
        for cp in local + sends:
            cp.start()

    def wait(self, ins, outs, sems):
        local, sends, arrivals = self._copies(ins, outs, sems)
        for cp in arrivals:
            cp.wait_recv()
        for cp in sends:
            cp.wait_send()
        for cp in local:
            cp.wait()


class _Rides:
    def __init__(self):
        self.queue = []
        self.finished = []
        self.flushes = 0
        self.ici_share = 1.0
        self.min_ici_us = 0.0

    def add(self, stream):
        self.queue.append(stream)
        return stream

    def take(self, budget, only=None, ici=True):
        ready = {id(s): (s.after.done if s.after is not None else s.R) for s in self.queue}
        left = {"ici": budget * self.ici_share if ici else 0.0, "d2d": budget}
        items = []
        for s in list(self.queue):
            if only is not None and not any(s is t for t in only):
                continue
            link = "d2d" if s.mode in ("gather_b", "pair") else "ici"
            avail = ready[id(s)] - s.done
            if avail <= 0:
                continue
            rc = avail if budget == float("inf") else min(avail, int(left[link] * s.rate) // ROW_QUANTUM * ROW_QUANTUM)
            if rc < min(avail, MIN_ROWS):
                continue
            if avail - rc <= 2 * MIN_ROWS:
                rc = avail
            items.append((s, s.done, rc))
            s.done += rc
            left[link] -= rc / s.rate
            if s.done == s.R:
                self.queue.remove(s)
                self.finished.append(s)
        return items

    def fire(self):
        finished, self.finished = self.finished, []
        for s in finished:
            if s.on_done is not None:
                s.on_done(s)

    def need(self, *streams):
        while any(s.done < s.R for s in streams):
            items = self.take(float("inf"), only=streams)
            comm = _Exchange(items)
            hbm = pl.BlockSpec(memory_space=pl.ANY)
            nin, nout = len(comm.ins), len(comm.holders)

            def body(*refs, comm=comm, nin=nin, nout=nout):
                comm.start(refs[:nin], refs[nin:nin + nout], refs[nin + nout:])
                comm.wait(refs[:nin], refs[nin:nin + nout], refs[nin + nout:])

            outs = pl.pallas_call(body, in_specs=[hbm] * nin, out_specs=[hbm] * nout, out_shape=comm.out_shape,
                                  scratch_shapes=comm.scratch, input_output_aliases=dict(comm.alias),
                                  name=f"exchange_alone{self.flushes}")(*comm.ins)
            self.flushes += 1
            comm.deliver(outs)
            self.fire()


def _call(body, *, grid, in_specs, out_specs, out_shape, scratch_shapes=(), args, name, rides=None, us=0.0):
    sem = ("arbitrary",) * len(grid)
    items = rides.take(us, ici=us >= rides.min_ici_us) if rides is not None else []
    if not items:
        return pl.pallas_call(body, grid=grid, in_specs=in_specs, out_specs=out_specs, out_shape=out_shape,
                              scratch_shapes=list(scratch_shapes), compiler_params=_params(sem), name=name)(*args)
    comm = _Exchange(items)
    nin, nout, nscr, cin, cout = len(in_specs), len(out_specs), len(scratch_shapes), len(comm.ins), len(comm.holders)
    hbm = pl.BlockSpec(memory_space=pl.ANY)

    def both(*refs):
        ins, c_ins = refs[:nin], refs[nin:nin + cin]
        pos = nin + cin
        outs, c_outs = refs[pos:pos + nout], refs[pos + nout:pos + nout + cout]
        pos += nout + cout
        scr, sems = refs[pos:pos + nscr], refs[pos + nscr:]
        first = functools.reduce(lambda p, q: p & q, [pl.program_id(d) == 0 for d in range(len(grid))])
        last = functools.reduce(lambda p, q: p & q, [pl.program_id(d) == grid[d] - 1 for d in range(len(grid))])

        @pl.when(first)
        def _():
            comm.start(c_ins, c_outs, sems)

        body(*ins, *outs, *scr)

        @pl.when(last)
        def _():
            comm.wait(c_ins, c_outs, sems)

    res = pl.pallas_call(both, grid=grid, in_specs=list(in_specs) + [hbm] * cin, out_specs=list(out_specs) + [hbm] * cout,
                         out_shape=list(out_shape) + comm.out_shape, scratch_shapes=list(scratch_shapes) + comm.scratch,
                         input_output_aliases={nin + i: nout + o for i, o in comm.alias.items()},
                         compiler_params=_params(sem), name=name)(*args, *comm.ins)
    comm.deliver(res[nout:])
    rides.fire()
    return res[:nout]


US_GLA, US_GLAPOST, US_MERGE, US_FFN, US_RESID, US_NORMMOD = 100.0, 40.0, 45.0, 110.0, 40.0, 24.0
US_FFN_BWD, US_GLA_BWD, US_MERGE_BWD, US_RESID_BWD, US_NORMMOD_BWD, US_GLAPOST_BWD = 215.0, 118.0, 57.0, 57.0, 54.0, 52.0
MM_VMEM_BUDGET = 44 * 1024 * 1024
MM_FLOPS_PER_US = 780e6


def _mm(a, b, *, ta=False, tb=False, out_dtype=F32, name, rides=None, out_shards=False, m_rows=None):
    if ta:
        K, M = a.shape
    else:
        M, K = a.shape
    m_first, M = (0, M) if m_rows is None else m_rows
    sharded = b.ndim == 3
    if sharded:
        ns, n = b.shape[0], b.shape[2]
        Kb, N = (ns * n, b.shape[1]) if tb else (b.shape[1], ns * n)
    elif tb:
        N, Kb = b.shape
    else:
        Kb, N = b.shape
    assert K == Kb, (a.shape, b.shape, ta, tb)
    tm = _pick(M, 1024 if ta else 2048)
    tn = N // NDEV if out_shards else n if (sharded and not tb) else _pick(N, 1152)
    sa, sb, so = a.dtype.itemsize, b.dtype.itemsize, jnp.dtype(out_dtype).itemsize

    def vmem(tk):
        blocks = 2 * (tm * tk * sa + tk * tn * sb) + 2 * tm * tn * so + (tm * tn * 4 if tk < K else 0)
        return blocks + (tm * K * 2 if ta else 0)

    if sharded and tb:
        tk = n
    else:
        tk = K
        if vmem(K) > MM_VMEM_BUDGET:
            fits = [d for d in range(LANES, K, LANES) if K % d == 0 and vmem(d) <= MM_VMEM_BUDGET]
            tk = max(fits) if fits else _pick(K, 512)
    nk = K // tk
    dims = (((1,), (1 if tb else 0,)), ((), ()))

    def dot(a_ref, b_ref, at_ref):
        if ta:
            k = pl.program_id(2)

            @pl.when(pl.program_id(1) == 0)
            def _():
                at_ref[k] = a_ref[...].astype(BF16).T

            lhs = at_ref[k]
        else:
            lhs = a_ref[...].astype(BF16)
        return lax.dot_general(lhs, b_ref[...].astype(BF16), dims, preferred_element_type=F32)

    def body_one(a_ref, b_ref, o_ref, *at_ref):
        o_ref[...] = dot(a_ref, b_ref, *at_ref or (None,)).astype(o_ref.dtype)

    def body_acc(a_ref, b_ref, o_ref, acc_ref, *at_ref):
        k = pl.program_id(2)
        at = (at_ref or (None,))[0]

        @pl.when(k == 0)
        def _():
            acc_ref[...] = dot(a_ref, b_ref, at)

        @pl.when((k > 0) & (k < nk - 1))
        def _():
            acc_ref[...] += dot(a_ref, b_ref, at)

        @pl.when(k == nk - 1)
        def _():
            o_ref[...] = (acc_ref[...] + dot(a_ref, b_ref, at)).astype(o_ref.dtype)

    assert m_first % tm == 0
    i0 = m_first // tm
    a_spec = (pl.BlockSpec((tk, tm), lambda i, j, k: (k, i0 + i)) if ta
              else pl.BlockSpec((tm, tk), lambda i, j, k: (i0 + i, k)))
    if sharded:
        b_spec = (pl.BlockSpec((None, tn, tk), lambda i, j, k: (k, j, 0)) if tb
                  else pl.BlockSpec((None, tk, tn), lambda i, j, k: (j, k, 0)))
    else:
        b_spec = (pl.BlockSpec((tn, tk), lambda i, j, k: (j, k)) if tb
                  else pl.BlockSpec((tk, tn), lambda i, j, k: (k, j)))
    if out_shards:
        out_spec = pl.BlockSpec((None, tm, tn), lambda i, j, k: (j, i, 0))
        out_shape = jax.ShapeDtypeStruct((NDEV, M, tn), out_dtype)
    else:
        out_spec = pl.BlockSpec((tm, tn), lambda i, j, k: (i, j))
        out_shape = jax.ShapeDtypeStruct((M, N), out_dtype)
    (out,) = _call(
        body_one if nk == 1 else body_acc, grid=(M // tm, N // tn, nk), in_specs=[a_spec, b_spec],
        out_specs=[out_spec], out_shape=[out_shape],
        scratch_shapes=([] if nk == 1 else [pltpu.VMEM((tm, tn), F32)]) + ([pltpu.VMEM((nk, tm, tk), BF16)] if ta else []),
        args=(a, b), name=name,
        rides=rides, us=2.0 * M * N * K / MM_FLOPS_PER_US)
    return out


class V(NamedTuple):
    arr: Any
    width: int
    base: Any


def _v(arr, width=None, col0=0):
    width = arr.shape[-1] if width is None else width
    assert col0 % width == 0
    return V(arr, width, col0 // width)


def _row_specs(v, tr, nrow8, halo):
    main = pl.BlockSpec((tr, v.width), lambda j, i, b=v.base: (i, b + j))
    if not halo:
        return [main]
    per = tr // HALO
    prev = pl.BlockSpec((HALO, v.width), lambda j, i, b=v.base: (jnp.maximum(i * per - 1, 0), b + j))
    nxt = pl.BlockSpec((HALO, v.width), lambda j, i, b=v.base: (jnp.minimum((i + 1) * per, nrow8 - 1), b + j))
    return [prev, main, nxt]


def _par_spec(p):
    if p.base is None:
        return pl.BlockSpec(p.arr.shape, lambda j, i: (0,) * p.arr.ndim)
    return pl.BlockSpec((p.arr.shape[0], p.width), lambda j, i, b=p.base: (0, b + j))


def _load_rows(refs, i, nrow, halo):
    if not halo:
        return refs[0][...].astype(F32)
    prev, main, nxt = refs
    pv = jnp.where(i > 0, prev[...].astype(F32), 0.0)
    nv = jnp.where(i < nrow - 1, nxt[...].astype(F32), 0.0)
    return jnp.concatenate([pv, main[...].astype(F32), nv], axis=0)


def _center_mask(tr):
    row = lax.broadcasted_iota(jnp.int32, (tr + 2 * HALO, 1), 0)
    return ((row >= HALO) & (row < HALO + tr)).astype(F32)


def _rw_fwd(f, rows, params, outs, *, tr, ncol=1, halo=False, name, rides=None, us=0.0):
    S = rows[0].arr.shape[0]
    tr = min(tr, S)
    nrow = S // tr
    per = 3 if halo else 1
    nr, npar = len(rows), len(params)

    def body(*refs):
        i = pl.program_id(1)
        vals = [_load_rows(refs[per * r: per * (r + 1)], i, nrow, halo) for r in range(nr)]
        pvals = [refs[per * nr + p][...].astype(F32) for p in range(npar)]
        out_refs = refs[per * nr + npar:]
        res = f(*vals, *pvals, _center_mask(tr)) if halo else f(*vals, *pvals)
        for r, o in zip(res, out_refs):
            o[...] = (r[HALO:HALO + tr] if halo else r).astype(o.dtype)

    in_specs = []
    for v in rows:
        in_specs += _row_specs(v, tr, S // HALO, halo)
    in_specs += [_par_spec(p) for p in params]
    args = []
    for v in rows:
        args += [v.arr] * per
    args += [p.arr for p in params]
    return _call(
        body, grid=(ncol, nrow), in_specs=in_specs,
        out_specs=[pl.BlockSpec((tr, tot // ncol), lambda j, i: (i, j)) for tot, _ in outs],
        out_shape=[jax.ShapeDtypeStruct((S, tot), dt) for tot, dt in outs],
        args=args, name=name, rides=rides, us=us)


def _rw_bwd(f, rows, params, douts, *, row_grads, par_grads, adds=None, tr, ncol=1, halo=False, name,
            rides=None, us=0.0):
    S = rows[0].arr.shape[0]
    tr = min(tr, S)
    nrow = S // tr
    per = 3 if halo else 1
    adds = adds or {}
    nr, npar, nd = len(rows), len(params), len(douts)
    add_keys = sorted(adds)
    rg_idx = [k for k in range(nr) if row_grads[k]]
    pg_idx = [k for k in range(npar) if par_grads[k]]

    def body(*refs):
        i = pl.program_id(1)
        pos = 0
        vals = [_load_rows(refs[per * r: per * (r + 1)], i, nrow, halo) for r in range(nr)]
        pos = per * nr
        pvals = [refs[pos + p][...].astype(F32) for p in range(npar)]
        pos += npar
        dvals = [_load_rows(refs[pos + per * d: pos + per * (d + 1)], i, nrow, halo) for d in range(nd)]
        pos += per * nd
        add_vals = {k: refs[pos + n][...].astype(F32) for n, k in enumerate(add_keys)}
        pos += len(add_keys)
        rg_refs = refs[pos: pos + len(rg_idx)]
        pg_refs = refs[pos + len(rg_idx):]
        if halo:
            center = _center_mask(tr)
            fun = lambda *a: f(*a, center)
        else:
            fun = f
        _, vjp = jax.vjp(fun, *vals, *pvals)
        grads = vjp(tuple(dvals))
        for ref, k in zip(rg_refs, rg_idx):
            g = grads[k]
            g = g[HALO:HALO + tr] if halo else g
            if k in add_vals:
                g = g + add_vals[k]
            ref[...] = g.astype(ref.dtype)
        for ref, k in zip(pg_refs, pg_idx):
            @pl.when(i == 0)
            def _(ref=ref):
                ref[...] = jnp.zeros_like(ref)
            ref[...] += grads[nr + k]

    in_specs, args = [], []
    for v in list(rows):
        in_specs += _row_specs(v, tr, S // HALO, halo)
        args += [v.arr] * per
    in_specs += [_par_spec(p) for p in params]
    args += [p.arr for p in params]
    for v in douts:
        in_specs += _row_specs(v, tr, S // HALO, halo)
        args += [v.arr] * per
    for k in add_keys:
        in_specs += _row_specs(_v(adds[k], rows[k].width), tr, S // HALO, False)
        args += [adds[k]]
    out_specs, out_shape = [], []
    for k in rg_idx:
        w = rows[k].width
        out_specs.append(pl.BlockSpec((tr, w), lambda j, i: (i, j)))
        out_shape.append(jax.ShapeDtypeStruct((S, ncol * w), row_grads[k]))
    for k in pg_idx:
        p = params[k]
        blk = p.arr.shape if p.base is None else (p.arr.shape[0], p.width)
        out_specs.append(pl.BlockSpec((None,) + tuple(blk), lambda j, i, n=len(blk): (j,) + (0,) * n))
        out_shape.append(jax.ShapeDtypeStruct((ncol,) + tuple(blk), F32))
    res = _call(body, grid=(ncol, nrow), in_specs=in_specs, out_specs=out_specs, out_shape=out_shape, args=args, name=name,
                rides=rides, us=us)
    rg = [None] * nr
    for n, k in enumerate(rg_idx):
        rg[k] = res[n]
    pg = [None] * npar
    for n, k in enumerate(pg_idx):
        pg[k] = res[len(rg_idx) + n]
    return rg, pg


def _sigmoid(x):
    return 1.0 / (1.0 + jnp.exp(-x))


def _rms(x, g):
    return x * lax.rsqrt(jnp.mean(x * x, axis=-1, keepdims=True) + EPS) * g


def _f_normmod(x, g, sc, sh):
    return (_rms(x, g) * (1.0 + sc) + sh,)


def _f_resid(x, y, gate, g):
    return (x + gate * _rms(y, g),)


def _f_merge(ga, gb, ya, yb):
    return (_sigmoid(ga) * ya + _sigmoid(gb) * yb,)


def _f_glapost(o, r, g):
    return (_rms(o, g) * (r * _sigmoid(r)),)


def _roll_rows(x, k):
    return pltpu.roll(x, k % x.shape[0], 0)


@functools.partial(jax.custom_vjp, nondiff_argnums=(1,))
def _shift(x, k):
    return _roll_rows(x, k)


def _shift_fwd(x, k):
    return _roll_rows(x, k), None


def _shift_bwd(k, _, g):
    return (_roll_rows(g, -k),)


_shift.defvjp(_shift_fwd, _shift_bwd)


def _taps(w, center):
    return jnp.where(center > 0.0, w, lax.stop_gradient(w))


def _conv3(u, w0, w1, w2, center):
    return (_taps(w0, center) * _shift(u, 2) + _taps(w1, center) * _shift(u, 1)) + _taps(w2, center) * u


def _f_convmix(cb, cc, cx, w0, w1, w2, center):
    return (cb * _conv3(cc * cx, w0, w1, w2, center),)


def _gelu_tanh(x):
    return 0.5 * x * (1.0 + jnp.tanh(math.sqrt(2.0 / math.pi) * (x + 0.044715 * (x * x * x))))


def _f_ffn(gate, up, w0, w1, w2, center):
    return (_gelu_tanh(_conv3(gate, w0, w1, w2, center)) * up,)


@jax.custom_vjp
def _bdot(a, b):
    return jnp.dot(a.astype(BF16), b.astype(BF16), preferred_element_type=F32)


def _bdot_fwd(a, b):
    return _bdot(a, b), (a, b)


def _bdot_bwd(res, g):
    a, b = res
    gb = g.astype(BF16)
    da = lax.dot_general(gb, b.astype(BF16), (((1,), (1,)), ((), ())), preferred_element_type=F32)
    db = lax.dot_general(a.astype(BF16), gb, (((0,), (0,)), ((), ())), preferred_element_type=F32)
    return da, db


_bdot.defvjp(_bdot_fwd, _bdot_bwd)


@jax.custom_vjp
def _log_sigmoid(z):
    return jnp.minimum(z, 0.0) - jnp.log(1.0 + jnp.exp(-jnp.abs(z)))


def _log_sigmoid_fwd(z):
    return _log_sigmoid(z), z


def _log_sigmoid_bwd(z, g):
    return (g * _sigmoid(-z),)


_log_sigmoid.defvjp(_log_sigmoid_fwd, _log_sigmoid_bwd)


def _f_loga(lr, w, b):
    return (_log_sigmoid(_bdot(lr, w) + b) / GLA_TAU,)


def _split3(x):
    hi = x.astype(BF16)
    r1 = x - hi.astype(F32)
    mid = r1.astype(BF16)
    lo = (r1 - mid.astype(F32)).astype(BF16)
    return hi, mid, lo


def _dot_exact(a01, x, dims):
    hi, mid, lo = _split3(x)
    d = lambda p: lax.dot_general(a01, p, dims, preferred_element_type=F32)
    return (d(lo) + d(mid)) + d(hi)


def _dot_exact_lhs(x, a01, dims):
    hi, mid, lo = _split3(x)
    d = lambda p: lax.dot_general(p, a01, dims, preferred_element_type=F32)
    return (d(lo) + d(mid)) + d(hi)


GLA_CHUNKS_PER_STEP = 8

NN = (((1,), (0,)), ((), ()))
NT = (((1,), (1,)), ((), ()))
TN = (((0,), (0,)), ((), ()))


def _bf_dot(a, b, dims):
    return lax.dot_general(a.astype(BF16), b.astype(BF16), dims, preferred_element_type=F32)


def _chunk_decay(a, dv):
    L = a.shape[0]
    r = lax.broadcasted_iota(jnp.int32, (L, L), 0)
    c = lax.broadcasted_iota(jnp.int32, (L, L), 1)
    tri = (r >= c).astype(BF16)
    cum = _dot_exact(tri, a, NN)
    ce = cum[L - 1:L, :]
    dec = jnp.exp(_dot_exact_lhs(a, jnp.ones((L, dv), BF16), TN))
    return cum, ce, jnp.exp(ce - cum), dec


def _gla_fwd(P, a, *, qk, vw, name, rides=None, us=0.0):
    S = P.shape[0]
    nc = S // CHUNK
    cps = math.gcd(nc, GLA_CHUNKS_PER_STEP)
    rows = cps * CHUNK
    dk, dv = qk // HEADS, vw // HEADS
    scale = dk ** -0.5

    def body(q_ref, k_ref, v_ref, a_ref, o_ref, st_ref, state):
        @pl.when(pl.program_id(0) == 0)
        def _():
            state[...] = jnp.zeros_like(state)

        for h in range(HEADS):
            ks, vs = slice(h * dk, (h + 1) * dk), slice(h * dv, (h + 1) * dv)
            s = state[h]
            for c in range(cps):
                rs = slice(c * CHUNK, (c + 1) * CHUNK)
                _, _, E, dec = _chunk_decay(a_ref[rs, ks], dv)
                kd = k_ref[rs, ks] * E
                s = dec * s + _bf_dot(kd, v_ref[rs, vs], TN)
                st_ref[c, h] = s
                o_ref[rs, vs] = _bf_dot(q_ref[rs, ks] * scale, s, NN).astype(o_ref.dtype)
            state[h] = s

    return _call(
        body, grid=(nc // cps,),
        in_specs=[pl.BlockSpec((rows, qk), lambda i: (i, 0)),
                  pl.BlockSpec((rows, qk), lambda i: (i, 1)),
                  pl.BlockSpec((rows, vw), lambda i: (i, 2 * qk // vw)),
                  pl.BlockSpec((rows, qk), lambda i: (i, 0))],
        out_specs=[pl.BlockSpec((rows, vw), lambda i: (i, 0)),
                   pl.BlockSpec((cps, HEADS, dk, dv), lambda i: (i, 0, 0, 0))],
        out_shape=[jax.ShapeDtypeStruct((S, vw), BF16), jax.ShapeDtypeStruct((nc, HEADS, dk, dv), F32)],
        scratch_shapes=[pltpu.VMEM((HEADS, dk, dv), F32)], args=(P, P, P, a), name=name, rides=rides, us=us)


def _gla_bwd(P, a, states, do, *, qk, vw, name, rides=None, us=0.0):
    S = P.shape[0]
    nc = S // CHUNK
    cps = math.gcd(nc, GLA_CHUNKS_PER_STEP)
    rows = cps * CHUNK
    nstep = nc // cps
    dk, dv = qk // HEADS, vw // HEADS
    scale = dk ** -0.5

    def body(q_ref, k_ref, v_ref, a_ref, do_ref, sc_ref, sp_ref, dq_ref, dk_ref, dv_ref, da_ref, ds_carry):
        i = pl.program_id(0)

        @pl.when(i == 0)
        def _():
            ds_carry[...] = jnp.zeros_like(ds_carry)

        has_prev = (i < nstep - 1).astype(F32)
        L = CHUNK
        r = lax.broadcasted_iota(jnp.int32, (L, L), 0)
        c = lax.broadcasted_iota(jnp.int32, (L, L), 1)
        tri_t = (c >= r).astype(BF16)
        last = (lax.broadcasted_iota(jnp.int32, (L, 1), 0) == L - 1).astype(F32)
        for h in range(HEADS):
            ks, vs = slice(h * dk, (h + 1) * dk), slice(h * dv, (h + 1) * dv)
            carry = ds_carry[h]
            for n in reversed(range(cps)):
                rs = slice(n * CHUNK, (n + 1) * CHUNK)
                _, _, E, dec = _chunk_decay(a_ref[rs, ks], dv)
                kk, vv, dov = k_ref[rs, ks], v_ref[rs, vs], do_ref[rs, vs]
                kd = kk * E
                ds = carry + _bf_dot(q_ref[rs, ks] * scale, dov, TN)
                dq_ref[rs, ks] = (_bf_dot(dov, sc_ref[n, h], NT) * scale).astype(dq_ref.dtype)
                dkd = _bf_dot(vv, ds, NT)
                dv_ref[rs, vs] = _bf_dot(kd, ds, NN).astype(dv_ref.dtype)
                dk_ref[rs, ks] = (dkd * E).astype(dk_ref.dtype)
                darg = dkd * kd
                before = sc_ref[n - 1, h] if n > 0 else sp_ref[0, h] * has_prev
                w = ds * before * dec
                dce = _dot_exact(jnp.ones((SUBLANES, dv), BF16), w, NT)[0:1, :]
                dce = dce + jnp.sum(darg, axis=0, keepdims=True)
                dcum = last * dce - darg
                da_ref[rs, ks] = _dot_exact(tri_t, dcum, NN)
                carry = dec * ds
            ds_carry[h] = carry

    rev = lambda i: nstep - 1 - i
    return _call(
        body, grid=(nstep,),
        in_specs=[pl.BlockSpec((rows, qk), lambda i: (rev(i), 0)),
                  pl.BlockSpec((rows, qk), lambda i: (rev(i), 1)),
                  pl.BlockSpec((rows, vw), lambda i: (rev(i), 2 * qk // vw)),
                  pl.BlockSpec((rows, qk), lambda i: (rev(i), 0)),
                  pl.BlockSpec((rows, vw), lambda i: (rev(i), 0)),
                  pl.BlockSpec((cps, HEADS, dk, dv), lambda i: (rev(i), 0, 0, 0)),
                  pl.BlockSpec((1, HEADS, dk, dv), lambda i: (jnp.maximum(rev(i) * cps - 1, 0), 0, 0, 0))],
        out_specs=[pl.BlockSpec((rows, qk), lambda i: (rev(i), 0)),
                   pl.BlockSpec((rows, qk), lambda i: (rev(i), 0)),
                   pl.BlockSpec((rows, vw), lambda i: (rev(i), 0)),
                   pl.BlockSpec((rows, qk), lambda i: (rev(i), 0))],
        out_shape=[jax.ShapeDtypeStruct((S, qk), BF16), jax.ShapeDtypeStruct((S, qk), BF16),
                   jax.ShapeDtypeStruct((S, vw), BF16), jax.ShapeDtypeStruct((S, qk), F32)],
        scratch_shapes=[pltpu.VMEM((HEADS, dk, dv), F32)], args=(P, P, P, a, do, states, states), name=name,
        rides=rides, us=us)


def _loss_head(y, target, *, tr, name):
    S, D = y.shape
    tr = min(tr, S)

    def body(y_ref, t_ref, l_ref, dy_ref):
        @pl.when(pl.program_id(0) == 0)
        def _():
            l_ref[...] = jnp.zeros_like(l_ref)

        e = y_ref[...] - t_ref[...]
        dy_ref[...] = e / D
        l_ref[...] += 0.5 * jnp.sum(jnp.mean(e * e, axis=-1, keepdims=True), axis=0, keepdims=True)

    loss, dy = pl.pallas_call(
        body, grid=(S // tr,),
        in_specs=[pl.BlockSpec((tr, D), lambda i: (i, 0))] * 2,
        out_specs=[pl.BlockSpec((SUBLANES, LANES), lambda i: (0, 0)), pl.BlockSpec((tr, D), lambda i: (i, 0))],
        out_shape=[jax.ShapeDtypeStruct((SUBLANES, LANES), F32), jax.ShapeDtypeStruct((S, D), F32)],
        compiler_params=_params(("arbitrary",)), name=name)(y, target)
    return loss[0, 0], dy


ADAMW_BYTES_PER_US = 2.5e6


def _adamw(parts, w, m, v, *, name, rides=None, row0=None):
    G, R, C = w.shape
    n = parts[0].shape[0]
    row0 = [0] * G if row0 is None else row0
    unit = 2 * SUBLANES if parts[0].dtype == BF16 else SUBLANES
    tr = _pick(R, max(unit, (256 * 1024) // C), unit)
    nblk = R // tr
    assert all(r % tr == 0 for r in row0)
    bc1 = 1.0 - ADAM_B1 ** ADAM_STEP
    bc2 = 1.0 - ADAM_B2 ** ADAM_STEP

    def body(*refs):
        p_refs = refs[:G]
        w_ref, m_ref, v_ref, g_ref, d_ref, mo_ref, vo_ref = refs[G:]
        for q in range(G):
            @pl.when(pl.program_id(0) == q)
            def _(q=q):
                g = p_refs[q][0].astype(F32)
                for s in range(1, n):
                    g = g + p_refs[q][s].astype(F32)
                m2 = ADAM_B1 * m_ref[...] + (1.0 - ADAM_B1) * g
                v2 = ADAM_B2 * v_ref[...] + (1.0 - ADAM_B2) * (g * g)
                g_ref[...] = g
                mo_ref[...] = m2
                vo_ref[...] = v2
                d_ref[...] = -ADAM_LR * ((m2 / bc1) / (jnp.sqrt(v2 / bc2) + ADAM_EPS) + ADAM_WD * w_ref[...])

    def part_spec(q):
        first = row0[q] // tr
        return pl.BlockSpec((n, tr, C), lambda g, i: (0, first + jnp.where(g == q, i, jnp.where(g < q, 0, nblk - 1)), 0))

    blk = pl.BlockSpec((None, tr, C), lambda g, i: (g, i, 0))
    traffic = G * R * C * (n * parts[0].dtype.itemsize + 7 * 4)
    return _call(
        body, grid=(G, nblk), in_specs=[part_spec(q) for q in range(G)] + [blk, blk, blk],
        out_specs=[blk] * 4, out_shape=[jax.ShapeDtypeStruct((G, R, C), F32)] * 4,
        args=(*parts, w, m, v), name=name, rides=rides, us=traffic / ADAMW_BYTES_PER_US)


def _pair_add(blocks, received, core, *, name):
    _, R, C = blocks.shape
    tr = _pick(R, max(ROW_QUANTUM, (1024 * 1024) // C), ROW_QUANTUM)

    def body(core_ref, mine_ref, theirs_ref, o_ref):
        o_ref[...] = (mine_ref[...].astype(F32) + theirs_ref[...].astype(F32)).astype(o_ref.dtype)

    grid_spec = pltpu.PrefetchScalarGridSpec(
        num_scalar_prefetch=1, grid=(NDEV // 2, R // tr),
        in_specs=[pl.BlockSpec((None, tr, C), lambda q, i, core_ref: (2 * q + core_ref[0], i, 0)),
                  pl.BlockSpec((None, tr, C), lambda q, i, core_ref: (q, i, 0))],
        out_specs=pl.BlockSpec((None, tr, C), lambda q, i, core_ref: (q, i, 0)))
    return pl.pallas_call(body, grid_spec=grid_spec, out_shape=jax.ShapeDtypeStruct((NDEV // 2, R, C), BF16),
                          compiler_params=_params(("arbitrary", "arbitrary")), name=name)(
        jnp.reshape(core, (1,)).astype(jnp.int32), blocks, received)


def _adamw_nd(parts, w, m, v, *, name):
    shape = w.shape
    C = shape[-1]
    R = math.prod(shape[:-1])
    outs = _adamw([parts.reshape(parts.shape[0], R, C)], w.reshape(1, R, C), m.reshape(1, R, C), v.reshape(1, R, C),
                  name=name)
    return [o.reshape(shape) for o in outs]


def _all_gather(arrs, *, name):
    n = len(arrs)
    HBM = pl.BlockSpec(memory_space=pl.ANY)

    def body(*refs):
        ins, outs = refs[:n], refs[n:2 * n]
        send_sems, recv_sems, local_sems = refs[2 * n:]
        x, y, c = _position()
        me, sibling = (x, y, c), (x, y, 1 - c)
        chips = [(1 - x, y), (x, 1 - y), (1 - x, 1 - y)]

        def copy(t, k, block, to, src=None):
            rows = outs[t].at[_index(block)]
            return pltpu.make_async_remote_copy(
                src_ref=rows if src is None else src, dst_ref=rows,
                send_sem=send_sems.at[t, k], recv_sem=recv_sems.at[t, k],
                device_id=to, device_id_type=pl.DeviceIdType.MESH)

        started = []
        mine = []
        for t in range(n):
            cp = pltpu.make_async_copy(ins[t], outs[t].at[_index(me)], local_sems.at[t])
            cp.start()
            mine.append(cp)
            first = [copy(t, 0, me, sibling, src=ins[t])]
            first += [copy(t, 1 + j, me, (*chip, c), src=ins[t]) for j, chip in enumerate(chips)]
            for cp in first:
                cp.start()
            started += first
        for t in range(n):
            for j, chip in enumerate(chips):
                copy(t, 1 + j, (*chip, c), me).wait_recv()
                cp = copy(t, 4 + j, (*chip, c), sibling)
                cp.start()
                started.append(cp)
        for t in range(n):
            copy(t, 0, sibling, me).wait_recv()
            for j, chip in enumerate(chips):
                copy(t, 4 + j, (*chip, 1 - c), me).wait_recv()
        for cp in started:
            cp.wait_send()
        for cp in mine:
            cp.wait()

    return pl.pallas_call(
        body, in_specs=[HBM] * n, out_specs=[HBM] * n,
        out_shape=[jax.ShapeDtypeStruct((NDEV,) + a.shape, a.dtype) for a in arrs],
        scratch_shapes=[pltpu.SemaphoreType.DMA((n, 7)), pltpu.SemaphoreType.DMA((n, 7)),
                        pltpu.SemaphoreType.DMA((n,))],
        name=name)(*arrs)


class _Pack:
    def __init__(self, arrs):
        self.shapes = [a.shape for a in arrs]
        self.sizes = [math.prod(s) for s in self.shapes]
        total = sum(self.sizes)
        unit = SUBLANES * LANES
        self.padded = -(-total // unit) * unit
        flat = jnp.concatenate([a.reshape(-1).astype(F32) for a in arrs] + [jnp.zeros((self.padded - total,), F32)])
        self.packed = flat.reshape(self.padded // LANES, LANES)

    def unpack(self, gathered):
        flat = gathered.reshape(NDEV, self.padded)
        out, off = [], 0
        for shape, size in zip(self.shapes, self.sizes):
            out.append(flat[:, off:off + size].reshape((NDEV,) + tuple(shape)))
            off += size
        return out


def _cols_from_shards(g):
    g = jnp.moveaxis(g, 0, -2)
    return g.reshape(g.shape[:-2] + (g.shape[-2] * g.shape[-1],))


def _cols_to_shards(w):
    K, N = w.shape
    return jnp.moveaxis(w.reshape(K, NDEV, N // NDEV), 1, 0)


def kernel(x, c, w_ada, b_ada, norm_g, w_in, w_a2, b_a2, gla_norm_g, w_out_gla, conv_mix_w, w_out_conv, w_o, w_up, ffn_conv_w, w_down, loss_target, m_w_ada, m_b_ada, m_norm_g, m_w_in, m_w_a2, m_b_a2, m_gla_norm_g, m_w_out_gla, m_conv_mix_w, m_w_out_conv, m_w_o, m_w_up, m_ffn_conv_w, m_w_down, v_w_ada, v_b_ada, v_norm_g, v_w_in, v_w_a2, v_b_a2, v_gla_norm_g, v_w_out_gla, v_conv_mix_w, v_w_out_conv, v_w_o, v_w_up, v_ffn_conv_w, v_w_down):
    depth = w_ada.shape[0]
    S, D = x.shape[1], x.shape[2]
    QK = w_a2.shape[2] * NDEV
    LR = w_a2.shape[1]
    VW = w_out_gla.shape[1]
    CW = w_out_conv.shape[1]
    FF = w_down.shape[1] * NDEV
    NIN = w_in.shape[2] * NDEV
    NADA = w_ada.shape[2]
    LRP = -(-LR // LANES) * LANES
    off_lr = 2 * QK + 2 * VW
    NP = NIN - LR + LRP
    assert NIN == 2 * QK + 2 * VW + LR + 3 * CW + 2 * D
    xi, yi, ci = _position()
    me = _index((xi, yi, ci))
    xs = x.reshape(S, D)
    target = loss_target.reshape(S, D)
    TR = 256
    TRC = 512
    TC = _pick(CW, 512)
    TCF = _pick(FF, 512)

    small = _Pack([c, norm_g, w_a2, conv_mix_w, ffn_conv_w])
    big_names = ["w_in", "w_out_gla", "w_out_conv", "w_o", "w_up", "w_down"]
    big = dict(w_in=w_in, w_out_gla=w_out_gla, w_out_conv=w_out_conv, w_o=w_o, w_up=w_up, w_down=w_down)

    def matrix(key, g):
        if key == "w_up":
            return g
        if key in ("w_o", "w_down"):
            return g.reshape(g.shape[0] * g.shape[1], g.shape[2])
        if key != "w_in":
            return _cols_from_shards(g)
        pieces = [g[d][:, lo - d * nin_shard:hi - d * nin_shard] for _, d, lo, hi in sorted(in_pieces)]
        return jnp.concatenate(pieces + [jnp.zeros((D, LRP - LR), BF16)], axis=-1)

    nin_shard = NIN // NDEV
    in_pieces = []
    for g_lo, g_hi, to in ((0, off_lr, 0), (off_lr, off_lr + LR, NIN - LR), (off_lr + LR, NIN, off_lr)):
        for d in range(NDEV):
            lo, hi = max(g_lo, d * nin_shard), min(g_hi, (d + 1) * nin_shard)
            if lo < hi:
                in_pieces.append((to + lo - g_lo, d, lo, hi))

    def in_grad_blocks(gW_all):
        blocks = []
        for d in range(NDEV):
            mine = sorted((lo, to, hi) for to, dd, lo, hi in in_pieces if dd == d)
            blocks.append(jnp.concatenate([gW_all[:, to:to + hi - lo] for lo, to, hi in mine], axis=1))
        return jnp.stack(blocks)

    (small_g, w_in0_g) = _all_gather([small.packed, w_in[0].astype(BF16)], name="gather_first")
    c_all, norm_g_s, w_a2_s, conv_w_s, ffn_w_s = small.unpack(small_g)
    c_all = c_all.reshape(NDEV, D)
    norm_g_f = _cols_from_shards(norm_g_s)
    w_a2_f = _cols_from_shards(w_a2_s)
    conv_w_f = _cols_from_shards(conv_w_s)
    ffn_w_f = _cols_from_shards(ffn_w_s)
    w_a2_p = jnp.concatenate([w_a2_f, jnp.zeros((depth, LRP - LR, QK), F32)], axis=1)

    rides = _Rides()
    GATHER_PARAMS_PER_US = 80e3
    FORWARD_PARAMS_PER_US = 800e3
    EXCHANGE_PARAMS_PER_US = 45e3
    PAIR_PARAMS_PER_US = 500e3
    BACKWARD_ICI_SHARE = 0.8
    weight_streams = {}
    for l in range(depth):
        for key in big_names:
            if (l, key) == (0, "w_in"):
                continue
            shard = big[key][l].astype(BF16)
            rows, cols = shard.shape
            first = rides.add(_Stream("gather_a", shard, rows, cols, GATHER_PARAMS_PER_US / (NDEV * cols)))
            second = rides.add(_Stream("gather_b", None, rows, cols, FORWARD_PARAMS_PER_US / (NDEV * cols), after=first))
            weight_streams[l, key] = (first, second)
    W = {(0, "w_in"): matrix("w_in", w_in0_g)}

    def weight(l, key):
        if (l, key) not in W:
            first, second = weight_streams[l, key]
            rides.need(first, second)
            W[l, key] = matrix(key, first.buf)
        return W[l, key]

    MP = LANES
    c_pad = jnp.concatenate([c_all, jnp.zeros((MP - NDEV, D), F32)], axis=0)
    (silu_c,) = _rw_fwd(lambda t: (t * _sigmoid(t),), [_v(c_pad)], [], [(D, BF16)], tr=MP, name="silu_c")
    mod_cols = _mm(silu_c, w_ada, name="ada_fwd")[:NDEV].reshape(NDEV, depth, NADA)
    (mod_g,) = _all_gather([jnp.moveaxis(mod_cols, 1, 0)], name="gather_mod")
    mod_all = jnp.moveaxis(mod_g, 0, 2).reshape(depth, NDEV, NDEV * NADA)
    mod_me = lax.dynamic_slice_in_dim(mod_all, me, 1, axis=1).reshape(depth, NDEV * NADA)
    (mod_me,) = _rw_fwd(lambda a, b: (a + b,), [_v(mod_me), _v(b_ada)], [], [(NDEV * NADA, F32)],
                        tr=depth, name="mod_bias")

    col = {}
    off = 0
    for nm, wd in (("q", QK), ("k", QK), ("v", VW), ("r", VW), ("cb", CW), ("cc", CW), ("cx", CW),
                   ("ga", D), ("gb", D), ("lr", LRP)):
        col[nm] = off
        off += wd

    saved = []
    xin = xs
    for l in range(depth):
        sh1, sc1, g1, sh2, sc2, g2 = [mod_me[l:l + 1, i * D:(i + 1) * D] for i in range(6)]
        ng = [norm_g_f[l, i:i + 1] for i in range(4)]
        cw = [conv_w_f[l, i:i + 1] for i in range(CONV_K)]
        fw = [ffn_w_f[l, i:i + 1] for i in range(CONV_K)]
        gn = gla_norm_g[l:l + 1]
        ba2 = b_a2[l:l + 1]
        def proj(act, key, name, out_dtype=BF16):
            return _mm(act, weight(l, key), name=name, rides=rides, out_dtype=out_dtype)

        (h1,) = _rw_fwd(_f_normmod, [_v(xin)], [_v(ng[0]), _v(sc1), _v(sh1)], [(D, BF16)], tr=TR, name=f"normmod1_{l}",
                        rides=rides, us=US_NORMMOD)
        P = proj(h1, "w_in", f"in_proj{l}")
        (a,) = _rw_fwd(_f_loga, [_v(P, LRP, col["lr"])], [V(w_a2_p[l], QK, None), _v(ba2)], [(QK, F32)],
                       tr=TR, name=f"loga{l}")
        o, states = _gla_fwd(P, a, qk=QK, vw=VW, name=f"gla_fwd{l}", rides=rides, us=US_GLA)
        dvh = VW // HEADS
        (ya_in,) = _rw_fwd(_f_glapost, [_v(o, dvh), _v(P, dvh, col["r"])], [V(gn, dvh, None)], [(VW, BF16)],
                           tr=TR, ncol=HEADS, name=f"glapost{l}", rides=rides, us=US_GLAPOST)
        ya = proj(ya_in, "w_out_gla", f"out_gla{l}")
        (yb_in,) = _rw_fwd(_f_convmix, [_v(P, TC, col["cb"]), _v(P, TC, col["cc"]), _v(P, TC, col["cx"])],
                           [_v(w, TC) for w in cw], [(CW, BF16)], tr=TRC, ncol=CW // TC, halo=True, name=f"convmix{l}")
        yb = proj(yb_in, "w_out_conv", f"out_conv{l}")
        (mg,) = _rw_fwd(_f_merge, [_v(P, D, col["ga"]), _v(P, D, col["gb"]), _v(ya), _v(yb)], [], [(D, BF16)],
                        tr=TR, name=f"merge{l}", rides=rides, us=US_MERGE)
        y = proj(mg, "w_o", f"o_proj{l}", F32)
        (x1,) = _rw_fwd(_f_resid, [_v(xin), _v(y)], [_v(g1), _v(ng[1])], [(D, F32)], tr=TR, name=f"resid1_{l}",
                        rides=rides, us=US_RESID)
        (h2,) = _rw_fwd(_f_normmod, [_v(x1)], [_v(ng[2]), _v(sc2), _v(sh2)], [(D, BF16)], tr=TR, name=f"normmod2_{l}",
                        rides=rides, us=US_NORMMOD)
        U = proj(h2, "w_up", f"up_proj{l}")
        (ff,) = _rw_fwd(_f_ffn, [_v(U, TCF, 0), _v(U, TCF, FF)], [_v(w, TCF) for w in fw], [(FF, BF16)],
                        tr=TRC, ncol=FF // TCF, halo=True, name=f"ffn{l}", rides=rides, us=US_FFN)
        y2 = proj(ff, "w_down", f"down_proj{l}", F32)
        (x2,) = _rw_fwd(_f_resid, [_v(x1), _v(y2)], [_v(g2), _v(ng[3])], [(D, F32)], tr=TR, name=f"resid2_{l}",
                        rides=rides, us=US_RESID)
        saved.append(dict(xin=xin, h1=h1, P=P, a=a, o=o, states=states, ya_in=ya_in, ya=ya, yb_in=yb_in, yb=yb,
                          mg=mg, y=y, x1=x1, h2=h2, U=U, ff=ff, y2=y2, mods=(sh1, sc1, g1, sh2, sc2, g2),
                          ng=ng, cw=cw, fw=fw, gn=gn, ba2=ba2))
        xin = x2

    loss_local, dx = _loss_head(xin, target, tr=TR, name="loss_head")
    loss = lax.psum(loss_local, MESH_AXES)

    grad_streams = {}
    IN_WGRAD_ROWS = [(0, 3 * D // 4), (3 * D // 4, D // 4)]
    UP_WGRAD_ROWS = [(0, D // 2), (D // 2, D // 2)]

    rides.ici_share = BACKWARD_ICI_SHARE
    rides.min_ici_us = MIN_ICI_CARRIER_US

    def send(l, key, blocks):
        _, rows, cols = blocks.shape
        entry = {}
        tag = f"{key}{l}_{len(grad_streams.get((l, key), []))}"

        def swapped(pair):
            sums = _pair_add(blocks, pair.buf, ci, name=f"pair_add_{tag}")
            entry["chips"] = rides.add(_Stream("chips", sums, rows, cols, 2 * EXCHANGE_PARAMS_PER_US / (NDEV * cols)))

        entry["pair"] = rides.add(_Stream("pair", blocks, rows, cols, PAIR_PARAMS_PER_US / (NDEV * cols), on_done=swapped))
        grad_streams.setdefault((l, key), []).append(entry)

    def bmm(a, b, **kw):
        kw.setdefault("out_dtype", BF16)
        return _mm(a, b, rides=rides, **kw)

    gsmall = [None] * depth
    dmod = [None] * depth
    for l in reversed(range(depth)):
        s = saved[l]
        sh1, sc1, g1, sh2, sc2, g2 = s["mods"]
        ng, cw, fw = s["ng"], s["cw"], s["fw"]
        (dx1_a, dy2), (dg2, dng3) = _rw_bwd(
            _f_resid, [_v(s["x1"]), _v(s["y2"])], [_v(g2), _v(ng[3])], [_v(dx)],
            row_grads=[F32, BF16], par_grads=[True, True], tr=TR, name=f"resid2_bwd{l}", rides=rides, us=US_RESID_BWD)
        gW_down = bmm(s["ff"], dy2, ta=True, out_dtype=BF16, name=f"down_wgrad{l}")
        send(l, "w_down", gW_down.reshape(NDEV, FF // NDEV, D))
        dff = bmm(dy2, weight(l, "w_down"), tb=True, name=f"down_dgrad{l}")
        U = s["U"]
        (dgate, dup), dfw = _rw_bwd(
            _f_ffn, [_v(U, TCF, 0), _v(U, TCF, FF)], [_v(w, TCF) for w in fw], [_v(dff, TCF)],
            row_grads=[BF16, BF16], par_grads=[True] * 3, tr=TRC, ncol=FF // TCF, halo=True, name=f"ffn_bwd{l}",
            rides=rides, us=US_FFN_BWD)
        dU = jnp.concatenate([dgate, dup], axis=1)
        for part, rows in enumerate(UP_WGRAD_ROWS):
            gW_up = bmm(s["h2"], dU, ta=True, out_dtype=BF16, out_shards=True, m_rows=rows, name=f"up_wgrad{l}_{part}")
            send(l, "w_up", gW_up)
        dh2 = bmm(dU, weight(l, "w_up"), tb=True, name=f"up_dgrad{l}")
        (dx1,), (dng2, dsc2, dsh2) = _rw_bwd(
            _f_normmod, [_v(s["x1"])], [_v(ng[2]), _v(sc2), _v(sh2)], [_v(dh2)],
            row_grads=[F32], par_grads=[True] * 3, adds={0: dx1_a}, tr=TR, name=f"normmod2_bwd{l}",
            rides=rides, us=US_NORMMOD_BWD)
        (dxin_a, dy), (dg1, dng1) = _rw_bwd(
            _f_resid, [_v(s["xin"]), _v(s["y"])], [_v(g1), _v(ng[1])], [_v(dx1)],
            row_grads=[F32, BF16], par_grads=[True, True], tr=TR, name=f"resid1_bwd{l}", rides=rides, us=US_RESID_BWD)
        gW_o = bmm(s["mg"], dy, ta=True, out_dtype=BF16, name=f"o_wgrad{l}")
        send(l, "w_o", gW_o.reshape(NDEV, D // NDEV, D))
        dmg = bmm(dy, weight(l, "w_o"), tb=True, name=f"o_dgrad{l}")
        P = s["P"]
        (dga, dgb, dya, dyb), _ = _rw_bwd(
            _f_merge, [_v(P, D, col["ga"]), _v(P, D, col["gb"]), _v(s["ya"]), _v(s["yb"])], [], [_v(dmg)],
            row_grads=[BF16] * 4, par_grads=[], tr=TR, name=f"merge_bwd{l}", rides=rides, us=US_MERGE_BWD)
        gW_og = bmm(s["ya_in"], dya, ta=True, out_dtype=BF16, name=f"out_gla_wgrad{l}")
        send(l, "w_out_gla", _cols_to_shards(gW_og))
        dya_in = bmm(dya, weight(l, "w_out_gla"), tb=True, name=f"out_gla_dgrad{l}")
        gW_oc = bmm(s["yb_in"], dyb, ta=True, out_dtype=BF16, name=f"out_conv_wgrad{l}")
        send(l, "w_out_conv", _cols_to_shards(gW_oc))
        dyb_in = bmm(dyb, weight(l, "w_out_conv"), tb=True, name=f"out_conv_dgrad{l}")
        (dcb, dcc, dcx), dcw = _rw_bwd(
            _f_convmix, [_v(P, TC, col["cb"]), _v(P, TC, col["cc"]), _v(P, TC, col["cx"])],
            [_v(w, TC) for w in cw], [_v(dyb_in, TC)],
            row_grads=[BF16] * 3, par_grads=[True] * 3, tr=TRC, ncol=CW // TC, halo=True, name=f"convmix_bwd{l}")
        dvh = VW // HEADS
        (do, dr), (dgn,) = _rw_bwd(
            _f_glapost, [_v(s["o"], dvh), _v(P, dvh, col["r"])], [V(s["gn"], dvh, None)], [_v(dya_in, dvh)],
            row_grads=[BF16, BF16], par_grads=[True], tr=TR, ncol=HEADS, name=f"glapost_bwd{l}",
            rides=rides, us=US_GLAPOST_BWD)
        dq, dk, dv, da = _gla_bwd(P, s["a"], s["states"], do, qk=QK, vw=VW, name=f"gla_bwd{l}",
                                  rides=rides, us=US_GLA_BWD)
        (dlr,), (dwa2, dba2) = _rw_bwd(
            _f_loga, [_v(P, LRP, col["lr"])], [V(w_a2_p[l], QK, None), _v(s["ba2"])], [_v(da)],
            row_grads=[BF16], par_grads=[True, True], tr=TR, name=f"loga_bwd{l}")
        dP = jnp.concatenate([dq, dk, dv, dr, dcb, dcc, dcx, dga, dgb, dlr], axis=1)
        for part, rows in enumerate(IN_WGRAD_ROWS):
            gW_all = bmm(s["h1"], dP, ta=True, out_dtype=BF16, m_rows=rows, name=f"in_wgrad{l}_{part}")
            send(l, "w_in", in_grad_blocks(gW_all))
        dh1 = bmm(dP, weight(l, "w_in"), tb=True, name=f"in_dgrad{l}")
        (dx,), (dng0, dsc1, dsh1) = _rw_bwd(
            _f_normmod, [_v(s["xin"])], [_v(ng[0]), _v(sc1), _v(sh1)], [_v(dh1)],
            row_grads=[F32], par_grads=[True] * 3, adds={0: dxin_a}, tr=TR, name=f"normmod1_bwd{l}",
            rides=rides, us=US_NORMMOD_BWD)
        vec = lambda t: t.reshape(1, -1)
        gsmall[l] = dict(
            norm_g=jnp.concatenate([vec(dng0), vec(dng1), vec(dng2), vec(dng3)], axis=0),
            w_a2=dwa2[0, :LR], b_a2=dba2.reshape(QK), gla_norm_g=jnp.sum(dgn, axis=0).reshape(dvh),
            conv_mix_w=jnp.concatenate([vec(t) for t in dcw], axis=0),
            ffn_conv_w=jnp.concatenate([vec(t) for t in dfw], axis=0))
        dmod[l] = jnp.concatenate([vec(t) for t in (dsh1, dsc1, dg1, dsh2, dsc2, dg2)], axis=1).reshape(-1)
    grad_x = dx.reshape(x.shape)

    stack = lambda key, src: jnp.stack([src[l][key] for l in range(depth)])
    small_names = ["norm_g", "w_a2", "b_a2", "gla_norm_g", "conv_mix_w", "ffn_conv_w"]
    gpack = _Pack([jnp.stack(dmod)] + [stack(k, gsmall) for k in small_names])
    (gsm,) = _all_gather([gpack.packed], name="gather_small_grads")
    gs = gpack.unpack(gsm)
    dmod_all = gs[0]
    parts_small = dict(zip(small_names, gs[1:]))

    def my_cols(t, n):
        return lax.dynamic_slice_in_dim(t, me * n, n, axis=t.ndim - 1)

    gW_ada = []
    for l in range(depth):
        dcols = my_cols(dmod_all[:, l], NADA)
        dcols = jnp.concatenate([dcols, jnp.zeros((MP - NDEV, NADA), F32)], axis=0)
        gW_ada.append(bmm(silu_c, dcols, ta=True, out_dtype=F32, name=f"ada_wgrad{l}")[None])
    parts = {}
    parts["b_ada"] = dmod_all
    parts["norm_g"] = my_cols(parts_small["norm_g"], D // NDEV)
    parts["w_a2"] = my_cols(parts_small["w_a2"], QK // NDEV)
    parts["b_a2"] = parts_small["b_a2"]
    parts["gla_norm_g"] = parts_small["gla_norm_g"]
    parts["conv_mix_w"] = my_cols(parts_small["conv_mix_w"], CW // NDEV)
    parts["ffn_conv_w"] = my_cols(parts_small["ffn_conv_w"], FF // NDEV)

    weights = dict(w_ada=(w_ada, m_w_ada, v_w_ada), b_ada=(b_ada, m_b_ada, v_b_ada),
                   norm_g=(norm_g, m_norm_g, v_norm_g), w_in=(w_in, m_w_in, v_w_in), w_a2=(w_a2, m_w_a2, v_w_a2),
                   b_a2=(b_a2, m_b_a2, v_b_a2), gla_norm_g=(gla_norm_g, m_gla_norm_g, v_gla_norm_g),
                   w_out_gla=(w_out_gla, m_w_out_gla, v_w_out_gla), conv_mix_w=(conv_mix_w, m_conv_mix_w, v_conv_mix_w),
                   w_out_conv=(w_out_conv, m_w_out_conv, v_w_out_conv), w_o=(w_o, m_w_o, v_w_o),
                   w_up=(w_up, m_w_up, v_w_up), ffn_conv_w=(ffn_conv_w, m_ffn_conv_w, v_ffn_conv_w),
                   w_down=(w_down, m_w_down, v_w_down))
    done = {}
    for nm in ["w_ada", "w_down", "w_up", "w_o", "w_out_gla", "w_out_conv", "w_in"]:
        w, m, v = weights[nm]
        if nm == "w_ada":
            done[nm] = _adamw(gW_ada, w, m, v, name=f"adamw_{nm}")
            continue
        entries = [e for l in range(depth) for e in grad_streams[l, nm]]
        rides.need(*[e["pair"] for e in entries])
        streams = [e["chips"] for e in entries]
        rides.need(*streams)
        rows = math.gcd(*[s.R for s in streams])
        bufs, row0 = [], []
        for s in streams:
            for first in range(0, s.R, rows):
                bufs.append(s.buf)
                row0.append(first)
        split = lambda t: t.reshape(len(bufs), rows, t.shape[2])
        outs = _adamw(bufs, split(w), split(m), split(v), name=f"adamw_{nm}", row0=row0)
        done[nm] = [o.reshape(w.shape) for o in outs]
    for nm, (w, m, v) in weights.items():
        if nm not in done:
            done[nm] = _adamw_nd(parts[nm], w, m, v, name=f"adamw_{nm}")
    grads, deltas, new_m, new_v = zip(*[done[nm] for nm in weights])
    return (loss, grad_x, *grads, *deltas, *new_m, *new_v)
```

```python
import functools
import math
from typing import Any, NamedTuple

import jax
import jax.numpy as jnp
from jax import lax
from jax.experimental import pallas as pl
from jax.experimental.pallas import tpu as pltpu

F32 = jnp.float32
BF16 = jnp.bfloat16
MESH_AXES = ("x", "y", "c")
NDEV = 8
EPS = 1e-6
CHUNK = 64
HEADS = 4
GLA_TAU = 16.0
CONV_K = 3
ADAM_LR = 0.001
ADAM_B1 = 0.9
ADAM_B2 = 0.999
ADAM_EPS = 1e-08
ADAM_WD = 0.01
ADAM_STEP = 10

LANES = 128
SUBLANES = 8
HALO = 2 * SUBLANES
VMEM_LIMIT = 56 * 1024 * 1024


def _params(sem=None):
    return pltpu.CompilerParams(dimension_semantics=sem, vmem_limit_bytes=VMEM_LIMIT)


def _pick(n, cap, unit=LANES):
    best = None
    for d in range(unit, min(n, cap) + 1, unit):
        if n % d == 0:
            best = d
    return best if best is not None else n


def _position():
    return lax.axis_index("x"), lax.axis_index("y"), lax.axis_index("c")


def _index(p):
    return 4 * p[0] + 2 * p[1] + p[2]


def _peer(k):
    x, y, c = _position()
    return ((1 - x) if k & 4 else x, (1 - y) if k & 2 else y, (1 - c) if k & 1 else c)


ROW_QUANTUM = 2 * SUBLANES
MIN_ROWS = 128
MIN_ICI_CARRIER_US = 50.0


class _Stream:
    def __init__(self, mode, src, rows, cols, rate, after=None, on_done=None):
        self.mode, self.src, self.R, self.rate, self.after = mode, src, rows, rate, after
        self.shape = (NDEV if mode.startswith("gather") else NDEV // 2, rows, cols)
        self.holder = self if after is None else after
        self.buf = None
        self.done = 0
        self.on_done = on_done


class _Exchange:
    def __init__(self, items):
        self.items = items
        self.ins, self.src_pos, self.alias, self.holders, self.out_pos = [], [], {}, [], []
        for s, _, _ in items:
            if not any(s.holder is h for h in self.holders):
                self.holders.append(s.holder)
        for s, _, _ in items:
            self.out_pos.append([n for n, h in enumerate(self.holders) if h is s.holder][0])
            if s.mode == "gather_b":
                self.src_pos.append(None)
            else:
                self.src_pos.append(len(self.ins))
                self.ins.append(s.src)
        for n, h in enumerate(self.holders):
            if h.buf is not None:
                self.alias[len(self.ins)] = n
                self.ins.append(h.buf)
        self.out_shape = [jax.ShapeDtypeStruct(h.shape, BF16) for h in self.holders]
        n = len(items)
        self.scratch = [pltpu.SemaphoreType.DMA((n, NDEV)), pltpu.SemaphoreType.DMA((n, NDEV)),
                        pltpu.SemaphoreType.DMA((n,))]

    def deliver(self, outs):
        for h, o in zip(self.holders, outs):
            h.buf = o

    def _copies(self, ins, outs, sems):
        send_sems, recv_sems, local_sems = sems
        me = _index(_position())
        slot = lambda k: _index(_peer(k))
        local, sends, arrivals = [], [], []
        for n, (s, r0, rc) in enumerate(self.items):
            out, rows = outs[self.out_pos[n]], pl.ds(r0, rc)
            src = out if self.src_pos[n] is None else ins[self.src_pos[n]]
            if s.mode == "pair":
                other = 1 - _position()[2]
                moves = [(q, src.at[2 * q + other, rows], out.at[q, rows], _peer(1), out.at[q, rows])
                         for q in range(NDEV // 2)]
            elif s.mode == "chips":
                chip = lambda k: slot(k) // 2
                local.append(pltpu.make_async_copy(src.at[me // 2, rows], out.at[me // 2, rows], local_sems.at[n]))
                moves = [(k, src.at[chip(k), rows], out.at[me // 2, rows], _peer(k), out.at[chip(k), rows])
                         for k in (2, 4, 6)]
            elif s.mode == "gather_a":
                local.append(pltpu.make_async_copy(src.at[rows], out.at[me, rows], local_sems.at[n]))
                moves = [(k, src.at[rows], out.at[me, rows], _peer(k), out.at[slot(k), rows]) for k in (1, 2, 4, 6)]
            else:
                moves = [(k, src.at[slot(k), rows], out.at[slot(k), rows], _peer(1), out.at[slot(k ^ 1), rows])
                         for k in (2, 4, 6)]
            for k, src_ref, dst_ref, to, landing in moves:
                common = dict(send_sem=send_sems.at[n, k], recv_sem=recv_sems.at[n, k],
                              device_id=to, device_id_type=pl.DeviceIdType.MESH)
                sends.append(pltpu.make_async_remote_copy(src_ref=src_ref, dst_ref=dst_ref, **common))
                arrivals.append(pltpu.make_async_remote_copy(src_ref=landing, dst_ref=landing, **common))
        return local, sends, arrivals

    def start(self, ins, outs, sems):
        local, sends, _ = self._copies(ins, outs, sems)
        for cp in local + sends:
            cp.start()

    def wait(self, ins, outs, sems):
        local, sends, arrivals = self._copies(ins, outs, sems)
        for cp in arrivals:
            cp.wait_recv()
        for cp in sends:
            cp.wait_send()
        for cp in local:
            cp.wait()


class _Rides:
    def __init__(self):
        self.queue = []
        self.finished = []
        self.flushes = 0
        self.ici_share = 1.0
        self.min_ici_us = 0.0

    def add(self, stream):
        self.queue.append(stream)
        return stream

    def take(self, budget, only=None, ici=True):
        ready = {id(s): (s.after.done if s.after is not None else s.R) for s in self.queue}
        left = {"ici": budget * self.ici_share if ici else 0.0, "d2d": budget}
        items = []
        for s in list(self.queue):
            if only is not None and not any(s is t for t in only):
                continue
            link = "d2d" if s.mode in ("gather_b", "pair") else "ici"
            avail = ready[id(s)] - s.done
            if avail <= 0:
                continue
            rc = avail if budget == float("inf") else min(avail, int(left[link] * s.rate) // ROW_QUANTUM * ROW_QUANTUM)
            if rc < min(avail, MIN_ROWS):
                continue
            if avail - rc <= 2 * MIN_ROWS:
                rc = avail
            items.append((s, s.done, rc))
            s.done += rc
            left[link] -= rc / s.rate
            if s.done == s.R:
                self.queue.remove(s)
                self.finished.append(s)
        return items

    def fire(self):
        finished, self.finished = self.finished, []
        for s in finished:
            if s.on_done is not None:
                s.on_done(s)

    def need(self, *streams):
        while any(s.done < s.R for s in streams):
            items = self.take(float("inf"), only=streams)
            comm = _Exchange(items)
            hbm = pl.BlockSpec(memory_space=pl.ANY)
            nin, nout = len(comm.ins), len(comm.holders)

            def body(*refs, comm=comm, nin=nin, nout=nout):
                comm.start(refs[:nin], refs[nin:nin + nout], refs[nin + nout:])
                comm.wait(refs[:nin], refs[nin:nin + nout], refs[nin + nout:])

            outs = pl.pallas_call(body, in_specs=[hbm] * nin, out_specs=[hbm] * nout, out_shape=comm.out_shape,
                                  scratch_shapes=comm.scratch, input_output_aliases=dict(comm.alias),
                                  name=f"exchange_alone{self.flushes}")(*comm.ins)
            self.flushes += 1
            comm.deliver(outs)
            self.fire()


def _call(body, *, grid, in_specs, out_specs, out_shape, scratch_shapes=(), args, name, rides=None, us=0.0):
    sem = ("arbitrary",) * len(grid)
    items = rides.take(us, ici=us >= rides.min_ici_us) if rides is not None else []
    if not items:
        return pl.pallas_call(body, grid=grid, in_specs=in_specs, out_specs=out_specs, out_shape=out_shape,
                              scratch_shapes=list(scratch_shapes), compiler_params=_params(sem), name=name)(*args)
    comm = _Exchange(items)
    nin, nout, nscr, cin, cout = len(in_specs), len(out_specs), len(scratch_shapes), len(comm.ins), len(comm.holders)
    hbm = pl.BlockSpec(memory_space=pl.ANY)

    def both(*refs):
        ins, c_ins = refs[:nin], refs[nin:nin + cin]
        pos = nin + cin
        outs, c_outs = refs[pos:pos + nout], refs[pos + nout:pos + nout + cout]
        pos += nout + cout
        scr, sems = refs[pos:pos + nscr], refs[pos + nscr:]
        first = functools.reduce(lambda p, q: p & q, [pl.program_id(d) == 0 for d in range(len(grid))])
        last = functools.reduce(lambda p, q: p & q, [pl.program_id(d) == grid[d] - 1 for d in range(len(grid))])

        @pl.when(first)
        def _():
            comm.start(c_ins, c_outs, sems)

        body(*ins, *outs, *scr)

        @pl.when(last)
        def _():
            comm.wait(c_ins, c_outs, sems)

    res = pl.pallas_call(both, grid=grid, in_specs=list(in_specs) + [hbm] * cin, out_specs=list(out_specs) + [hbm] * cout,
                         out_shape=list(out_shape) + comm.out_shape, scratch_shapes=list(scratch_shapes) + comm.scratch,
                         input_output_aliases={nin + i: nout + o for i, o in comm.alias.items()},
                         compiler_params=_params(sem), name=name)(*args, *comm.ins)
    comm.deliver(res[nout:])
    rides.fire()
    return res[:nout]


US_GLA, US_GLAPOST, US_MERGE, US_FFN, US_RESID, US_NORMMOD = 100.0, 40.0, 45.0, 110.0, 40.0, 24.0
US_FFN_BWD, US_GLA_BWD, US_MERGE_BWD, US_RESID_BWD, US_NORMMOD_BWD, US_GLAPOST_BWD = 215.0, 118.0, 57.0, 57.0, 54.0, 52.0
MM_VMEM_BUDGET = 44 * 1024 * 1024
MM_FLOPS_PER_US = 780e6


def _mm(a, b, *, ta=False, tb=False, out_dtype=F32, name, rides=None, out_shards=False, m_rows=None, tall=False):
    if ta:
        K, M = a.shape
    else:
        M, K = a.shape
    m_first, M = (0, M) if m_rows is None else m_rows
    sharded = b.ndim == 3
    if sharded:
        ns, n = b.shape[0], b.shape[2]
        Kb, N = (ns * n, b.shape[1]) if tb else (b.shape[1], ns * n)
    elif tb:
        N, Kb = b.shape
    else:
        Kb, N = b.shape
    assert K == Kb, (a.shape, b.shape, ta, tb)
    tm = _pick(M, 2048 if tall else 1024)
    tn = N // NDEV if out_shards else n if (sharded and not tb) else _pick(N, 1152)
    sa, sb, so = a.dtype.itemsize, b.dtype.itemsize, jnp.dtype(out_dtype).itemsize

    def vmem(tk):
        blocks = 2 * (tm * tk * sa + tk * tn * sb) + 2 * tm * tn * so + (tm * tn * 4 if tk < K else 0)
        return blocks + (tm * K * 2 if ta else 0)

    if sharded and tb:
        tk = n
    else:
        tk = K
        if vmem(K) > MM_VMEM_BUDGET:
            fits = [d for d in range(LANES, K, LANES) if K % d == 0 and vmem(d) <= MM_VMEM_BUDGET]
            tk = max(fits) if fits else _pick(K, 512)
    nk = K // tk
    dims = (((1,), (1 if tb else 0,)), ((), ()))

    def dot(a_ref, b_ref, at_ref):
        if ta:
            k = pl.program_id(2)

            @pl.when(pl.program_id(1) == 0)
            def _():
                at_ref[k] = a_ref[...].astype(BF16).T

            lhs = at_ref[k]
        else:
            lhs = a_ref[...].astype(BF16)
        return lax.dot_general(lhs, b_ref[...].astype(BF16), dims, preferred_element_type=F32)

    def body_one(a_ref, b_ref, o_ref, *at_ref):
        o_ref[...] = dot(a_ref, b_ref, *at_ref or (None,)).astype(o_ref.dtype)

    def body_acc(a_ref, b_ref, o_ref, acc_ref, *at_ref):
        k = pl.program_id(2)
        at = (at_ref or (None,))[0]

        @pl.when(k == 0)
        def _():
            acc_ref[...] = dot(a_ref, b_ref, at)

        @pl.when((k > 0) & (k < nk - 1))
        def _():
            acc_ref[...] += dot(a_ref, b_ref, at)

        @pl.when(k == nk - 1)
        def _():
            o_ref[...] = (acc_ref[...] + dot(a_ref, b_ref, at)).astype(o_ref.dtype)

    assert m_first % tm == 0
    i0 = m_first // tm
    a_spec = (pl.BlockSpec((tk, tm), lambda i, j, k: (k, i0 + i)) if ta
              else pl.BlockSpec((tm, tk), lambda i, j, k: (i0 + i, k)))
    if sharded:
        b_spec = (pl.BlockSpec((None, tn, tk), lambda i, j, k: (k, j, 0)) if tb
                  else pl.BlockSpec((None, tk, tn), lambda i, j, k: (j, k, 0)))
    else:
        b_spec = (pl.BlockSpec((tn, tk), lambda i, j, k: (j, k)) if tb
                  else pl.BlockSpec((tk, tn), lambda i, j, k: (k, j)))
    if out_shards:
        out_spec = pl.BlockSpec((None, tm, tn), lambda i, j, k: (j, i, 0))
        out_shape = jax.ShapeDtypeStruct((NDEV, M, tn), out_dtype)
    else:
        out_spec = pl.BlockSpec((tm, tn), lambda i, j, k: (i, j))
        out_shape = jax.ShapeDtypeStruct((M, N), out_dtype)
    (out,) = _call(
        body_one if nk == 1 else body_acc, grid=(M // tm, N // tn, nk), in_specs=[a_spec, b_spec],
        out_specs=[out_spec], out_shape=[out_shape],
        scratch_shapes=([] if nk == 1 else [pltpu.VMEM((tm, tn), F32)]) + ([pltpu.VMEM((nk, tm, tk), BF16)] if ta else []),
        args=(a, b), name=name,
        rides=rides, us=2.0 * M * N * K / MM_FLOPS_PER_US)
    return out


class V(NamedTuple):
    arr: Any
    width: int
    base: Any


def _v(arr, width=None, col0=0):
    width = arr.shape[-1] if width is None else width
    assert col0 % width == 0
    return V(arr, width, col0 // width)


def _row_specs(v, tr, nrow8, halo):
    main = pl.BlockSpec((tr, v.width), lambda j, i, b=v.base: (i, b + j))
    if not halo:
        return [main]
    per = tr // HALO
    prev = pl.BlockSpec((HALO, v.width), lambda j, i, b=v.base: (jnp.maximum(i * per - 1, 0), b + j))
    nxt = pl.BlockSpec((HALO, v.width), lambda j, i, b=v.base: (jnp.minimum((i + 1) * per, nrow8 - 1), b + j))
    return [prev, main, nxt]


def _par_spec(p):
    if p.base is None:
        return pl.BlockSpec(p.arr.shape, lambda j, i: (0,) * p.arr.ndim)
    return pl.BlockSpec((p.arr.shape[0], p.width), lambda j, i, b=p.base: (0, b + j))


def _load_rows(refs, i, nrow, halo):
    if not halo:
        return refs[0][...].astype(F32)
    prev, main, nxt = refs
    pv = jnp.where(i > 0, prev[...].astype(F32), 0.0)
    nv = jnp.where(i < nrow - 1, nxt[...].astype(F32), 0.0)
    return jnp.concatenate([pv, main[...].astype(F32), nv], axis=0)


def _center_mask(tr):
    row = lax.broadcasted_iota(jnp.int32, (tr + 2 * HALO, 1), 0)
    return ((row >= HALO) & (row < HALO + tr)).astype(F32)


def _rw_fwd(f, rows, params, outs, *, tr, ncol=1, halo=False, name, rides=None, us=0.0):
    S = rows[0].arr.shape[0]
    tr = min(tr, S)
    nrow = S // tr
    per = 3 if halo else 1
    nr, npar = len(rows), len(params)

    def body(*refs):
        i = pl.program_id(1)
        vals = [_load_rows(refs[per * r: per * (r + 1)], i, nrow, halo) for r in range(nr)]
        pvals = [refs[per * nr + p][...].astype(F32) for p in range(npar)]
        out_refs = refs[per * nr + npar:]
        res = f(*vals, *pvals, _center_mask(tr)) if halo else f(*vals, *pvals)
        for r, o in zip(res, out_refs):
            o[...] = (r[HALO:HALO + tr] if halo else r).astype(o.dtype)

    in_specs = []
    for v in rows:
        in_specs += _row_specs(v, tr, S // HALO, halo)
    in_specs += [_par_spec(p) for p in params]
    args = []
    for v in rows:
        args += [v.arr] * per
    args += [p.arr for p in params]
    return _call(
        body, grid=(ncol, nrow), in_specs=in_specs,
        out_specs=[pl.BlockSpec((tr, tot // ncol), lambda j, i: (i, j)) for tot, _ in outs],
        out_shape=[jax.ShapeDtypeStruct((S, tot), dt) for tot, dt in outs],
        args=args, name=name, rides=rides, us=us)


def _rw_bwd(f, rows, params, douts, *, row_grads, par_grads, adds=None, tr, ncol=1, halo=False, name,
            rides=None, us=0.0):
    S = rows[0].arr.shape[0]
    tr = min(tr, S)
    nrow = S // tr
    per = 3 if halo else 1
    adds = adds or {}
    nr, npar, nd = len(rows), len(params), len(douts)
    add_keys = sorted(adds)
    rg_idx = [k for k in range(nr) if row_grads[k]]
    pg_idx = [k for k in range(npar) if par_grads[k]]

    def body(*refs):
        i = pl.program_id(1)
        pos = 0
        vals = [_load_rows(refs[per * r: per * (r + 1)], i, nrow, halo) for r in range(nr)]
        pos = per * nr
        pvals = [refs[pos + p][...].astype(F32) for p in range(npar)]
        pos += npar
        dvals = [_load_rows(refs[pos + per * d: pos + per * (d + 1)], i, nrow, halo) for d in range(nd)]
        pos += per * nd
        add_vals = {k: refs[pos + n][...].astype(F32) for n, k in enumerate(add_keys)}
        pos += len(add_keys)
        rg_refs = refs[pos: pos + len(rg_idx)]
        pg_refs = refs[pos + len(rg_idx):]
        if halo:
            center = _center_mask(tr)
            fun = lambda *a: f(*a, center)
        else:
            fun = f
        _, vjp = jax.vjp(fun, *vals, *pvals)
        grads = vjp(tuple(dvals))
        for ref, k in zip(rg_refs, rg_idx):
            g = grads[k]
            g = g[HALO:HALO + tr] if halo else g
            if k in add_vals:
                g = g + add_vals[k]
            ref[...] = g.astype(ref.dtype)
        for ref, k in zip(pg_refs, pg_idx):
            @pl.when(i == 0)
            def _(ref=ref):
                ref[...] = jnp.zeros_like(ref)
            ref[...] += grads[nr + k]

    in_specs, args = [], []
    for v in list(rows):
        in_specs += _row_specs(v, tr, S // HALO, halo)
        args += [v.arr] * per
    in_specs += [_par_spec(p) for p in params]
    args += [p.arr for p in params]
    for v in douts:
        in_specs += _row_specs(v, tr, S // HALO, halo)
        args += [v.arr] * per
    for k in add_keys:
        in_specs += _row_specs(_v(adds[k], rows[k].width), tr, S // HALO, False)
        args += [adds[k]]
    out_specs, out_shape = [], []
    for k in rg_idx:
        w = rows[k].width
        out_specs.append(pl.BlockSpec((tr, w), lambda j, i: (i, j)))
        out_shape.append(jax.ShapeDtypeStruct((S, ncol * w), row_grads[k]))
    for k in pg_idx:
        p = params[k]
        blk = p.arr.shape if p.base is None else (p.arr.shape[0], p.width)
        out_specs.append(pl.BlockSpec((None,) + tuple(blk), lambda j, i, n=len(blk): (j,) + (0,) * n))
        out_shape.append(jax.ShapeDtypeStruct((ncol,) + tuple(blk), F32))
    res = _call(body, grid=(ncol, nrow), in_specs=in_specs, out_specs=out_specs, out_shape=out_shape, args=args, name=name,
                rides=rides, us=us)
    rg = [None] * nr
    for n, k in enumerate(rg_idx):
        rg[k] = res[n]
    pg = [None] * npar
    for n, k in enumerate(pg_idx):
        pg[k] = res[len(rg_idx) + n]
    return rg, pg


def _sigmoid(x):
    return 1.0 / (1.0 + jnp.exp(-x))


def _rms(x, g):
    return x * lax.rsqrt(jnp.mean(x * x, axis=-1, keepdims=True) + EPS) * g


def _f_normmod(x, g, sc, sh):
    return (_rms(x, g) * (1.0 + sc) + sh,)


def _f_resid(x, y, gate, g):
    return (x + gate * _rms(y, g),)


def _f_merge(ga, gb, ya, yb):
    return (_sigmoid(ga) * ya + _sigmoid(gb) * yb,)


def _f_glapost(o, r, g):
    return (_rms(o, g) * (r * _sigmoid(r)),)


def _roll_rows(x, k):
    return pltpu.roll(x, k % x.shape[0], 0)


@functools.partial(jax.custom_vjp, nondiff_argnums=(1,))
def _shift(x, k):
    return _roll_rows(x, k)


def _shift_fwd(x, k):
    return _roll_rows(x, k), None


def _shift_bwd(k, _, g):
    return (_roll_rows(g, -k),)


_shift.defvjp(_shift_fwd, _shift_bwd)


def _taps(w, center):
    return jnp.where(center > 0.0, w, lax.stop_gradient(w))


def _conv3(u, w0, w1, w2, center):
    return (_taps(w0, center) * _shift(u, 2) + _taps(w1, center) * _shift(u, 1)) + _taps(w2, center) * u


def _f_convmix(cb, cc, cx, w0, w1, w2, center):
    return (cb * _conv3(cc * cx, w0, w1, w2, center),)


def _gelu_tanh(x):
    return 0.5 * x * (1.0 + jnp.tanh(math.sqrt(2.0 / math.pi) * (x + 0.044715 * (x * x * x))))


def _f_ffn(gate, up, w0, w1, w2, center):
    return (_gelu_tanh(_conv3(gate, w0, w1, w2, center)) * up,)


@jax.custom_vjp
def _bdot(a, b):
    return jnp.dot(a.astype(BF16), b.astype(BF16), preferred_element_type=F32)


def _bdot_fwd(a, b):
    return _bdot(a, b), (a, b)


def _bdot_bwd(res, g):
    a, b = res
    gb = g.astype(BF16)
    da = lax.dot_general(gb, b.astype(BF16), (((1,), (1,)), ((), ())), preferred_element_type=F32)
    db = lax.dot_general(a.astype(BF16), gb, (((0,), (0,)), ((), ())), preferred_element_type=F32)
    return da, db


_bdot.defvjp(_bdot_fwd, _bdot_bwd)


@jax.custom_vjp
def _log_sigmoid(z):
    return jnp.minimum(z, 0.0) - jnp.log(1.0 + jnp.exp(-jnp.abs(z)))


def _log_sigmoid_fwd(z):
    return _log_sigmoid(z), z


def _log_sigmoid_bwd(z, g):
    return (g * _sigmoid(-z),)


_log_sigmoid.defvjp(_log_sigmoid_fwd, _log_sigmoid_bwd)


def _f_loga(lr, w, b):
    return (_log_sigmoid(_bdot(lr, w) + b) / GLA_TAU,)


def _split3(x):
    hi = x.astype(BF16)
    r1 = x - hi.astype(F32)
    mid = r1.astype(BF16)
    lo = (r1 - mid.astype(F32)).astype(BF16)
    return hi, mid, lo


def _dot_exact(a01, x, dims):
    hi, mid, lo = _split3(x)
    d = lambda p: lax.dot_general(a01, p, dims, preferred_element_type=F32)
    return (d(lo) + d(mid)) + d(hi)


def _dot_exact_lhs(x, a01, dims):
    hi, mid, lo = _split3(x)
    d = lambda p: lax.dot_general(p, a01, dims, preferred_element_type=F32)
    return (d(lo) + d(mid)) + d(hi)


GLA_CHUNKS_PER_STEP = 8

NN = (((1,), (0,)), ((), ()))
NT = (((1,), (1,)), ((), ()))
TN = (((0,), (0,)), ((), ()))


def _bf_dot(a, b, dims):
    return lax.dot_general(a.astype(BF16), b.astype(BF16), dims, preferred_element_type=F32)


def _chunk_decay(a, dv):
    L = a.shape[0]
    r = lax.broadcasted_iota(jnp.int32, (L, L), 0)
    c = lax.broadcasted_iota(jnp.int32, (L, L), 1)
    tri = (r >= c).astype(BF16)
    cum = _dot_exact(tri, a, NN)
    ce = cum[L - 1:L, :]
    dec = jnp.exp(_dot_exact_lhs(a, jnp.ones((L, dv), BF16), TN))
    return cum, ce, jnp.exp(ce - cum), dec


def _gla_fwd(P, a, *, qk, vw, name, rides=None, us=0.0):
    S = P.shape[0]
    nc = S // CHUNK
    cps = math.gcd(nc, GLA_CHUNKS_PER_STEP)
    rows = cps * CHUNK
    dk, dv = qk // HEADS, vw // HEADS
    scale = dk ** -0.5

    def body(q_ref, k_ref, v_ref, a_ref, o_ref, st_ref, state):
        @pl.when(pl.program_id(0) == 0)
        def _():
            state[...] = jnp.zeros_like(state)

        for h in range(HEADS):
            ks, vs = slice(h * dk, (h + 1) * dk), slice(h * dv, (h + 1) * dv)
            s = state[h]
            for c in range(cps):
                rs = slice(c * CHUNK, (c + 1) * CHUNK)
                _, _, E, dec = _chunk_decay(a_ref[rs, ks], dv)
                kd = k_ref[rs, ks] * E
                s = dec * s + _bf_dot(kd, v_ref[rs, vs], TN)
                st_ref[c, h] = s
                o_ref[rs, vs] = _bf_dot(q_ref[rs, ks] * scale, s, NN).astype(o_ref.dtype)
            state[h] = s

    return _call(
        body, grid=(nc // cps,),
        in_specs=[pl.BlockSpec((rows, qk), lambda i: (i, 0)),
                  pl.BlockSpec((rows, qk), lambda i: (i, 1)),
                  pl.BlockSpec((rows, vw), lambda i: (i, 2 * qk // vw)),
                  pl.BlockSpec((rows, qk), lambda i: (i, 0))],
        out_specs=[pl.BlockSpec((rows, vw), lambda i: (i, 0)),
                   pl.BlockSpec((cps, HEADS, dk, dv), lambda i: (i, 0, 0, 0))],
        out_shape=[jax.ShapeDtypeStruct((S, vw), BF16), jax.ShapeDtypeStruct((nc, HEADS, dk, dv), F32)],
        scratch_shapes=[pltpu.VMEM((HEADS, dk, dv), F32)], args=(P, P, P, a), name=name, rides=rides, us=us)


def _gla_bwd(P, a, states, do, *, qk, vw, name, rides=None, us=0.0):
    S = P.shape[0]
    nc = S // CHUNK
    cps = math.gcd(nc, GLA_CHUNKS_PER_STEP)
    rows = cps * CHUNK
    nstep = nc // cps
    dk, dv = qk // HEADS, vw // HEADS
    scale = dk ** -0.5

    def body(q_ref, k_ref, v_ref, a_ref, do_ref, sc_ref, sp_ref, dq_ref, dk_ref, dv_ref, da_ref, ds_carry):
        i = pl.program_id(0)

        @pl.when(i == 0)
        def _():
            ds_carry[...] = jnp.zeros_like(ds_carry)

        has_prev = (i < nstep - 1).astype(F32)
        L = CHUNK
        r = lax.broadcasted_iota(jnp.int32, (L, L), 0)
        c = lax.broadcasted_iota(jnp.int32, (L, L), 1)
        tri_t = (c >= r).astype(BF16)
        last = (lax.broadcasted_iota(jnp.int32, (L, 1), 0) == L - 1).astype(F32)
        for h in range(HEADS):
            ks, vs = slice(h * dk, (h + 1) * dk), slice(h * dv, (h + 1) * dv)
            carry = ds_carry[h]
            for n in reversed(range(cps)):
                rs = slice(n * CHUNK, (n + 1) * CHUNK)
                _, _, E, dec = _chunk_decay(a_ref[rs, ks], dv)
                kk, vv, dov = k_ref[rs, ks], v_ref[rs, vs], do_ref[rs, vs]
                kd = kk * E
                ds = carry + _bf_dot(q_ref[rs, ks] * scale, dov, TN)
                dq_ref[rs, ks] = (_bf_dot(dov, sc_ref[n, h], NT) * scale).astype(dq_ref.dtype)
                dkd = _bf_dot(vv, ds, NT)
                dv_ref[rs, vs] = _bf_dot(kd, ds, NN).astype(dv_ref.dtype)
                dk_ref[rs, ks] = (dkd * E).astype(dk_ref.dtype)
                darg = dkd * kd
                before = sc_ref[n - 1, h] if n > 0 else sp_ref[0, h] * has_prev
                w = ds * before * dec
                dce = _dot_exact(jnp.ones((SUBLANES, dv), BF16), w, NT)[0:1, :]
                dce = dce + jnp.sum(darg, axis=0, keepdims=True)
                dcum = last * dce - darg
                da_ref[rs, ks] = _dot_exact(tri_t, dcum, NN)
                carry = dec * ds
            ds_carry[h] = carry

    rev = lambda i: nstep - 1 - i
    return _call(
        body, grid=(nstep,),
        in_specs=[pl.BlockSpec((rows, qk), lambda i: (rev(i), 0)),
                  pl.BlockSpec((rows, qk), lambda i: (rev(i), 1)),
                  pl.BlockSpec((rows, vw), lambda i: (rev(i), 2 * qk // vw)),
                  pl.BlockSpec((rows, qk), lambda i: (rev(i), 0)),
                  pl.BlockSpec((rows, vw), lambda i: (rev(i), 0)),
                  pl.BlockSpec((cps, HEADS, dk, dv), lambda i: (rev(i), 0, 0, 0)),
                  pl.BlockSpec((1, HEADS, dk, dv), lambda i: (jnp.maximum(rev(i) * cps - 1, 0), 0, 0, 0))],
        out_specs=[pl.BlockSpec((rows, qk), lambda i: (rev(i), 0)),
                   pl.BlockSpec((rows, qk), lambda i: (rev(i), 0)),
                   pl.BlockSpec((rows, vw), lambda i: (rev(i), 0)),
                   pl.BlockSpec((rows, qk), lambda i: (rev(i), 0))],
        out_shape=[jax.ShapeDtypeStruct((S, qk), BF16), jax.ShapeDtypeStruct((S, qk), BF16),
                   jax.ShapeDtypeStruct((S, vw), BF16), jax.ShapeDtypeStruct((S, qk), F32)],
        scratch_shapes=[pltpu.VMEM((HEADS, dk, dv), F32)], args=(P, P, P, a, do, states, states), name=name,
        rides=rides, us=us)


def _loss_head(y, target, *, tr, name):
    S, D = y.shape
    tr = min(tr, S)

    def body(y_ref, t_ref, l_ref, dy_ref):
        @pl.when(pl.program_id(0) == 0)
        def _():
            l_ref[...] = jnp.zeros_like(l_ref)

        e = y_ref[...] - t_ref[...]
        dy_ref[...] = e / D
        l_ref[...] += 0.5 * jnp.sum(jnp.mean(e * e, axis=-1, keepdims=True), axis=0, keepdims=True)

    loss, dy = pl.pallas_call(
        body, grid=(S // tr,),
        in_specs=[pl.BlockSpec((tr, D), lambda i: (i, 0))] * 2,
        out_specs=[pl.BlockSpec((SUBLANES, LANES), lambda i: (0, 0)), pl.BlockSpec((tr, D), lambda i: (i, 0))],
        out_shape=[jax.ShapeDtypeStruct((SUBLANES, LANES), F32), jax.ShapeDtypeStruct((S, D), F32)],
        compiler_params=_params(("arbitrary",)), name=name)(y, target)
    return loss[0, 0], dy


ADAMW_BYTES_PER_US = 2.5e6


def _adamw(parts, w, m, v, *, name, rides=None, row0=None):
    G, R, C = w.shape
    n = parts[0].shape[0]
    row0 = [0] * G if row0 is None else row0
    unit = 2 * SUBLANES if parts[0].dtype == BF16 else SUBLANES
    tr = _pick(R, max(unit, (256 * 1024) // C), unit)
    nblk = R // tr
    assert all(r % tr == 0 for r in row0)
    bc1 = 1.0 - ADAM_B1 ** ADAM_STEP
    bc2 = 1.0 - ADAM_B2 ** ADAM_STEP

    def body(*refs):
        p_refs = refs[:G]
        w_ref, m_ref, v_ref, g_ref, d_ref, mo_ref, vo_ref = refs[G:]
        for q in range(G):
            @pl.when(pl.program_id(0) == q)
            def _(q=q):
                g = p_refs[q][0].astype(F32)
                for s in range(1, n):
                    g = g + p_refs[q][s].astype(F32)
                m2 = ADAM_B1 * m_ref[...] + (1.0 - ADAM_B1) * g
                v2 = ADAM_B2 * v_ref[...] + (1.0 - ADAM_B2) * (g * g)
                g_ref[...] = g
                mo_ref[...] = m2
                vo_ref[...] = v2
                d_ref[...] = -ADAM_LR * ((m2 / bc1) / (jnp.sqrt(v2 / bc2) + ADAM_EPS) + ADAM_WD * w_ref[...])

    def part_spec(q):
        first = row0[q] // tr
        return pl.BlockSpec((n, tr, C), lambda g, i: (0, first + jnp.where(g == q, i, jnp.where(g < q, 0, nblk - 1)), 0))

    blk = pl.BlockSpec((None, tr, C), lambda g, i: (g, i, 0))
    traffic = G * R * C * (n * parts[0].dtype.itemsize + 7 * 4)
    return _call(
        body, grid=(G, nblk), in_specs=[part_spec(q) for q in range(G)] + [blk, blk, blk],
        out_specs=[blk] * 4, out_shape=[jax.ShapeDtypeStruct((G, R, C), F32)] * 4,
        args=(*parts, w, m, v), name=name, rides=rides, us=traffic / ADAMW_BYTES_PER_US)


def _pair_add(blocks, received, core, *, name):
    _, R, C = blocks.shape
    tr = _pick(R, max(ROW_QUANTUM, (1024 * 1024) // C), ROW_QUANTUM)

    def body(core_ref, mine_ref, theirs_ref, o_ref):
        o_ref[...] = (mine_ref[...].astype(F32) + theirs_ref[...].astype(F32)).astype(o_ref.dtype)

    grid_spec = pltpu.PrefetchScalarGridSpec(
        num_scalar_prefetch=1, grid=(NDEV // 2, R // tr),
        in_specs=[pl.BlockSpec((None, tr, C), lambda q, i, core_ref: (2 * q + core_ref[0], i, 0)),
                  pl.BlockSpec((None, tr, C), lambda q, i, core_ref: (q, i, 0))],
        out_specs=pl.BlockSpec((None, tr, C), lambda q, i, core_ref: (q, i, 0)))
    return pl.pallas_call(body, grid_spec=grid_spec, out_shape=jax.ShapeDtypeStruct((NDEV // 2, R, C), BF16),
                          compiler_params=_params(("arbitrary", "arbitrary")), name=name)(
        jnp.reshape(core, (1,)).astype(jnp.int32), blocks, received)


def _adamw_nd(parts, w, m, v, *, name):
    shape = w.shape
    C = shape[-1]
    R = math.prod(shape[:-1])
    outs = _adamw([parts.reshape(parts.shape[0], R, C)], w.reshape(1, R, C), m.reshape(1, R, C), v.reshape(1, R, C),
                  name=name)
    return [o.reshape(shape) for o in outs]


def _all_gather(arrs, *, name):
    n = len(arrs)
    HBM = pl.BlockSpec(memory_space=pl.ANY)

    def body(*refs):
        ins, outs = refs[:n], refs[n:2 * n]
        send_sems, recv_sems, local_sems = refs[2 * n:]
        x, y, c = _position()
        me, sibling = (x, y, c), (x, y, 1 - c)
        chips = [(1 - x, y), (x, 1 - y), (1 - x, 1 - y)]

        def copy(t, k, block, to, src=None):
            rows = outs[t].at[_index(block)]
            return pltpu.make_async_remote_copy(
                src_ref=rows if src is None else src, dst_ref=rows,
                send_sem=send_sems.at[t, k], recv_sem=recv_sems.at[t, k],
                device_id=to, device_id_type=pl.DeviceIdType.MESH)

        started = []
        mine = []
        for t in range(n):
            cp = pltpu.make_async_copy(ins[t], outs[t].at[_index(me)], local_sems.at[t])
            cp.start()
            mine.append(cp)
            first = [copy(t, 0, me, sibling, src=ins[t])]
            first += [copy(t, 1 + j, me, (*chip, c), src=ins[t]) for j, chip in enumerate(chips)]
            for cp in first:
                cp.start()
            started += first
        for t in range(n):
            for j, chip in enumerate(chips):
                copy(t, 1 + j, (*chip, c), me).wait_recv()
                cp = copy(t, 4 + j, (*chip, c), sibling)
                cp.start()
                started.append(cp)
        for t in range(n):
            copy(t, 0, sibling, me).wait_recv()
            for j, chip in enumerate(chips):
                copy(t, 4 + j, (*chip, 1 - c), me).wait_recv()
        for cp in started:
            cp.wait_send()
        for cp in mine:
            cp.wait()

    return pl.pallas_call(
        body, in_specs=[HBM] * n, out_specs=[HBM] * n,
        out_shape=[jax.ShapeDtypeStruct((NDEV,) + a.shape, a.dtype) for a in arrs],
        scratch_shapes=[pltpu.SemaphoreType.DMA((n, 7)), pltpu.SemaphoreType.DMA((n, 7)),
                        pltpu.SemaphoreType.DMA((n,))],
        name=name)(*arrs)


class _Pack:
    def __init__(self, arrs):
        self.shapes = [a.shape for a in arrs]
        self.sizes = [math.prod(s) for s in self.shapes]
        total = sum(self.sizes)
        unit = SUBLANES * LANES
        self.padded = -(-total // unit) * unit
        flat = jnp.concatenate([a.reshape(-1).astype(F32) for a in arrs] + [jnp.zeros((self.padded - total,), F32)])
        self.packed = flat.reshape(self.padded // LANES, LANES)

    def unpack(self, gathered):
        flat = gathered.reshape(NDEV, self.padded)
        out, off = [], 0
        for shape, size in zip(self.shapes, self.sizes):
            out.append(flat[:, off:off + size].reshape((NDEV,) + tuple(shape)))
            off += size
        return out


def _cols_from_shards(g):
    g = jnp.moveaxis(g, 0, -2)
    return g.reshape(g.shape[:-2] + (g.shape[-2] * g.shape[-1],))


def _cols_to_shards(w):
    K, N = w.shape
    return jnp.moveaxis(w.reshape(K, NDEV, N // NDEV), 1, 0)


def kernel(x, c, w_ada, b_ada, norm_g, w_in, w_a2, b_a2, gla_norm_g, w_out_gla, conv_mix_w, w_out_conv, w_o, w_up, ffn_conv_w, w_down, loss_target, m_w_ada, m_b_ada, m_norm_g, m_w_in, m_w_a2, m_b_a2, m_gla_norm_g, m_w_out_gla, m_conv_mix_w, m_w_out_conv, m_w_o, m_w_up, m_ffn_conv_w, m_w_down, v_w_ada, v_b_ada, v_norm_g, v_w_in, v_w_a2, v_b_a2, v_gla_norm_g, v_w_out_gla, v_conv_mix_w, v_w_out_conv, v_w_o, v_w_up, v_ffn_conv_w, v_w_down):
    depth = w_ada.shape[0]
    S, D = x.shape[1], x.shape[2]
    QK = w_a2.shape[2] * NDEV
    LR = w_a2.shape[1]
    VW = w_out_gla.shape[1]
    CW = w_out_conv.shape[1]
    FF = w_down.shape[1] * NDEV
    NIN = w_in.shape[2] * NDEV
    NADA = w_ada.shape[2]
    LRP = -(-LR // LANES) * LANES
    off_lr = 2 * QK + 2 * VW
    NP = NIN - LR + LRP
    assert NIN == 2 * QK + 2 * VW + LR + 3 * CW + 2 * D
    xi, yi, ci = _position()
    me = _index((xi, yi, ci))
    xs = x.reshape(S, D)
    target = loss_target.reshape(S, D)
    TR = 256
    TRC = 512
    TC = _pick(CW, 512)
    TCF = _pick(FF, 512)

    small = _Pack([c, norm_g, w_a2, conv_mix_w, ffn_conv_w])
    big_names = ["w_in", "w_out_gla", "w_out_conv", "w_o", "w_up", "w_down"]
    big = dict(w_in=w_in, w_out_gla=w_out_gla, w_out_conv=w_out_conv, w_o=w_o, w_up=w_up, w_down=w_down)

    def matrix(key, g):
        if key == "w_up":
            return g
        if key in ("w_o", "w_down"):
            return g.reshape(g.shape[0] * g.shape[1], g.shape[2])
        if key != "w_in":
            return _cols_from_shards(g)
        pieces = [g[d][:, lo - d * nin_shard:hi - d * nin_shard] for _, d, lo, hi in sorted(in_pieces)]
        return jnp.concatenate(pieces + [jnp.zeros((D, LRP - LR), BF16)], axis=-1)

    nin_shard = NIN // NDEV
    in_pieces = []
    for g_lo, g_hi, to in ((0, off_lr, 0), (off_lr, off_lr + LR, NIN - LR), (off_lr + LR, NIN, off_lr)):
        for d in range(NDEV):
            lo, hi = max(g_lo, d * nin_shard), min(g_hi, (d + 1) * nin_shard)
            if lo < hi:
                in_pieces.append((to + lo - g_lo, d, lo, hi))

    def in_grad_blocks(gW_all):
        blocks = []
        for d in range(NDEV):
            mine = sorted((lo, to, hi) for to, dd, lo, hi in in_pieces if dd == d)
            blocks.append(jnp.concatenate([gW_all[:, to:to + hi - lo] for lo, to, hi in mine], axis=1))
        return jnp.stack(blocks)

    (small_g, w_in0_g) = _all_gather([small.packed, w_in[0].astype(BF16)], name="gather_first")
    c_all, norm_g_s, w_a2_s, conv_w_s, ffn_w_s = small.unpack(small_g)
    c_all = c_all.reshape(NDEV, D)
    norm_g_f = _cols_from_shards(norm_g_s)
    w_a2_f = _cols_from_shards(w_a2_s)
    conv_w_f = _cols_from_shards(conv_w_s)
    ffn_w_f = _cols_from_shards(ffn_w_s)
    w_a2_p = jnp.concatenate([w_a2_f, jnp.zeros((depth, LRP - LR, QK), F32)], axis=1)

    rides = _Rides()
    GATHER_PARAMS_PER_US = 80e3
    FORWARD_PARAMS_PER_US = 800e3
    EXCHANGE_PARAMS_PER_US = 45e3
    PAIR_PARAMS_PER_US = 500e3
    BACKWARD_ICI_SHARE = 0.8
    weight_streams = {}
    for l in range(depth):
        for key in big_names:
            if (l, key) == (0, "w_in"):
                continue
            shard = big[key][l].astype(BF16)
            rows, cols = shard.shape
            first = rides.add(_Stream("gather_a", shard, rows, cols, GATHER_PARAMS_PER_US / (NDEV * cols)))
            second = rides.add(_Stream("gather_b", None, rows, cols, FORWARD_PARAMS_PER_US / (NDEV * cols), after=first))
            weight_streams[l, key] = (first, second)
    W = {(0, "w_in"): matrix("w_in", w_in0_g)}

    def weight(l, key):
        if (l, key) not in W:
            first, second = weight_streams[l, key]
            rides.need(first, second)
            W[l, key] = matrix(key, first.buf)
        return W[l, key]

    MP = LANES
    c_pad = jnp.concatenate([c_all, jnp.zeros((MP - NDEV, D), F32)], axis=0)
    (silu_c,) = _rw_fwd(lambda t: (t * _sigmoid(t),), [_v(c_pad)], [], [(D, BF16)], tr=MP, name="silu_c")
    mod_cols = _mm(silu_c, w_ada, name="ada_fwd")[:NDEV].reshape(NDEV, depth, NADA)
    (mod_g,) = _all_gather([jnp.moveaxis(mod_cols, 1, 0)], name="gather_mod")
    mod_all = jnp.moveaxis(mod_g, 0, 2).reshape(depth, NDEV, NDEV * NADA)
    mod_me = lax.dynamic_slice_in_dim(mod_all, me, 1, axis=1).reshape(depth, NDEV * NADA)
    (mod_me,) = _rw_fwd(lambda a, b: (a + b,), [_v(mod_me), _v(b_ada)], [], [(NDEV * NADA, F32)],
                        tr=depth, name="mod_bias")

    col = {}
    off = 0
    for nm, wd in (("q", QK), ("k", QK), ("v", VW), ("r", VW), ("cb", CW), ("cc", CW), ("cx", CW),
                   ("ga", D), ("gb", D), ("lr", LRP)):
        col[nm] = off
        off += wd

    saved = []
    xin = xs
    for l in range(depth):
        sh1, sc1, g1, sh2, sc2, g2 = [mod_me[l:l + 1, i * D:(i + 1) * D] for i in range(6)]
        ng = [norm_g_f[l, i:i + 1] for i in range(4)]
        cw = [conv_w_f[l, i:i + 1] for i in range(CONV_K)]
        fw = [ffn_w_f[l, i:i + 1] for i in range(CONV_K)]
        gn = gla_norm_g[l:l + 1]
        ba2 = b_a2[l:l + 1]
        def proj(act, key, name, out_dtype=BF16):
            return _mm(act, weight(l, key), name=name, rides=rides, out_dtype=out_dtype)

        (h1,) = _rw_fwd(_f_normmod, [_v(xin)], [_v(ng[0]), _v(sc1), _v(sh1)], [(D, BF16)], tr=TR, name=f"normmod1_{l}",
                        rides=rides, us=US_NORMMOD)
        P = proj(h1, "w_in", f"in_proj{l}")
        (a,) = _rw_fwd(_f_loga, [_v(P, LRP, col["lr"])], [V(w_a2_p[l], QK, None), _v(ba2)], [(QK, F32)],
                       tr=TR, name=f"loga{l}")
        o, states = _gla_fwd(P, a, qk=QK, vw=VW, name=f"gla_fwd{l}", rides=rides, us=US_GLA)
        dvh = VW // HEADS
        (ya_in,) = _rw_fwd(_f_glapost, [_v(o, dvh), _v(P, dvh, col["r"])], [V(gn, dvh, None)], [(VW, BF16)],
                           tr=TR, ncol=HEADS, name=f"glapost{l}", rides=rides, us=US_GLAPOST)
        ya = proj(ya_in, "w_out_gla", f"out_gla{l}")
        (yb_in,) = _rw_fwd(_f_convmix, [_v(P, TC, col["cb"]), _v(P, TC, col["cc"]), _v(P, TC, col["cx"])],
                           [_v(w, TC) for w in cw], [(CW, BF16)], tr=TRC, ncol=CW // TC, halo=True, name=f"convmix{l}")
        yb = proj(yb_in, "w_out_conv", f"out_conv{l}")
        (mg,) = _rw_fwd(_f_merge, [_v(P, D, col["ga"]), _v(P, D, col["gb"]), _v(ya), _v(yb)], [], [(D, BF16)],
                        tr=TR, name=f"merge{l}", rides=rides, us=US_MERGE)
        y = proj(mg, "w_o", f"o_proj{l}", F32)
        (x1,) = _rw_fwd(_f_resid, [_v(xin), _v(y)], [_v(g1), _v(ng[1])], [(D, F32)], tr=TR, name=f"resid1_{l}",
                        rides=rides, us=US_RESID)
        (h2,) = _rw_fwd(_f_normmod, [_v(x1)], [_v(ng[2]), _v(sc2), _v(sh2)], [(D, BF16)], tr=TR, name=f"normmod2_{l}",
                        rides=rides, us=US_NORMMOD)
        U = proj(h2, "w_up", f"up_proj{l}")
        (ff,) = _rw_fwd(_f_ffn, [_v(U, TCF, 0), _v(U, TCF, FF)], [_v(w, TCF) for w in fw], [(FF, BF16)],
                        tr=TRC, ncol=FF // TCF, halo=True, name=f"ffn{l}", rides=rides, us=US_FFN)
        y2 = proj(ff, "w_down", f"down_proj{l}", F32)
        (x2,) = _rw_fwd(_f_resid, [_v(x1), _v(y2)], [_v(g2), _v(ng[3])], [(D, F32)], tr=TR, name=f"resid2_{l}",
                        rides=rides, us=US_RESID)
        saved.append(dict(xin=xin, h1=h1, P=P, a=a, o=o, states=states, ya_in=ya_in, ya=ya, yb_in=yb_in, yb=yb,
                          mg=mg, y=y, x1=x1, h2=h2, U=U, ff=ff, y2=y2, mods=(sh1, sc1, g1, sh2, sc2, g2),
                          ng=ng, cw=cw, fw=fw, gn=gn, ba2=ba2))
        xin = x2

    loss_local, dx = _loss_head(xin, target, tr=TR, name="loss_head")
    loss = lax.psum(loss_local, MESH_AXES)

    grad_streams = {}
    IN_WGRAD_ROWS = [(0, 3 * D // 4), (3 * D // 4, D // 4)]
    UP_WGRAD_ROWS = [(0, D // 2), (D // 2, D // 2)]

    rides.ici_share = BACKWARD_ICI_SHARE
    rides.min_ici_us = MIN_ICI_CARRIER_US

    def send(l, key, blocks):
        _, rows, cols = blocks.shape
        entry = {}
        tag = f"{key}{l}_{len(grad_streams.get((l, key), []))}"

        def swapped(pair):
            sums = _pair_add(blocks, pair.buf, ci, name=f"pair_add_{tag}")
            entry["chips"] = rides.add(_Stream("chips", sums, rows, cols, 2 * EXCHANGE_PARAMS_PER_US / (NDEV * cols)))

        entry["pair"] = rides.add(_Stream("pair", blocks, rows, cols, PAIR_PARAMS_PER_US / (NDEV * cols), on_done=swapped))
        grad_streams.setdefault((l, key), []).append(entry)

    def bmm(a, b, **kw):
        kw.setdefault("out_dtype", BF16)
        return _mm(a, b, rides=rides, **kw)

    gsmall = [None] * depth
    dmod = [None] * depth
    for l in reversed(range(depth)):
        s = saved[l]
        sh1, sc1, g1, sh2, sc2, g2 = s["mods"]
        ng, cw, fw = s["ng"], s["cw"], s["fw"]
        (dx1_a, dy2), (dg2, dng3) = _rw_bwd(
            _f_resid, [_v(s["x1"]), _v(s["y2"])], [_v(g2), _v(ng[3])], [_v(dx)],
            row_grads=[F32, BF16], par_grads=[True, True], tr=TR, name=f"resid2_bwd{l}", rides=rides, us=US_RESID_BWD)
        gW_down = bmm(s["ff"], dy2, ta=True, out_dtype=BF16, name=f"down_wgrad{l}")
        send(l, "w_down", gW_down.reshape(NDEV, FF // NDEV, D))
        dff = bmm(dy2, weight(l, "w_down"), tb=True, tall=True, name=f"down_dgrad{l}")
        U = s["U"]
        (dgate, dup), dfw = _rw_bwd(
            _f_ffn, [_v(U, TCF, 0), _v(U, TCF, FF)], [_v(w, TCF) for w in fw], [_v(dff, TCF)],
            row_grads=[BF16, BF16], par_grads=[True] * 3, tr=TRC, ncol=FF // TCF, halo=True, name=f"ffn_bwd{l}",
            rides=rides, us=US_FFN_BWD)
        dU = jnp.concatenate([dgate, dup], axis=1)
        for part, rows in enumerate(UP_WGRAD_ROWS):
            gW_up = bmm(s["h2"], dU, ta=True, out_dtype=BF16, out_shards=True, m_rows=rows, name=f"up_wgrad{l}_{part}")
            send(l, "w_up", gW_up)
        dh2 = bmm(dU, weight(l, "w_up"), tb=True, tall=True, name=f"up_dgrad{l}")
        (dx1,), (dng2, dsc2, dsh2) = _rw_bwd(
            _f_normmod, [_v(s["x1"])], [_v(ng[2]), _v(sc2), _v(sh2)], [_v(dh2)],
            row_grads=[F32], par_grads=[True] * 3, adds={0: dx1_a}, tr=TR, name=f"normmod2_bwd{l}",
            rides=rides, us=US_NORMMOD_BWD)
        (dxin_a, dy), (dg1, dng1) = _rw_bwd(
            _f_resid, [_v(s["xin"]), _v(s["y"])], [_v(g1), _v(ng[1])], [_v(dx1)],
            row_grads=[F32, BF16], par_grads=[True, True], tr=TR, name=f"resid1_bwd{l}", rides=rides, us=US_RESID_BWD)
        gW_o = bmm(s["mg"], dy, ta=True, out_dtype=BF16, name=f"o_wgrad{l}")
        send(l, "w_o", gW_o.reshape(NDEV, D // NDEV, D))
        dmg = bmm(dy, weight(l, "w_o"), tb=True, tall=True, name=f"o_dgrad{l}")
        P = s["P"]
        (dga, dgb, dya, dyb), _ = _rw_bwd(
            _f_merge, [_v(P, D, col["ga"]), _v(P, D, col["gb"]), _v(s["ya"]), _v(s["yb"])], [], [_v(dmg)],
            row_grads=[BF16] * 4, par_grads=[], tr=TR, name=f"merge_bwd{l}", rides=rides, us=US_MERGE_BWD)
        gW_og = bmm(s["ya_in"], dya, ta=True, out_dtype=BF16, name=f"out_gla_wgrad{l}")
        send(l, "w_out_gla", _cols_to_shards(gW_og))
        dya_in = bmm(dya, weight(l, "w_out_gla"), tb=True, name=f"out_gla_dgrad{l}")
        gW_oc = bmm(s["yb_in"], dyb, ta=True, out_dtype=BF16, name=f"out_conv_wgrad{l}")
        send(l, "w_out_conv", _cols_to_shards(gW_oc))
        dyb_in = bmm(dyb, weight(l, "w_out_conv"), tb=True, name=f"out_conv_dgrad{l}")
        (dcb, dcc, dcx), dcw = _rw_bwd(
            _f_convmix, [_v(P, TC, col["cb"]), _v(P, TC, col["cc"]), _v(P, TC, col["cx"])],
            [_v(w, TC) for w in cw], [_v(dyb_in, TC)],
            row_grads=[BF16] * 3, par_grads=[True] * 3, tr=TRC, ncol=CW // TC, halo=True, name=f"convmix_bwd{l}")
        dvh = VW // HEADS
        (do, dr), (dgn,) = _rw_bwd(
            _f_glapost, [_v(s["o"], dvh), _v(P, dvh, col["r"])], [V(s["gn"], dvh, None)], [_v(dya_in, dvh)],
            row_grads=[BF16, BF16], par_grads=[True], tr=TR, ncol=HEADS, name=f"glapost_bwd{l}",
            rides=rides, us=US_GLAPOST_BWD)
        dq, dk, dv, da = _gla_bwd(P, s["a"], s["states"], do, qk=QK, vw=VW, name=f"gla_bwd{l}",
                                  rides=rides, us=US_GLA_BWD)
        (dlr,), (dwa2, dba2) = _rw_bwd(
            _f_loga, [_v(P, LRP, col["lr"])], [V(w_a2_p[l], QK, None), _v(s["ba2"])], [_v(da)],
            row_grads=[BF16], par_grads=[True, True], tr=TR, name=f"loga_bwd{l}")
        dP = jnp.concatenate([dq, dk, dv, dr, dcb, dcc, dcx, dga, dgb, dlr], axis=1)
        for part, rows in enumerate(IN_WGRAD_ROWS):
            gW_all = bmm(s["h1"], dP, ta=True, out_dtype=BF16, m_rows=rows, name=f"in_wgrad{l}_{part}")
            send(l, "w_in", in_grad_blocks(gW_all))
        dh1 = bmm(dP, weight(l, "w_in"), tb=True, name=f"in_dgrad{l}")
        (dx,), (dng0, dsc1, dsh1) = _rw_bwd(
            _f_normmod, [_v(s["xin"])], [_v(ng[0]), _v(sc1), _v(sh1)], [_v(dh1)],
            row_grads=[F32], par_grads=[True] * 3, adds={0: dxin_a}, tr=TR, name=f"normmod1_bwd{l}",
            rides=rides, us=US_NORMMOD_BWD)
        vec = lambda t: t.reshape(1, -1)
        gsmall[l] = dict(
            norm_g=jnp.concatenate([vec(dng0), vec(dng1), vec(dng2), vec(dng3)], axis=0),
            w_a2=dwa2[0, :LR], b_a2=dba2.reshape(QK), gla_norm_g=jnp.sum(dgn, axis=0).reshape(dvh),
            conv_mix_w=jnp.concatenate([vec(t) for t in dcw], axis=0),
            ffn_conv_w=jnp.concatenate([vec(t) for t in dfw], axis=0))
        dmod[l] = jnp.concatenate([vec(t) for t in (dsh1, dsc1, dg1, dsh2, dsc2, dg2)], axis=1).reshape(-1)
    grad_x = dx.reshape(x.shape)

    stack = lambda key, src: jnp.stack([src[l][key] for l in range(depth)])
    small_names = ["norm_g", "w_a2", "b_a2", "gla_norm_g", "conv_mix_w", "ffn_conv_w"]
    gpack = _Pack([jnp.stack(dmod)] + [stack(k, gsmall) for k in small_names])
    (gsm,) = _all_gather([gpack.packed], name="gather_small_grads")
    gs = gpack.unpack(gsm)
    dmod_all = gs[0]
    parts_small = dict(zip(small_names, gs[1:]))

    def my_cols(t, n):
        return lax.dynamic_slice_in_dim(t, me * n, n, axis=t.ndim - 1)

    gW_ada = []
    for l in range(depth):
        dcols = my_cols(dmod_all[:, l], NADA)
        dcols = jnp.concatenate([dcols, jnp.zeros((MP - NDEV, NADA), F32)], axis=0)
        gW_ada.append(bmm(silu_c, dcols, ta=True, out_dtype=F32, name=f"ada_wgrad{l}")[None])
    parts = {}
    parts["b_ada"] = dmod_all
    parts["norm_g"] = my_cols(parts_small["norm_g"], D // NDEV)
    parts["w_a2"] = my_cols(parts_small["w_a2"], QK // NDEV)
    parts["b_a2"] = parts_small["b_a2"]
    parts["gla_norm_g"] = parts_small["gla_norm_g"]
    parts["conv_mix_w"] = my_cols(parts_small["conv_mix_w"], CW // NDEV)
    parts["ffn_conv_w"] = my_cols(parts_small["ffn_conv_w"], FF // NDEV)

    weights = dict(w_ada=(w_ada, m_w_ada, v_w_ada), b_ada=(b_ada, m_b_ada, v_b_ada),
                   norm_g=(norm_g, m_norm_g, v_norm_g), w_in=(w_in, m_w_in, v_w_in), w_a2=(w_a2, m_w_a2, v_w_a2),
                   b_a2=(b_a2, m_b_a2, v_b_a2), gla_norm_g=(gla_norm_g, m_gla_norm_g, v_gla_norm_g),
                   w_out_gla=(w_out_gla, m_w_out_gla, v_w_out_gla), conv_mix_w=(conv_mix_w, m_conv_mix_w, v_conv_mix_w),
                   w_out_conv=(w_out_conv, m_w_out_conv, v_w_out_conv), w_o=(w_o, m_w_o, v_w_o),
                   w_up=(w_up, m_w_up, v_w_up), ffn_conv_w=(ffn_conv_w, m_ffn_conv_w, v_ffn_conv_w),
                   w_down=(w_down, m_w_down, v_w_down))
    done = {}
    for nm in ["w_ada", "w_down", "w_up", "w_o", "w_out_gla", "w_out_conv", "w_in"]:
        w, m, v = weights[nm]
        if nm == "w_ada":
            done[nm] = _adamw(gW_ada, w, m, v, name=f"adamw_{nm}")
            continue
        entries = [e for l in range(depth) for e in grad_streams[l, nm]]
        rides.need(*[e["pair"] for e in entries])
        streams = [e["chips"] for e in entries]
        rides.need(*streams)
        rows = math.gcd(*[s.R for s in streams])
        bufs, row0 = [], []
        for s in streams:
            for first in range(0, s.R, rows):
                bufs.append(s.buf)
                row0.append(first)
        split = lambda t: t.reshape(len(bufs), rows, t.shape[2])
        outs = _adamw(bufs, split(w), split(m), split(v), name=f"adamw_{nm}", row0=row0)
        done[nm] = [o.reshape(w.shape) for o in outs]
    for nm, (w, m, v) in weights.items():
        if nm not in done:
            done[nm] = _adamw_nd(parts[nm], w, m, v, name=f"adamw_{nm}")
    grads, deltas, new_m, new_v = zip(*[done[nm] for nm in weights])
    return (loss, grad_x, *grads, *deltas, *new_m, *new_v)
```

```python
import functools
import math
from typing import Any, NamedTuple

import jax
import jax.numpy as jnp
from jax import lax
from jax.experimental import pallas as pl
from jax.experimental.pallas import tpu as pltpu

F32 = jnp.float32
BF16 = jnp.bfloat16
MESH_AXES = ("x", "y", "c")
NDEV = 8
EPS = 1e-6
CHUNK = 64
HEADS = 4
GLA_TAU = 16.0
CONV_K = 3
ADAM_LR = 0.001
ADAM_B1 = 0.9
ADAM_B2 = 0.999
ADAM_EPS = 1e-08
ADAM_WD = 0.01
ADAM_STEP = 10

LANES = 128
SUBLANES = 8
HALO = 2 * SUBLANES
VMEM_LIMIT = 56 * 1024 * 1024


def _params(sem=None):
    return pltpu.CompilerParams(dimension_semantics=sem, vmem_limit_bytes=VMEM_LIMIT)


def _pick(n, cap, unit=LANES):
    best = None
    for d in range(unit, min(n, cap) + 1, unit):
        if n % d == 0:
            best = d
    return best if best is not None else n


def _position():
    return lax.axis_index("x"), lax.axis_index("y"), lax.axis_index("c")


def _index(p):
    return 4 * p[0] + 2 * p[1] + p[2]


def _peer(k):
    x, y, c = _position()
    return ((1 - x) if k & 4 else x, (1 - y) if k & 2 else y, (1 - c) if k & 1 else c)


ROW_QUANTUM = 2 * SUBLANES
MIN_ROWS = 128
MIN_ICI_CARRIER_US = 50.0


class _Stream:
    def __init__(self, mode, src, rows, cols, rate, after=None, on_done=None):
        self.mode, self.src, self.R, self.rate, self.after = mode, src, rows, rate, after
        self.shape = (NDEV if mode.startswith("gather") else NDEV // 2, rows, cols)
        self.holder = self if after is None else after
        self.buf = None
        self.done = 0
        self.on_done = on_done


class _Exchange:
    def __init__(self, items):
        self.items = items
        self.ins, self.src_pos, self.alias, self.holders, self.out_pos = [], [], {}, [], []
        for s, _, _ in items:
            if not any(s.holder is h for h in self.holders):
                self.holders.append(s.holder)
        for s, _, _ in items:
            self.out_pos.append([n for n, h in enumerate(self.holders) if h is s.holder][0])
            if s.mode == "gather_b":
                self.src_pos.append(None)
            else:
                self.src_pos.append(len(self.ins))
                self.ins.append(s.src)
        for n, h in enumerate(self.holders):
            if h.buf is not None:
                self.alias[len(self.ins)] = n
                self.ins.append(h.buf)
        self.out_shape = [jax.ShapeDtypeStruct(h.shape, BF16) for h in self.holders]
        n = len(items)
        self.scratch = [pltpu.SemaphoreType.DMA((n, NDEV)), pltpu.SemaphoreType.DMA((n, NDEV)),
                        pltpu.SemaphoreType.DMA((n,))]

    def deliver(self, outs):
        for h, o in zip(self.holders, outs):
            h.buf = o

    def _copies(self, ins, outs, sems):
        send_sems, recv_sems, local_sems = sems
        me = _index(_position())
        slot = lambda k: _index(_peer(k))
        local, sends, arrivals = [], [], []
        for n, (s, r0, rc) in enumerate(self.items):
            out, rows = outs[self.out_pos[n]], pl.ds(r0, rc)
            src = out if self.src_pos[n] is None else ins[self.src_pos[n]]
            if s.mode == "pair":
                other = 1 - _position()[2]
                moves = [(q, src.at[2 * q + other, rows], out.at[q, rows], _peer(1), out.at[q, rows])
                         for q in range(NDEV // 2)]
            elif s.mode == "chips":
                chip = lambda k: slot(k) // 2
                local.append(pltpu.make_async_copy(src.at[me // 2, rows], out.at[me // 2, rows], local_sems.at[n]))
                moves = [(k, src.at[chip(k), rows], out.at[me // 2, rows], _peer(k), out.at[chip(k), rows])
                         for k in (2, 4, 6)]
            elif s.mode == "gather_a":
                local.append(pltpu.make_async_copy(src.at[rows], out.at[me, rows], local_sems.at[n]))
                moves = [(k, src.at[rows], out.at[me, rows], _peer(k), out.at[slot(k), rows]) for k in (1, 2, 4, 6)]
            else:
                moves = [(k, src.at[slot(k), rows], out.at[slot(k), rows], _peer(1), out.at[slot(k ^ 1), rows])
                         for k in (2, 4, 6)]
            for k, src_ref, dst_ref, to, landing in moves:
                common = dict(send_sem=send_sems.at[n, k], recv_sem=recv_sems.at[n, k],
                              device_id=to, device_id_type=pl.DeviceIdType.MESH)
                sends.append(pltpu.make_async_remote_copy(src_ref=src_ref, dst_ref=dst_ref, **common))
                arrivals.append(pltpu.make_async_remote_copy(src_ref=landing, dst_ref=landing, **common))
        return local, sends, arrivals

    def start(self, ins, outs, sems):
        local, sends, _ = self._copies(ins, outs, sems)
        for cp in local + sends:
            cp.start()

    def wait(self, ins, outs, sems):
        local, sends, arrivals = self._copies(ins, outs, sems)
        for cp in arrivals:
            cp.wait_recv()
        for cp in sends:
            cp.wait_send()
        for cp in local:
            cp.wait()


class _Rides:
    def __init__(self):
        self.queue = []
        self.finished = []
        self.flushes = 0
        self.ici_share = 1.0
        self.min_ici_us = 45.0

    def add(self, stream):
        self.queue.append(stream)
        return stream

    def take(self, budget, only=None, ici=True):
        ready = {id(s): (s.after.done if s.after is not None else s.R) for s in self.queue}
        left = {"ici": budget * self.ici_share if ici else 0.0, "d2d": budget}
        items = []
        for s in list(self.queue):
            if only is not None and not any(s is t for t in only):
                continue
            link = "d2d" if s.mode in ("gather_b", "pair") else "ici"
            avail = ready[id(s)] - s.done
            if avail <= 0:
                continue
            rc = avail if budget == float("inf") else min(avail, int(left[link] * s.rate) // ROW_QUANTUM * ROW_QUANTUM)
            if rc < min(avail, MIN_ROWS):
                continue
            if avail - rc <= 2 * MIN_ROWS:
                rc = avail
            items.append((s, s.done, rc))
            s.done += rc
            left[link] -= rc / s.rate
            if s.done == s.R:
                self.queue.remove(s)
                self.finished.append(s)
        return items

    def fire(self):
        finished, self.finished = self.finished, []
        for s in finished:
            if s.on_done is not None:
                s.on_done(s)

    def need(self, *streams):
        while any(s.done < s.R for s in streams):
            items = self.take(float("inf"), only=streams)
            comm = _Exchange(items)
            hbm = pl.BlockSpec(memory_space=pl.ANY)
            nin, nout = len(comm.ins), len(comm.holders)

            def body(*refs, comm=comm, nin=nin, nout=nout):
                comm.start(refs[:nin], refs[nin:nin + nout], refs[nin + nout:])
                comm.wait(refs[:nin], refs[nin:nin + nout], refs[nin + nout:])

            outs = pl.pallas_call(body, in_specs=[hbm] * nin, out_specs=[hbm] * nout, out_shape=comm.out_shape,
                                  scratch_shapes=comm.scratch, input_output_aliases=dict(comm.alias),
                                  name=f"exchange_alone{self.flushes}")(*comm.ins)
            self.flushes += 1
            comm.deliver(outs)
            self.fire()


def _call(body, *, grid, in_specs, out_specs, out_shape, scratch_shapes=(), args, name, rides=None, us=0.0):
    sem = ("arbitrary",) * len(grid)
    items = rides.take(us, ici=us >= rides.min_ici_us) if rides is not None else []
    if not items:
        return pl.pallas_call(body, grid=grid, in_specs=in_specs, out_specs=out_specs, out_shape=out_shape,
                              scratch_shapes=list(scratch_shapes), compiler_params=_params(sem), name=name)(*args)
    comm = _Exchange(items)
    nin, nout, nscr, cin, cout = len(in_specs), len(out_specs), len(scratch_shapes), len(comm.ins), len(comm.holders)
    hbm = pl.BlockSpec(memory_space=pl.ANY)

    def both(*refs):
        ins, c_ins = refs[:nin], refs[nin:nin + cin]
        pos = nin + cin
        outs, c_outs = refs[pos:pos + nout], refs[pos + nout:pos + nout + cout]
        pos += nout + cout
        scr, sems = refs[pos:pos + nscr], refs[pos + nscr:]
        first = functools.reduce(lambda p, q: p & q, [pl.program_id(d) == 0 for d in range(len(grid))])
        last = functools.reduce(lambda p, q: p & q, [pl.program_id(d) == grid[d] - 1 for d in range(len(grid))])

        @pl.when(first)
        def _():
            comm.start(c_ins, c_outs, sems)

        body(*ins, *outs, *scr)

        @pl.when(last)
        def _():
            comm.wait(c_ins, c_outs, sems)

    res = pl.pallas_call(both, grid=grid, in_specs=list(in_specs) + [hbm] * cin, out_specs=list(out_specs) + [hbm] * cout,
                         out_shape=list(out_shape) + comm.out_shape, scratch_shapes=list(scratch_shapes) + comm.scratch,
                         input_output_aliases={nin + i: nout + o for i, o in comm.alias.items()},
                         compiler_params=_params(sem), name=name)(*args, *comm.ins)
    comm.deliver(res[nout:])
    rides.fire()
    return res[:nout]


US_GLA, US_GLAPOST, US_MERGE, US_FFN, US_RESID, US_NORMMOD = 100.0, 40.0, 45.0, 110.0, 40.0, 24.0
US_FFN_BWD, US_GLA_BWD, US_MERGE_BWD, US_RESID_BWD, US_NORMMOD_BWD, US_GLAPOST_BWD = 215.0, 118.0, 57.0, 57.0, 54.0, 52.0
MM_VMEM_BUDGET = 44 * 1024 * 1024
MM_FLOPS_PER_US = 780e6


def _mm(a, b, *, ta=False, tb=False, out_dtype=F32, name, rides=None, out_shards=False, m_rows=None, tall=False):
    if ta:
        K, M = a.shape
    else:
        M, K = a.shape
    m_first, M = (0, M) if m_rows is None else m_rows
    sharded = b.ndim == 3
    if sharded:
        ns, n = b.shape[0], b.shape[2]
        Kb, N = (ns * n, b.shape[1]) if tb else (b.shape[1], ns * n)
    elif tb:
        N, Kb = b.shape
    else:
        Kb, N = b.shape
    assert K == Kb, (a.shape, b.shape, ta, tb)
    tm = _pick(M, 2048 if tall else 1024)
    tn = N // NDEV if out_shards else n if (sharded and not tb) else _pick(N, 1152)
    sa, sb, so = a.dtype.itemsize, b.dtype.itemsize, jnp.dtype(out_dtype).itemsize

    def vmem(tk):
        blocks = 2 * (tm * tk * sa + tk * tn * sb) + 2 * tm * tn * so + (tm * tn * 4 if tk < K else 0)
        return blocks + (tm * K * 2 if ta else 0)

    if sharded and tb:
        tk = n
    else:
        tk = K
        if vmem(K) > MM_VMEM_BUDGET:
            fits = [d for d in range(LANES, K, LANES) if K % d == 0 and vmem(d) <= MM_VMEM_BUDGET]
            tk = max(fits) if fits else _pick(K, 512)
    nk = K // tk
    dims = (((1,), (1 if tb else 0,)), ((), ()))

    def dot(a_ref, b_ref, at_ref):
        if ta:
            k = pl.program_id(2)

            @pl.when(pl.program_id(1) == 0)
            def _():
                at_ref[k] = a_ref[...].astype(BF16).T

            lhs = at_ref[k]
        else:
            lhs = a_ref[...].astype(BF16)
        return lax.dot_general(lhs, b_ref[...].astype(BF16), dims, preferred_element_type=F32)

    def body_one(a_ref, b_ref, o_ref, *at_ref):
        o_ref[...] = dot(a_ref, b_ref, *at_ref or (None,)).astype(o_ref.dtype)

    def body_acc(a_ref, b_ref, o_ref, acc_ref, *at_ref):
        k = pl.program_id(2)
        at = (at_ref or (None,))[0]

        @pl.when(k == 0)
        def _():
            acc_ref[...] = dot(a_ref, b_ref, at)

        @pl.when((k > 0) & (k < nk - 1))
        def _():
            acc_ref[...] += dot(a_ref, b_ref, at)

        @pl.when(k == nk - 1)
        def _():
            o_ref[...] = (acc_ref[...] + dot(a_ref, b_ref, at)).astype(o_ref.dtype)

    assert m_first % tm == 0
    i0 = m_first // tm
    a_spec = (pl.BlockSpec((tk, tm), lambda i, j, k: (k, i0 + i)) if ta
              else pl.BlockSpec((tm, tk), lambda i, j, k: (i0 + i, k)))
    if sharded:
        b_spec = (pl.BlockSpec((None, tn, tk), lambda i, j, k: (k, j, 0)) if tb
                  else pl.BlockSpec((None, tk, tn), lambda i, j, k: (j, k, 0)))
    else:
        b_spec = (pl.BlockSpec((tn, tk), lambda i, j, k: (j, k)) if tb
                  else pl.BlockSpec((tk, tn), lambda i, j, k: (k, j)))
    if out_shards:
        out_spec = pl.BlockSpec((None, tm, tn), lambda i, j, k: (j, i, 0))
        out_shape = jax.ShapeDtypeStruct((NDEV, M, tn), out_dtype)
    else:
        out_spec = pl.BlockSpec((tm, tn), lambda i, j, k: (i, j))
        out_shape = jax.ShapeDtypeStruct((M, N), out_dtype)
    (out,) = _call(
        body_one if nk == 1 else body_acc, grid=(M // tm, N // tn, nk), in_specs=[a_spec, b_spec],
        out_specs=[out_spec], out_shape=[out_shape],
        scratch_shapes=([] if nk == 1 else [pltpu.VMEM((tm, tn), F32)]) + ([pltpu.VMEM((nk, tm, tk), BF16)] if ta else []),
        args=(a, b), name=name,
        rides=rides, us=2.0 * M * N * K / MM_FLOPS_PER_US)
    return out


class V(NamedTuple):
    arr: Any
    width: int
    base: Any


def _v(arr, width=None, col0=0):
    width = arr.shape[-1] if width is None else width
    assert col0 % width == 0
    return V(arr, width, col0 // width)


def _row_specs(v, tr, nrow8, halo):
    main = pl.BlockSpec((tr, v.width), lambda j, i, b=v.base: (i, b + j))
    if not halo:
        return [main]
    per = tr // HALO
    prev = pl.BlockSpec((HALO, v.width), lambda j, i, b=v.base: (jnp.maximum(i * per - 1, 0), b + j))
    nxt = pl.BlockSpec((HALO, v.width), lambda j, i, b=v.base: (jnp.minimum((i + 1) * per, nrow8 - 1), b + j))
    return [prev, main, nxt]


def _par_spec(p):
    if p.base is None:
        return pl.BlockSpec(p.arr.shape, lambda j, i: (0,) * p.arr.ndim)
    return pl.BlockSpec((p.arr.shape[0], p.width), lambda j, i, b=p.base: (0, b + j))


def _load_rows(refs, i, nrow, halo):
    if not halo:
        return refs[0][...].astype(F32)
    prev, main, nxt = refs
    pv = jnp.where(i > 0, prev[...].astype(F32), 0.0)
    nv = jnp.where(i < nrow - 1, nxt[...].astype(F32), 0.0)
    return jnp.concatenate([pv, main[...].astype(F32), nv], axis=0)


def _center_mask(tr):
    row = lax.broadcasted_iota(jnp.int32, (tr + 2 * HALO, 1), 0)
    return ((row >= HALO) & (row < HALO + tr)).astype(F32)


def _rw_fwd(f, rows, params, outs, *, tr, ncol=1, halo=False, name, rides=None, us=0.0):
    S = rows[0].arr.shape[0]
    tr = min(tr, S)
    nrow = S // tr
    per = 3 if halo else 1
    nr, npar = len(rows), len(params)

    def body(*refs):
        i = pl.program_id(1)
        vals = [_load_rows(refs[per * r: per * (r + 1)], i, nrow, halo) for r in range(nr)]
        pvals = [refs[per * nr + p][...].astype(F32) for p in range(npar)]
        out_refs = refs[per * nr + npar:]
        res = f(*vals, *pvals, _center_mask(tr)) if halo else f(*vals, *pvals)
        for r, o in zip(res, out_refs):
            o[...] = (r[HALO:HALO + tr] if halo else r).astype(o.dtype)

    in_specs = []
    for v in rows:
        in_specs += _row_specs(v, tr, S // HALO, halo)
    in_specs += [_par_spec(p) for p in params]
    args = []
    for v in rows:
        args += [v.arr] * per
    args += [p.arr for p in params]
    return _call(
        body, grid=(ncol, nrow), in_specs=in_specs,
        out_specs=[pl.BlockSpec((tr, tot // ncol), lambda j, i: (i, j)) for tot, _ in outs],
        out_shape=[jax.ShapeDtypeStruct((S, tot), dt) for tot, dt in outs],
        args=args, name=name, rides=rides, us=us)


def _rw_bwd(f, rows, params, douts, *, row_grads, par_grads, adds=None, tr, ncol=1, halo=False, name,
            rides=None, us=0.0):
    S = rows[0].arr.shape[0]
    tr = min(tr, S)
    nrow = S // tr
    per = 3 if halo else 1
    adds = adds or {}
    nr, npar, nd = len(rows), len(params), len(douts)
    add_keys = sorted(adds)
    rg_idx = [k for k in range(nr) if row_grads[k]]
    pg_idx = [k for k in range(npar) if par_grads[k]]

    def body(*refs):
        i = pl.program_id(1)
        pos = 0
        vals = [_load_rows(refs[per * r: per * (r + 1)], i, nrow, halo) for r in range(nr)]
        pos = per * nr
        pvals = [refs[pos + p][...].astype(F32) for p in range(npar)]
        pos += npar
        dvals = [_load_rows(refs[pos + per * d: pos + per * (d + 1)], i, nrow, halo) for d in range(nd)]
        pos += per * nd
        add_vals = {k: refs[pos + n][...].astype(F32) for n, k in enumerate(add_keys)}
        pos += len(add_keys)
        rg_refs = refs[pos: pos + len(rg_idx)]
        pg_refs = refs[pos + len(rg_idx):]
        if halo:
            center = _center_mask(tr)
            fun = lambda *a: f(*a, center)
        else:
            fun = f
        _, vjp = jax.vjp(fun, *vals, *pvals)
        grads = vjp(tuple(dvals))
        for ref, k in zip(rg_refs, rg_idx):
            g = grads[k]
            g = g[HALO:HALO + tr] if halo else g
            if k in add_vals:
                g = g + add_vals[k]
            ref[...] = g.astype(ref.dtype)
        for ref, k in zip(pg_refs, pg_idx):
            @pl.when(i == 0)
            def _(ref=ref):
                ref[...] = jnp.zeros_like(ref)
            ref[...] += grads[nr + k]

    in_specs, args = [], []
    for v in list(rows):
        in_specs += _row_specs(v, tr, S // HALO, halo)
        args += [v.arr] * per
    in_specs += [_par_spec(p) for p in params]
    args += [p.arr for p in params]
    for v in douts:
        in_specs += _row_specs(v, tr, S // HALO, halo)
        args += [v.arr] * per
    for k in add_keys:
        in_specs += _row_specs(_v(adds[k], rows[k].width), tr, S // HALO, False)
        args += [adds[k]]
    out_specs, out_shape = [], []
    for k in rg_idx:
        w = rows[k].width
        out_specs.append(pl.BlockSpec((tr, w), lambda j, i: (i, j)))
        out_shape.append(jax.ShapeDtypeStruct((S, ncol * w), row_grads[k]))
    for k in pg_idx:
        p = params[k]
        blk = p.arr.shape if p.base is None else (p.arr.shape[0], p.width)
        out_specs.append(pl.BlockSpec((None,) + tuple(blk), lambda j, i, n=len(blk): (j,) + (0,) * n))
        out_shape.append(jax.ShapeDtypeStruct((ncol,) + tuple(blk), F32))
    res = _call(body, grid=(ncol, nrow), in_specs=in_specs, out_specs=out_specs, out_shape=out_shape, args=args, name=name,
                rides=rides, us=us)
    rg = [None] * nr
    for n, k in enumerate(rg_idx):
        rg[k] = res[n]
    pg = [None] * npar
    for n, k in enumerate(pg_idx):
        pg[k] = res[len(rg_idx) + n]
    return rg, pg


def _sigmoid(x):
    return 1.0 / (1.0 + jnp.exp(-x))


def _rms(x, g):
    return x * lax.rsqrt(jnp.mean(x * x, axis=-1, keepdims=True) + EPS) * g


def _f_normmod(x, g, sc, sh):
    return (_rms(x, g) * (1.0 + sc) + sh,)


def _f_resid(x, y, gate, g):
    return (x + gate * _rms(y, g),)


def _f_merge(ga, gb, ya, yb):
    return (_sigmoid(ga) * ya + _sigmoid(gb) * yb,)


def _f_glapost(o, r, g):
    return (_rms(o, g) * (r * _sigmoid(r)),)


def _roll_rows(x, k):
    return pltpu.roll(x, k % x.shape[0], 0)


@functools.partial(jax.custom_vjp, nondiff_argnums=(1,))
def _shift(x, k):
    return _roll_rows(x, k)


def _shift_fwd(x, k):
    return _roll_rows(x, k), None


def _shift_bwd(k, _, g):
    return (_roll_rows(g, -k),)


_shift.defvjp(_shift_fwd, _shift_bwd)


def _taps(w, center):
    return jnp.where(center > 0.0, w, lax.stop_gradient(w))


def _conv3(u, w0, w1, w2, center):
    return (_taps(w0, center) * _shift(u, 2) + _taps(w1, center) * _shift(u, 1)) + _taps(w2, center) * u


def _f_convmix(cb, cc, cx, w0, w1, w2, center):
    return (cb * _conv3(cc * cx, w0, w1, w2, center),)


def _gelu_tanh(x):
    return 0.5 * x * (1.0 + jnp.tanh(math.sqrt(2.0 / math.pi) * (x + 0.044715 * (x * x * x))))


def _f_ffn(gate, up, w0, w1, w2, center):
    return (_gelu_tanh(_conv3(gate, w0, w1, w2, center)) * up,)


@jax.custom_vjp
def _bdot(a, b):
    return jnp.dot(a.astype(BF16), b.astype(BF16), preferred_element_type=F32)


def _bdot_fwd(a, b):
    return _bdot(a, b), (a, b)


def _bdot_bwd(res, g):
    a, b = res
    gb = g.astype(BF16)
    da = lax.dot_general(gb, b.astype(BF16), (((1,), (1,)), ((), ())), preferred_element_type=F32)
    db = lax.dot_general(a.astype(BF16), gb, (((0,), (0,)), ((), ())), preferred_element_type=F32)
    return da, db


_bdot.defvjp(_bdot_fwd, _bdot_bwd)


@jax.custom_vjp
def _log_sigmoid(z):
    return jnp.minimum(z, 0.0) - jnp.log(1.0 + jnp.exp(-jnp.abs(z)))


def _log_sigmoid_fwd(z):
    return _log_sigmoid(z), z


def _log_sigmoid_bwd(z, g):
    return (g * _sigmoid(-z),)


_log_sigmoid.defvjp(_log_sigmoid_fwd, _log_sigmoid_bwd)


def _f_loga(lr, w, b):
    return (_log_sigmoid(_bdot(lr, w) + b) / GLA_TAU,)


def _split3(x):
    hi = x.astype(BF16)
    r1 = x - hi.astype(F32)
    mid = r1.astype(BF16)
    lo = (r1 - mid.astype(F32)).astype(BF16)
    return hi, mid, lo


def _dot_exact(a01, x, dims):
    hi, mid, lo = _split3(x)
    d = lambda p: lax.dot_general(a01, p, dims, preferred_element_type=F32)
    return (d(lo) + d(mid)) + d(hi)


def _dot_exact_lhs(x, a01, dims):
    hi, mid, lo = _split3(x)
    d = lambda p: lax.dot_general(p, a01, dims, preferred_element_type=F32)
    return (d(lo) + d(mid)) + d(hi)


GLA_CHUNKS_PER_STEP = 8

NN = (((1,), (0,)), ((), ()))
NT = (((1,), (1,)), ((), ()))
TN = (((0,), (0,)), ((), ()))


def _bf_dot(a, b, dims):
    return lax.dot_general(a.astype(BF16), b.astype(BF16), dims, preferred_element_type=F32)


def _chunk_decay(a, dv):
    L = a.shape[0]
    r = lax.broadcasted_iota(jnp.int32, (L, L), 0)
    c = lax.broadcasted_iota(jnp.int32, (L, L), 1)
    tri = (r >= c).astype(BF16)
    cum = _dot_exact(tri, a, NN)
    ce = cum[L - 1:L, :]
    dec = jnp.exp(_dot_exact_lhs(a, jnp.ones((L, dv), BF16), TN))
    return cum, ce, jnp.exp(ce - cum), dec


def _gla_fwd(P, a, *, qk, vw, name, rides=None, us=0.0):
    S = P.shape[0]
    nc = S // CHUNK
    cps = math.gcd(nc, GLA_CHUNKS_PER_STEP)
    rows = cps * CHUNK
    dk, dv = qk // HEADS, vw // HEADS
    scale = dk ** -0.5

    def body(q_ref, k_ref, v_ref, a_ref, o_ref, st_ref, state):
        @pl.when(pl.program_id(0) == 0)
        def _():
            state[...] = jnp.zeros_like(state)

        for h in range(HEADS):
            ks, vs = slice(h * dk, (h + 1) * dk), slice(h * dv, (h + 1) * dv)
            s = state[h]
            for c in range(cps):
                rs = slice(c * CHUNK, (c + 1) * CHUNK)
                _, _, E, dec = _chunk_decay(a_ref[rs, ks], dv)
                kd = k_ref[rs, ks] * E
                s = dec * s + _bf_dot(kd, v_ref[rs, vs], TN)
                st_ref[c, h] = s
                o_ref[rs, vs] = _bf_dot(q_ref[rs, ks] * scale, s, NN).astype(o_ref.dtype)
            state[h] = s

    return _call(
        body, grid=(nc // cps,),
        in_specs=[pl.BlockSpec((rows, qk), lambda i: (i, 0)),
                  pl.BlockSpec((rows, qk), lambda i: (i, 1)),
                  pl.BlockSpec((rows, vw), lambda i: (i, 2 * qk // vw)),
                  pl.BlockSpec((rows, qk), lambda i: (i, 0))],
        out_specs=[pl.BlockSpec((rows, vw), lambda i: (i, 0)),
                   pl.BlockSpec((cps, HEADS, dk, dv), lambda i: (i, 0, 0, 0))],
        out_shape=[jax.ShapeDtypeStruct((S, vw), BF16), jax.ShapeDtypeStruct((nc, HEADS, dk, dv), F32)],
        scratch_shapes=[pltpu.VMEM((HEADS, dk, dv), F32)], args=(P, P, P, a), name=name, rides=rides, us=us)


def _gla_bwd(P, a, states, do, *, qk, vw, name, rides=None, us=0.0):
    S = P.shape[0]
    nc = S // CHUNK
    cps = math.gcd(nc, GLA_CHUNKS_PER_STEP)
    rows = cps * CHUNK
    nstep = nc // cps
    dk, dv = qk // HEADS, vw // HEADS
    scale = dk ** -0.5

    def body(q_ref, k_ref, v_ref, a_ref, do_ref, sc_ref, sp_ref, dq_ref, dk_ref, dv_ref, da_ref, ds_carry):
        i = pl.program_id(0)

        @pl.when(i == 0)
        def _():
            ds_carry[...] = jnp.zeros_like(ds_carry)

        has_prev = (i < nstep - 1).astype(F32)
        L = CHUNK
        r = lax.broadcasted_iota(jnp.int32, (L, L), 0)
        c = lax.broadcasted_iota(jnp.int32, (L, L), 1)
        tri_t = (c >= r).astype(BF16)
        last = (lax.broadcasted_iota(jnp.int32, (L, 1), 0) == L - 1).astype(F32)
        for h in range(HEADS):
            ks, vs = slice(h * dk, (h + 1) * dk), slice(h * dv, (h + 1) * dv)
            carry = ds_carry[h]
            for n in reversed(range(cps)):
                rs = slice(n * CHUNK, (n + 1) * CHUNK)
                _, _, E, dec = _chunk_decay(a_ref[rs, ks], dv)
                kk, vv, dov = k_ref[rs, ks], v_ref[rs, vs], do_ref[rs, vs]
                kd = kk * E
                ds = carry + _bf_dot(q_ref[rs, ks] * scale, dov, TN)
                dq_ref[rs, ks] = (_bf_dot(dov, sc_ref[n, h], NT) * scale).astype(dq_ref.dtype)
                dkd = _bf_dot(vv, ds, NT)
                dv_ref[rs, vs] = _bf_dot(kd, ds, NN).astype(dv_ref.dtype)
                dk_ref[rs, ks] = (dkd * E).astype(dk_ref.dtype)
                darg = dkd * kd
                before = sc_ref[n - 1, h] if n > 0 else sp_ref[0, h] * has_prev
                w = ds * before * dec
                dce = _dot_exact(jnp.ones((SUBLANES, dv), BF16), w, NT)[0:1, :]
                dce = dce + jnp.sum(darg, axis=0, keepdims=True)
                dcum = last * dce - darg
                da_ref[rs, ks] = _dot_exact(tri_t, dcum, NN)
                carry = dec * ds
            ds_carry[h] = carry

    rev = lambda i: nstep - 1 - i
    return _call(
        body, grid=(nstep,),
        in_specs=[pl.BlockSpec((rows, qk), lambda i: (rev(i), 0)),
                  pl.BlockSpec((rows, qk), lambda i: (rev(i), 1)),
                  pl.BlockSpec((rows, vw), lambda i: (rev(i), 2 * qk // vw)),
                  pl.BlockSpec((rows, qk), lambda i: (rev(i), 0)),
                  pl.BlockSpec((rows, vw), lambda i: (rev(i), 0)),
                  pl.BlockSpec((cps, HEADS, dk, dv), lambda i: (rev(i), 0, 0, 0)),
                  pl.BlockSpec((1, HEADS, dk, dv), lambda i: (jnp.maximum(rev(i) * cps - 1, 0), 0, 0, 0))],
        out_specs=[pl.BlockSpec((rows, qk), lambda i: (rev(i), 0)),
                   pl.BlockSpec((rows, qk), lambda i: (rev(i), 0)),
                   pl.BlockSpec((rows, vw), lambda i: (rev(i), 0)),
                   pl.BlockSpec((rows, qk), lambda i: (rev(i), 0))],
        out_shape=[jax.ShapeDtypeStruct((S, qk), BF16), jax.ShapeDtypeStruct((S, qk), BF16),
                   jax.ShapeDtypeStruct((S, vw), BF16), jax.ShapeDtypeStruct((S, qk), F32)],
        scratch_shapes=[pltpu.VMEM((HEADS, dk, dv), F32)], args=(P, P, P, a, do, states, states), name=name,
        rides=rides, us=us)


def _loss_head(y, target, *, tr, name):
    S, D = y.shape
    tr = min(tr, S)

    def body(y_ref, t_ref, l_ref, dy_ref):
        @pl.when(pl.program_id(0) == 0)
        def _():
            l_ref[...] = jnp.zeros_like(l_ref)

        e = y_ref[...] - t_ref[...]
        dy_ref[...] = e / D
        l_ref[...] += 0.5 * jnp.sum(jnp.mean(e * e, axis=-1, keepdims=True), axis=0, keepdims=True)

    loss, dy = pl.pallas_call(
        body, grid=(S // tr,),
        in_specs=[pl.BlockSpec((tr, D), lambda i: (i, 0))] * 2,
        out_specs=[pl.BlockSpec((SUBLANES, LANES), lambda i: (0, 0)), pl.BlockSpec((tr, D), lambda i: (i, 0))],
        out_shape=[jax.ShapeDtypeStruct((SUBLANES, LANES), F32), jax.ShapeDtypeStruct((S, D), F32)],
        compiler_params=_params(("arbitrary",)), name=name)(y, target)
    return loss[0, 0], dy


ADAMW_BYTES_PER_US = 2.5e6


def _adamw(parts, w, m, v, *, name, rides=None, row0=None):
    G, R, C = w.shape
    n = parts[0].shape[0]
    row0 = [0] * G if row0 is None else row0
    unit = 2 * SUBLANES if parts[0].dtype == BF16 else SUBLANES
    tr = _pick(R, max(unit, (256 * 1024) // C), unit)
    nblk = R // tr
    assert all(r % tr == 0 for r in row0)
    bc1 = 1.0 - ADAM_B1 ** ADAM_STEP
    bc2 = 1.0 - ADAM_B2 ** ADAM_STEP

    def body(*refs):
        p_refs = refs[:G]
        w_ref, m_ref, v_ref, g_ref, d_ref, mo_ref, vo_ref = refs[G:]
        for q in range(G):
            @pl.when(pl.program_id(0) == q)
            def _(q=q):
                g = p_refs[q][0].astype(F32)
                for s in range(1, n):
                    g = g + p_refs[q][s].astype(F32)
                m2 = ADAM_B1 * m_ref[...] + (1.0 - ADAM_B1) * g
                v2 = ADAM_B2 * v_ref[...] + (1.0 - ADAM_B2) * (g * g)
                g_ref[...] = g
                mo_ref[...] = m2
                vo_ref[...] = v2
                d_ref[...] = -ADAM_LR * ((m2 / bc1) / (jnp.sqrt(v2 / bc2) + ADAM_EPS) + ADAM_WD * w_ref[...])

    def part_spec(q):
        first = row0[q] // tr
        return pl.BlockSpec((n, tr, C), lambda g, i: (0, first + jnp.where(g == q, i, jnp.where(g < q, 0, nblk - 1)), 0))

    blk = pl.BlockSpec((None, tr, C), lambda g, i: (g, i, 0))
    traffic = G * R * C * (n * parts[0].dtype.itemsize + 7 * 4)
    return _call(
        body, grid=(G, nblk), in_specs=[part_spec(q) for q in range(G)] + [blk, blk, blk],
        out_specs=[blk] * 4, out_shape=[jax.ShapeDtypeStruct((G, R, C), F32)] * 4,
        args=(*parts, w, m, v), name=name, rides=rides, us=traffic / ADAMW_BYTES_PER_US)


def _pair_add(blocks, received, core, *, name):
    _, R, C = blocks.shape
    tr = _pick(R, max(ROW_QUANTUM, (1024 * 1024) // C), ROW_QUANTUM)

    def body(core_ref, mine_ref, theirs_ref, o_ref):
        o_ref[...] = (mine_ref[...].astype(F32) + theirs_ref[...].astype(F32)).astype(o_ref.dtype)

    grid_spec = pltpu.PrefetchScalarGridSpec(
        num_scalar_prefetch=1, grid=(NDEV // 2, R // tr),
        in_specs=[pl.BlockSpec((None, tr, C), lambda q, i, core_ref: (2 * q + core_ref[0], i, 0)),
                  pl.BlockSpec((None, tr, C), lambda q, i, core_ref: (q, i, 0))],
        out_specs=pl.BlockSpec((None, tr, C), lambda q, i, core_ref: (q, i, 0)))
    return pl.pallas_call(body, grid_spec=grid_spec, out_shape=jax.ShapeDtypeStruct((NDEV // 2, R, C), BF16),
                          compiler_params=_params(("arbitrary", "arbitrary")), name=name)(
        jnp.reshape(core, (1,)).astype(jnp.int32), blocks, received)


def _adamw_nd(parts, w, m, v, *, name):
    shape = w.shape
    C = shape[-1]
    R = math.prod(shape[:-1])
    outs = _adamw([parts.reshape(parts.shape[0], R, C)], w.reshape(1, R, C), m.reshape(1, R, C), v.reshape(1, R, C),
                  name=name)
    return [o.reshape(shape) for o in outs]


def _all_gather(arrs, *, name):
    n = len(arrs)
    HBM = pl.BlockSpec(memory_space=pl.ANY)

    def body(*refs):
        ins, outs = refs[:n], refs[n:2 * n]
        send_sems, recv_sems, local_sems = refs[2 * n:]
        x, y, c = _position()
        me, sibling = (x, y, c), (x, y, 1 - c)
        chips = [(1 - x, y), (x, 1 - y), (1 - x, 1 - y)]

        def copy(t, k, block, to, src=None):
            rows = outs[t].at[_index(block)]
            return pltpu.make_async_remote_copy(
                src_ref=rows if src is None else src, dst_ref=rows,
                send_sem=send_sems.at[t, k], recv_sem=recv_sems.at[t, k],
                device_id=to, device_id_type=pl.DeviceIdType.MESH)

        started = []
        mine = []
        for t in range(n):
            cp = pltpu.make_async_copy(ins[t], outs[t].at[_index(me)], local_sems.at[t])
            cp.start()
            mine.append(cp)
            first = [copy(t, 0, me, sibling, src=ins[t])]
            first += [copy(t, 1 + j, me, (*chip, c), src=ins[t]) for j, chip in enumerate(chips)]
            for cp in first:
                cp.start()
            started += first
        for t in range(n):
            for j, chip in enumerate(chips):
                copy(t, 1 + j, (*chip, c), me).wait_recv()
                cp = copy(t, 4 + j, (*chip, c), sibling)
                cp.start()
                started.append(cp)
        for t in range(n):
            copy(t, 0, sibling, me).wait_recv()
            for j, chip in enumerate(chips):
                copy(t, 4 + j, (*chip, 1 - c), me).wait_recv()
        for cp in started:
            cp.wait_send()
        for cp in mine:
            cp.wait()

    return pl.pallas_call(
        body, in_specs=[HBM] * n, out_specs=[HBM] * n,
        out_shape=[jax.ShapeDtypeStruct((NDEV,) + a.shape, a.dtype) for a in arrs],
        scratch_shapes=[pltpu.SemaphoreType.DMA((n, 7)), pltpu.SemaphoreType.DMA((n, 7)),
                        pltpu.SemaphoreType.DMA((n,))],
        name=name)(*arrs)


class _Pack:
    def __init__(self, arrs):
        self.shapes = [a.shape for a in arrs]
        self.sizes = [math.prod(s) for s in self.shapes]
        total = sum(self.sizes)
        unit = SUBLANES * LANES
        self.padded = -(-total // unit) * unit
        flat = jnp.concatenate([a.reshape(-1).astype(F32) for a in arrs] + [jnp.zeros((self.padded - total,), F32)])
        self.packed = flat.reshape(self.padded // LANES, LANES)

    def unpack(self, gathered):
        flat = gathered.reshape(NDEV, self.padded)
        out, off = [], 0
        for shape, size in zip(self.shapes, self.sizes):
            out.append(flat[:, off:off + size].reshape((NDEV,) + tuple(shape)))
            off += size
        return out


def _cols_from_shards(g):
    g = jnp.moveaxis(g, 0, -2)
    return g.reshape(g.shape[:-2] + (g.shape[-2] * g.shape[-1],))


def _cols_to_shards(w):
    K, N = w.shape
    return jnp.moveaxis(w.reshape(K, NDEV, N // NDEV), 1, 0)


def kernel(x, c, w_ada, b_ada, norm_g, w_in, w_a2, b_a2, gla_norm_g, w_out_gla, conv_mix_w, w_out_conv, w_o, w_up, ffn_conv_w, w_down, loss_target, m_w_ada, m_b_ada, m_norm_g, m_w_in, m_w_a2, m_b_a2, m_gla_norm_g, m_w_out_gla, m_conv_mix_w, m_w_out_conv, m_w_o, m_w_up, m_ffn_conv_w, m_w_down, v_w_ada, v_b_ada, v_norm_g, v_w_in, v_w_a2, v_b_a2, v_gla_norm_g, v_w_out_gla, v_conv_mix_w, v_w_out_conv, v_w_o, v_w_up, v_ffn_conv_w, v_w_down):
    depth = w_ada.shape[0]
    S, D = x.shape[1], x.shape[2]
    QK = w_a2.shape[2] * NDEV
    LR = w_a2.shape[1]
    VW = w_out_gla.shape[1]
    CW = w_out_conv.shape[1]
    FF = w_down.shape[1] * NDEV
    NIN = w_in.shape[2] * NDEV
    NADA = w_ada.shape[2]
    LRP = -(-LR // LANES) * LANES
    off_lr = 2 * QK + 2 * VW
    NP = NIN - LR + LRP
    assert NIN == 2 * QK + 2 * VW + LR + 3 * CW + 2 * D
    xi, yi, ci = _position()
    me = _index((xi, yi, ci))
    xs = x.reshape(S, D)
    target = loss_target.reshape(S, D)
    TR = 256
    TRC = 512
    TC = _pick(CW, 512)
    TCF = _pick(FF, 512)

    small = _Pack([c, norm_g, w_a2, conv_mix_w, ffn_conv_w])
    big_names = ["w_in", "w_out_gla", "w_out_conv", "w_o", "w_up", "w_down"]
    big = dict(w_in=w_in, w_out_gla=w_out_gla, w_out_conv=w_out_conv, w_o=w_o, w_up=w_up, w_down=w_down)

    def matrix(key, g):
        if key == "w_up":
            return g
        if key in ("w_o", "w_down"):
            return g.reshape(g.shape[0] * g.shape[1], g.shape[2])
        if key != "w_in":
            return _cols_from_shards(g)
        pieces = [g[d][:, lo - d * nin_shard:hi - d * nin_shard] for _, d, lo, hi in sorted(in_pieces)]
        return jnp.concatenate(pieces + [jnp.zeros((D, LRP - LR), BF16)], axis=-1)

    nin_shard = NIN // NDEV
    in_pieces = []
    for g_lo, g_hi, to in ((0, off_lr, 0), (off_lr, off_lr + LR, NIN - LR), (off_lr + LR, NIN, off_lr)):
        for d in range(NDEV):
            lo, hi = max(g_lo, d * nin_shard), min(g_hi, (d + 1) * nin_shard)
            if lo < hi:
                in_pieces.append((to + lo - g_lo, d, lo, hi))

    def in_grad_blocks(gW_all):
        blocks = []
        for d in range(NDEV):
            mine = sorted((lo, to, hi) for to, dd, lo, hi in in_pieces if dd == d)
            blocks.append(jnp.concatenate([gW_all[:, to:to + hi - lo] for lo, to, hi in mine], axis=1))
        return jnp.stack(blocks)

    (small_g, w_in0_g) = _all_gather([small.packed, w_in[0].astype(BF16)], name="gather_first")
    c_all, norm_g_s, w_a2_s, conv_w_s, ffn_w_s = small.unpack(small_g)
    c_all = c_all.reshape(NDEV, D)
    norm_g_f = _cols_from_shards(norm_g_s)
    w_a2_f = _cols_from_shards(w_a2_s)
    conv_w_f = _cols_from_shards(conv_w_s)
    ffn_w_f = _cols_from_shards(ffn_w_s)
    w_a2_p = jnp.concatenate([w_a2_f, jnp.zeros((depth, LRP - LR, QK), F32)], axis=1)

    rides = _Rides()
    GATHER_PARAMS_PER_US = 80e3
    FORWARD_PARAMS_PER_US = 800e3
    EXCHANGE_PARAMS_PER_US = 45e3
    PAIR_PARAMS_PER_US = 500e3
    BACKWARD_ICI_SHARE = 0.8
    weight_streams = {}
    for l in range(depth):
        for key in big_names:
            if (l, key) == (0, "w_in"):
                continue
            shard = big[key][l].astype(BF16)
            rows, cols = shard.shape
            first = rides.add(_Stream("gather_a", shard, rows, cols, GATHER_PARAMS_PER_US / (NDEV * cols)))
            second = rides.add(_Stream("gather_b", None, rows, cols, FORWARD_PARAMS_PER_US / (NDEV * cols), after=first))
            weight_streams[l, key] = (first, second)
    W = {(0, "w_in"): matrix("w_in", w_in0_g)}

    def weight(l, key):
        if (l, key) not in W:
            first, second = weight_streams[l, key]
            rides.need(first, second)
            W[l, key] = matrix(key, first.buf)
        return W[l, key]

    MP = LANES
    c_pad = jnp.concatenate([c_all, jnp.zeros((MP - NDEV, D), F32)], axis=0)
    (silu_c,) = _rw_fwd(lambda t: (t * _sigmoid(t),), [_v(c_pad)], [], [(D, BF16)], tr=MP, name="silu_c")
    mod_cols = _mm(silu_c, w_ada, name="ada_fwd")[:NDEV].reshape(NDEV, depth, NADA)
    (mod_g,) = _all_gather([jnp.moveaxis(mod_cols, 1, 0)], name="gather_mod")
    mod_all = jnp.moveaxis(mod_g, 0, 2).reshape(depth, NDEV, NDEV * NADA)
    mod_me = lax.dynamic_slice_in_dim(mod_all, me, 1, axis=1).reshape(depth, NDEV * NADA)
    (mod_me,) = _rw_fwd(lambda a, b: (a + b,), [_v(mod_me), _v(b_ada)], [], [(NDEV * NADA, F32)],
                        tr=depth, name="mod_bias")

    col = {}
    off = 0
    for nm, wd in (("q", QK), ("k", QK), ("v", VW), ("r", VW), ("cb", CW), ("cc", CW), ("cx", CW),
                   ("ga", D), ("gb", D), ("lr", LRP)):
        col[nm] = off
        off += wd

    saved = []
    xin = xs
    for l in range(depth):
        sh1, sc1, g1, sh2, sc2, g2 = [mod_me[l:l + 1, i * D:(i + 1) * D] for i in range(6)]
        ng = [norm_g_f[l, i:i + 1] for i in range(4)]
        cw = [conv_w_f[l, i:i + 1] for i in range(CONV_K)]
        fw = [ffn_w_f[l, i:i + 1] for i in range(CONV_K)]
        gn = gla_norm_g[l:l + 1]
        ba2 = b_a2[l:l + 1]
        def proj(act, key, name, out_dtype=BF16):
            return _mm(act, weight(l, key), name=name, rides=rides, out_dtype=out_dtype)

        (h1,) = _rw_fwd(_f_normmod, [_v(xin)], [_v(ng[0]), _v(sc1), _v(sh1)], [(D, BF16)], tr=TR, name=f"normmod1_{l}",
                        rides=rides, us=US_NORMMOD)
        P = proj(h1, "w_in", f"in_proj{l}")
        (a,) = _rw_fwd(_f_loga, [_v(P, LRP, col["lr"])], [V(w_a2_p[l], QK, None), _v(ba2)], [(QK, F32)],
                       tr=TR, name=f"loga{l}")
        o, states = _gla_fwd(P, a, qk=QK, vw=VW, name=f"gla_fwd{l}", rides=rides, us=US_GLA)
        dvh = VW // HEADS
        (ya_in,) = _rw_fwd(_f_glapost, [_v(o, dvh), _v(P, dvh, col["r"])], [V(gn, dvh, None)], [(VW, BF16)],
                           tr=TR, ncol=HEADS, name=f"glapost{l}", rides=rides, us=US_GLAPOST)
        ya = proj(ya_in, "w_out_gla", f"out_gla{l}")
        (yb_in,) = _rw_fwd(_f_convmix, [_v(P, TC, col["cb"]), _v(P, TC, col["cc"]), _v(P, TC, col["cx"])],
                           [_v(w, TC) for w in cw], [(CW, BF16)], tr=TRC, ncol=CW // TC, halo=True, name=f"convmix{l}")
        yb = proj(yb_in, "w_out_conv", f"out_conv{l}")
        (mg,) = _rw_fwd(_f_merge, [_v(P, D, col["ga"]), _v(P, D, col["gb"]), _v(ya), _v(yb)], [], [(D, BF16)],
                        tr=TR, name=f"merge{l}", rides=rides, us=US_MERGE)
        y = proj(mg, "w_o", f"o_proj{l}", F32)
        (x1,) = _rw_fwd(_f_resid, [_v(xin), _v(y)], [_v(g1), _v(ng[1])], [(D, F32)], tr=TR, name=f"resid1_{l}",
                        rides=rides, us=US_RESID)
        (h2,) = _rw_fwd(_f_normmod, [_v(x1)], [_v(ng[2]), _v(sc2), _v(sh2)], [(D, BF16)], tr=TR, name=f"normmod2_{l}",
                        rides=rides, us=US_NORMMOD)
        U = proj(h2, "w_up", f"up_proj{l}")
        (ff,) = _rw_fwd(_f_ffn, [_v(U, TCF, 0), _v(U, TCF, FF)], [_v(w, TCF) for w in fw], [(FF, BF16)],
                        tr=TRC, ncol=FF // TCF, halo=True, name=f"ffn{l}", rides=rides, us=US_FFN)
        y2 = proj(ff, "w_down", f"down_proj{l}", F32)
        (x2,) = _rw_fwd(_f_resid, [_v(x1), _v(y2)], [_v(g2), _v(ng[3])], [(D, F32)], tr=TR, name=f"resid2_{l}",
                        rides=rides, us=US_RESID)
        saved.append(dict(xin=xin, h1=h1, P=P, a=a, o=o, states=states, ya_in=ya_in, ya=ya, yb_in=yb_in, yb=yb,
                          mg=mg, y=y, x1=x1, h2=h2, U=U, ff=ff, y2=y2, mods=(sh1, sc1, g1, sh2, sc2, g2),
                          ng=ng, cw=cw, fw=fw, gn=gn, ba2=ba2))
        xin = x2

    loss_local, dx = _loss_head(xin, target, tr=TR, name="loss_head")
    loss = lax.psum(loss_local, MESH_AXES)

    grad_streams = {}
    IN_WGRAD_ROWS = [(0, 3 * D // 4), (3 * D // 4, D // 4)]
    UP_WGRAD_ROWS = [(0, D // 2), (D // 2, D // 2)]

    rides.ici_share = BACKWARD_ICI_SHARE
    rides.min_ici_us = MIN_ICI_CARRIER_US

    def send(l, key, blocks):
        _, rows, cols = blocks.shape
        entry = {}
        tag = f"{key}{l}_{len(grad_streams.get((l, key), []))}"

        def swapped(pair):
            sums = _pair_add(blocks, pair.buf, ci, name=f"pair_add_{tag}")
            entry["chips"] = rides.add(_Stream("chips", sums, rows, cols, 2 * EXCHANGE_PARAMS_PER_US / (NDEV * cols)))

        entry["pair"] = rides.add(_Stream("pair", blocks, rows, cols, PAIR_PARAMS_PER_US / (NDEV * cols), on_done=swapped))
        grad_streams.setdefault((l, key), []).append(entry)

    def bmm(a, b, **kw):
        kw.setdefault("out_dtype", BF16)
        return _mm(a, b, rides=rides, **kw)

    gsmall = [None] * depth
    dmod = [None] * depth
    for l in reversed(range(depth)):
        s = saved[l]
        sh1, sc1, g1, sh2, sc2, g2 = s["mods"]
        ng, cw, fw = s["ng"], s["cw"], s["fw"]
        (dx1_a, dy2), (dg2, dng3) = _rw_bwd(
            _f_resid, [_v(s["x1"]), _v(s["y2"])], [_v(g2), _v(ng[3])], [_v(dx)],
            row_grads=[F32, BF16], par_grads=[True, True], tr=TR, name=f"resid2_bwd{l}", rides=rides, us=US_RESID_BWD)
        gW_down = bmm(s["ff"], dy2, ta=True, out_dtype=BF16, name=f"down_wgrad{l}")
        send(l, "w_down", gW_down.reshape(NDEV, FF // NDEV, D))
        dff = bmm(dy2, weight(l, "w_down"), tb=True, tall=True, name=f"down_dgrad{l}")
        U = s["U"]
        (dgate, dup), dfw = _rw_bwd(
            _f_ffn, [_v(U, TCF, 0), _v(U, TCF, FF)], [_v(w, TCF) for w in fw], [_v(dff, TCF)],
            row_grads=[BF16, BF16], par_grads=[True] * 3, tr=TRC, ncol=FF // TCF, halo=True, name=f"ffn_bwd{l}",
            rides=rides, us=US_FFN_BWD)
        dU = jnp.concatenate([dgate, dup], axis=1)
        for part, rows in enumerate(UP_WGRAD_ROWS):
            gW_up = bmm(s["h2"], dU, ta=True, out_dtype=BF16, out_shards=True, m_rows=rows, name=f"up_wgrad{l}_{part}")
            send(l, "w_up", gW_up)
        dh2 = bmm(dU, weight(l, "w_up"), tb=True, tall=True, name=f"up_dgrad{l}")
        (dx1,), (dng2, dsc2, dsh2) = _rw_bwd(
            _f_normmod, [_v(s["x1"])], [_v(ng[2]), _v(sc2), _v(sh2)], [_v(dh2)],
            row_grads=[F32], par_grads=[True] * 3, adds={0: dx1_a}, tr=TR, name=f"normmod2_bwd{l}",
            rides=rides, us=US_NORMMOD_BWD)
        (dxin_a, dy), (dg1, dng1) = _rw_bwd(
            _f_resid, [_v(s["xin"]), _v(s["y"])], [_v(g1), _v(ng[1])], [_v(dx1)],
            row_grads=[F32, BF16], par_grads=[True, True], tr=TR, name=f"resid1_bwd{l}", rides=rides, us=US_RESID_BWD)
        gW_o = bmm(s["mg"], dy, ta=True, out_dtype=BF16, name=f"o_wgrad{l}")
        send(l, "w_o", gW_o.reshape(NDEV, D // NDEV, D))
        dmg = bmm(dy, weight(l, "w_o"), tb=True, tall=True, name=f"o_dgrad{l}")
        P = s["P"]
        (dga, dgb, dya, dyb), _ = _rw_bwd(
            _f_merge, [_v(P, D, col["ga"]), _v(P, D, col["gb"]), _v(s["ya"]), _v(s["yb"])], [], [_v(dmg)],
            row_grads=[BF16] * 4, par_grads=[], tr=TR, name=f"merge_bwd{l}", rides=rides, us=US_MERGE_BWD)
        gW_og = bmm(s["ya_in"], dya, ta=True, out_dtype=BF16, name=f"out_gla_wgrad{l}")
        send(l, "w_out_gla", _cols_to_shards(gW_og))
        dya_in = bmm(dya, weight(l, "w_out_gla"), tb=True, name=f"out_gla_dgrad{l}")
        gW_oc = bmm(s["yb_in"], dyb, ta=True, out_dtype=BF16, name=f"out_conv_wgrad{l}")
        send(l, "w_out_conv", _cols_to_shards(gW_oc))
        dyb_in = bmm(dyb, weight(l, "w_out_conv"), tb=True, name=f"out_conv_dgrad{l}")
        (dcb, dcc, dcx), dcw = _rw_bwd(
            _f_convmix, [_v(P, TC, col["cb"]), _v(P, TC, col["cc"]), _v(P, TC, col["cx"])],
            [_v(w, TC) for w in cw], [_v(dyb_in, TC)],
            row_grads=[BF16] * 3, par_grads=[True] * 3, tr=TRC, ncol=CW // TC, halo=True, name=f"convmix_bwd{l}")
        dvh = VW // HEADS
        (do, dr), (dgn,) = _rw_bwd(
            _f_glapost, [_v(s["o"], dvh), _v(P, dvh, col["r"])], [V(s["gn"], dvh, None)], [_v(dya_in, dvh)],
            row_grads=[BF16, BF16], par_grads=[True], tr=TR, ncol=HEADS, name=f"glapost_bwd{l}",
            rides=rides, us=US_GLAPOST_BWD)
        dq, dk, dv, da = _gla_bwd(P, s["a"], s["states"], do, qk=QK, vw=VW, name=f"gla_bwd{l}",
                                  rides=rides, us=US_GLA_BWD)
        (dlr,), (dwa2, dba2) = _rw_bwd(
            _f_loga, [_v(P, LRP, col["lr"])], [V(w_a2_p[l], QK, None), _v(s["ba2"])], [_v(da)],
            row_grads=[BF16], par_grads=[True, True], tr=TR, name=f"loga_bwd{l}")
        dP = jnp.concatenate([dq, dk, dv, dr, dcb, dcc, dcx, dga, dgb, dlr], axis=1)
        for part, rows in enumerate(IN_WGRAD_ROWS):
            gW_all = bmm(s["h1"], dP, ta=True, out_dtype=BF16, m_rows=rows, name=f"in_wgrad{l}_{part}")
            send(l, "w_in", in_grad_blocks(gW_all))
        dh1 = bmm(dP, weight(l, "w_in"), tb=True, name=f"in_dgrad{l}")
        (dx,), (dng0, dsc1, dsh1) = _rw_bwd(
            _f_normmod, [_v(s["xin"])], [_v(ng[0]), _v(sc1), _v(sh1)], [_v(dh1)],
            row_grads=[F32], par_grads=[True] * 3, adds={0: dxin_a}, tr=TR, name=f"normmod1_bwd{l}",
            rides=rides, us=US_NORMMOD_BWD)
        vec = lambda t: t.reshape(1, -1)
        gsmall[l] = dict(
            norm_g=jnp.concatenate([vec(dng0), vec(dng1), vec(dng2), vec(dng3)], axis=0),
            w_a2=dwa2[0, :LR], b_a2=dba2.reshape(QK), gla_norm_g=jnp.sum(dgn, axis=0).reshape(dvh),
            conv_mix_w=jnp.concatenate([vec(t) for t in dcw], axis=0),
            ffn_conv_w=jnp.concatenate([vec(t) for t in dfw], axis=0))
        dmod[l] = jnp.concatenate([vec(t) for t in (dsh1, dsc1, dg1, dsh2, dsc2, dg2)], axis=1).reshape(-1)
    grad_x = dx.reshape(x.shape)

    stack = lambda key, src: jnp.stack([src[l][key] for l in range(depth)])
    small_names = ["norm_g", "w_a2", "b_a2", "gla_norm_g", "conv_mix_w", "ffn_conv_w"]
    gpack = _Pack([jnp.stack(dmod)] + [stack(k, gsmall) for k in small_names])
    (gsm,) = _all_gather([gpack.packed], name="gather_small_grads")
    gs = gpack.unpack(gsm)
    dmod_all = gs[0]
    parts_small = dict(zip(small_names, gs[1:]))

    def my_cols(t, n):
        return lax.dynamic_slice_in_dim(t, me * n, n, axis=t.ndim - 1)

    gW_ada = []
    for l in range(depth):
        dcols = my_cols(dmod_all[:, l], NADA)
        dcols = jnp.concatenate([dcols, jnp.zeros((MP - NDEV, NADA), F32)], axis=0)
        gW_ada.append(bmm(silu_c, dcols, ta=True, out_dtype=F32, name=f"ada_wgrad{l}")[None])
    parts = {}
    parts["b_ada"] = dmod_all
    parts["norm_g"] = my_cols(parts_small["norm_g"], D // NDEV)
    parts["w_a2"] = my_cols(parts_small["w_a2"], QK // NDEV)
    parts["b_a2"] = parts_small["b_a2"]
    parts["gla_norm_g"] = parts_small["gla_norm_g"]
    parts["conv_mix_w"] = my_cols(parts_small["conv_mix_w"], CW // NDEV)
    parts["ffn_conv_w"] = my_cols(parts_small["ffn_conv_w"], FF // NDEV)

    weights = dict(w_ada=(w_ada, m_w_ada, v_w_ada), b_ada=(b_ada, m_b_ada, v_b_ada),
                   norm_g=(norm_g, m_norm_g, v_norm_g), w_in=(w_in, m_w_in, v_w_in), w_a2=(w_a2, m_w_a2, v_w_a2),
                   b_a2=(b_a2, m_b_a2, v_b_a2), gla_norm_g=(gla_norm_g, m_gla_norm_g, v_gla_norm_g),
                   w_out_gla=(w_out_gla, m_w_out_gla, v_w_out_gla), conv_mix_w=(conv_mix_w, m_conv_mix_w, v_conv_mix_w),
                   w_out_conv=(w_out_conv, m_w_out_conv, v_w_out_conv), w_o=(w_o, m_w_o, v_w_o),
                   w_up=(w_up, m_w_up, v_w_up), ffn_conv_w=(ffn_conv_w, m_ffn_conv_w, v_ffn_conv_w),
                   w_down=(w_down, m_w_down, v_w_down))
    done = {}
    for nm in ["w_ada", "w_down", "w_up", "w_o", "w_out_gla", "w_out_conv", "w_in"]:
        w, m, v = weights[nm]
        if nm == "w_ada":
            done[nm] = _adamw(gW_ada, w, m, v, name=f"adamw_{nm}")
            continue
        entries = [e for l in range(depth) for e in grad_streams[l, nm]]
        rides.need(*[e["pair"] for e in entries])
        streams = [e["chips"] for e in entries]
        rides.need(*streams)
        rows = math.gcd(*[s.R for s in streams])
        bufs, row0 = [], []
        for s in streams:
            for first in range(0, s.R, rows):
                bufs.append(s.buf)
                row0.append(first)
        split = lambda t: t.reshape(len(bufs), rows, t.shape[2])
        outs = _adamw(bufs, split(w), split(m), split(v), name=f"adamw_{nm}", row0=row0)
        done[nm] = [o.reshape(w.shape) for o in outs]
    for nm, (w, m, v) in weights.items():
        if nm not in done:
            done[nm] = _adamw_nd(parts[nm], w, m, v, name=f"adamw_{nm}")
    grads, deltas, new_m, new_v = zip(*[done[nm] for nm in weights])
    return (loss, grad_x, *grads, *deltas, *new_m, *new_v)
```

```python
import functools
import math
from typing import Any, NamedTuple

import jax
import jax.numpy as jnp
from jax import lax
from jax.experimental import pallas as pl
from jax.experimental.pallas import tpu as pltpu

F32 = jnp.float32
BF16 = jnp.bfloat16
MESH_AXES = ("x", "y", "c")
NDEV = 8
EPS = 1e-6
CHUNK = 64
HEADS = 4
GLA_TAU = 16.0
CONV_K = 3
ADAM_LR = 0.001
ADAM_B1 = 0.9
ADAM_B2 = 0.999
ADAM_EPS = 1e-08
ADAM_WD = 0.01
ADAM_STEP = 10

LANES = 128
SUBLANES = 8
HALO = 2 * SUBLANES
VMEM_LIMIT = 56 * 1024 * 1024


def _params(sem=None):
    return pltpu.CompilerParams(dimension_semantics=sem, vmem_limit_bytes=VMEM_LIMIT)


def _pick(n, cap, unit=LANES):
    best = None
    for d in range(unit, min(n, cap) + 1, unit):
        if n % d == 0:
            best = d
    return best if best is not None else n


def _position():
    return lax.axis_index("x"), lax.axis_index("y"), lax.axis_index("c")


def _index(p):
    return 4 * p[0] + 2 * p[1] + p[2]


def _peer(k):
    x, y, c = _position()
    return ((1 - x) if k & 4 else x, (1 - y) if k & 2 else y, (1 - c) if k & 1 else c)


ROW_QUANTUM = 2 * SUBLANES
MIN_ROWS = 128
MIN_ICI_CARRIER_US = 50.0


class _Stream:
    def __init__(self, mode, src, rows, cols, rate, after=None, on_done=None):
        self.mode, self.src, self.R, self.rate, self.after = mode, src, rows, rate, after
        self.shape = (NDEV if mode.startswith("gather") else NDEV // 2, rows, cols)
        self.holder = self if after is None else after
        self.buf = None
        self.done = 0
        self.on_done = on_done


class _Exchange:
    def __init__(self, items):
        self.items = items
        self.ins, self.src_pos, self.alias, self.holders, self.out_pos = [], [], {}, [], []
        for s, _, _ in items:
            if not any(s.holder is h for h in self.holders):
                self.holders.append(s.holder)
        for s, _, _ in items:
            self.out_pos.append([n for n, h in enumerate(self.holders) if h is s.holder][0])
            if s.mode == "gather_b":
                self.src_pos.append(None)
            else:
                self.src_pos.append(len(self.ins))
                self.ins.append(s.src)
        for n, h in enumerate(self.holders):
            if h.buf is not None:
                self.alias[len(self.ins)] = n
                self.ins.append(h.buf)
        self.out_shape = [jax.ShapeDtypeStruct(h.shape, BF16) for h in self.holders]
        n = len(items)
        self.scratch = [pltpu.SemaphoreType.DMA((n, NDEV)), pltpu.SemaphoreType.DMA((n, NDEV)),
                        pltpu.SemaphoreType.DMA((n,))]

    def deliver(self, outs):
        for h, o in zip(self.holders, outs):
            h.buf = o

    def _copies(self, ins, outs, sems):
        send_sems, recv_sems, local_sems = sems
        me = _index(_position())
        slot = lambda k: _index(_peer(k))
        local, sends, arrivals = [], [], []
        for n, (s, r0, rc) in enumerate(self.items):
            out, rows = outs[self.out_pos[n]], pl.ds(r0, rc)
            src = out if self.src_pos[n] is None else ins[self.src_pos[n]]
            if s.mode == "pair":
                other = 1 - _position()[2]
                moves = [(q, src.at[2 * q + other, rows], out.at[q, rows], _peer(1), out.at[q, rows])
                         for q in range(NDEV // 2)]
            elif s.mode == "chips":
                chip = lambda k: slot(k) // 2
                local.append(pltpu.make_async_copy(src.at[me // 2, rows], out.at[me // 2, rows], local_sems.at[n]))
                moves = [(k, src.at[chip(k), rows], out.at[me // 2, rows], _peer(k), out.at[chip(k), rows])
                         for k in (2, 4, 6)]
            elif s.mode == "gather_a":
                local.append(pltpu.make_async_copy(src.at[rows], out.at[me, rows], local_sems.at[n]))
                moves = [(k, src.at[rows], out.at[me, rows], _peer(k), out.at[slot(k), rows]) for k in (1, 2, 4, 6)]
            else:
                moves = [(k, src.at[slot(k), rows], out.at[slot(k), rows], _peer(1), out.at[slot(k ^ 1), rows])
                         for k in (2, 4, 6)]
            for k, src_ref, dst_ref, to, landing in moves:
                common = dict(send_sem=send_sems.at[n, k], recv_sem=recv_sems.at[n, k],
                              device_id=to, device_id_type=pl.DeviceIdType.MESH)
                sends.append(pltpu.make_async_remote_copy(src_ref=src_ref, dst_ref=dst_ref, **common))
                arrivals.append(pltpu.make_async_remote_copy(src_ref=landing, dst_ref=landing, **common))
        return local, sends, arrivals

    def start(self, ins, outs, sems):
        local, sends, _ = self._copies(ins, outs, sems)
        for cp in local + sends:
            cp.start()

    def wait(self, ins, outs, sems):
        local, sends, arrivals = self._copies(ins, outs, sems)
        for cp in arrivals:
            cp.wait_recv()
        for cp in sends:
            cp.wait_send()
        for cp in local:
            cp.wait()


class _Rides:
    def __init__(self):
        self.queue = []
        self.finished = []
        self.flushes = 0
        self.ici_share = 1.0
        self.min_ici_us = 45.0

    def add(self, stream):
        self.queue.append(stream)
        return stream

    def take(self, budget, only=None, ici=True):
        ready = {id(s): (s.after.done if s.after is not None else s.R) for s in self.queue}
        left = {"ici": budget * self.ici_share if ici else 0.0, "d2d": budget}
        items = []
        for s in list(self.queue):
            if only is not None and not any(s is t for t in only):
                continue
            link = "d2d" if s.mode in ("gather_b", "pair") else "ici"
            avail = ready[id(s)] - s.done
            if avail <= 0:
                continue
            rc = avail if budget == float("inf") else min(avail, int(left[link] * s.rate) // ROW_QUANTUM * ROW_QUANTUM)
            if rc < min(avail, MIN_ROWS):
                continue
            if avail - rc <= 2 * MIN_ROWS:
                rc = avail
            items.append((s, s.done, rc))
            s.done += rc
            left[link] -= rc / s.rate
            if s.done == s.R:
                self.queue.remove(s)
                self.finished.append(s)
        return items

    def fire(self):
        finished, self.finished = self.finished, []
        for s in finished:
            if s.on_done is not None:
                s.on_done(s)

    def need(self, *streams):
        while any(s.done < s.R for s in streams):
            items = self.take(float("inf"), only=streams)
            comm = _Exchange(items)
            hbm = pl.BlockSpec(memory_space=pl.ANY)
            nin, nout = len(comm.ins), len(comm.holders)

            def body(*refs, comm=comm, nin=nin, nout=nout):
                comm.start(refs[:nin], refs[nin:nin + nout], refs[nin + nout:])
                comm.wait(refs[:nin], refs[nin:nin + nout], refs[nin + nout:])

            outs = pl.pallas_call(body, in_specs=[hbm] * nin, out_specs=[hbm] * nout, out_shape=comm.out_shape,
                                  scratch_shapes=comm.scratch, input_output_aliases=dict(comm.alias),
                                  name=f"exchange_alone{self.flushes}")(*comm.ins)
            self.flushes += 1
            comm.deliver(outs)
            self.fire()


def _call(body, *, grid, in_specs, out_specs, out_shape, scratch_shapes=(), args, name, rides=None, us=0.0):
    sem = ("arbitrary",) * len(grid)
    items = rides.take(us, ici=us >= rides.min_ici_us) if rides is not None else []
    if not items:
        return pl.pallas_call(body, grid=grid, in_specs=in_specs, out_specs=out_specs, out_shape=out_shape,
                              scratch_shapes=list(scratch_shapes), compiler_params=_params(sem), name=name)(*args)
    comm = _Exchange(items)
    nin, nout, nscr, cin, cout = len(in_specs), len(out_specs), len(scratch_shapes), len(comm.ins), len(comm.holders)
    hbm = pl.BlockSpec(memory_space=pl.ANY)

    def both(*refs):
        ins, c_ins = refs[:nin], refs[nin:nin + cin]
        pos = nin + cin
        outs, c_outs = refs[pos:pos + nout], refs[pos + nout:pos + nout + cout]
        pos += nout + cout
        scr, sems = refs[pos:pos + nscr], refs[pos + nscr:]
        first = functools.reduce(lambda p, q: p & q, [pl.program_id(d) == 0 for d in range(len(grid))])
        last = functools.reduce(lambda p, q: p & q, [pl.program_id(d) == grid[d] - 1 for d in range(len(grid))])

        @pl.when(first)
        def _():
            comm.start(c_ins, c_outs, sems)

        body(*ins, *outs, *scr)

        @pl.when(last)
        def _():
            comm.wait(c_ins, c_outs, sems)

    res = pl.pallas_call(both, grid=grid, in_specs=list(in_specs) + [hbm] * cin, out_specs=list(out_specs) + [hbm] * cout,
                         out_shape=list(out_shape) + comm.out_shape, scratch_shapes=list(scratch_shapes) + comm.scratch,
                         input_output_aliases={nin + i: nout + o for i, o in comm.alias.items()},
                         compiler_params=_params(sem), name=name)(*args, *comm.ins)
    comm.deliver(res[nout:])
    rides.fire()
    return res[:nout]


US_GLA, US_GLAPOST, US_MERGE, US_FFN, US_RESID, US_NORMMOD = 100.0, 40.0, 45.0, 110.0, 40.0, 24.0
US_FFN_BWD, US_GLA_BWD, US_MERGE_BWD, US_RESID_BWD, US_NORMMOD_BWD, US_GLAPOST_BWD = 215.0, 118.0, 57.0, 57.0, 54.0, 52.0
MM_VMEM_BUDGET = 44 * 1024 * 1024
MM_FLOPS_PER_US = 780e6


def _mm(a, b, *, ta=False, tb=False, out_dtype=F32, name, rides=None, out_shards=False, m_rows=None, tall=False):
    if ta:
        K, M = a.shape
    else:
        M, K = a.shape
    m_first, M = (0, M) if m_rows is None else m_rows
    sharded = b.ndim == 3
    if sharded:
        ns, n = b.shape[0], b.shape[2]
        Kb, N = (ns * n, b.shape[1]) if tb else (b.shape[1], ns * n)
    elif tb:
        N, Kb = b.shape
    else:
        Kb, N = b.shape
    assert K == Kb, (a.shape, b.shape, ta, tb)
    tm = _pick(M, 2048 if tall else 1024)
    tn = N // NDEV if out_shards else n if (sharded and not tb) else _pick(N, 1152)
    sa, sb, so = a.dtype.itemsize, b.dtype.itemsize, jnp.dtype(out_dtype).itemsize

    def vmem(tk):
        blocks = 2 * (tm * tk * sa + tk * tn * sb) + 2 * tm * tn * so + (tm * tn * 4 if tk < K else 0)
        return blocks + (tm * K * 2 if ta else 0)

    if sharded and tb:
        tk = n
    else:
        tk = K
        if vmem(K) > MM_VMEM_BUDGET:
            fits = [d for d in range(LANES, K, LANES) if K % d == 0 and vmem(d) <= MM_VMEM_BUDGET]
            tk = max(fits) if fits else _pick(K, 512)
    nk = K // tk
    dims = (((1,), (1 if tb else 0,)), ((), ()))

    def dot(a_ref, b_ref, at_ref):
        if ta:
            k = pl.program_id(2)

            @pl.when(pl.program_id(1) == 0)
            def _():
                at_ref[k] = a_ref[...].astype(BF16).T

            lhs = at_ref[k]
        else:
            lhs = a_ref[...].astype(BF16)
        return lax.dot_general(lhs, b_ref[...].astype(BF16), dims, preferred_element_type=F32)

    def body_one(a_ref, b_ref, o_ref, *at_ref):
        o_ref[...] = dot(a_ref, b_ref, *at_ref or (None,)).astype(o_ref.dtype)

    def body_acc(a_ref, b_ref, o_ref, acc_ref, *at_ref):
        k = pl.program_id(2)
        at = (at_ref or (None,))[0]

        @pl.when(k == 0)
        def _():
            acc_ref[...] = dot(a_ref, b_ref, at)

        @pl.when((k > 0) & (k < nk - 1))
        def _():
            acc_ref[...] += dot(a_ref, b_ref, at)

        @pl.when(k == nk - 1)
        def _():
            o_ref[...] = (acc_ref[...] + dot(a_ref, b_ref, at)).astype(o_ref.dtype)

    assert m_first % tm == 0
    i0 = m_first // tm
    a_spec = (pl.BlockSpec((tk, tm), lambda i, j, k: (k, i0 + i)) if ta
              else pl.BlockSpec((tm, tk), lambda i, j, k: (i0 + i, k)))
    if sharded:
        b_spec = (pl.BlockSpec((None, tn, tk), lambda i, j, k: (k, j, 0)) if tb
                  else pl.BlockSpec((None, tk, tn), lambda i, j, k: (j, k, 0)))
    else:
        b_spec = (pl.BlockSpec((tn, tk), lambda i, j, k: (j, k)) if tb
                  else pl.BlockSpec((tk, tn), lambda i, j, k: (k, j)))
    if out_shards:
        out_spec = pl.BlockSpec((None, tm, tn), lambda i, j, k: (j, i, 0))
        out_shape = jax.ShapeDtypeStruct((NDEV, M, tn), out_dtype)
    else:
        out_spec = pl.BlockSpec((tm, tn), lambda i, j, k: (i, j))
        out_shape = jax.ShapeDtypeStruct((M, N), out_dtype)
    (out,) = _call(
        body_one if nk == 1 else body_acc, grid=(M // tm, N // tn, nk), in_specs=[a_spec, b_spec],
        out_specs=[out_spec], out_shape=[out_shape],
        scratch_shapes=([] if nk == 1 else [pltpu.VMEM((tm, tn), F32)]) + ([pltpu.VMEM((nk, tm, tk), BF16)] if ta else []),
        args=(a, b), name=name,
        rides=rides, us=2.0 * M * N * K / MM_FLOPS_PER_US)
    return out


class V(NamedTuple):
    arr: Any
    width: int
    base: Any


def _v(arr, width=None, col0=0):
    width = arr.shape[-1] if width is None else width
    assert col0 % width == 0
    return V(arr, width, col0 // width)


def _row_specs(v, tr, nrow8, halo):
    main = pl.BlockSpec((tr, v.width), lambda j, i, b=v.base: (i, b + j))
    if not halo:
        return [main]
    per = tr // HALO
    prev = pl.BlockSpec((HALO, v.width), lambda j, i, b=v.base: (jnp.maximum(i * per - 1, 0), b + j))
    nxt = pl.BlockSpec((HALO, v.width), lambda j, i, b=v.base: (jnp.minimum((i + 1) * per, nrow8 - 1), b + j))
    return [prev, main, nxt]


def _par_spec(p):
    if p.base is None:
        return pl.BlockSpec(p.arr.shape, lambda j, i: (0,) * p.arr.ndim)
    return pl.BlockSpec((p.arr.shape[0], p.width), lambda j, i, b=p.base: (0, b + j))


def _load_rows(refs, i, nrow, halo):
    if not halo:
        return refs[0][...].astype(F32)
    prev, main, nxt = refs
    pv = jnp.where(i > 0, prev[...].astype(F32), 0.0)
    nv = jnp.where(i < nrow - 1, nxt[...].astype(F32), 0.0)
    return jnp.concatenate([pv, main[...].astype(F32), nv], axis=0)


def _center_mask(tr):
    row = lax.broadcasted_iota(jnp.int32, (tr + 2 * HALO, 1), 0)
    return ((row >= HALO) & (row < HALO + tr)).astype(F32)


def _rw_fwd(f, rows, params, outs, *, tr, ncol=1, halo=False, name, rides=None, us=0.0):
    S = rows[0].arr.shape[0]
    tr = min(tr, S)
    nrow = S // tr
    per = 3 if halo else 1
    nr, npar = len(rows), len(params)

    def body(*refs):
        i = pl.program_id(1)
        vals = [_load_rows(refs[per * r: per * (r + 1)], i, nrow, halo) for r in range(nr)]
        pvals = [refs[per * nr + p][...].astype(F32) for p in range(npar)]
        out_refs = refs[per * nr + npar:]
        res = f(*vals, *pvals, _center_mask(tr)) if halo else f(*vals, *pvals)
        for r, o in zip(res, out_refs):
            o[...] = (r[HALO:HALO + tr] if halo else r).astype(o.dtype)

    in_specs = []
    for v in rows:
        in_specs += _row_specs(v, tr, S // HALO, halo)
    in_specs += [_par_spec(p) for p in params]
    args = []
    for v in rows:
        args += [v.arr] * per
    args += [p.arr for p in params]
    return _call(
        body, grid=(ncol, nrow), in_specs=in_specs,
        out_specs=[pl.BlockSpec((tr, tot // ncol), lambda j, i: (i, j)) for tot, _ in outs],
        out_shape=[jax.ShapeDtypeStruct((S, tot), dt) for tot, dt in outs],
        args=args, name=name, rides=rides, us=us)


def _rw_bwd(f, rows, params, douts, *, row_grads, par_grads, adds=None, tr, ncol=1, halo=False, name,
            rides=None, us=0.0):
    S = rows[0].arr.shape[0]
    tr = min(tr, S)
    nrow = S // tr
    per = 3 if halo else 1
    adds = adds or {}
    nr, npar, nd = len(rows), len(params), len(douts)
    add_keys = sorted(adds)
    rg_idx = [k for k in range(nr) if row_grads[k]]
    pg_idx = [k for k in range(npar) if par_grads[k]]

    def body(*refs):
        i = pl.program_id(1)
        pos = 0
        vals = [_load_rows(refs[per * r: per * (r + 1)], i, nrow, halo) for r in range(nr)]
        pos = per * nr
        pvals = [refs[pos + p][...].astype(F32) for p in range(npar)]
        pos += npar
        dvals = [_load_rows(refs[pos + per * d: pos + per * (d + 1)], i, nrow, halo) for d in range(nd)]
        pos += per * nd
        add_vals = {k: refs[pos + n][...].astype(F32) for n, k in enumerate(add_keys)}
        pos += len(add_keys)
        rg_refs = refs[pos: pos + len(rg_idx)]
        pg_refs = refs[pos + len(rg_idx):]
        if halo:
            center = _center_mask(tr)
            fun = lambda *a: f(*a, center)
        else:
            fun = f
        _, vjp = jax.vjp(fun, *vals, *pvals)
        grads = vjp(tuple(dvals))
        for ref, k in zip(rg_refs, rg_idx):
            g = grads[k]
            g = g[HALO:HALO + tr] if halo else g
            if k in add_vals:
                g = g + add_vals[k]
            ref[...] = g.astype(ref.dtype)
        for ref, k in zip(pg_refs, pg_idx):
            @pl.when(i == 0)
            def _(ref=ref):
                ref[...] = jnp.zeros_like(ref)
            ref[...] += grads[nr + k]

    in_specs, args = [], []
    for v in list(rows):
        in_specs += _row_specs(v, tr, S // HALO, halo)
        args += [v.arr] * per
    in_specs += [_par_spec(p) for p in params]
    args += [p.arr for p in params]
    for v in douts:
        in_specs += _row_specs(v, tr, S // HALO, halo)
        args += [v.arr] * per
    for k in add_keys:
        in_specs += _row_specs(_v(adds[k], rows[k].width), tr, S // HALO, False)
        args += [adds[k]]
    out_specs, out_shape = [], []
    for k in rg_idx:
        w = rows[k].width
        out_specs.append(pl.BlockSpec((tr, w), lambda j, i: (i, j)))
        out_shape.append(jax.ShapeDtypeStruct((S, ncol * w), row_grads[k]))
    for k in pg_idx:
        p = params[k]
        blk = p.arr.shape if p.base is None else (p.arr.shape[0], p.width)
        out_specs.append(pl.BlockSpec((None,) + tuple(blk), lambda j, i, n=len(blk): (j,) + (0,) * n))
        out_shape.append(jax.ShapeDtypeStruct((ncol,) + tuple(blk), F32))
    res = _call(body, grid=(ncol, nrow), in_specs=in_specs, out_specs=out_specs, out_shape=out_shape, args=args, name=name,
                rides=rides, us=us)
    rg = [None] * nr
    for n, k in enumerate(rg_idx):
        rg[k] = res[n]
    pg = [None] * npar
    for n, k in enumerate(pg_idx):
        pg[k] = res[len(rg_idx) + n]
    return rg, pg


def _sigmoid(x):
    return 1.0 / (1.0 + jnp.exp(-x))


def _rms(x, g):
    return x * lax.rsqrt(jnp.mean(x * x, axis=-1, keepdims=True) + EPS) * g


def _f_normmod(x, g, sc, sh):
    return (_rms(x, g) * (1.0 + sc) + sh,)


def _f_resid(x, y, gate, g):
    return (x + gate * _rms(y, g),)


def _f_merge(ga, gb, ya, yb):
    return (_sigmoid(ga) * ya + _sigmoid(gb) * yb,)


def _f_glapost(o, r, g):
    return (_rms(o, g) * (r * _sigmoid(r)),)


def _roll_rows(x, k):
    return pltpu.roll(x, k % x.shape[0], 0)


@functools.partial(jax.custom_vjp, nondiff_argnums=(1,))
def _shift(x, k):
    return _roll_rows(x, k)


def _shift_fwd(x, k):
    return _roll_rows(x, k), None


def _shift_bwd(k, _, g):
    return (_roll_rows(g, -k),)


_shift.defvjp(_shift_fwd, _shift_bwd)


def _taps(w, center):
    return jnp.where(center > 0.0, w, lax.stop_gradient(w))


def _conv3(u, w0, w1, w2, center):
    return (_taps(w0, center) * _shift(u, 2) + _taps(w1, center) * _shift(u, 1)) + _taps(w2, center) * u


def _f_convmix(cb, cc, cx, w0, w1, w2, center):
    return (cb * _conv3(cc * cx, w0, w1, w2, center),)


def _gelu_tanh(x):
    return 0.5 * x * (1.0 + jnp.tanh(math.sqrt(2.0 / math.pi) * (x + 0.044715 * (x * x * x))))


def _f_ffn(gate, up, w0, w1, w2, center):
    return (_gelu_tanh(_conv3(gate, w0, w1, w2, center)) * up,)


@jax.custom_vjp
def _bdot(a, b):
    return jnp.dot(a.astype(BF16), b.astype(BF16), preferred_element_type=F32)


def _bdot_fwd(a, b):
    return _bdot(a, b), (a, b)


def _bdot_bwd(res, g):
    a, b = res
    gb = g.astype(BF16)
    da = lax.dot_general(gb, b.astype(BF16), (((1,), (1,)), ((), ())), preferred_element_type=F32)
    db = lax.dot_general(a.astype(BF16), gb, (((0,), (0,)), ((), ())), preferred_element_type=F32)
    return da, db


_bdot.defvjp(_bdot_fwd, _bdot_bwd)


@jax.custom_vjp
def _log_sigmoid(z):
    return jnp.minimum(z, 0.0) - jnp.log(1.0 + jnp.exp(-jnp.abs(z)))


def _log_sigmoid_fwd(z):
    return _log_sigmoid(z), z


def _log_sigmoid_bwd(z, g):
    return (g * _sigmoid(-z),)


_log_sigmoid.defvjp(_log_sigmoid_fwd, _log_sigmoid_bwd)


def _f_loga(lr, w, b):
    return (_log_sigmoid(_bdot(lr, w) + b) / GLA_TAU,)


def _split3(x):
    hi = x.astype(BF16)
    r1 = x - hi.astype(F32)
    mid = r1.astype(BF16)
    lo = (r1 - mid.astype(F32)).astype(BF16)
    return hi, mid, lo


def _dot_exact(a01, x, dims):
    hi, mid, lo = _split3(x)
    d = lambda p: lax.dot_general(a01, p, dims, preferred_element_type=F32)
    return (d(lo) + d(mid)) + d(hi)


def _dot_exact_lhs(x, a01, dims):
    hi, mid, lo = _split3(x)
    d = lambda p: lax.dot_general(p, a01, dims, preferred_element_type=F32)
    return (d(lo) + d(mid)) + d(hi)


GLA_CHUNKS_PER_STEP = 8

NN = (((1,), (0,)), ((), ()))
NT = (((1,), (1,)), ((), ()))
TN = (((0,), (0,)), ((), ()))


def _bf_dot(a, b, dims):
    return lax.dot_general(a.astype(BF16), b.astype(BF16), dims, preferred_element_type=F32)


def _chunk_decay(a, dv):
    L = a.shape[0]
    r = lax.broadcasted_iota(jnp.int32, (L, L), 0)
    c = lax.broadcasted_iota(jnp.int32, (L, L), 1)
    tri = (r >= c).astype(BF16)
    cum = _dot_exact(tri, a, NN)
    ce = cum[L - 1:L, :]
    dec = jnp.exp(_dot_exact_lhs(a, jnp.ones((L, dv), BF16), TN))
    return cum, ce, jnp.exp(ce - cum), dec


def _gla_fwd(P, a, *, qk, vw, name, rides=None, us=0.0):
    S = P.shape[0]
    nc = S // CHUNK
    cps = math.gcd(nc, GLA_CHUNKS_PER_STEP)
    rows = cps * CHUNK
    dk, dv = qk // HEADS, vw // HEADS
    scale = dk ** -0.5

    def body(q_ref, k_ref, v_ref, a_ref, o_ref, st_ref, state):
        @pl.when(pl.program_id(0) == 0)
        def _():
            state[...] = jnp.zeros_like(state)

        for h in range(HEADS):
            ks, vs = slice(h * dk, (h + 1) * dk), slice(h * dv, (h + 1) * dv)
            s = state[h]
            for c in range(cps):
                rs = slice(c * CHUNK, (c + 1) * CHUNK)
                _, _, E, dec = _chunk_decay(a_ref[rs, ks], dv)
                kd = k_ref[rs, ks] * E
                s = dec * s + _bf_dot(kd, v_ref[rs, vs], TN)
                st_ref[c, h] = s
                o_ref[rs, vs] = _bf_dot(q_ref[rs, ks] * scale, s, NN).astype(o_ref.dtype)
            state[h] = s

    return _call(
        body, grid=(nc // cps,),
        in_specs=[pl.BlockSpec((rows, qk), lambda i: (i, 0)),
                  pl.BlockSpec((rows, qk), lambda i: (i, 1)),
                  pl.BlockSpec((rows, vw), lambda i: (i, 2 * qk // vw)),
                  pl.BlockSpec((rows, qk), lambda i: (i, 0))],
        out_specs=[pl.BlockSpec((rows, vw), lambda i: (i, 0)),
                   pl.BlockSpec((cps, HEADS, dk, dv), lambda i: (i, 0, 0, 0))],
        out_shape=[jax.ShapeDtypeStruct((S, vw), BF16), jax.ShapeDtypeStruct((nc, HEADS, dk, dv), F32)],
        scratch_shapes=[pltpu.VMEM((HEADS, dk, dv), F32)], args=(P, P, P, a), name=name, rides=rides, us=us)


def _gla_bwd(P, a, states, do, *, qk, vw, name, rides=None, us=0.0):
    S = P.shape[0]
    nc = S // CHUNK
    cps = math.gcd(nc, GLA_CHUNKS_PER_STEP)
    rows = cps * CHUNK
    nstep = nc // cps
    dk, dv = qk // HEADS, vw // HEADS
    scale = dk ** -0.5

    def body(q_ref, k_ref, v_ref, a_ref, do_ref, sc_ref, sp_ref, dq_ref, dk_ref, dv_ref, da_ref, ds_carry):
        i = pl.program_id(0)

        @pl.when(i == 0)
        def _():
            ds_carry[...] = jnp.zeros_like(ds_carry)

        has_prev = (i < nstep - 1).astype(F32)
        L = CHUNK
        r = lax.broadcasted_iota(jnp.int32, (L, L), 0)
        c = lax.broadcasted_iota(jnp.int32, (L, L), 1)
        tri_t = (c >= r).astype(BF16)
        last = (lax.broadcasted_iota(jnp.int32, (L, 1), 0) == L - 1).astype(F32)
        for h in range(HEADS):
            ks, vs = slice(h * dk, (h + 1) * dk), slice(h * dv, (h + 1) * dv)
            carry = ds_carry[h]
            for n in reversed(range(cps)):
                rs = slice(n * CHUNK, (n + 1) * CHUNK)
                _, _, E, dec = _chunk_decay(a_ref[rs, ks], dv)
                kk, vv, dov = k_ref[rs, ks], v_ref[rs, vs], do_ref[rs, vs]
                kd = kk * E
                ds = carry + _bf_dot(q_ref[rs, ks] * scale, dov, TN)
                dq_ref[rs, ks] = (_bf_dot(dov, sc_ref[n, h], NT) * scale).astype(dq_ref.dtype)
                dkd = _bf_dot(vv, ds, NT)
                dv_ref[rs, vs] = _bf_dot(kd, ds, NN).astype(dv_ref.dtype)
                dk_ref[rs, ks] = (dkd * E).astype(dk_ref.dtype)
                darg = dkd * kd
                before = sc_ref[n - 1, h] if n > 0 else sp_ref[0, h] * has_prev
                w = ds * before * dec
                dce = _dot_exact(jnp.ones((SUBLANES, dv), BF16), w, NT)[0:1, :]
                dce = dce + jnp.sum(darg, axis=0, keepdims=True)
                dcum = last * dce - darg
                da_ref[rs, ks] = _dot_exact(tri_t, dcum, NN)
                carry = dec * ds
            ds_carry[h] = carry

    rev = lambda i: nstep - 1 - i
    return _call(
        body, grid=(nstep,),
        in_specs=[pl.BlockSpec((rows, qk), lambda i: (rev(i), 0)),
                  pl.BlockSpec((rows, qk), lambda i: (rev(i), 1)),
                  pl.BlockSpec((rows, vw), lambda i: (rev(i), 2 * qk // vw)),
                  pl.BlockSpec((rows, qk), lambda i: (rev(i), 0)),
                  pl.BlockSpec((rows, vw), lambda i: (rev(i), 0)),
                  pl.BlockSpec((cps, HEADS, dk, dv), lambda i: (rev(i), 0, 0, 0)),
                  pl.BlockSpec((1, HEADS, dk, dv), lambda i: (jnp.maximum(rev(i) * cps - 1, 0), 0, 0, 0))],
        out_specs=[pl.BlockSpec((rows, qk), lambda i: (rev(i), 0)),
                   pl.BlockSpec((rows, qk), lambda i: (rev(i), 0)),
                   pl.BlockSpec((rows, vw), lambda i: (rev(i), 0)),
                   pl.BlockSpec((rows, qk), lambda i: (rev(i), 0))],
        out_shape=[jax.ShapeDtypeStruct((S, qk), BF16), jax.ShapeDtypeStruct((S, qk), BF16),
                   jax.ShapeDtypeStruct((S, vw), BF16), jax.ShapeDtypeStruct((S, qk), F32)],
        scratch_shapes=[pltpu.VMEM((HEADS, dk, dv), F32)], args=(P, P, P, a, do, states, states), name=name,
        rides=rides, us=us)


def _loss_head(y, target, *, tr, name):
    S, D = y.shape
    tr = min(tr, S)

    def body(y_ref, t_ref, l_ref, dy_ref):
        @pl.when(pl.program_id(0) == 0)
        def _():
            l_ref[...] = jnp.zeros_like(l_ref)

        e = y_ref[...] - t_ref[...]
        dy_ref[...] = e / D
        l_ref[...] += 0.5 * jnp.sum(jnp.mean(e * e, axis=-1, keepdims=True), axis=0, keepdims=True)

    loss, dy = pl.pallas_call(
        body, grid=(S // tr,),
        in_specs=[pl.BlockSpec((tr, D), lambda i: (i, 0))] * 2,
        out_specs=[pl.BlockSpec((SUBLANES, LANES), lambda i: (0, 0)), pl.BlockSpec((tr, D), lambda i: (i, 0))],
        out_shape=[jax.ShapeDtypeStruct((SUBLANES, LANES), F32), jax.ShapeDtypeStruct((S, D), F32)],
        compiler_params=_params(("arbitrary",)), name=name)(y, target)
    return loss[0, 0], dy


ADAMW_BYTES_PER_US = 2.5e6


def _adamw(parts, w, m, v, *, name, rides=None, row0=None):
    G, R, C = w.shape
    n = parts[0].shape[0]
    row0 = [0] * G if row0 is None else row0
    unit = 2 * SUBLANES if parts[0].dtype == BF16 else SUBLANES
    tr = _pick(R, max(unit, (256 * 1024) // C), unit)
    nblk = R // tr
    assert all(r % tr == 0 for r in row0)
    bc1 = 1.0 - ADAM_B1 ** ADAM_STEP
    bc2 = 1.0 - ADAM_B2 ** ADAM_STEP

    def body(*refs):
        p_refs = refs[:G]
        w_ref, m_ref, v_ref, g_ref, d_ref, mo_ref, vo_ref = refs[G:]
        for q in range(G):
            @pl.when(pl.program_id(0) == q)
            def _(q=q):
                g = p_refs[q][0].astype(F32)
                for s in range(1, n):
                    g = g + p_refs[q][s].astype(F32)
                m2 = ADAM_B1 * m_ref[...] + (1.0 - ADAM_B1) * g
                v2 = ADAM_B2 * v_ref[...] + (1.0 - ADAM_B2) * (g * g)
                g_ref[...] = g
                mo_ref[...] = m2
                vo_ref[...] = v2
                d_ref[...] = -ADAM_LR * ((m2 / bc1) / (jnp.sqrt(v2 / bc2) + ADAM_EPS) + ADAM_WD * w_ref[...])

    def part_spec(q):
        first = row0[q] // tr
        return pl.BlockSpec((n, tr, C), lambda g, i: (0, first + jnp.where(g == q, i, jnp.where(g < q, 0, nblk - 1)), 0))

    blk = pl.BlockSpec((None, tr, C), lambda g, i: (g, i, 0))
    traffic = G * R * C * (n * parts[0].dtype.itemsize + 7 * 4)
    return _call(
        body, grid=(G, nblk), in_specs=[part_spec(q) for q in range(G)] + [blk, blk, blk],
        out_specs=[blk] * 4, out_shape=[jax.ShapeDtypeStruct((G, R, C), F32)] * 4,
        args=(*parts, w, m, v), name=name, rides=rides, us=traffic / ADAMW_BYTES_PER_US)


def _pair_add(blocks, received, core, *, name):
    _, R, C = blocks.shape
    tr = _pick(R, max(ROW_QUANTUM, (1024 * 1024) // C), ROW_QUANTUM)

    def body(core_ref, mine_ref, theirs_ref, o_ref):
        o_ref[...] = (mine_ref[...].astype(F32) + theirs_ref[...].astype(F32)).astype(o_ref.dtype)

    grid_spec = pltpu.PrefetchScalarGridSpec(
        num_scalar_prefetch=1, grid=(NDEV // 2, R // tr),
        in_specs=[pl.BlockSpec((None, tr, C), lambda q, i, core_ref: (2 * q + core_ref[0], i, 0)),
                  pl.BlockSpec((None, tr, C), lambda q, i, core_ref: (q, i, 0))],
        out_specs=pl.BlockSpec((None, tr, C), lambda q, i, core_ref: (q, i, 0)))
    return pl.pallas_call(body, grid_spec=grid_spec, out_shape=jax.ShapeDtypeStruct((NDEV // 2, R, C), BF16),
                          compiler_params=_params(("arbitrary", "arbitrary")), name=name)(
        jnp.reshape(core, (1,)).astype(jnp.int32), blocks, received)


def _adamw_nd(parts, w, m, v, *, name):
    shape = w.shape
    C = shape[-1]
    R = math.prod(shape[:-1])
    outs = _adamw([parts.reshape(parts.shape[0], R, C)], w.reshape(1, R, C), m.reshape(1, R, C), v.reshape(1, R, C),
                  name=name)
    return [o.reshape(shape) for o in outs]


def _all_gather(arrs, *, name):
    n = len(arrs)
    HBM = pl.BlockSpec(memory_space=pl.ANY)

    def body(*refs):
        ins, outs = refs[:n], refs[n:2 * n]
        send_sems, recv_sems, local_sems = refs[2 * n:]
        x, y, c = _position()
        me, sibling = (x, y, c), (x, y, 1 - c)
        chips = [(1 - x, y), (x, 1 - y), (1 - x, 1 - y)]

        def copy(t, k, block, to, src=None):
            rows = outs[t].at[_index(block)]
            return pltpu.make_async_remote_copy(
                src_ref=rows if src is None else src, dst_ref=rows,
                send_sem=send_sems.at[t, k], recv_sem=recv_sems.at[t, k],
                device_id=to, device_id_type=pl.DeviceIdType.MESH)

        started = []
        mine = []
        for t in range(n):
            cp = pltpu.make_async_copy(ins[t], outs[t].at[_index(me)], local_sems.at[t])
            cp.start()
            mine.append(cp)
            first = [copy(t, 0, me, sibling, src=ins[t])]
            first += [copy(t, 1 + j, me, (*chip, c), src=ins[t]) for j, chip in enumerate(chips)]
            for cp in first:
                cp.start()
            started += first
        for t in range(n):
            for j, chip in enumerate(chips):
                copy(t, 1 + j, (*chip, c), me).wait_recv()
                cp = copy(t, 4 + j, (*chip, c), sibling)
                cp.start()
                started.append(cp)
        for t in range(n):
            copy(t, 0, sibling, me).wait_recv()
            for j, chip in enumerate(chips):
                copy(t, 4 + j, (*chip, 1 - c), me).wait_recv()
        for cp in started:
            cp.wait_send()
        for cp in mine:
            cp.wait()

    return pl.pallas_call(
        body, in_specs=[HBM] * n, out_specs=[HBM] * n,
        out_shape=[jax.ShapeDtypeStruct((NDEV,) + a.shape, a.dtype) for a in arrs],
        scratch_shapes=[pltpu.SemaphoreType.DMA((n, 7)), pltpu.SemaphoreType.DMA((n, 7)),
                        pltpu.SemaphoreType.DMA((n,))],
        name=name)(*arrs)


class _Pack:
    def __init__(self, arrs):
        self.shapes = [a.shape for a in arrs]
        self.sizes = [math.prod(s) for s in self.shapes]
        total = sum(self.sizes)
        unit = SUBLANES * LANES
        self.padded = -(-total // unit) * unit
        flat = jnp.concatenate([a.reshape(-1).astype(F32) for a in arrs] + [jnp.zeros((self.padded - total,), F32)])
        self.packed = flat.reshape(self.padded // LANES, LANES)

    def unpack(self, gathered):
        flat = gathered.reshape(NDEV, self.padded)
        out, off = [], 0
        for shape, size in zip(self.shapes, self.sizes):
            out.append(flat[:, off:off + size].reshape((NDEV,) + tuple(shape)))
            off += size
        return out


def _cols_from_shards(g):
    g = jnp.moveaxis(g, 0, -2)
    return g.reshape(g.shape[:-2] + (g.shape[-2] * g.shape[-1],))


def _cols_to_shards(w):
    K, N = w.shape
    return jnp.moveaxis(w.reshape(K, NDEV, N // NDEV), 1, 0)


def kernel(x, c, w_ada, b_ada, norm_g, w_in, w_a2, b_a2, gla_norm_g, w_out_gla, conv_mix_w, w_out_conv, w_o, w_up, ffn_conv_w, w_down, loss_target, m_w_ada, m_b_ada, m_norm_g, m_w_in, m_w_a2, m_b_a2, m_gla_norm_g, m_w_out_gla, m_conv_mix_w, m_w_out_conv, m_w_o, m_w_up, m_ffn_conv_w, m_w_down, v_w_ada, v_b_ada, v_norm_g, v_w_in, v_w_a2, v_b_a2, v_gla_norm_g, v_w_out_gla, v_conv_mix_w, v_w_out_conv, v_w_o, v_w_up, v_ffn_conv_w, v_w_down):
    depth = w_ada.shape[0]
    S, D = x.shape[1], x.shape[2]
    QK = w_a2.shape[2] * NDEV
    LR = w_a2.shape[1]
    VW = w_out_gla.shape[1]
    CW = w_out_conv.shape[1]
    FF = w_down.shape[1] * NDEV
    NIN = w_in.shape[2] * NDEV
    NADA = w_ada.shape[2]
    LRP = -(-LR // LANES) * LANES
    off_lr = 2 * QK + 2 * VW
    NP = NIN - LR + LRP
    assert NIN == 2 * QK + 2 * VW + LR + 3 * CW + 2 * D
    xi, yi, ci = _position()
    me = _index((xi, yi, ci))
    xs = x.reshape(S, D)
    target = loss_target.reshape(S, D)
    TR = 256
    TRC = 512
    TC = _pick(CW, 512)
    TCF = _pick(FF, 512)

    small = _Pack([c, norm_g, w_a2, conv_mix_w, ffn_conv_w])
    big_names = ["w_in", "w_out_gla", "w_out_conv", "w_o", "w_up", "w_down"]
    big = dict(w_in=w_in, w_out_gla=w_out_gla, w_out_conv=w_out_conv, w_o=w_o, w_up=w_up, w_down=w_down)

    def matrix(key, g):
        if key == "w_up":
            return g
        if key in ("w_o", "w_down"):
            return g.reshape(g.shape[0] * g.shape[1], g.shape[2])
        if key != "w_in":
            return _cols_from_shards(g)
        pieces = [g[d][:, lo - d * nin_shard:hi - d * nin_shard] for _, d, lo, hi in sorted(in_pieces)]
        return jnp.concatenate(pieces + [jnp.zeros((D, LRP - LR), BF16)], axis=-1)

    nin_shard = NIN // NDEV
    in_pieces = []
    for g_lo, g_hi, to in ((0, off_lr, 0), (off_lr, off_lr + LR, NIN - LR), (off_lr + LR, NIN, off_lr)):
        for d in range(NDEV):
            lo, hi = max(g_lo, d * nin_shard), min(g_hi, (d + 1) * nin_shard)
            if lo < hi:
                in_pieces.append((to + lo - g_lo, d, lo, hi))

    def in_grad_blocks(gW_all):
        blocks = []
        for d in range(NDEV):
            mine = sorted((lo, to, hi) for to, dd, lo, hi in in_pieces if dd == d)
            blocks.append(jnp.concatenate([gW_all[:, to:to + hi - lo] for lo, to, hi in mine], axis=1))
        return jnp.stack(blocks)

    (small_g, w_in0_g) = _all_gather([small.packed, w_in[0].astype(BF16)], name="gather_first")
    c_all, norm_g_s, w_a2_s, conv_w_s, ffn_w_s = small.unpack(small_g)
    c_all = c_all.reshape(NDEV, D)
    norm_g_f = _cols_from_shards(norm_g_s)
    w_a2_f = _cols_from_shards(w_a2_s)
    conv_w_f = _cols_from_shards(conv_w_s)
    ffn_w_f = _cols_from_shards(ffn_w_s)
    w_a2_p = jnp.concatenate([w_a2_f, jnp.zeros((depth, LRP - LR, QK), F32)], axis=1)

    rides = _Rides()
    GATHER_PARAMS_PER_US = 80e3
    FORWARD_PARAMS_PER_US = 800e3
    EXCHANGE_PARAMS_PER_US = 45e3
    PAIR_PARAMS_PER_US = 500e3
    BACKWARD_ICI_SHARE = 0.8
    weight_streams = {}
    for l in range(depth):
        for key in big_names:
            if (l, key) == (0, "w_in"):
                continue
            shard = big[key][l].astype(BF16)
            rows, cols = shard.shape
            first = rides.add(_Stream("gather_a", shard, rows, cols, GATHER_PARAMS_PER_US / (NDEV * cols)))
            second = rides.add(_Stream("gather_b", None, rows, cols, FORWARD_PARAMS_PER_US / (NDEV * cols), after=first))
            weight_streams[l, key] = (first, second)
    W = {(0, "w_in"): matrix("w_in", w_in0_g)}

    def weight(l, key):
        if (l, key) not in W:
            first, second = weight_streams[l, key]
            rides.need(first, second)
            W[l, key] = matrix(key, first.buf)
        return W[l, key]

    MP = LANES
    c_pad = jnp.concatenate([c_all, jnp.zeros((MP - NDEV, D), F32)], axis=0)
    (silu_c,) = _rw_fwd(lambda t: (t * _sigmoid(t),), [_v(c_pad)], [], [(D, BF16)], tr=MP, name="silu_c")
    mod_cols = _mm(silu_c, w_ada, name="ada_fwd")[:NDEV].reshape(NDEV, depth, NADA)
    (mod_g,) = _all_gather([jnp.moveaxis(mod_cols, 1, 0)], name="gather_mod")
    mod_all = jnp.moveaxis(mod_g, 0, 2).reshape(depth, NDEV, NDEV * NADA)
    mod_me = lax.dynamic_slice_in_dim(mod_all, me, 1, axis=1).reshape(depth, NDEV * NADA)
    (mod_me,) = _rw_fwd(lambda a, b: (a + b,), [_v(mod_me), _v(b_ada)], [], [(NDEV * NADA, F32)],
                        tr=depth, name="mod_bias")

    col = {}
    off = 0
    for nm, wd in (("q", QK), ("k", QK), ("v", VW), ("r", VW), ("cb", CW), ("cc", CW), ("cx", CW),
                   ("ga", D), ("gb", D), ("lr", LRP)):
        col[nm] = off
        off += wd

    saved = []
    xin = xs
    for l in range(depth):
        sh1, sc1, g1, sh2, sc2, g2 = [mod_me[l:l + 1, i * D:(i + 1) * D] for i in range(6)]
        ng = [norm_g_f[l, i:i + 1] for i in range(4)]
        cw = [conv_w_f[l, i:i + 1] for i in range(CONV_K)]
        fw = [ffn_w_f[l, i:i + 1] for i in range(CONV_K)]
        gn = gla_norm_g[l:l + 1]
        ba2 = b_a2[l:l + 1]
        def proj(act, key, name, out_dtype=BF16):
            return _mm(act, weight(l, key), name=name, rides=rides, out_dtype=out_dtype, tall=key == "w_in")

        (h1,) = _rw_fwd(_f_normmod, [_v(xin)], [_v(ng[0]), _v(sc1), _v(sh1)], [(D, BF16)], tr=TR, name=f"normmod1_{l}",
                        rides=rides, us=US_NORMMOD)
        P = proj(h1, "w_in", f"in_proj{l}")
        (a,) = _rw_fwd(_f_loga, [_v(P, LRP, col["lr"])], [V(w_a2_p[l], QK, None), _v(ba2)], [(QK, F32)],
                       tr=TR, name=f"loga{l}")
        o, states = _gla_fwd(P, a, qk=QK, vw=VW, name=f"gla_fwd{l}", rides=rides, us=US_GLA)
        dvh = VW // HEADS
        (ya_in,) = _rw_fwd(_f_glapost, [_v(o, dvh), _v(P, dvh, col["r"])], [V(gn, dvh, None)], [(VW, BF16)],
                           tr=TR, ncol=HEADS, name=f"glapost{l}", rides=rides, us=US_GLAPOST)
        ya = proj(ya_in, "w_out_gla", f"out_gla{l}")
        (yb_in,) = _rw_fwd(_f_convmix, [_v(P, TC, col["cb"]), _v(P, TC, col["cc"]), _v(P, TC, col["cx"])],
                           [_v(w, TC) for w in cw], [(CW, BF16)], tr=TRC, ncol=CW // TC, halo=True, name=f"convmix{l}")
        yb = proj(yb_in, "w_out_conv", f"out_conv{l}")
        (mg,) = _rw_fwd(_f_merge, [_v(P, D, col["ga"]), _v(P, D, col["gb"]), _v(ya), _v(yb)], [], [(D, BF16)],
                        tr=TR, name=f"merge{l}", rides=rides, us=US_MERGE)
        y = proj(mg, "w_o", f"o_proj{l}", F32)
        (x1,) = _rw_fwd(_f_resid, [_v(xin), _v(y)], [_v(g1), _v(ng[1])], [(D, F32)], tr=TR, name=f"resid1_{l}",
                        rides=rides, us=US_RESID)
        (h2,) = _rw_fwd(_f_normmod, [_v(x1)], [_v(ng[2]), _v(sc2), _v(sh2)], [(D, BF16)], tr=TR, name=f"normmod2_{l}",
                        rides=rides, us=US_NORMMOD)
        U = proj(h2, "w_up", f"up_proj{l}")
        (ff,) = _rw_fwd(_f_ffn, [_v(U, TCF, 0), _v(U, TCF, FF)], [_v(w, TCF) for w in fw], [(FF, BF16)],
                        tr=TRC, ncol=FF // TCF, halo=True, name=f"ffn{l}", rides=rides, us=US_FFN)
        y2 = proj(ff, "w_down", f"down_proj{l}", F32)
        (x2,) = _rw_fwd(_f_resid, [_v(x1), _v(y2)], [_v(g2), _v(ng[3])], [(D, F32)], tr=TR, name=f"resid2_{l}",
                        rides=rides, us=US_RESID)
        saved.append(dict(xin=xin, h1=h1, P=P, a=a, o=o, states=states, ya_in=ya_in, ya=ya, yb_in=yb_in, yb=yb,
                          mg=mg, y=y, x1=x1, h2=h2, U=U, ff=ff, y2=y2, mods=(sh1, sc1, g1, sh2, sc2, g2),
                          ng=ng, cw=cw, fw=fw, gn=gn, ba2=ba2))
        xin = x2

    loss_local, dx = _loss_head(xin, target, tr=TR, name="loss_head")
    loss = lax.psum(loss_local, MESH_AXES)

    grad_streams = {}
    IN_WGRAD_ROWS = [(0, 3 * D // 4), (3 * D // 4, D // 4)]
    UP_WGRAD_ROWS = [(0, D // 2), (D // 2, D // 2)]

    rides.ici_share = BACKWARD_ICI_SHARE
    rides.min_ici_us = MIN_ICI_CARRIER_US

    def send(l, key, blocks):
        _, rows, cols = blocks.shape
        entry = {}
        tag = f"{key}{l}_{len(grad_streams.get((l, key), []))}"

        def swapped(pair):
            sums = _pair_add(blocks, pair.buf, ci, name=f"pair_add_{tag}")
            entry["chips"] = rides.add(_Stream("chips", sums, rows, cols, 2 * EXCHANGE_PARAMS_PER_US / (NDEV * cols)))

        entry["pair"] = rides.add(_Stream("pair", blocks, rows, cols, PAIR_PARAMS_PER_US / (NDEV * cols), on_done=swapped))
        grad_streams.setdefault((l, key), []).append(entry)

    def bmm(a, b, **kw):
        kw.setdefault("out_dtype", BF16)
        return _mm(a, b, rides=rides, **kw)

    gsmall = [None] * depth
    dmod = [None] * depth
    for l in reversed(range(depth)):
        s = saved[l]
        sh1, sc1, g1, sh2, sc2, g2 = s["mods"]
        ng, cw, fw = s["ng"], s["cw"], s["fw"]
        (dx1_a, dy2), (dg2, dng3) = _rw_bwd(
            _f_resid, [_v(s["x1"]), _v(s["y2"])], [_v(g2), _v(ng[3])], [_v(dx)],
            row_grads=[F32, BF16], par_grads=[True, True], tr=TR, name=f"resid2_bwd{l}", rides=rides, us=US_RESID_BWD)
        gW_down = bmm(s["ff"], dy2, ta=True, out_dtype=BF16, name=f"down_wgrad{l}")
        send(l, "w_down", gW_down.reshape(NDEV, FF // NDEV, D))
        dff = bmm(dy2, weight(l, "w_down"), tb=True, tall=True, name=f"down_dgrad{l}")
        U = s["U"]
        (dgate, dup), dfw = _rw_bwd(
            _f_ffn, [_v(U, TCF, 0), _v(U, TCF, FF)], [_v(w, TCF) for w in fw], [_v(dff, TCF)],
            row_grads=[BF16, BF16], par_grads=[True] * 3, tr=TRC, ncol=FF // TCF, halo=True, name=f"ffn_bwd{l}",
            rides=rides, us=US_FFN_BWD)
        dU = jnp.concatenate([dgate, dup], axis=1)
        for part, rows in enumerate(UP_WGRAD_ROWS):
            gW_up = bmm(s["h2"], dU, ta=True, out_dtype=BF16, out_shards=True, m_rows=rows, name=f"up_wgrad{l}_{part}")
            send(l, "w_up", gW_up)
        dh2 = bmm(dU, weight(l, "w_up"), tb=True, tall=True, name=f"up_dgrad{l}")
        (dx1,), (dng2, dsc2, dsh2) = _rw_bwd(
            _f_normmod, [_v(s["x1"])], [_v(ng[2]), _v(sc2), _v(sh2)], [_v(dh2)],
            row_grads=[F32], par_grads=[True] * 3, adds={0: dx1_a}, tr=TR, name=f"normmod2_bwd{l}",
            rides=rides, us=US_NORMMOD_BWD)
        (dxin_a, dy), (dg1, dng1) = _rw_bwd(
            _f_resid, [_v(s["xin"]), _v(s["y"])], [_v(g1), _v(ng[1])], [_v(dx1)],
            row_grads=[F32, BF16], par_grads=[True, True], tr=TR, name=f"resid1_bwd{l}", rides=rides, us=US_RESID_BWD)
        gW_o = bmm(s["mg"], dy, ta=True, out_dtype=BF16, name=f"o_wgrad{l}")
        send(l, "w_o", gW_o.reshape(NDEV, D // NDEV, D))
        dmg = bmm(dy, weight(l, "w_o"), tb=True, tall=True, name=f"o_dgrad{l}")
        P = s["P"]
        (dga, dgb, dya, dyb), _ = _rw_bwd(
            _f_merge, [_v(P, D, col["ga"]), _v(P, D, col["gb"]), _v(s["ya"]), _v(s["yb"])], [], [_v(dmg)],
            row_grads=[BF16] * 4, par_grads=[], tr=TR, name=f"merge_bwd{l}", rides=rides, us=US_MERGE_BWD)
        gW_og = bmm(s["ya_in"], dya, ta=True, out_dtype=BF16, name=f"out_gla_wgrad{l}")
        send(l, "w_out_gla", _cols_to_shards(gW_og))
        dya_in = bmm(dya, weight(l, "w_out_gla"), tb=True, tall=True, name=f"out_gla_dgrad{l}")
        gW_oc = bmm(s["yb_in"], dyb, ta=True, out_dtype=BF16, name=f"out_conv_wgrad{l}")
        send(l, "w_out_conv", _cols_to_shards(gW_oc))
        dyb_in = bmm(dyb, weight(l, "w_out_conv"), tb=True, tall=True, name=f"out_conv_dgrad{l}")
        (dcb, dcc, dcx), dcw = _rw_bwd(
            _f_convmix, [_v(P, TC, col["cb"]), _v(P, TC, col["cc"]), _v(P, TC, col["cx"])],
            [_v(w, TC) for w in cw], [_v(dyb_in, TC)],
            row_grads=[BF16] * 3, par_grads=[True] * 3, tr=TRC, ncol=CW // TC, halo=True, name=f"convmix_bwd{l}")
        dvh = VW // HEADS
        (do, dr), (dgn,) = _rw_bwd(
            _f_glapost, [_v(s["o"], dvh), _v(P, dvh, col["r"])], [V(s["gn"], dvh, None)], [_v(dya_in, dvh)],
            row_grads=[BF16, BF16], par_grads=[True], tr=TR, ncol=HEADS, name=f"glapost_bwd{l}",
            rides=rides, us=US_GLAPOST_BWD)
        dq, dk, dv, da = _gla_bwd(P, s["a"], s["states"], do, qk=QK, vw=VW, name=f"gla_bwd{l}",
                                  rides=rides, us=US_GLA_BWD)
        (dlr,), (dwa2, dba2) = _rw_bwd(
            _f_loga, [_v(P, LRP, col["lr"])], [V(w_a2_p[l], QK, None), _v(s["ba2"])], [_v(da)],
            row_grads=[BF16], par_grads=[True, True], tr=TR, name=f"loga_bwd{l}")
        dP = jnp.concatenate([dq, dk, dv, dr, dcb, dcc, dcx, dga, dgb, dlr], axis=1)
        for part, rows in enumerate(IN_WGRAD_ROWS):
            gW_all = bmm(s["h1"], dP, ta=True, out_dtype=BF16, m_rows=rows, name=f"in_wgrad{l}_{part}")
            send(l, "w_in", in_grad_blocks(gW_all))
        dh1 = bmm(dP, weight(l, "w_in"), tb=True, name=f"in_dgrad{l}")
        (dx,), (dng0, dsc1, dsh1) = _rw_bwd(
            _f_normmod, [_v(s["xin"])], [_v(ng[0]), _v(sc1), _v(sh1)], [_v(dh1)],
            row_grads=[F32], par_grads=[True] * 3, adds={0: dxin_a}, tr=TR, name=f"normmod1_bwd{l}",
            rides=rides, us=US_NORMMOD_BWD)
        vec = lambda t: t.reshape(1, -1)
        gsmall[l] = dict(
            norm_g=jnp.concatenate([vec(dng0), vec(dng1), vec(dng2), vec(dng3)], axis=0),
            w_a2=dwa2[0, :LR], b_a2=dba2.reshape(QK), gla_norm_g=jnp.sum(dgn, axis=0).reshape(dvh),
            conv_mix_w=jnp.concatenate([vec(t) for t in dcw], axis=0),
            ffn_conv_w=jnp.concatenate([vec(t) for t in dfw], axis=0))
        dmod[l] = jnp.concatenate([vec(t) for t in (dsh1, dsc1, dg1, dsh2, dsc2, dg2)], axis=1).reshape(-1)
    grad_x = dx.reshape(x.shape)

    stack = lambda key, src: jnp.stack([src[l][key] for l in range(depth)])
    small_names = ["norm_g", "w_a2", "b_a2", "gla_norm_g", "conv_mix_w", "ffn_conv_w"]
    gpack = _Pack([jnp.stack(dmod)] + [stack(k, gsmall) for k in small_names])
    (gsm,) = _all_gather([gpack.packed], name="gather_small_grads")
    gs = gpack.unpack(gsm)
    dmod_all = gs[0]
    parts_small = dict(zip(small_names, gs[1:]))

    def my_cols(t, n):
        return lax.dynamic_slice_in_dim(t, me * n, n, axis=t.ndim - 1)

    gW_ada = []
    for l in range(depth):
        dcols = my_cols(dmod_all[:, l], NADA)
        dcols = jnp.concatenate([dcols, jnp.zeros((MP - NDEV, NADA), F32)], axis=0)
        gW_ada.append(bmm(silu_c, dcols, ta=True, out_dtype=F32, name=f"ada_wgrad{l}")[None])
    parts = {}
    parts["b_ada"] = dmod_all
    parts["norm_g"] = my_cols(parts_small["norm_g"], D // NDEV)
    parts["w_a2"] = my_cols(parts_small["w_a2"], QK // NDEV)
    parts["b_a2"] = parts_small["b_a2"]
    parts["gla_norm_g"] = parts_small["gla_norm_g"]
    parts["conv_mix_w"] = my_cols(parts_small["conv_mix_w"], CW // NDEV)
    parts["ffn_conv_w"] = my_cols(parts_small["ffn_conv_w"], FF // NDEV)

    weights = dict(w_ada=(w_ada, m_w_ada, v_w_ada), b_ada=(b_ada, m_b_ada, v_b_ada),
                   norm_g=(norm_g, m_norm_g, v_norm_g), w_in=(w_in, m_w_in, v_w_in), w_a2=(w_a2, m_w_a2, v_w_a2),
                   b_a2=(b_a2, m_b_a2, v_b_a2), gla_norm_g=(gla_norm_g, m_gla_norm_g, v_gla_norm_g),
                   w_out_gla=(w_out_gla, m_w_out_gla, v_w_out_gla), conv_mix_w=(conv_mix_w, m_conv_mix_w, v_conv_mix_w),
                   w_out_conv=(w_out_conv, m_w_out_conv, v_w_out_conv), w_o=(w_o, m_w_o, v_w_o),
                   w_up=(w_up, m_w_up, v_w_up), ffn_conv_w=(ffn_conv_w, m_ffn_conv_w, v_ffn_conv_w),
                   w_down=(w_down, m_w_down, v_w_down))
    done = {}
    for nm in ["w_ada", "w_down", "w_up", "w_o", "w_out_gla", "w_out_conv", "w_in"]:
        w, m, v = weights[nm]
        if nm == "w_ada":
            done[nm] = _adamw(gW_ada, w, m, v, name=f"adamw_{nm}")
            continue
        entries = [e for l in range(depth) for e in grad_streams[l, nm]]
        rides.need(*[e["pair"] for e in entries])
        streams = [e["chips"] for e in entries]
        rides.need(*streams)
        rows = math.gcd(*[s.R for s in streams])
        bufs, row0 = [], []
        for s in streams:
            for first in range(0, s.R, rows):
                bufs.append(s.buf)
                row0.append(first)
        split = lambda t: t.reshape(len(bufs), rows, t.shape[2])
        outs = _adamw(bufs, split(w), split(m), split(v), name=f"adamw_{nm}", row0=row0)
        done[nm] = [o.reshape(w.shape) for o in outs]
    for nm, (w, m, v) in weights.items():
        if nm not in done:
            done[nm] = _adamw_nd(parts[nm], w, m, v, name=f"adamw_{nm}")
    grads, deltas, new_m, new_v = zip(*[done[nm] for nm in weights])
    return (loss, grad_x, *grads, *deltas, *new_m, *new_v)
```
